```python
import jax, jax.numpy as jnp
from jax import lax
import numpy as np

D_MODEL = 1024
BATCH = 8
SEQ = 2048
DEPTH = 1
DEC_BATCH = 128
DEC_SEQ = 4
PAST_LEN = 2048
PAGE_SIZE = 128

HEAD_DIM = 64
N_HEADS = 8
N_KV_HEADS = 2
GQA = N_HEADS // N_KV_HEADS
ATTN_WIDTH = N_HEADS * HEAD_DIM
D_RNN = D_MODEL - ATTN_WIDTH
RNN_BLOCKS = 8
RNN_BLOCK = D_RNN // RNN_BLOCKS
CONV_WIDTH = 4
LRU_C = 8.0
D_FF = 4 * D_MODEL
ROT_DIM = HEAD_DIM // 4
ROPE_THETA = 500000.0
CMP_BLOCK = 32
CMP_STRIDE = 16
SEL_BLOCK = 64
TOP_N = 16
WINDOW = 512
WIN_QBLK = 128
SLC_QBLK = 64
N_KV_SLOTS = 4
KV_COLS = N_KV_HEADS * HEAD_DIM
GATE_COLS = 3 * N_HEADS
IN_COLS = ATTN_WIDTH + 6 * KV_COLS + GATE_COLS + 2 * D_RNN
SPLITS = (ATTN_WIDTH, ATTN_WIDTH + KV_COLS, ATTN_WIDTH + 2 * KV_COLS, ATTN_WIDTH + 3 * KV_COLS,
          ATTN_WIDTH + 4 * KV_COLS, ATTN_WIDTH + 5 * KV_COLS, ATTN_WIDTH + 6 * KV_COLS,
          ATTN_WIDTH + 6 * KV_COLS + GATE_COLS, ATTN_WIDTH + 6 * KV_COLS + GATE_COLS + D_RNN)
EPS = 1e-6
NEG = -1e30
SEL_BONUS = 1e4

kernel_name = 'nsa_rglru_hybrid_step'


def rmsnorm(x, g):
    xf = x.astype(jnp.float32)
    y = xf * lax.rsqrt(jnp.mean(xf * xf, axis=-1, keepdims=True) + EPS) * g.astype(jnp.float32)
    return y.astype(x.dtype)


def rotary(x, pos):
    half = ROT_DIM // 2
    inv = ROPE_THETA ** (-jnp.arange(half, dtype=jnp.float32) / half)
    ang = pos.astype(jnp.float32)[:, None] * inv
    shp = (pos.shape[0],) + (1,) * (x.ndim - 3) + (half,)
    cos, sin = jnp.cos(ang).reshape(shp), jnp.sin(ang).reshape(shp)
    xf = x.astype(jnp.float32)
    x1, x2, rest = xf[..., :half], xf[..., half:ROT_DIM], xf[..., ROT_DIM:]
    return jnp.concatenate([x1 * cos - x2 * sin, x2 * cos + x1 * sin, rest], axis=-1).astype(x.dtype)


def attend(q, k, v, mask):
    s = jnp.einsum('...qhgd,...khd->...hgqk', q, k).astype(jnp.float32) * (HEAD_DIM ** -0.5)
    p = jax.nn.softmax(jnp.where(mask, s, NEG), axis=-1) * mask
    o = jnp.einsum('...hgqk,...khd->...qhgd', p.astype(v.dtype), v)
    return o, p


def compress(x, pe, w1, w2):
    T = x.shape[1]
    n_cmp = (T - CMP_BLOCK) // CMP_STRIDE + 1
    idx = np.arange(n_cmp)[:, None] * CMP_STRIDE + np.arange(CMP_BLOCK)[None, :]
    blocks = x[:, idx] + pe[:, None, :]
    hid = jax.nn.gelu(jnp.einsum('ncjhd,jde->nche', blocks, w1))
    return jnp.einsum('nche,ef->nchf', hid, w2)


def select_blocks(p_cmp, n_sel, qpos):
    n_cmp = p_cmp.shape[-1]
    c0 = np.arange(n_cmp) * CMP_STRIDE
    j0 = np.arange(n_sel) * SEL_BLOCK
    overlap = (c0[:, None] < j0[None, :] + SEL_BLOCK) & (c0[:, None] + CMP_BLOCK > j0[None, :])
    imp = jnp.einsum('nhgqc,cj->nhqj', p_cmp, jnp.asarray(overlap, jnp.float32))
    cur = qpos // SEL_BLOCK
    jj = jnp.arange(n_sel)
    valid = jj[None, :] <= cur[:, None]
    forced = (jj[None, :] == 0) | (jj[None, :] == cur[:, None]) | (jj[None, :] == cur[:, None] - 1)
    score = jnp.where(valid, imp, -SEL_BONUS) + jnp.where(forced, SEL_BONUS, 0.0)
    _, idx = lax.top_k(score, min(TOP_N, n_sel))
    return idx


def selected_attend(q, kb, vb, idx, qpos):
    gather = jax.vmap(jax.vmap(lambda blocks, ix: blocks[ix]))
    kg, vg = gather(kb, idx), gather(vb, idx)
    kpos = idx[..., None] * SEL_BLOCK + jnp.arange(SEL_BLOCK)
    mask = (kpos <= qpos[:, None, None])[:, :, None]
    s = jnp.einsum('nqhgd,nhqksd->nhgqks', q, kg).astype(jnp.float32) * (HEAD_DIM ** -0.5)
    s = jnp.where(mask, s, NEG)
    p = jax.nn.softmax(s.reshape(s.shape[:-2] + (-1,)), axis=-1).reshape(s.shape) * mask
    return jnp.einsum('nhgqks,nhqksd->nqhgd', p.astype(vg.dtype), vg)


def nsa_global(q, q_rot, kv_all, qpos, lw, q_chunk):
    N, T = kv_all.shape[:2]
    kc = compress(kv_all[:, :, 0], lw['pe_ck'], lw['w_ck1'], lw['w_ck2'])
    vc = compress(kv_all[:, :, 1], lw['pe_cv'], lw['w_cv1'], lw['w_cv2'])
    cmp_end = jnp.arange(kc.shape[1]) * CMP_STRIDE + CMP_BLOCK - 1
    o_cmp, p_cmp = attend(q, kc, vc, cmp_end[None, :] <= qpos[:, None])
    n_sel = -(-T // SEL_BLOCK)
    idx = select_blocks(p_cmp, n_sel, qpos)
    pad = n_sel * SEL_BLOCK - T

    def to_blocks(t):
        tp = jnp.pad(t, ((0, 0), (0, pad), (0, 0), (0, 0)))
        return tp.reshape(N, n_sel, SEL_BLOCK, N_KV_HEADS, HEAD_DIM).transpose(0, 3, 1, 2, 4)

    kb, vb = to_blocks(kv_all[:, :, 2]), to_blocks(kv_all[:, :, 3])
    Q = q.shape[1]
    if q_chunk is None or q_chunk >= Q:
        o_slc = selected_attend(q_rot, kb, vb, idx, qpos)
    else:
        nc = Q // q_chunk
        qs = jnp.moveaxis(q_rot.reshape((N, nc, q_chunk) + q_rot.shape[2:]), 1, 0)
        ids = jnp.moveaxis(idx.reshape(N, N_KV_HEADS, nc, q_chunk, idx.shape[-1]), 2, 0)
        ps = qpos.reshape(nc, q_chunk)
        o = lax.map(lambda a: selected_attend(a[0], kb, vb, a[1], a[2]), (qs, ids, ps))
        o_slc = jnp.moveaxis(o, 0, 1).reshape(q_rot.shape)
    return o_cmp, o_slc


def window_banded(q, k, v):
    N, S = q.shape[:2]
    nb, npv = S // WIN_QBLK, WINDOW // WIN_QBLK

    def band(t):
        tp = jnp.pad(t, ((0, 0), (WINDOW, 0), (0, 0), (0, 0)))
        tp = tp.reshape((N, nb + npv, WIN_QBLK) + t.shape[2:])
        return jnp.concatenate([tp[:, i:i + nb] for i in range(npv + 1)], axis=2)

    qb = q.reshape((N, nb, WIN_QBLK) + q.shape[2:])
    qpos = np.arange(nb)[:, None] * WIN_QBLK + np.arange(WIN_QBLK)[None, :]
    kpos = np.arange(nb)[:, None] * WIN_QBLK - WINDOW + np.arange((npv + 1) * WIN_QBLK)[None, :]
    qp, kp = qpos[:, :, None], kpos[:, None, :]
    mask = (kp <= qp) & (kp > qp - WINDOW) & (kp >= 0)
    o, _ = attend(qb, band(k), band(v), jnp.asarray(mask)[None, :, None, None])
    return o.reshape(q.shape)


def window_cached(q_rot, win_buf, win_new, qpos):
    Wc, Q = win_buf.shape[1], win_new.shape[1]
    rows = jnp.concatenate([win_buf, win_new], axis=1)
    kpos = PAST_LEN - Wc + jnp.arange(Wc + Q)
    mask = (kpos[None, :] <= qpos[:, None]) & (kpos[None, :] > qpos[:, None] - WINDOW)
    o, _ = attend(q_rot, rows[:, :, 0], rows[:, :, 1], mask)
    return o, rows[:, rows.shape[1] - Wc:]


def rglru(xr, conv_prev, h0, lw):
    N, T, _ = xr.shape
    xp = jnp.concatenate([conv_prev, xr], axis=1)
    w = lw['conv_w']
    xc = lw['conv_b'] + w[0] * xp[:, 0:T]
    for tap in range(1, CONV_WIDTH):
        xc = xc + w[tap] * xp[:, tap:tap + T]
    xb = xc.reshape(N, T, RNN_BLOCKS, RNN_BLOCK)
    r = jax.nn.sigmoid(jnp.einsum('ntbi,bij->ntbj', xb, lw['w_ra']).reshape(N, T, D_RNN) + lw['b_ra'])
    i = jax.nn.sigmoid(jnp.einsum('ntbi,bij->ntbj', xb, lw['w_ri']).reshape(N, T, D_RNN) + lw['b_ri'])
    log_a = -LRU_C * r.astype(jnp.float32) * jax.nn.softplus(-lw['lam'].astype(jnp.float32))
    a = jnp.exp(log_a)
    b = jnp.sqrt(-jnp.expm1(2.0 * log_a)) * (i * xc).astype(jnp.float32)
    b = b.at[:, 0].add(a[:, 0] * h0.astype(jnp.float32))
    _, h = lax.associative_scan(lambda l, rr: (l[0] * rr[0], rr[0] * l[1] + rr[1]), (a, b), axis=1)
    return h.astype(xr.dtype), xp[:, xp.shape[1] - (CONV_WIDTH - 1):], h[:, -1].astype(xr.dtype)


def project(u, w_in, pos):
    N, T, _ = u.shape
    z = u @ w_in
    q, kc, vc, ks, vs, kw, vw, gates, rg, rx = jnp.split(z, SPLITS, axis=-1)
    q = q.reshape(N, T, N_KV_HEADS, GQA, HEAD_DIM)
    kvr = lambda t: t.reshape(N, T, N_KV_HEADS, HEAD_DIM)
    kv_rows = jnp.stack([kvr(kc), kvr(vc), rotary(kvr(ks), pos), kvr(vs)], axis=2)
    win_rows = jnp.stack([rotary(kvr(kw), pos), kvr(vw)], axis=2)
    gates = jax.nn.sigmoid(gates).reshape(N, T, 3, N_KV_HEADS, GQA)
    return q, rotary(q, pos), kv_rows, win_rows, gates, rg, rx


def layer_step(x, pos, lw, kv_past=None, win_buf=None, conv_prev=None, h0=None):
    N, T, _ = x.shape
    u = rmsnorm(x, lw['norm_mix'])
    q, q_rot, kv_rows, win_rows, gates, rg, rx = project(u, lw['w_in'], pos)
    if kv_past is None:
        kv_all = kv_rows
        o_win = window_banded(q_rot, win_rows[:, :, 0], win_rows[:, :, 1])
        win_state = win_rows[:, T - min(WINDOW, T):]
        conv_prev = jnp.zeros((N, CONV_WIDTH - 1, D_RNN), x.dtype)
        h0 = jnp.zeros((N, D_RNN), x.dtype)
        q_chunk = SLC_QBLK
    else:
        kv_all = jnp.concatenate([kv_past, kv_rows], axis=1)
        o_win, win_state = window_cached(q_rot, win_buf, win_rows, pos)
        q_chunk = None
    o_cmp, o_slc = nsa_global(q, q_rot, kv_all, pos, lw, q_chunk)
    o_attn = (gates[:, :, 0, :, :, None] * o_cmp + gates[:, :, 1, :, :, None] * o_slc
              + gates[:, :, 2, :, :, None] * o_win).reshape(N, T, ATTN_WIDTH)
    h_rnn, conv_state, h_last = rglru(rx, conv_prev, h0, lw)
    mixed = jnp.concatenate([rmsnorm(o_attn, lw['g_attn']),
                             rmsnorm(jax.nn.gelu(rg) * h_rnn, lw['g_rnn'])], axis=-1)
    x = x + mixed @ lw['w_out']
    v = rmsnorm(x, lw['norm_mlp'])
    x = x + jnp.square(jax.nn.relu(v @ lw['w_up'])) @ lw['w_down']
    return x, kv_rows, win_state, conv_state, h_last


def setup_inputs(seed: int = 0) -> dict:
    key = jax.random.key(seed)
    ks = jax.random.split(key, 32)
    f32 = jnp.float32
    nrm = lambda k, shp, s: jax.random.normal(k, shp, f32) * s
    n_pages = PAST_LEN // PAGE_SIZE
    n_used = DEC_BATCH * n_pages
    n_pool = n_used + (n_used + 3) // 4
    perm = jax.random.permutation(ks[6], n_pool)
    page_table = perm[:n_used].reshape(DEC_BATCH, n_pages).astype(jnp.int32)
    a_base = jax.random.uniform(ks[22], (DEPTH, D_RNN), f32, 0.9 ** (1.0 / LRU_C), 0.999 ** (1.0 / LRU_C))
    return {
        'x_prompt': nrm(ks[0], (BATCH, SEQ, D_MODEL), 1.0),
        'x_sample': nrm(ks[1], (DEC_BATCH, DEC_SEQ, D_MODEL), 1.0),
        'cache_kv': nrm(ks[2], (DEPTH, n_pool, PAGE_SIZE, N_KV_SLOTS, N_KV_HEADS, HEAD_DIM), 1.0),
        'cache_win': nrm(ks[3], (DEPTH, DEC_BATCH, min(WINDOW, PAST_LEN), 2, N_KV_HEADS, HEAD_DIM), 1.0),
        'state_conv': nrm(ks[4], (DEPTH, DEC_BATCH, CONV_WIDTH - 1, D_RNN), 1.0),
        'state_rnn': nrm(ks[5], (DEPTH, DEC_BATCH, D_RNN), 0.5),
        'page_table': page_table,
        'w_in': nrm(ks[7], (DEPTH, D_MODEL, IN_COLS), D_MODEL ** -0.5),
        'pe_ck': nrm(ks[8], (DEPTH, CMP_BLOCK, HEAD_DIM), 0.5),
        'w_ck1': nrm(ks[9], (DEPTH, CMP_BLOCK, HEAD_DIM, HEAD_DIM), (CMP_BLOCK * HEAD_DIM) ** -0.5),
        'w_ck2': nrm(ks[10], (DEPTH, HEAD_DIM, HEAD_DIM), HEAD_DIM ** -0.5),
        'pe_cv': nrm(ks[11], (DEPTH, CMP_BLOCK, HEAD_DIM), 0.5),
        'w_cv1': nrm(ks[12], (DEPTH, CMP_BLOCK, HEAD_DIM, HEAD_DIM), (CMP_BLOCK * HEAD_DIM) ** -0.5),
        'w_cv2': nrm(ks[13], (DEPTH, HEAD_DIM, HEAD_DIM), HEAD_DIM ** -0.5),
        'g_attn': 1.0 + nrm(ks[14], (DEPTH, ATTN_WIDTH), 0.1),
        'g_rnn': 1.0 + nrm(ks[15], (DEPTH, D_RNN), 0.1),
        'conv_w': nrm(ks[16], (DEPTH, CONV_WIDTH, D_RNN), CONV_WIDTH ** -0.5),
        'conv_b': nrm(ks[17], (DEPTH, D_RNN), 0.01),
        'w_ra': nrm(ks[18], (DEPTH, RNN_BLOCKS, RNN_BLOCK, RNN_BLOCK), RNN_BLOCK ** -0.5),
        'b_ra': nrm(ks[19], (DEPTH, D_RNN), 0.01),
        'w_ri': nrm(ks[20], (DEPTH, RNN_BLOCKS, RNN_BLOCK, RNN_BLOCK), RNN_BLOCK ** -0.5),
        'b_ri': nrm(ks[21], (DEPTH, D_RNN), 0.01),
        'lam': jnp.log(a_base / (1.0 - a_base)),
        'w_out': nrm(ks[23], (DEPTH, D_MODEL, D_MODEL), D_MODEL ** -0.5),
        'norm_mix': 1.0 + nrm(ks[24], (DEPTH, D_MODEL), 0.1),
        'norm_mlp': 1.0 + nrm(ks[25], (DEPTH, D_MODEL), 0.1),
        'w_up': nrm(ks[26], (DEPTH, D_MODEL, D_FF), D_MODEL ** -0.5),
        'w_down': nrm(ks[27], (DEPTH, D_FF, D_MODEL), D_FF ** -0.5),
        'norm_final': 1.0 + nrm(ks[28], (D_MODEL,), 0.1),
    }


def reference(x_prompt, x_sample, cache_kv, cache_win, state_conv, state_rnn, page_table,
              w_in, pe_ck, w_ck1, w_ck2, pe_cv, w_cv1, w_cv2, g_attn, g_rnn,
              conv_w, conv_b, w_ra, b_ra, w_ri, b_ri, lam, w_out,
              norm_mix, norm_mlp, w_up, w_down, norm_final):
    pos_p = jnp.arange(x_prompt.shape[1], dtype=jnp.int32)
    pos_s = PAST_LEN + jnp.arange(x_sample.shape[1], dtype=jnp.int32)
    n_db = x_sample.shape[0]
    yp, ys = x_prompt, x_sample
    kv_p, kv_s, win_p, win_s, conv_p, conv_s, h_p, h_s = [], [], [], [], [], [], [], []
    for l in range(DEPTH):
        lw = {'w_in': w_in[l], 'pe_ck': pe_ck[l], 'w_ck1': w_ck1[l], 'w_ck2': w_ck2[l],
              'pe_cv': pe_cv[l], 'w_cv1': w_cv1[l], 'w_cv2': w_cv2[l], 'g_attn': g_attn[l], 'g_rnn': g_rnn[l],
              'conv_w': conv_w[l], 'conv_b': conv_b[l], 'w_ra': w_ra[l], 'b_ra': b_ra[l],
              'w_ri': w_ri[l], 'b_ri': b_ri[l], 'lam': lam[l], 'w_out': w_out[l],
              'norm_mix': norm_mix[l], 'norm_mlp': norm_mlp[l], 'w_up': w_up[l], 'w_down': w_down[l]}
        yp, a, b, c, d = layer_step(yp, pos_p, lw)
        kv_p.append(a); win_p.append(b); conv_p.append(c); h_p.append(d)
        past = cache_kv[l][page_table].reshape(n_db, -1, N_KV_SLOTS, N_KV_HEADS, HEAD_DIM)
        ys, a, b, c, d = layer_step(ys, pos_s, lw, past, cache_win[l], state_conv[l], state_rnn[l])
        kv_s.append(a); win_s.append(b); conv_s.append(c); h_s.append(d)
    y_prompt = rmsnorm(yp, norm_final)
    y_sample = rmsnorm(ys, norm_final)
    return (y_prompt, y_sample, jnp.stack(kv_p), jnp.stack(kv_s), jnp.stack(win_p), jnp.stack(win_s),
            jnp.stack(conv_p), jnp.stack(conv_s), jnp.stack(h_p), jnp.stack(h_s))
```

```python
import functools

import jax
import jax.numpy as jnp
import numpy as np
from jax import lax
from jax.experimental import pallas as pl
from jax.experimental.pallas import tpu as pltpu

D_MODEL = 1024
PAST_LEN = 2048
PAGE_SIZE = 128
HEAD_DIM = 64
N_HEADS = 8
N_KV_HEADS = 2
GQA = N_HEADS // N_KV_HEADS
ATTN_WIDTH = N_HEADS * HEAD_DIM
D_RNN = D_MODEL - ATTN_WIDTH
RNN_BLOCKS = 8
CONV_WIDTH = 4
LRU_C = 8.0
D_FF = 4 * D_MODEL
ROT_DIM = HEAD_DIM // 4
ROT_HALF = ROT_DIM // 2
ROPE_THETA = 500000.0
CMP_BLOCK = 32
CMP_STRIDE = 16
SEL_BLOCK = 64
TOP_N = 16
WINDOW = 512
KV_COLS = N_KV_HEADS * HEAD_DIM
EPS = 1e-6
NEG = -1e30
SEL_BONUS = 1e4
SCALE = HEAD_DIM ** -0.5

SUBLANES = 8
LANES = 128
VMEM_LIMIT = 48 * 1024 * 1024

N_CMP_PAD = 128
GATE_LANES = LANES

F32 = jnp.float32
BF16 = jnp.bfloat16
NT_DIMS = (((1,), (1,)), ((), ()))
TN_DIMS = (((0,), (0,)), ((), ()))


def _nt(a, b):
    return lax.dot_general(a, b, NT_DIMS, preferred_element_type=F32)


def _nn(a, b):
    return jnp.dot(a, b, preferred_element_type=F32)


def _rms(x, g):
    return x * lax.rsqrt(jnp.mean(x * x, axis=-1, keepdims=True) + EPS) * g


def _params(*sem):
    return pltpu.CompilerParams(dimension_semantics=sem, vmem_limit_bytes=VMEM_LIMIT)


def _proj_kernel(x_ref, nm_ref, wq_ref, wkv_ref, wr_ref, cosr_ref, sina_ref, sinb_ref, cost_ref, sint_ref,
                 q_ref, qr_ref, kvt_ref, wint_ref, kvb_ref, xc_ref, rg_ref, rx_ref, gates_ref):
    u = _rms(x_ref[...], nm_ref[...]).astype(BF16)
    q = _nt(u, wq_ref[...])
    cosr, sina, sinb = cosr_ref[...], sina_ref[...], sinb_ref[...]
    for k in range(ATTN_WIDTH // LANES):
        qg = q[:, k * LANES:(k + 1) * LANES]
        qrg = (qg * cosr + pltpu.roll(qg, LANES - ROT_HALF, 1) * sina
               + pltpu.roll(qg, ROT_HALF, 1) * sinb)
        for hh in range(LANES // HEAD_DIM):
            h = k * (LANES // HEAD_DIM) + hh
            sl = slice(hh * HEAD_DIM, (hh + 1) * HEAD_DIM)
            q_ref[0, h] = (qg[:, sl] * SCALE).astype(BF16)
            qr_ref[0, h] = (qrg[:, sl] * SCALE).astype(BF16)

    kvt = _nt(wkv_ref[...], u)
    cost, sint = cost_ref[...], sint_ref[...]
    kvt_ref[0] = kvt[0:4 * KV_COLS]
    wint_ref[0] = kvt[4 * KV_COLS:6 * KV_COLS]
    for out_ref, src0, dst0 in ((kvt_ref, 2 * KV_COLS, 2 * KV_COLS), (wint_ref, 4 * KV_COLS, 0)):
        for h in range(N_KV_HEADS):
            s, d = src0 + h * HEAD_DIM, dst0 + h * HEAD_DIM
            x1, x2 = kvt[s:s + ROT_HALF], kvt[s + ROT_HALF:s + ROT_DIM]
            out_ref[0, d:d + ROT_HALF] = x1 * cost - x2 * sint
            out_ref[0, d + ROT_HALF:d + ROT_DIM] = x2 * cost + x1 * sint
    kvb_ref[0, 0:2 * KV_COLS] = kvt_ref[0, 2 * KV_COLS:4 * KV_COLS].astype(BF16)
    kvb_ref[0, 2 * KV_COLS:4 * KV_COLS] = wint_ref[0].astype(BF16)

    xc_ref[...] = _nt(u, wkv_ref[0:2 * KV_COLS, :])
    r = _nt(u, wr_ref[...])
    rg_ref[...] = r[:, 0:D_RNN]
    rx_ref[...] = r[:, D_RNN:2 * D_RNN]
    gates_ref[...] = jax.nn.sigmoid(r[:, 2 * D_RNN:])


def _rope_tables(pos):
    inv = ROPE_THETA ** (-jnp.arange(ROT_HALF, dtype=F32) / ROT_HALF)
    ang = pos.astype(F32)[:, None] * inv
    cos, sin = jnp.cos(ang), jnp.sin(ang)
    d = np.arange(LANES) % HEAD_DIM
    f = d % ROT_HALF
    cosr = jnp.where(d < ROT_DIM, cos[:, f], 1.0)
    sina = jnp.where(d < ROT_HALF, -sin[:, f], 0.0)
    sinb = jnp.where((d >= ROT_HALF) & (d < ROT_DIM), sin[:, f], 0.0)
    return cosr, sina, sinb, cos.T, sin.T


def _proj(x, pos, nm, wq, wkv, wr, nb, tt, tm):
    rows = nb * tt
    nt = tt // tm
    cosr, sina, sinb, cost, sint = _rope_tables(pos)
    row = lambda w: pl.BlockSpec((tm, w), lambda i: (i, 0))
    full = lambda a: pl.BlockSpec(a.shape, lambda i: (0,) * a.ndim)
    tab_r = pl.BlockSpec((tm, LANES), lambda i: (i % nt, 0))
    tab_t = pl.BlockSpec((ROT_HALF, tm), lambda i: (0, i % nt))
    heads = pl.BlockSpec((1, N_HEADS, tm, HEAD_DIM), lambda i: (i // nt, 0, i % nt, 0))
    tr = lambda r: pl.BlockSpec((1, r, tm), lambda i: (i // nt, 0, i % nt))
    out_shape = (
        jax.ShapeDtypeStruct((nb, N_HEADS, tt, HEAD_DIM), BF16),
        jax.ShapeDtypeStruct((nb, N_HEADS, tt, HEAD_DIM), BF16),
        jax.ShapeDtypeStruct((nb, 4 * KV_COLS, tt), F32),
        jax.ShapeDtypeStruct((nb, 2 * KV_COLS, tt), F32),
        jax.ShapeDtypeStruct((nb, 4 * KV_COLS, tt), BF16),
        jax.ShapeDtypeStruct((rows, 2 * KV_COLS), F32),
        jax.ShapeDtypeStruct((rows, D_RNN), F32),
        jax.ShapeDtypeStruct((rows, D_RNN), F32),
        jax.ShapeDtypeStruct((rows, N_KV_HEADS * GATE_LANES), F32),
    )
    out_specs = (heads, heads, tr(4 * KV_COLS), tr(2 * KV_COLS), tr(4 * KV_COLS),
                 row(2 * KV_COLS), row(D_RNN), row(D_RNN), row(N_KV_HEADS * GATE_LANES))
    return pl.pallas_call(
        _proj_kernel, grid=(rows // tm,),
        in_specs=[row(D_MODEL), full(nm), full(wq), full(wkv), full(wr), tab_r, tab_r, tab_r, tab_t, tab_t],
        out_specs=out_specs, out_shape=out_shape, compiler_params=_params("parallel"), name="proj",
    )(x, nm, wq, wkv, wr, cosr, sina, sinb, cost, sint)


def _expm1(x):
    return jnp.tanh(0.5 * x) * (jnp.exp(x) + 1.0)


def _softplus(x):
    return jnp.maximum(x, 0.0) + jnp.log1p(jnp.exp(-jnp.abs(x)))


def _lru_coeffs(xc, wra_ref, wri_ref, bra_ref, bri_ref, lam_ref):
    xb = xc.astype(BF16)
    r = jax.nn.sigmoid(_nn(xb, wra_ref[...]) + bra_ref[...])
    i = jax.nn.sigmoid(_nn(xb, wri_ref[...]) + bri_ref[...])
    log_a = -LRU_C * r * _softplus(-lam_ref[...])
    a = jnp.exp(log_a)
    b = jnp.sqrt(-_expm1(2.0 * log_a)) * (i * xc)
    return a, b


def _rnn_out(rg, h, g):
    return _rms(jax.nn.gelu(rg) * h, g).astype(BF16)


def _rglru_prompt_kernel(rx_ref, rg_ref, cw_ref, cb_ref, wra_ref, wri_ref, bra_ref, bri_ref, lam_ref, g_ref,
                         y_ref, hl_ref, prev_sc, h_sc):
    tt = rx_ref.shape[0]

    @pl.when(pl.program_id(1) == 0)
    def _():
        prev_sc[...] = jnp.zeros_like(prev_sc)
        h_sc[...] = jnp.zeros_like(h_sc)

    rx = rx_ref[...]
    ext = jnp.concatenate([prev_sc[...], rx], axis=0)
    shifted = lambda d: pltpu.roll(ext, d, 0)[SUBLANES:SUBLANES + tt]
    cw = cw_ref[...]
    xc = cb_ref[...] + cw[0:1] * shifted(3)
    xc = xc + cw[1:2] * shifted(2)
    xc = xc + cw[2:3] * shifted(1)
    xc = xc + cw[3:4] * rx
    prev_sc[...] = rx[tt - SUBLANES:tt]

    a, b = _lru_coeffs(xc, wra_ref, wri_ref, bra_ref, bri_ref, lam_ref)
    row = lax.broadcasted_iota(jnp.int32, (tt, 1), 0)
    s = 1
    while s < tt:
        if s < SUBLANES:
            a_sh = jnp.where(row >= s, pltpu.roll(a, s, 0), 1.0)
            b_sh = jnp.where(row >= s, pltpu.roll(b, s, 0), 0.0)
        else:
            a_sh = jnp.concatenate([jnp.ones((s, a.shape[1]), F32), a[:tt - s]], axis=0)
            b_sh = jnp.concatenate([jnp.zeros((s, b.shape[1]), F32), b[:tt - s]], axis=0)
        b = a * b_sh + b
        a = a * a_sh
        s *= 2
    h = a * h_sc[...] + b
    h_sc[...] = h[tt - 1:tt]
    hl_ref[0] = h[tt - 1:tt]
    y_ref[...] = _rnn_out(rg_ref[...], h, g_ref[...])


def _rglru_sample_kernel(rx_ref, rg_ref, cp_ref, h0_ref, cw_ref, cb_ref, wra_ref, wri_ref, bra_ref, bri_ref,
                         lam_ref, g_ref, y_ref, cs_ref, hl_ref):
    nb = h0_ref.shape[0]
    steps = rx_ref.shape[0] // nb
    xp = [cp_ref[k] for k in range(CONV_WIDTH - 1)] + [rx_ref[t * nb:(t + 1) * nb] for t in range(steps)]
    cw, cb = cw_ref[...], cb_ref[...]
    xcs = []
    for t in range(steps):
        xc = cb + cw[0:1] * xp[t]
        for tap in range(1, CONV_WIDTH):
            xc = xc + cw[tap:tap + 1] * xp[t + tap]
        xcs.append(xc)
    a, b = _lru_coeffs(jnp.concatenate(xcs, axis=0), wra_ref, wri_ref, bra_ref, bri_ref, lam_ref)
    h = h0_ref[...]
    for t in range(steps):
        sl = slice(t * nb, (t + 1) * nb)
        h = a[sl] * h + b[sl]
        y_ref[sl] = _rnn_out(rg_ref[sl], h, g_ref[...])
    for k in range(CONV_WIDTH - 1):
        cs_ref[k] = xp[steps + k]
    hl_ref[...] = h


def _rglru_prompt(rx, rg, weights, nb, tt, chunk):
    nc = tt // chunk
    row = pl.BlockSpec((chunk, D_RNN), lambda n, c: (n * nc + c, 0))
    full = lambda a: pl.BlockSpec(a.shape, lambda n, c: (0,) * a.ndim)
    return pl.pallas_call(
        _rglru_prompt_kernel, grid=(nb, nc),
        in_specs=[row, row] + [full(w) for w in weights],
        out_specs=(row, pl.BlockSpec((1, 1, D_RNN), lambda n, c: (n, 0, 0))),
        out_shape=(jax.ShapeDtypeStruct((nb * tt, D_RNN), BF16), jax.ShapeDtypeStruct((nb, 1, D_RNN), F32)),
        scratch_shapes=[pltpu.VMEM((SUBLANES, D_RNN), F32), pltpu.VMEM((1, D_RNN), F32)],
        compiler_params=_params("parallel", "arbitrary"), name="rglru_prompt",
    )(rx, rg, *weights)


def _rglru_sample(rx, rg, conv_prev, h0, weights):
    nb = h0.shape[0]
    return pl.pallas_call(
        _rglru_sample_kernel,
        out_shape=(jax.ShapeDtypeStruct(rx.shape, BF16), jax.ShapeDtypeStruct(conv_prev.shape, F32),
                   jax.ShapeDtypeStruct((nb, D_RNN), F32)),
        compiler_params=pltpu.CompilerParams(vmem_limit_bytes=VMEM_LIMIT), name="rglru_sample",
    )(rx, rg, conv_prev, h0, *weights)


def _compress_kernel(pt_ref, *refs, n_pages):
    del pt_ref
    page_refs = refs[:n_pages]
    pe_ref, w1_ref, w2_ref, out_ref = refs[n_pages:]
    for cs in range(2 * N_KV_HEADS):
        slot = cs // N_KV_HEADS
        x = jnp.concatenate([r[0, cs] for r in page_refs], axis=0)
        p0 = _nn((x + pe_ref[slot, 0]).astype(BF16), w1_ref[slot, 0])
        p1 = _nn((x + pe_ref[slot, 1]).astype(BF16), w1_ref[slot, 1])
        hid = jax.nn.gelu(p0 + pltpu.roll(p1, N_CMP_PAD - 1, 0))
        out_ref[0, cs] = _nn(hid.astype(BF16), w2_ref[slot]).astype(BF16)


def _compress(pages, page_table, pe, w1, w2, nb):
    n_pages = page_table.shape[1]
    rows = pages.shape[2]
    assert n_pages * rows == N_CMP_PAD
    page_spec = lambda k: pl.BlockSpec((1, 2 * N_KV_HEADS, rows, pages.shape[3]),
                                       lambda n, pt: (pt[n, k], 0, 0, 0))
    full = lambda a: pl.BlockSpec(a.shape, lambda n, pt: (0,) * a.ndim)
    grid_spec = pltpu.PrefetchScalarGridSpec(
        num_scalar_prefetch=1, grid=(nb,),
        in_specs=[page_spec(k) for k in range(n_pages)] + [full(pe), full(w1), full(w2)],
        out_specs=pl.BlockSpec((1, 2 * N_KV_HEADS, N_CMP_PAD, HEAD_DIM), lambda n, pt: (n, 0, 0, 0)))
    return pl.pallas_call(
        functools.partial(_compress_kernel, n_pages=n_pages), grid_spec=grid_spec,
        out_shape=jax.ShapeDtypeStruct((nb, 2 * N_KV_HEADS, N_CMP_PAD, HEAD_DIM), BF16),
        compiler_params=_params("parallel"), name="compress",
    )(page_table, *([pages] * n_pages), pe, w1, w2)


def _overlap_matrix(n_sel):
    n_cmp = N_CMP_PAD - 1
    c0 = np.arange(N_CMP_PAD)[:, None] * CMP_STRIDE
    j0 = np.arange(LANES)[None, :] * SEL_BLOCK
    ov = (c0 < j0 + SEL_BLOCK) & (c0 + CMP_BLOCK > j0)
    ov &= (np.arange(N_CMP_PAD)[:, None] < n_cmp) & (np.arange(LANES)[None, :] < n_sel)
    return ov.astype(np.float32)


def _topk_member(score, idx, n, axis):
    rank = jnp.zeros(score.shape, jnp.int32)
    for i in range(n):
        si = score[i:i + 1, :] if axis == 0 else score[:, i:i + 1]
        beats = (si > score) | ((si == score) & (i < idx))
        rank = rank + beats.astype(jnp.int32)
    return rank < TOP_N


def _sel_score(imp, j, cur):
    valid = j <= cur
    forced = (j == 0) | (j == cur) | (j == cur - 1)
    return jnp.where(valid, imp, -SEL_BONUS) + jnp.where(forced, SEL_BONUS, 0.0)


def _nsa_prompt_kernel(q_ref, qr_ref, ks_ref, vs_ref, kw_ref, vw_ref, kc_ref, vc_ref, gates_ref, ovt_ref, e_ref,
                       o_ref, mask_sc, m_sc, l_sc, acc_sc, *, tq, tk, n_sel):
    qi = pl.program_id(2)
    q4 = q_ref[0].reshape(GQA * tq, HEAD_DIM)
    qr4 = qr_ref[0].reshape(GQA * tq, HEAD_DIM)
    qpos = qi * tq + lax.broadcasted_iota(jnp.int32, (tq, 1), 0)

    cidx = lax.broadcasted_iota(jnp.int32, (1, N_CMP_PAD), 1)
    real = cidx < N_CMP_PAD - 1
    cmask = (cidx * CMP_STRIDE + CMP_BLOCK - 1 <= qpos) & real
    s = _nt(q4, kc_ref[0, 0]).reshape(GQA, tq, N_CMP_PAD)
    s = jnp.where(real, jnp.where(cmask, s, NEG), -jnp.inf)
    e = jnp.exp(s - jnp.max(s, axis=-1, keepdims=True))
    p = (e / jnp.sum(e, axis=-1, keepdims=True)) * cmask.astype(F32)
    pb = p.astype(BF16)
    o_cmp = _nn(pb.reshape(GQA * tq, N_CMP_PAD), vc_ref[0, 0])

    imp_t = _nt(ovt_ref[...], pb[0])
    for g in range(1, GQA):
        imp_t = imp_t + _nt(ovt_ref[...], pb[g])
    qpos_t = qi * tq + lax.broadcasted_iota(jnp.int32, (1, tq), 1)
    j_t = lax.broadcasted_iota(jnp.int32, (n_sel, 1), 0)
    score_t = _sel_score(imp_t, j_t, qpos_t // SEL_BLOCK)
    sel_t = _topk_member(score_t, j_t, n_sel, axis=0).astype(BF16)
    mask_sc[...] = lax.dot_general(sel_t, e_ref[...], TN_DIMS, preferred_element_type=F32)

    def flash(qx, kt_ref, vt_ref, lo, hi, mask_fn):
        m_sc[...] = jnp.full(m_sc.shape, NEG, F32)
        l_sc[...] = jnp.zeros_like(l_sc)
        acc_sc[...] = jnp.zeros_like(acc_sc)

        def body(kt, carry):
            off = pl.multiple_of(kt * tk, tk)
            k = kt_ref[0, 0, :, pl.ds(off, tk)]
            v = vt_ref[0, 0, :, pl.ds(off, tk)]
            kpos = off + lax.broadcasted_iota(jnp.int32, (1, tk), 1)
            msk = mask_fn(off, kpos)
            sc = jnp.where(msk, _nn(qx, k).reshape(GQA, tq, tk), NEG)
            m_old = m_sc[...]
            m_new = jnp.maximum(m_old, jnp.max(sc, axis=-1, keepdims=True))
            alpha = jnp.exp(m_old - m_new)
            pe = jnp.where(msk, jnp.exp(sc - m_new), 0.0)
            l_sc[...] = alpha * l_sc[...] + jnp.sum(pe, axis=-1, keepdims=True)
            pv = _nt(pe.astype(BF16).reshape(GQA * tq, tk), v).reshape(GQA, tq, HEAD_DIM)
            acc_sc[...] = alpha * acc_sc[...] + pv
            m_sc[...] = m_new
            return carry

        lax.fori_loop(lo, hi, body, 0)
        return acc_sc[...] / l_sc[...]

    slc_mask = lambda off, kpos: (mask_sc[:, pl.ds(off, tk)] > 0.5) & (kpos <= qpos)
    o_slc = flash(qr4, ks_ref, vs_ref, 0, (qi * tq + tq - 1) // tk + 1, slc_mask)
    win_mask = lambda off, kpos: (kpos <= qpos) & (kpos > qpos - WINDOW)
    o_win = flash(qr4, kw_ref, vw_ref, jnp.maximum(qi * tq - WINDOW + 1, 0) // tk,
                  (qi * tq + tq - 1) // tk + 1, win_mask)

    gt = gates_ref[...]
    o_cmp = o_cmp.reshape(GQA, tq, HEAD_DIM)
    for g in range(GQA):
        o = gt[:, g:g + 1] * o_cmp[g] + gt[:, GQA + g:GQA + g + 1] * o_slc[g]
        o = o + gt[:, 2 * GQA + g:2 * GQA + g + 1] * o_win[g]
        o_ref[:, g * HEAD_DIM:(g + 1) * HEAD_DIM] = o


def _nsa_prompt(q, qr, kvb, cmp, gates, nb, tt, tq, tk):
    n_sel = -(-tt // SEL_BLOCK)
    nq = tt // tq
    ovt = jnp.asarray(_overlap_matrix(n_sel)[:, :n_sel].T, BF16)
    expand = jnp.asarray(np.arange(tt)[None, :] // SEL_BLOCK == np.arange(n_sel)[:, None], BF16)
    kvb = kvb.reshape(nb, 4 * N_KV_HEADS, HEAD_DIM, tt)
    qspec = pl.BlockSpec((1, GQA, tq, HEAD_DIM), lambda n, h, i: (n, h, i, 0))
    kvspec = lambda slot: pl.BlockSpec((1, 1, HEAD_DIM, tt), lambda n, h, i: (n, slot * N_KV_HEADS + h, 0, 0))
    cspec = lambda slot: pl.BlockSpec((1, 1, N_CMP_PAD, HEAD_DIM), lambda n, h, i: (n, slot * N_KV_HEADS + h, 0, 0))
    full = lambda a: pl.BlockSpec(a.shape, lambda n, h, i: (0,) * a.ndim)
    return pl.pallas_call(
        functools.partial(_nsa_prompt_kernel, tq=tq, tk=tk, n_sel=n_sel), grid=(nb, N_KV_HEADS, nq),
        in_specs=[qspec, qspec, kvspec(0), kvspec(1), kvspec(2), kvspec(3), cspec(0), cspec(1),
                  pl.BlockSpec((tq, GATE_LANES), lambda n, h, i: (n * nq + i, h)), full(ovt), full(expand)],
        out_specs=pl.BlockSpec((tq, GQA * HEAD_DIM), lambda n, h, i: (n * nq + i, h)),
        out_shape=jax.ShapeDtypeStruct((nb * tt, ATTN_WIDTH), F32),
        scratch_shapes=[pltpu.VMEM((tq, tt), F32), pltpu.VMEM((GQA, tq, 1), F32), pltpu.VMEM((GQA, tq, 1), F32),
                        pltpu.VMEM((GQA, tq, HEAD_DIM), F32)],
        compiler_params=_params("parallel", "parallel", "arbitrary"), name="nsa_prompt",
    )(q, qr, kvb, kvb, kvb, kvb, cmp, cmp, gates, ovt, expand)


def _nsa_sample_kernel(pt_ref, *refs, n_pages, steps):
    del pt_ref
    q_ref, qr_ref = refs[0:2]
    page_refs = refs[2:2 + n_pages]
    cmp_ref, cw_ref, new_ref, gates_ref, ov_ref, o_ref, wout_ref = refs[2 + n_pages:]
    rows = GQA * steps
    past = n_pages * PAGE_SIZE
    wc = cw_ref.shape[-1]
    new0 = LANES - steps
    trow = lax.broadcasted_iota(jnp.int32, (rows, 1), 0) % steps
    lane = lax.broadcasted_iota(jnp.int32, (1, LANES), 1)
    new_ok = (lane >= new0) & (lane - new0 <= trow)
    n_sel = -(-(past + steps) // SEL_BLOCK)
    cur = (past + trow) // SEL_BLOCK

    for h in range(N_KV_HEADS):
        q, qr = q_ref[0, h], qr_ref[0, h]
        row0 = h * HEAD_DIM
        new_rows = lambda slot: new_ref[0, slot * KV_COLS + row0:slot * KV_COLS + row0 + HEAD_DIM, :]

        real = lane < N_CMP_PAD - 1
        s = jnp.where(real, _nt(q, cmp_ref[0, h]), -jnp.inf)
        e = jnp.exp(s - jnp.max(s, axis=-1, keepdims=True))
        pb = (e / jnp.sum(e, axis=-1, keepdims=True)).astype(BF16)
        o_cmp = _nn(pb, cmp_ref[0, N_KV_HEADS + h])

        part = _nn(pb, ov_ref[...])
        imp = part
        for g in range(1, GQA):
            imp = imp + pltpu.roll(part, g * steps, 0)
        score = jnp.where(lane < n_sel, _sel_score(imp, lane, cur), -jnp.inf)
        sel = _topk_member(score, lane, n_sel, axis=1) & (lane < n_sel)
        sel_f = sel.astype(F32)

        ks = jnp.concatenate([page_refs[k][0, 0, h].astype(BF16) for k in range(n_pages)]
                             + [new_rows(0).astype(BF16)], axis=1)
        vs = jnp.concatenate([page_refs[k][0, 1, h].astype(BF16) for k in range(n_pages)]
                             + [new_rows(1).astype(BF16)], axis=1)
        per_tile = PAGE_SIZE // SEL_BLOCK
        tiles = []
        for k in range(n_pages):
            m = sel_f[:, per_tile * k:per_tile * k + 1]
            for b in range(1, per_tile):
                m = jnp.where(lane < b * SEL_BLOCK, m, sel_f[:, per_tile * k + b:per_tile * k + b + 1])
            tiles.append(m > 0.5)
        tiles.append((sel_f[:, n_sel - 1:n_sel] > 0.5) & new_ok)
        msk = jnp.concatenate(tiles, axis=1)
        sc = jnp.where(msk, _nn(qr, ks), NEG)
        pe = jnp.where(msk, jnp.exp(sc - jnp.max(sc, axis=-1, keepdims=True)), 0.0)
        o_slc = _nt(pe.astype(BF16), vs) / jnp.sum(pe, axis=-1, keepdims=True)

        kw = jnp.concatenate([cw_ref[0, 0, h].astype(BF16), new_rows(2).astype(BF16)], axis=1)
        vw = jnp.concatenate([cw_ref[0, 1, h].astype(BF16), new_rows(3).astype(BF16)], axis=1)
        wlane = lax.broadcasted_iota(jnp.int32, (1, wc), 1)
        wmsk = jnp.concatenate([jnp.broadcast_to(wlane + (WINDOW - wc) > trow, (rows, wc)),
                                jnp.broadcast_to(new_ok, (rows, LANES))], axis=1)
        sw = jnp.where(wmsk, _nn(qr, kw), NEG)
        pw = jnp.where(wmsk, jnp.exp(sw - jnp.max(sw, axis=-1, keepdims=True)), 0.0)
        o_win = _nt(pw.astype(BF16), vw) / jnp.sum(pw, axis=-1, keepdims=True)

        gt = gates_ref[0, h]
        o_ref[0, h] = gt[:, 0:1] * o_cmp + gt[:, 1:2] * o_slc + gt[:, 2:3] * o_win

        for slot in range(2):
            shifted = pltpu.roll(cw_ref[0, slot, h], wc - steps, 1)
            wout_ref[0, slot, h, :, 0:wc - LANES] = shifted[:, 0:wc - LANES]
            wout_ref[0, slot, h, :, wc - LANES:wc] = jnp.where(lane >= new0, new_rows(2 + slot),
                                                               shifted[:, wc - LANES:wc])


def _nsa_sample(q, qr, pages, page_table, cmp, cache_win_t, new_t, gates, steps):
    nb, n_pages = page_table.shape
    wc = cache_win_t.shape[-1]
    n_sel = -(-(n_pages * PAGE_SIZE + steps) // SEL_BLOCK)
    ov = jnp.asarray(_overlap_matrix(n_sel), BF16)
    rows = GQA * steps
    lead = lambda a: pl.BlockSpec((1,) + a.shape[1:], lambda n, pt: (n,) + (0,) * (a.ndim - 1))
    page_spec = lambda k: pl.BlockSpec((1, 2, N_KV_HEADS, HEAD_DIM, PAGE_SIZE),
                                       lambda n, pt: (pt[n, k], 1, 0, 0, 0))
    grid_spec = pltpu.PrefetchScalarGridSpec(
        num_scalar_prefetch=1, grid=(nb,),
        in_specs=[lead(q), lead(qr)] + [page_spec(k) for k in range(n_pages)]
        + [lead(cmp), lead(cache_win_t), lead(new_t), lead(gates), pl.BlockSpec(ov.shape, lambda n, pt: (0, 0))],
        out_specs=(pl.BlockSpec((1, N_KV_HEADS, rows, HEAD_DIM), lambda n, pt: (n, 0, 0, 0)), lead(cache_win_t)))
    return pl.pallas_call(
        functools.partial(_nsa_sample_kernel, n_pages=n_pages, steps=steps), grid_spec=grid_spec,
        out_shape=(jax.ShapeDtypeStruct((nb, N_KV_HEADS, rows, HEAD_DIM), F32),
                   jax.ShapeDtypeStruct(cache_win_t.shape, F32)),
        compiler_params=_params("parallel"), name="nsa_sample",
    )(page_table, q, qr, *([pages] * n_pages), cmp, cache_win_t, new_t, gates, ov)


def _mlp_kernel(x_ref, oa_ref, yr_ref, ga_ref, wo_ref, nmlp_ref, wup_ref, wdn_ref, nfin_ref, y_ref, *, ff_chunk):
    a = _rms(oa_ref[...], ga_ref[...]).astype(BF16)
    x1 = x_ref[...] + (_nn(a, wo_ref[0:ATTN_WIDTH, :]) + _nn(yr_ref[...], wo_ref[ATTN_WIDTH:, :]))
    v = _rms(x1, nmlp_ref[...]).astype(BF16)
    acc = jnp.zeros_like(x1)
    for c in range(D_FF // ff_chunk):
        sl = slice(c * ff_chunk, (c + 1) * ff_chunk)
        hid = jnp.square(jnp.maximum(_nn(v, wup_ref[:, sl]), 0.0)).astype(BF16)
        acc = acc + _nn(hid, wdn_ref[sl, :])
    y_ref[...] = _rms(x1 + acc, nfin_ref[...])


def _mlp(x, o_attn, y_rnn, g_attn, w_out, norm_mlp, w_up, w_down, norm_final, tm, ff_chunk):
    rows = x.shape[0]
    row = lambda w: pl.BlockSpec((tm, w), lambda i: (i, 0))
    full = lambda a: pl.BlockSpec(a.shape, lambda i: (0,) * a.ndim)
    once = lambda a: pl.BlockSpec(a.shape, lambda i: (0,) * a.ndim, pipeline_mode=pl.Buffered(1))
    return pl.pallas_call(
        functools.partial(_mlp_kernel, ff_chunk=ff_chunk), grid=(rows // tm,),
        in_specs=[row(D_MODEL), row(ATTN_WIDTH), row(D_RNN), full(g_attn), once(w_out), full(norm_mlp),
                  once(w_up), once(w_down), full(norm_final)],
        out_specs=row(D_MODEL), out_shape=jax.ShapeDtypeStruct((rows, D_MODEL), F32),
        compiler_params=_params("parallel"), name="mlp",
    )(x, o_attn, y_rnn, g_attn, w_out, norm_mlp, w_up, w_down, norm_final)


def _block_diag(w):
    nb, bs, _ = w.shape
    eye = jnp.eye(nb, dtype=w.dtype)
    return (eye[:, None, :, None] * w[:, :, None, :]).reshape(nb * bs, nb * bs)


def _cmp_rows16(x, lead):
    t = x.shape[1]
    x = x.reshape(lead, t // CMP_STRIDE, CMP_STRIDE, 2 * N_KV_HEADS, HEAD_DIM)
    return x.transpose(0, 3, 1, 2, 4).reshape(lead, 2 * N_KV_HEADS, t // CMP_STRIDE, CMP_STRIDE * HEAD_DIM)


def kernel(x_prompt, x_sample, cache_kv, cache_win, state_conv, state_rnn, page_table, w_in, pe_ck, w_ck1, w_ck2,
           pe_cv, w_cv1, w_cv2, g_attn, g_rnn, conv_w, conv_b, w_ra, b_ra, w_ri, b_ri, lam, w_out, norm_mix,
           norm_mlp, w_up, w_down, norm_final):
    assert w_in.shape[0] == 1, "single layer"
    nbp, tp, _ = x_prompt.shape
    nbs, steps, _ = x_sample.shape
    n_pages = page_table.shape[1]

    wt = jnp.transpose(w_in[0]).astype(BF16)
    c_kv, c_g = ATTN_WIDTH, ATTN_WIDTH + 6 * KV_COLS
    c_rg = c_g + 3 * N_HEADS
    wq, wkv = wt[0:c_kv], wt[c_kv:c_g]
    wg = wt[c_g:c_rg].reshape(3, N_KV_HEADS, GQA, D_MODEL).transpose(1, 0, 2, 3).reshape(N_KV_HEADS, 3 * GQA, D_MODEL)
    wg = jnp.pad(wg, ((0, 0), (0, GATE_LANES - 3 * GQA), (0, 0))).reshape(N_KV_HEADS * GATE_LANES, D_MODEL)
    wr = jnp.concatenate([wt[c_rg:c_rg + 2 * D_RNN], wg], axis=0)
    row2 = lambda a: a.reshape(1, -1)
    rnn_w = (conv_w[0], row2(conv_b[0]), _block_diag(w_ra[0]).astype(BF16), _block_diag(w_ri[0]).astype(BF16),
             row2(b_ra[0]), row2(b_ri[0]), row2(lam[0]), row2(g_rnn[0]))
    half = CMP_BLOCK // 2
    cmp_pe = jnp.stack([pe_ck[0], pe_cv[0]]).reshape(2, 2, 1, half * HEAD_DIM)
    cmp_w1 = jnp.stack([w_ck1[0], w_cv1[0]]).reshape(2, 2, half * HEAD_DIM, HEAD_DIM).astype(BF16)
    cmp_w2 = jnp.stack([w_ck2[0], w_cv2[0]]).astype(BF16)
    mlp_w = (row2(g_attn[0]), w_out[0].astype(BF16), row2(norm_mlp[0]), w_up[0].astype(BF16),
             w_down[0].astype(BF16), row2(norm_final))
    nm = row2(norm_mix[0])

    tm = 512
    xp = x_prompt.reshape(nbp * tp, D_MODEL)
    q, qr, kvt, wint, kvb, xc, rg, rx, gates = _proj(xp, jnp.arange(tp), nm, wq, wkv, wr, nbp, tp, tm)
    y_rnn, h_last = _rglru_prompt(rx, rg, rnn_w, nbp, tp, 256)
    seq_pages = jnp.arange(nbp, dtype=jnp.int32)[:, None]
    cmp = _compress(_cmp_rows16(xc.reshape(nbp, tp, 2 * N_KV_HEADS, HEAD_DIM), nbp), seq_pages,
                    cmp_pe, cmp_w1, cmp_w2, nbp)
    o_attn = _nsa_prompt(q, qr, kvb, cmp, gates, nbp, tp, 256, 256)
    y_prompt = _mlp(xp, o_attn, y_rnn, *mlp_w, tm, 1024).reshape(nbp, tp, D_MODEL)
    wlen = min(WINDOW, tp)
    kv_prompt = kvt.reshape(nbp, 4, N_KV_HEADS, HEAD_DIM, tp).transpose(0, 4, 1, 2, 3)[None]
    win_prompt = wint[:, :, tp - wlen:].reshape(nbp, 2, N_KV_HEADS, HEAD_DIM, wlen).transpose(0, 4, 1, 2, 3)[None]
    conv_prompt = rx.reshape(nbp, tp, D_RNN)[:, tp - (CONV_WIDTH - 1):][None]
    h_prompt = h_last.reshape(1, nbp, D_RNN)

    rows_s = nbs * steps
    xs = x_sample.transpose(1, 0, 2).reshape(rows_s, D_MODEL)
    pos_s = PAST_LEN + jnp.arange(rows_s) // nbs
    q, qr, kvt, wint, _, _, rg, rx, gates = _proj(xs, pos_s, nm, wq, wkv, wr, 1, rows_s, rows_s)
    y_rnn, conv_s, h_s = _rglru_sample(rx, rg, state_conv[0].transpose(1, 0, 2), state_rnn[0], rnn_w)
    pool = cache_kv.shape[1]
    pages_t = cache_kv[0].transpose(0, 2, 3, 4, 1)
    pages16 = _cmp_rows16(cache_kv[0][:, :, 0:2].reshape(pool, PAGE_SIZE, 2 * N_KV_HEADS, HEAD_DIM), pool)
    cmp = _compress(pages16, page_table, cmp_pe, cmp_w1, cmp_w2, nbs)
    by_seq = lambda a: a.reshape(N_KV_HEADS, GQA, steps, nbs, HEAD_DIM).transpose(3, 0, 1, 2, 4).reshape(
        nbs, N_KV_HEADS, GQA * steps, HEAD_DIM)
    new_t = jnp.concatenate([kvt[0, 2 * KV_COLS:], wint[0]], axis=0).reshape(4 * KV_COLS, steps, nbs)
    new_t = jnp.pad(new_t.transpose(2, 0, 1), ((0, 0), (0, 0), (LANES - steps, 0)))
    gates_s = gates.reshape(steps, nbs, N_KV_HEADS, GATE_LANES)[..., :3 * GQA].reshape(steps, nbs, N_KV_HEADS, 3, GQA)
    gates_s = gates_s.transpose(1, 2, 4, 0, 3).reshape(nbs, N_KV_HEADS, GQA * steps, 3)
    cache_win_t = cache_win[0].transpose(0, 2, 3, 4, 1)
    o_s, win_t = _nsa_sample(by_seq(q[0]), by_seq(qr[0]), pages_t, page_table, cmp, cache_win_t, new_t, gates_s, steps)
    o_attn = o_s.reshape(nbs, N_KV_HEADS, GQA, steps, HEAD_DIM).transpose(3, 0, 1, 2, 4).reshape(rows_s, ATTN_WIDTH)
    y_sample = _mlp(xs, o_attn, y_rnn, *mlp_w, rows_s, 1024).reshape(steps, nbs, D_MODEL).transpose(1, 0, 2)
    kv_sample = kvt[0].reshape(4, N_KV_HEADS, HEAD_DIM, steps, nbs).transpose(4, 3, 0, 1, 2)[None]
    win_sample = win_t.transpose(0, 4, 1, 2, 3)[None]
    conv_sample = conv_s.transpose(1, 0, 2)[None]
    h_sample = h_s[None]

    return (y_prompt, y_sample, kv_prompt, kv_sample, win_prompt, win_sample, conv_prompt, conv_sample,
            h_prompt, h_sample)
```

```python
import functools

import jax
import jax.numpy as jnp
import numpy as np
from jax import lax
from jax.experimental import pallas as pl
from jax.experimental.pallas import tpu as pltpu

D_MODEL = 1024
PAST_LEN = 2048
PAGE_SIZE = 128
HEAD_DIM = 64
N_HEADS = 8
N_KV_HEADS = 2
GQA = N_HEADS // N_KV_HEADS
ATTN_WIDTH = N_HEADS * HEAD_DIM
D_RNN = D_MODEL - ATTN_WIDTH
RNN_BLOCKS = 8
CONV_WIDTH = 4
LRU_C = 8.0
D_FF = 4 * D_MODEL
ROT_DIM = HEAD_DIM // 4
ROT_HALF = ROT_DIM // 2
ROPE_THETA = 500000.0
CMP_BLOCK = 32
CMP_STRIDE = 16
SEL_BLOCK = 64
TOP_N = 16
WINDOW = 512
KV_COLS = N_KV_HEADS * HEAD_DIM
EPS = 1e-6
NEG = -1e30
SEL_BONUS = 1e4
SCALE = HEAD_DIM ** -0.5

SUBLANES = 8
LANES = 128
VMEM_LIMIT = 48 * 1024 * 1024

N_CMP_PAD = 128
GATE_LANES = LANES

F32 = jnp.float32
BF16 = jnp.bfloat16
NT_DIMS = (((1,), (1,)), ((), ()))
TN_DIMS = (((0,), (0,)), ((), ()))


def _nt(a, b):
    return lax.dot_general(a, b, NT_DIMS, preferred_element_type=F32)


def _nn(a, b):
    return jnp.dot(a, b, preferred_element_type=F32)


def _rms(x, g):
    return x * lax.rsqrt(jnp.mean(x * x, axis=-1, keepdims=True) + EPS) * g


def _params(*sem):
    return pltpu.CompilerParams(dimension_semantics=sem, vmem_limit_bytes=VMEM_LIMIT)


def _proj_kernel(x_ref, nm_ref, wq_ref, wkv_ref, wr_ref, cosr_ref, sina_ref, sinb_ref, cost_ref, sint_ref,
                 q_ref, qr_ref, kvt_ref, wint_ref, kb_ref, vb_ref, xc_ref, rg_ref, rx_ref, gates_ref):
    u = _rms(x_ref[...], nm_ref[...]).astype(BF16)
    q = _nt(u, wq_ref[...])
    cosr, sina, sinb = cosr_ref[...], sina_ref[...], sinb_ref[...]
    for k in range(ATTN_WIDTH // LANES):
        qg = q[:, k * LANES:(k + 1) * LANES]
        qrg = (qg * cosr + pltpu.roll(qg, LANES - ROT_HALF, 1) * sina
               + pltpu.roll(qg, ROT_HALF, 1) * sinb)
        for hh in range(LANES // HEAD_DIM):
            h = k * (LANES // HEAD_DIM) + hh
            sl = slice(hh * HEAD_DIM, (hh + 1) * HEAD_DIM)
            q_ref[0, h] = (qg[:, sl] * SCALE).astype(BF16)
            qr_ref[0, h] = (qrg[:, sl] * SCALE).astype(BF16)

    kvt = _nt(wkv_ref[...], u)
    cost, sint = cost_ref[...], sint_ref[...]
    kvt_ref[0] = kvt[0:4 * KV_COLS]
    wint_ref[0] = kvt[4 * KV_COLS:6 * KV_COLS]
    for out_ref, src0, dst0 in ((kvt_ref, 2 * KV_COLS, 2 * KV_COLS), (wint_ref, 4 * KV_COLS, 0)):
        for h in range(N_KV_HEADS):
            s, d = src0 + h * HEAD_DIM, dst0 + h * HEAD_DIM
            x1, x2 = kvt[s:s + ROT_HALF], kvt[s + ROT_HALF:s + ROT_DIM]
            out_ref[0, d:d + ROT_HALF] = x1 * cost - x2 * sint
            out_ref[0, d + ROT_HALF:d + ROT_DIM] = x2 * cost + x1 * sint
    tm = x_ref.shape[0]
    ones_row = (lax.broadcasted_iota(jnp.int32, (HEAD_DIM, tm), 0) == 0).astype(BF16)
    for branch, src_ref, k0, v0 in ((0, kvt_ref, 2 * KV_COLS, 3 * KV_COLS), (1, wint_ref, 0, KV_COLS)):
        for h in range(N_KV_HEADS):
            i, r = branch * N_KV_HEADS + h, h * HEAD_DIM
            kb_ref[0, i] = src_ref[0, k0 + r:k0 + r + HEAD_DIM].astype(BF16)
            vb_ref[0, i, 0:HEAD_DIM] = src_ref[0, v0 + r:v0 + r + HEAD_DIM].astype(BF16)
            vb_ref[0, i, HEAD_DIM:2 * HEAD_DIM] = ones_row

    xc_ref[...] = _nt(u, wkv_ref[0:2 * KV_COLS, :])
    r = _nt(u, wr_ref[...])
    rg_ref[...] = r[:, 0:D_RNN]
    rx_ref[...] = r[:, D_RNN:2 * D_RNN]
    gates_ref[...] = jax.nn.sigmoid(r[:, 2 * D_RNN:])


def _rope_tables(pos):
    inv = ROPE_THETA ** (-jnp.arange(ROT_HALF, dtype=F32) / ROT_HALF)
    ang = pos.astype(F32)[:, None] * inv
    cos, sin = jnp.cos(ang), jnp.sin(ang)
    d = np.arange(LANES) % HEAD_DIM
    f = d % ROT_HALF
    cosr = jnp.where(d < ROT_DIM, cos[:, f], 1.0)
    sina = jnp.where(d < ROT_HALF, -sin[:, f], 0.0)
    sinb = jnp.where((d >= ROT_HALF) & (d < ROT_DIM), sin[:, f], 0.0)
    return cosr, sina, sinb, cos.T, sin.T


def _proj(x, pos, nm, wq, wkv, wr, nb, tt, tm):
    rows = nb * tt
    nt = tt // tm
    cosr, sina, sinb, cost, sint = _rope_tables(pos)
    row = lambda w: pl.BlockSpec((tm, w), lambda i: (i, 0))
    full = lambda a: pl.BlockSpec(a.shape, lambda i: (0,) * a.ndim)
    tab_r = pl.BlockSpec((tm, LANES), lambda i: (i % nt, 0))
    tab_t = pl.BlockSpec((ROT_HALF, tm), lambda i: (0, i % nt))
    heads = pl.BlockSpec((1, N_HEADS, tm, HEAD_DIM), lambda i: (i // nt, 0, i % nt, 0))
    tr = lambda r: pl.BlockSpec((1, r, tm), lambda i: (i // nt, 0, i % nt))
    out_shape = (
        jax.ShapeDtypeStruct((nb, N_HEADS, tt, HEAD_DIM), BF16),
        jax.ShapeDtypeStruct((nb, N_HEADS, tt, HEAD_DIM), BF16),
        jax.ShapeDtypeStruct((nb, 4 * KV_COLS, tt), F32),
        jax.ShapeDtypeStruct((nb, 2 * KV_COLS, tt), F32),
        jax.ShapeDtypeStruct((nb, 2 * N_KV_HEADS, HEAD_DIM, tt), BF16),
        jax.ShapeDtypeStruct((nb, 2 * N_KV_HEADS, 2 * HEAD_DIM, tt), BF16),
        jax.ShapeDtypeStruct((rows, 2 * KV_COLS), F32),
        jax.ShapeDtypeStruct((rows, D_RNN), F32),
        jax.ShapeDtypeStruct((rows, D_RNN), F32),
        jax.ShapeDtypeStruct((rows, N_KV_HEADS * GATE_LANES), F32),
    )
    tr4 = lambda r: pl.BlockSpec((1, 2 * N_KV_HEADS, r, tm), lambda i: (i // nt, 0, 0, i % nt))
    out_specs = (heads, heads, tr(4 * KV_COLS), tr(2 * KV_COLS), tr4(HEAD_DIM), tr4(2 * HEAD_DIM),
                 row(2 * KV_COLS), row(D_RNN), row(D_RNN), row(N_KV_HEADS * GATE_LANES))
    return pl.pallas_call(
        _proj_kernel, grid=(rows // tm,),
        in_specs=[row(D_MODEL), full(nm), full(wq), full(wkv), full(wr), tab_r, tab_r, tab_r, tab_t, tab_t],
        out_specs=out_specs, out_shape=out_shape, compiler_params=_params("parallel"), name="proj",
    )(x, nm, wq, wkv, wr, cosr, sina, sinb, cost, sint)


def _expm1(x):
    return jnp.tanh(0.5 * x) * (jnp.exp(x) + 1.0)


def _softplus(x):
    return jnp.maximum(x, 0.0) + jnp.log1p(jnp.exp(-jnp.abs(x)))


def _lru_coeffs(xc, wra_ref, wri_ref, bra_ref, bri_ref, lam_ref):
    xb = xc.astype(BF16)
    r = jax.nn.sigmoid(_nn(xb, wra_ref[...]) + bra_ref[...])
    i = jax.nn.sigmoid(_nn(xb, wri_ref[...]) + bri_ref[...])
    log_a = -LRU_C * r * _softplus(-lam_ref[...])
    a = jnp.exp(log_a)
    b = jnp.sqrt(-_expm1(2.0 * log_a)) * (i * xc)
    return a, b


def _rnn_out(rg, h, g):
    return _rms(jax.nn.gelu(rg) * h, g).astype(BF16)


def _rglru_prompt_kernel(rx_ref, rg_ref, cw_ref, cb_ref, wra_ref, wri_ref, bra_ref, bri_ref, lam_ref, g_ref,
                         y_ref, hl_ref, prev_sc, h_sc):
    tt = rx_ref.shape[0]

    @pl.when(pl.program_id(1) == 0)
    def _():
        prev_sc[...] = jnp.zeros_like(prev_sc)
        h_sc[...] = jnp.zeros_like(h_sc)

    rx = rx_ref[...]
    ext = jnp.concatenate([prev_sc[...], rx], axis=0)
    shifted = lambda d: pltpu.roll(ext, d, 0)[SUBLANES:SUBLANES + tt]
    cw = cw_ref[...]
    xc = cb_ref[...] + cw[0:1] * shifted(3)
    xc = xc + cw[1:2] * shifted(2)
    xc = xc + cw[2:3] * shifted(1)
    xc = xc + cw[3:4] * rx
    prev_sc[...] = rx[tt - SUBLANES:tt]

    a, b = _lru_coeffs(xc, wra_ref, wri_ref, bra_ref, bri_ref, lam_ref)
    row = lax.broadcasted_iota(jnp.int32, (tt, 1), 0)
    s = 1
    while s < tt:
        if s < SUBLANES:
            a_sh = jnp.where(row >= s, pltpu.roll(a, s, 0), 1.0)
            b_sh = jnp.where(row >= s, pltpu.roll(b, s, 0), 0.0)
        else:
            a_sh = jnp.concatenate([jnp.ones((s, a.shape[1]), F32), a[:tt - s]], axis=0)
            b_sh = jnp.concatenate([jnp.zeros((s, b.shape[1]), F32), b[:tt - s]], axis=0)
        b = a * b_sh + b
        a = a * a_sh
        s *= 2
    h = a * h_sc[...] + b
    h_sc[...] = h[tt - 1:tt]
    hl_ref[0] = h[tt - 1:tt]
    y_ref[...] = _rnn_out(rg_ref[...], h, g_ref[...])


def _rglru_sample_kernel(rx_ref, rg_ref, cp_ref, h0_ref, cw_ref, cb_ref, wra_ref, wri_ref, bra_ref, bri_ref,
                         lam_ref, g_ref, y_ref, cs_ref, hl_ref):
    nb = h0_ref.shape[0]
    steps = rx_ref.shape[0] // nb
    xp = [cp_ref[k] for k in range(CONV_WIDTH - 1)] + [rx_ref[t * nb:(t + 1) * nb] for t in range(steps)]
    cw, cb = cw_ref[...], cb_ref[...]
    xcs = []
    for t in range(steps):
        xc = cb + cw[0:1] * xp[t]
        for tap in range(1, CONV_WIDTH):
            xc = xc + cw[tap:tap + 1] * xp[t + tap]
        xcs.append(xc)
    a, b = _lru_coeffs(jnp.concatenate(xcs, axis=0), wra_ref, wri_ref, bra_ref, bri_ref, lam_ref)
    h = h0_ref[...]
    for t in range(steps):
        sl = slice(t * nb, (t + 1) * nb)
        h = a[sl] * h + b[sl]
        y_ref[sl] = _rnn_out(rg_ref[sl], h, g_ref[...])
    for k in range(CONV_WIDTH - 1):
        cs_ref[k] = xp[steps + k]
    hl_ref[...] = h


def _rglru_prompt(rx, rg, weights, nb, tt, chunk):
    nc = tt // chunk
    row = pl.BlockSpec((chunk, D_RNN), lambda n, c: (n * nc + c, 0))
    full = lambda a: pl.BlockSpec(a.shape, lambda n, c: (0,) * a.ndim)
    return pl.pallas_call(
        _rglru_prompt_kernel, grid=(nb, nc),
        in_specs=[row, row] + [full(w) for w in weights],
        out_specs=(row, pl.BlockSpec((1, 1, D_RNN), lambda n, c: (n, 0, 0))),
        out_shape=(jax.ShapeDtypeStruct((nb * tt, D_RNN), BF16), jax.ShapeDtypeStruct((nb, 1, D_RNN), F32)),
        scratch_shapes=[pltpu.VMEM((SUBLANES, D_RNN), F32), pltpu.VMEM((1, D_RNN), F32)],
        compiler_params=_params("parallel", "arbitrary"), name="rglru_prompt",
    )(rx, rg, *weights)


def _rglru_sample(rx, rg, conv_prev, h0, weights):
    nb = h0.shape[0]
    return pl.pallas_call(
        _rglru_sample_kernel,
        out_shape=(jax.ShapeDtypeStruct(rx.shape, BF16), jax.ShapeDtypeStruct(conv_prev.shape, F32),
                   jax.ShapeDtypeStruct((nb, D_RNN), F32)),
        compiler_params=pltpu.CompilerParams(vmem_limit_bytes=VMEM_LIMIT), name="rglru_sample",
    )(rx, rg, conv_prev, h0, *weights)


def _compress_kernel(pt_ref, *refs, n_pages):
    del pt_ref
    page_refs = refs[:n_pages]
    pe_ref, w1_ref, w2_ref, out_ref = refs[n_pages:]
    for cs in range(2 * N_KV_HEADS):
        slot = cs // N_KV_HEADS
        x = jnp.concatenate([r[0, cs] for r in page_refs], axis=0)
        p0 = _nn((x + pe_ref[slot, 0]).astype(BF16), w1_ref[slot, 0])
        p1 = _nn((x + pe_ref[slot, 1]).astype(BF16), w1_ref[slot, 1])
        hid = jax.nn.gelu(p0 + pltpu.roll(p1, N_CMP_PAD - 1, 0))
        out_ref[0, cs] = _nn(hid.astype(BF16), w2_ref[slot]).astype(BF16)


def _compress(pages, page_table, pe, w1, w2, nb):
    n_pages = page_table.shape[1]
    rows = pages.shape[2]
    assert n_pages * rows == N_CMP_PAD
    page_spec = lambda k: pl.BlockSpec((1, 2 * N_KV_HEADS, rows, pages.shape[3]),
                                       lambda n, pt: (pt[n, k], 0, 0, 0))
    full = lambda a: pl.BlockSpec(a.shape, lambda n, pt: (0,) * a.ndim)
    grid_spec = pltpu.PrefetchScalarGridSpec(
        num_scalar_prefetch=1, grid=(nb,),
        in_specs=[page_spec(k) for k in range(n_pages)] + [full(pe), full(w1), full(w2)],
        out_specs=pl.BlockSpec((1, 2 * N_KV_HEADS, N_CMP_PAD, HEAD_DIM), lambda n, pt: (n, 0, 0, 0)))
    return pl.pallas_call(
        functools.partial(_compress_kernel, n_pages=n_pages), grid_spec=grid_spec,
        out_shape=jax.ShapeDtypeStruct((nb, 2 * N_KV_HEADS, N_CMP_PAD, HEAD_DIM), BF16),
        compiler_params=_params("parallel"), name="compress",
    )(page_table, *([pages] * n_pages), pe, w1, w2)


def _overlap_matrix(n_sel):
    n_cmp = N_CMP_PAD - 1
    c0 = np.arange(N_CMP_PAD)[:, None] * CMP_STRIDE
    j0 = np.arange(LANES)[None, :] * SEL_BLOCK
    ov = (c0 < j0 + SEL_BLOCK) & (c0 + CMP_BLOCK > j0)
    ov &= (np.arange(N_CMP_PAD)[:, None] < n_cmp) & (np.arange(LANES)[None, :] < n_sel)
    return ov.astype(np.float32)


def _topk_member(score, idx, n, axis):
    rank = jnp.zeros(score.shape, jnp.int32)
    for i in range(n):
        si = score[i:i + 1, :] if axis == 0 else score[:, i:i + 1]
        beats = (si > score) | ((si == score) & (i < idx))
        rank = rank + beats.astype(jnp.int32)
    return rank < TOP_N


def _sel_score(imp, j, cur):
    valid = j <= cur
    forced = (j == 0) | (j == cur) | (j == cur - 1)
    return jnp.where(valid, imp, -SEL_BONUS) + jnp.where(forced, SEL_BONUS, 0.0)


def _nsa_prompt_kernel(q_ref, qr_ref, ks_ref, vs_ref, kw_ref, vw_ref, kc_ref, vc_ref, gates_ref, ovt_ref, e_ref,
                       o_ref, m_sc, acc_sc, *, tq, slc_chunk, row_block, n_sel):
    q0 = pl.program_id(2) * tq
    q4 = q_ref[0].reshape(GQA * tq, HEAD_DIM)
    qpos = q0 + lax.broadcasted_iota(jnp.int32, (tq, 1), 0)

    cidx = lax.broadcasted_iota(jnp.int32, (1, N_CMP_PAD), 1)
    real = cidx < N_CMP_PAD - 1
    cmask = (cidx * CMP_STRIDE + CMP_BLOCK - 1 <= qpos) & real
    s = _nt(q4, kc_ref[0, 0]).reshape(GQA, tq, N_CMP_PAD)
    s = jnp.where(real, jnp.where(cmask, s, NEG), -jnp.inf)
    e = jnp.exp(s - jnp.max(s, axis=-1, keepdims=True))
    p = (e / jnp.sum(e, axis=-1, keepdims=True)) * cmask.astype(F32)
    pb = p.astype(BF16)
    o_cmp = _nn(pb.reshape(GQA * tq, N_CMP_PAD), vc_ref[0, 0])

    imp_t = _nt(ovt_ref[...], pb[0])
    for g in range(1, GQA):
        imp_t = imp_t + _nt(ovt_ref[...], pb[g])
    qpos_t = q0 + lax.broadcasted_iota(jnp.int32, (1, tq), 1)
    j_t = lax.broadcasted_iota(jnp.int32, (n_sel, 1), 0)
    score_t = _sel_score(imp_t, j_t, qpos_t // SEL_BLOCK)
    sel_t = _topk_member(score_t, j_t, n_sel, axis=0).astype(BF16)
    eye = (j_t == lax.broadcasted_iota(jnp.int32, (1, n_sel), 1)).astype(BF16)
    sel = lax.dot_general(sel_t, eye, TN_DIMS, preferred_element_type=F32).astype(BF16)

    qr4 = qr_ref[0].reshape(GQA * tq, HEAD_DIM)

    def attend(k, v, allowed, first):
        n = k.shape[1]
        sc = _nn(qr4, k)
        bias = jnp.where(allowed, 0.0, NEG)
        pes, ms, alphas = [], [], []
        for r0 in range(0, GQA * tq, row_block):
            t0 = r0 % tq
            x = sc[r0:r0 + row_block] + bias[t0:t0 + row_block]
            m = jnp.max(x, axis=-1, keepdims=True)
            if first:
                ms.append(jnp.broadcast_to(m, (row_block, LANES)))
            else:
                m_prev = m_sc[r0:r0 + row_block]
                m_wide = jnp.maximum(m_prev, m)
                alphas.append(jnp.exp(m_prev - m_wide))
                ms.append(m_wide)
                m = jnp.concatenate([m_wide] * (n // LANES), axis=1)
            pes.append(jnp.exp(x - m).astype(BF16))
        pv = _nt(jnp.concatenate(pes, axis=0), v)
        m_sc[...] = jnp.concatenate(ms, axis=0)
        acc_sc[...] = pv if first else jnp.concatenate(alphas, axis=0) * acc_sc[...] + pv

    def normalised():
        acc = acc_sc[...].reshape(GQA, tq, 2 * HEAD_DIM)
        return acc[:, :, 0:HEAD_DIM] / acc[:, :, HEAD_DIM:HEAD_DIM + 1]

    wn = WINDOW + tq
    w0 = pl.multiple_of(jnp.maximum(q0 - WINDOW, 0), LANES)
    kpos = w0 + lax.broadcasted_iota(jnp.int32, (1, wn), 1)
    attend(kw_ref[0, 0, :, pl.ds(w0, wn)], vw_ref[0, 0, :, pl.ds(w0, wn)],
           (kpos <= qpos) & (kpos > qpos - WINDOW), True)
    o_win = normalised()

    for c0 in range(0, ks_ref.shape[3], slc_chunk):
        def chunk(c0=c0):
            kpos = c0 + lax.broadcasted_iota(jnp.int32, (1, slc_chunk), 1)
            picked = _nn(sel, e_ref[:, c0:c0 + slc_chunk])
            attend(ks_ref[0, 0, :, c0:c0 + slc_chunk], vs_ref[0, 0, :, c0:c0 + slc_chunk],
                   (picked > 0.5) & (kpos <= qpos), c0 == 0)

        if c0 == 0:
            chunk()
        else:
            pl.when(q0 + tq > c0)(chunk)
    o_slc = normalised()

    gt = gates_ref[...]
    o_cmp = o_cmp.reshape(GQA, tq, HEAD_DIM)
    for g in range(GQA):
        o = gt[:, g:g + 1] * o_cmp[g] + gt[:, GQA + g:GQA + g + 1] * o_slc[g]
        o = o + gt[:, 2 * GQA + g:2 * GQA + g + 1] * o_win[g]
        o_ref[:, g * HEAD_DIM:(g + 1) * HEAD_DIM] = o


def _nsa_prompt(q, qr, kb, vb, cmp, gates, nb, tt, tq, slc_chunk):
    n_sel = -(-tt // SEL_BLOCK)
    nq = tt // tq
    ovt = jnp.asarray(_overlap_matrix(n_sel)[:, :n_sel].T, BF16)
    expand = jnp.asarray(np.arange(tt)[None, :] // SEL_BLOCK == np.arange(n_sel)[:, None], BF16)
    qspec = pl.BlockSpec((1, GQA, tq, HEAD_DIM), lambda n, h, i: (n, h, i, 0))
    kvspec = lambda a, branch: pl.BlockSpec((1, 1, a.shape[2], tt),
                                            lambda n, h, i: (n, branch * N_KV_HEADS + h, 0, 0))
    cspec = lambda slot: pl.BlockSpec((1, 1, N_CMP_PAD, HEAD_DIM), lambda n, h, i: (n, slot * N_KV_HEADS + h, 0, 0))
    full = lambda a: pl.BlockSpec(a.shape, lambda n, h, i: (0,) * a.ndim)
    return pl.pallas_call(
        functools.partial(_nsa_prompt_kernel, tq=tq, slc_chunk=slc_chunk, row_block=32, n_sel=n_sel),
        grid=(nb, N_KV_HEADS, nq),
        in_specs=[qspec, qspec, kvspec(kb, 0), kvspec(vb, 0), kvspec(kb, 1), kvspec(vb, 1), cspec(0), cspec(1),
                  pl.BlockSpec((tq, GATE_LANES), lambda n, h, i: (n * nq + i, h)), full(ovt), full(expand)],
        out_specs=pl.BlockSpec((tq, GQA * HEAD_DIM), lambda n, h, i: (n * nq + i, h)),
        out_shape=jax.ShapeDtypeStruct((nb * tt, ATTN_WIDTH), F32),
        scratch_shapes=[pltpu.VMEM((GQA * tq, LANES), F32), pltpu.VMEM((GQA * tq, 2 * HEAD_DIM), F32)],
        compiler_params=_params("parallel", "parallel", "arbitrary"), name="nsa_prompt",
    )(q, qr, kb, vb, kb, vb, cmp, cmp, gates, ovt, expand)


def _nsa_sample_kernel(pt_ref, *refs, n_pages, steps):
    del pt_ref
    q_ref, qr_ref = refs[0:2]
    page_refs = refs[2:2 + n_pages]
    cmp_ref, cw_ref, new_ref, gates_ref, ov_ref, o_ref, wout_ref = refs[2 + n_pages:]
    rows = GQA * steps
    past = n_pages * PAGE_SIZE
    wc = cw_ref.shape[-1]
    new0 = LANES - steps
    trow = lax.broadcasted_iota(jnp.int32, (rows, 1), 0) % steps
    lane = lax.broadcasted_iota(jnp.int32, (1, LANES), 1)
    new_ok = (lane >= new0) & (lane - new0 <= trow)
    n_sel = -(-(past + steps) // SEL_BLOCK)
    cur = (past + trow) // SEL_BLOCK

    for h in range(N_KV_HEADS):
        q, qr = q_ref[0, h], qr_ref[0, h]
        row0 = h * HEAD_DIM
        new_rows = lambda slot: new_ref[0, slot * KV_COLS + row0:slot * KV_COLS + row0 + HEAD_DIM, :]

        real = lane < N_CMP_PAD - 1
        s = jnp.where(real, _nt(q, cmp_ref[0, h]), -jnp.inf)
        e = jnp.exp(s - jnp.max(s, axis=-1, keepdims=True))
        pb = (e / jnp.sum(e, axis=-1, keepdims=True)).astype(BF16)
        o_cmp = _nn(pb, cmp_ref[0, N_KV_HEADS + h])

        part = _nn(pb, ov_ref[...])
        imp = part
        for g in range(1, GQA):
            imp = imp + pltpu.roll(part, g * steps, 0)
        score = jnp.where(lane < n_sel, _sel_score(imp, lane, cur), -jnp.inf)
        sel = _topk_member(score, lane, n_sel, axis=1) & (lane < n_sel)
        sel_f = sel.astype(F32)

        ks = jnp.concatenate([page_refs[k][0, 0, h].astype(BF16) for k in range(n_pages)]
                             + [new_rows(0).astype(BF16)], axis=1)
        vs = jnp.concatenate([page_refs[k][0, 1, h].astype(BF16) for k in range(n_pages)]
                             + [new_rows(1).astype(BF16)], axis=1)
        per_tile = PAGE_SIZE // SEL_BLOCK
        tiles = []
        for k in range(n_pages):
            m = sel_f[:, per_tile * k:per_tile * k + 1]
            for b in range(1, per_tile):
                m = jnp.where(lane < b * SEL_BLOCK, m, sel_f[:, per_tile * k + b:per_tile * k + b + 1])
            tiles.append(m > 0.5)
        tiles.append((sel_f[:, n_sel - 1:n_sel] > 0.5) & new_ok)
        msk = jnp.concatenate(tiles, axis=1)
        sc = jnp.where(msk, _nn(qr, ks), NEG)
        pe = jnp.where(msk, jnp.exp(sc - jnp.max(sc, axis=-1, keepdims=True)), 0.0)
        o_slc = _nt(pe.astype(BF16), vs) / jnp.sum(pe, axis=-1, keepdims=True)

        kw = jnp.concatenate([cw_ref[0, 0, h].astype(BF16), new_rows(2).astype(BF16)], axis=1)
        vw = jnp.concatenate([cw_ref[0, 1, h].astype(BF16), new_rows(3).astype(BF16)], axis=1)
        wlane = lax.broadcasted_iota(jnp.int32, (1, wc), 1)
        wmsk = jnp.concatenate([jnp.broadcast_to(wlane + (WINDOW - wc) > trow, (rows, wc)),
                                jnp.broadcast_to(new_ok, (rows, LANES))], axis=1)
        sw = jnp.where(wmsk, _nn(qr, kw), NEG)
        pw = jnp.where(wmsk, jnp.exp(sw - jnp.max(sw, axis=-1, keepdims=True)), 0.0)
        o_win = _nt(pw.astype(BF16), vw) / jnp.sum(pw, axis=-1, keepdims=True)

        gt = gates_ref[0, h]
        o_ref[0, h] = gt[:, 0:1] * o_cmp + gt[:, 1:2] * o_slc + gt[:, 2:3] * o_win

        for slot in range(2):
            shifted = pltpu.roll(cw_ref[0, slot, h], wc - steps, 1)
            wout_ref[0, slot, h, :, 0:wc - LANES] = shifted[:, 0:wc - LANES]
            wout_ref[0, slot, h, :, wc - LANES:wc] = jnp.where(lane >= new0, new_rows(2 + slot),
                                                               shifted[:, wc - LANES:wc])


def _nsa_sample(q, qr, pages, page_table, cmp, cache_win_t, new_t, gates, steps):
    nb, n_pages = page_table.shape
    wc = cache_win_t.shape[-1]
    n_sel = -(-(n_pages * PAGE_SIZE + steps) // SEL_BLOCK)
    ov = jnp.asarray(_overlap_matrix(n_sel), BF16)
    rows = GQA * steps
    lead = lambda a: pl.BlockSpec((1,) + a.shape[1:], lambda n, pt: (n,) + (0,) * (a.ndim - 1))
    page_spec = lambda k: pl.BlockSpec((1, 2, N_KV_HEADS, HEAD_DIM, PAGE_SIZE),
                                       lambda n, pt: (pt[n, k], 1, 0, 0, 0))
    grid_spec = pltpu.PrefetchScalarGridSpec(
        num_scalar_prefetch=1, grid=(nb,),
        in_specs=[lead(q), lead(qr)] + [page_spec(k) for k in range(n_pages)]
        + [lead(cmp), lead(cache_win_t), lead(new_t), lead(gates), pl.BlockSpec(ov.shape, lambda n, pt: (0, 0))],
        out_specs=(pl.BlockSpec((1, N_KV_HEADS, rows, HEAD_DIM), lambda n, pt: (n, 0, 0, 0)), lead(cache_win_t)))
    return pl.pallas_call(
        functools.partial(_nsa_sample_kernel, n_pages=n_pages, steps=steps), grid_spec=grid_spec,
        out_shape=(jax.ShapeDtypeStruct((nb, N_KV_HEADS, rows, HEAD_DIM), F32),
                   jax.ShapeDtypeStruct(cache_win_t.shape, F32)),
        compiler_params=_params("parallel"), name="nsa_sample",
    )(page_table, q, qr, *([pages] * n_pages), cmp, cache_win_t, new_t, gates, ov)


def _mlp_kernel(x_ref, oa_ref, yr_ref, ga_ref, wo_ref, nmlp_ref, wup_ref, wdn_ref, nfin_ref, y_ref, *, ff_chunk):
    a = _rms(oa_ref[...], ga_ref[...]).astype(BF16)
    x1 = x_ref[...] + (_nn(a, wo_ref[0:ATTN_WIDTH, :]) + _nn(yr_ref[...], wo_ref[ATTN_WIDTH:, :]))
    v = _rms(x1, nmlp_ref[...]).astype(BF16)
    acc = jnp.zeros_like(x1)
    for c in range(D_FF // ff_chunk):
        sl = slice(c * ff_chunk, (c + 1) * ff_chunk)
        hid = jnp.square(jnp.maximum(_nn(v, wup_ref[:, sl]), 0.0)).astype(BF16)
        acc = acc + _nn(hid, wdn_ref[sl, :])
    y_ref[...] = _rms(x1 + acc, nfin_ref[...])


def _mlp(x, o_attn, y_rnn, g_attn, w_out, norm_mlp, w_up, w_down, norm_final, tm, ff_chunk):
    rows = x.shape[0]
    row = lambda w: pl.BlockSpec((tm, w), lambda i: (i, 0))
    full = lambda a: pl.BlockSpec(a.shape, lambda i: (0,) * a.ndim)
    once = lambda a: pl.BlockSpec(a.shape, lambda i: (0,) * a.ndim, pipeline_mode=pl.Buffered(1))
    return pl.pallas_call(
        functools.partial(_mlp_kernel, ff_chunk=ff_chunk), grid=(rows // tm,),
        in_specs=[row(D_MODEL), row(ATTN_WIDTH), row(D_RNN), full(g_attn), once(w_out), full(norm_mlp),
                  once(w_up), once(w_down), full(norm_final)],
        out_specs=row(D_MODEL), out_shape=jax.ShapeDtypeStruct((rows, D_MODEL), F32),
        compiler_params=_params("parallel"), name="mlp",
    )(x, o_attn, y_rnn, g_attn, w_out, norm_mlp, w_up, w_down, norm_final)


def _block_diag(w):
    nb, bs, _ = w.shape
    eye = jnp.eye(nb, dtype=w.dtype)
    return (eye[:, None, :, None] * w[:, :, None, :]).reshape(nb * bs, nb * bs)


def _cmp_rows16(x, lead):
    t = x.shape[1]
    x = x.reshape(lead, t // CMP_STRIDE, CMP_STRIDE, 2 * N_KV_HEADS, HEAD_DIM)
    return x.transpose(0, 3, 1, 2, 4).reshape(lead, 2 * N_KV_HEADS, t // CMP_STRIDE, CMP_STRIDE * HEAD_DIM)


def kernel(x_prompt, x_sample, cache_kv, cache_win, state_conv, state_rnn, page_table, w_in, pe_ck, w_ck1, w_ck2,
           pe_cv, w_cv1, w_cv2, g_attn, g_rnn, conv_w, conv_b, w_ra, b_ra, w_ri, b_ri, lam, w_out, norm_mix,
           norm_mlp, w_up, w_down, norm_final):
    assert w_in.shape[0] == 1, "single layer"
    nbp, tp, _ = x_prompt.shape
    nbs, steps, _ = x_sample.shape
    n_pages = page_table.shape[1]

    wt = jnp.transpose(w_in[0]).astype(BF16)
    c_kv, c_g = ATTN_WIDTH, ATTN_WIDTH + 6 * KV_COLS
    c_rg = c_g + 3 * N_HEADS
    wq, wkv = wt[0:c_kv], wt[c_kv:c_g]
    wg = wt[c_g:c_rg].reshape(3, N_KV_HEADS, GQA, D_MODEL).transpose(1, 0, 2, 3).reshape(N_KV_HEADS, 3 * GQA, D_MODEL)
    wg = jnp.pad(wg, ((0, 0), (0, GATE_LANES - 3 * GQA), (0, 0))).reshape(N_KV_HEADS * GATE_LANES, D_MODEL)
    wr = jnp.concatenate([wt[c_rg:c_rg + 2 * D_RNN], wg], axis=0)
    row2 = lambda a: a.reshape(1, -1)
    rnn_w = (conv_w[0], row2(conv_b[0]), _block_diag(w_ra[0]).astype(BF16), _block_diag(w_ri[0]).astype(BF16),
             row2(b_ra[0]), row2(b_ri[0]), row2(lam[0]), row2(g_rnn[0]))
    half = CMP_BLOCK // 2
    cmp_pe = jnp.stack([pe_ck[0], pe_cv[0]]).reshape(2, 2, 1, half * HEAD_DIM)
    cmp_w1 = jnp.stack([w_ck1[0], w_cv1[0]]).reshape(2, 2, half * HEAD_DIM, HEAD_DIM).astype(BF16)
    cmp_w2 = jnp.stack([w_ck2[0], w_cv2[0]]).astype(BF16)
    mlp_w = (row2(g_attn[0]), w_out[0].astype(BF16), row2(norm_mlp[0]), w_up[0].astype(BF16),
             w_down[0].astype(BF16), row2(norm_final))
    nm = row2(norm_mix[0])

    tm = 512
    xp = x_prompt.reshape(nbp * tp, D_MODEL)
    q, qr, kvt, wint, kb, vb, xc, rg, rx, gates = _proj(xp, jnp.arange(tp), nm, wq, wkv, wr, nbp, tp, tm)
    y_rnn, h_last = _rglru_prompt(rx, rg, rnn_w, nbp, tp, 256)
    seq_pages = jnp.arange(nbp, dtype=jnp.int32)[:, None]
    cmp = _compress(_cmp_rows16(xc.reshape(nbp, tp, 2 * N_KV_HEADS, HEAD_DIM), nbp), seq_pages,
                    cmp_pe, cmp_w1, cmp_w2, nbp)
    o_attn = _nsa_prompt(q, qr, kb, vb, cmp, gates, nbp, tp, 128, 512)
    y_prompt = _mlp(xp, o_attn, y_rnn, *mlp_w, tm, 1024).reshape(nbp, tp, D_MODEL)
    wlen = min(WINDOW, tp)
    kv_prompt = kvt.reshape(nbp, 4, N_KV_HEADS, HEAD_DIM, tp).transpose(0, 4, 1, 2, 3)[None]
    win_prompt = wint[:, :, tp - wlen:].reshape(nbp, 2, N_KV_HEADS, HEAD_DIM, wlen).transpose(0, 4, 1, 2, 3)[None]
    conv_prompt = rx.reshape(nbp, tp, D_RNN)[:, tp - (CONV_WIDTH - 1):][None]
    h_prompt = h_last.reshape(1, nbp, D_RNN)

    rows_s = nbs * steps
    xs = x_sample.transpose(1, 0, 2).reshape(rows_s, D_MODEL)
    pos_s = PAST_LEN + jnp.arange(rows_s) // nbs
    q, qr, kvt, wint, _, _, _, rg, rx, gates = _proj(xs, pos_s, nm, wq, wkv, wr, 1, rows_s, rows_s)
    y_rnn, conv_s, h_s = _rglru_sample(rx, rg, state_conv[0].transpose(1, 0, 2), state_rnn[0], rnn_w)
    pool = cache_kv.shape[1]
    pages_t = cache_kv[0].transpose(0, 2, 3, 4, 1)
    pages16 = _cmp_rows16(cache_kv[0][:, :, 0:2].reshape(pool, PAGE_SIZE, 2 * N_KV_HEADS, HEAD_DIM), pool)
    cmp = _compress(pages16, page_table, cmp_pe, cmp_w1, cmp_w2, nbs)
    by_seq = lambda a: a.reshape(N_KV_HEADS, GQA, steps, nbs, HEAD_DIM).transpose(3, 0, 1, 2, 4).reshape(
        nbs, N_KV_HEADS, GQA * steps, HEAD_DIM)
    new_t = jnp.concatenate([kvt[0, 2 * KV_COLS:], wint[0]], axis=0).reshape(4 * KV_COLS, steps, nbs)
    new_t = jnp.pad(new_t.transpose(2, 0, 1), ((0, 0), (0, 0), (LANES - steps, 0)))
    gates_s = gates.reshape(steps, nbs, N_KV_HEADS, GATE_LANES)[..., :3 * GQA].reshape(steps, nbs, N_KV_HEADS, 3, GQA)
    gates_s = gates_s.transpose(1, 2, 4, 0, 3).reshape(nbs, N_KV_HEADS, GQA * steps, 3)
    cache_win_t = cache_win[0].transpose(0, 2, 3, 4, 1)
    o_s, win_t = _nsa_sample(by_seq(q[0]), by_seq(qr[0]), pages_t, page_table, cmp, cache_win_t, new_t, gates_s, steps)
    o_attn = o_s.reshape(nbs, N_KV_HEADS, GQA, steps, HEAD_DIM).transpose(3, 0, 1, 2, 4).reshape(rows_s, ATTN_WIDTH)
    y_sample = _mlp(xs, o_attn, y_rnn, *mlp_w, rows_s, 1024).reshape(steps, nbs, D_MODEL).transpose(1, 0, 2)
    kv_sample = kvt[0].reshape(4, N_KV_HEADS, HEAD_DIM, steps, nbs).transpose(4, 3, 0, 1, 2)[None]
    win_sample = win_t.transpose(0, 4, 1, 2, 3)[None]
    conv_sample = conv_s.transpose(1, 0, 2)[None]
    h_sample = h_s[None]

    return (y_prompt, y_sample, kv_prompt, kv_sample, win_prompt, win_sample, conv_prompt, conv_sample,
            h_prompt, h_sample)
```

```python
import functools

import jax
import jax.numpy as jnp
import numpy as np
from jax import lax
from jax.experimental import pallas as pl
from jax.experimental.pallas import tpu as pltpu

D_MODEL = 1024
PAST_LEN = 2048
PAGE_SIZE = 128
HEAD_DIM = 64
N_HEADS = 8
N_KV_HEADS = 2
GQA = N_HEADS // N_KV_HEADS
ATTN_WIDTH = N_HEADS * HEAD_DIM
D_RNN = D_MODEL - ATTN_WIDTH
RNN_BLOCKS = 8
CONV_WIDTH = 4
LRU_C = 8.0
D_FF = 4 * D_MODEL
ROT_DIM = HEAD_DIM // 4
ROT_HALF = ROT_DIM // 2
ROPE_THETA = 500000.0
CMP_BLOCK = 32
CMP_STRIDE = 16
SEL_BLOCK = 64
TOP_N = 16
WINDOW = 512
KV_COLS = N_KV_HEADS * HEAD_DIM
EPS = 1e-6
NEG = -1e30
SEL_BONUS = 1e4
SCALE = HEAD_DIM ** -0.5

SUBLANES = 8
LANES = 128
VMEM_LIMIT = 48 * 1024 * 1024

N_CMP_PAD = 128
GATE_LANES = LANES

F32 = jnp.float32
BF16 = jnp.bfloat16
NT_DIMS = (((1,), (1,)), ((), ()))
TN_DIMS = (((0,), (0,)), ((), ()))


def _nt(a, b):
    return lax.dot_general(a, b, NT_DIMS, preferred_element_type=F32)


def _nn(a, b):
    return jnp.dot(a, b, preferred_element_type=F32)


def _rms(x, g):
    return x * lax.rsqrt(jnp.mean(x * x, axis=-1, keepdims=True) + EPS) * g


def _params(*sem):
    return pltpu.CompilerParams(dimension_semantics=sem, vmem_limit_bytes=VMEM_LIMIT)


def _proj_kernel(x_ref, nm_ref, wq_ref, wkv_ref, wr_ref, cosr_ref, sina_ref, sinb_ref, cost_ref, sint_ref,
                 q_ref, qr_ref, kvt_ref, wint_ref, kb_ref, vb_ref, xc_ref, rg_ref, rx_ref, gates_ref):
    u = _rms(x_ref[...], nm_ref[...]).astype(BF16)
    q = _nt(u, wq_ref[...])
    cosr, sina, sinb = cosr_ref[...], sina_ref[...], sinb_ref[...]
    for k in range(ATTN_WIDTH // LANES):
        qg = q[:, k * LANES:(k + 1) * LANES]
        qrg = (qg * cosr + pltpu.roll(qg, LANES - ROT_HALF, 1) * sina
               + pltpu.roll(qg, ROT_HALF, 1) * sinb)
        for hh in range(LANES // HEAD_DIM):
            h = k * (LANES // HEAD_DIM) + hh
            sl = slice(hh * HEAD_DIM, (hh + 1) * HEAD_DIM)
            q_ref[0, h] = (qg[:, sl] * SCALE).astype(BF16)
            qr_ref[0, h] = (qrg[:, sl] * SCALE).astype(BF16)

    kvt = _nt(wkv_ref[...], u)
    cost, sint = cost_ref[...], sint_ref[...]
    kvt_ref[0] = kvt[0:4 * KV_COLS]
    wint_ref[0] = kvt[4 * KV_COLS:6 * KV_COLS]
    for out_ref, src0, dst0 in ((kvt_ref, 2 * KV_COLS, 2 * KV_COLS), (wint_ref, 4 * KV_COLS, 0)):
        for h in range(N_KV_HEADS):
            s, d = src0 + h * HEAD_DIM, dst0 + h * HEAD_DIM
            x1, x2 = kvt[s:s + ROT_HALF], kvt[s + ROT_HALF:s + ROT_DIM]
            out_ref[0, d:d + ROT_HALF] = x1 * cost - x2 * sint
            out_ref[0, d + ROT_HALF:d + ROT_DIM] = x2 * cost + x1 * sint
    tm = x_ref.shape[0]
    ones_row = (lax.broadcasted_iota(jnp.int32, (HEAD_DIM, tm), 0) == 0).astype(BF16)
    for branch, src_ref, k0, v0 in ((0, kvt_ref, 2 * KV_COLS, 3 * KV_COLS), (1, wint_ref, 0, KV_COLS)):
        for h in range(N_KV_HEADS):
            i, r = branch * N_KV_HEADS + h, h * HEAD_DIM
            kb_ref[0, i] = src_ref[0, k0 + r:k0 + r + HEAD_DIM].astype(BF16)
            vb_ref[0, i, 0:HEAD_DIM] = src_ref[0, v0 + r:v0 + r + HEAD_DIM].astype(BF16)
            vb_ref[0, i, HEAD_DIM:2 * HEAD_DIM] = ones_row

    xc_ref[...] = _nt(u, wkv_ref[0:2 * KV_COLS, :])
    r = _nt(u, wr_ref[...])
    rg_ref[...] = r[:, 0:D_RNN]
    rx_ref[...] = r[:, D_RNN:2 * D_RNN]
    gates_ref[...] = jax.nn.sigmoid(r[:, 2 * D_RNN:])


def _rope_tables(pos):
    inv = ROPE_THETA ** (-jnp.arange(ROT_HALF, dtype=F32) / ROT_HALF)
    ang = pos.astype(F32)[:, None] * inv
    cos, sin = jnp.cos(ang), jnp.sin(ang)
    d = np.arange(LANES) % HEAD_DIM
    f = d % ROT_HALF
    cosr = jnp.where(d < ROT_DIM, cos[:, f], 1.0)
    sina = jnp.where(d < ROT_HALF, -sin[:, f], 0.0)
    sinb = jnp.where((d >= ROT_HALF) & (d < ROT_DIM), sin[:, f], 0.0)
    return cosr, sina, sinb, cos.T, sin.T


def _proj(x, pos, nm, wq, wkv, wr, nb, tt, tm):
    rows = nb * tt
    nt = tt // tm
    cosr, sina, sinb, cost, sint = _rope_tables(pos)
    row = lambda w: pl.BlockSpec((tm, w), lambda i: (i, 0))
    full = lambda a: pl.BlockSpec(a.shape, lambda i: (0,) * a.ndim)
    tab_r = pl.BlockSpec((tm, LANES), lambda i: (i % nt, 0))
    tab_t = pl.BlockSpec((ROT_HALF, tm), lambda i: (0, i % nt))
    heads = pl.BlockSpec((1, N_HEADS, tm, HEAD_DIM), lambda i: (i // nt, 0, i % nt, 0))
    tr = lambda r: pl.BlockSpec((1, r, tm), lambda i: (i // nt, 0, i % nt))
    out_shape = (
        jax.ShapeDtypeStruct((nb, N_HEADS, tt, HEAD_DIM), BF16),
        jax.ShapeDtypeStruct((nb, N_HEADS, tt, HEAD_DIM), BF16),
        jax.ShapeDtypeStruct((nb, 4 * KV_COLS, tt), F32),
        jax.ShapeDtypeStruct((nb, 2 * KV_COLS, tt), F32),
        jax.ShapeDtypeStruct((nb, 2 * N_KV_HEADS, HEAD_DIM, tt), BF16),
        jax.ShapeDtypeStruct((nb, 2 * N_KV_HEADS, 2 * HEAD_DIM, tt), BF16),
        jax.ShapeDtypeStruct((rows, 2 * KV_COLS), F32),
        jax.ShapeDtypeStruct((rows, D_RNN), F32),
        jax.ShapeDtypeStruct((rows, D_RNN), F32),
        jax.ShapeDtypeStruct((rows, N_KV_HEADS * GATE_LANES), F32),
    )
    tr4 = lambda r: pl.BlockSpec((1, 2 * N_KV_HEADS, r, tm), lambda i: (i // nt, 0, 0, i % nt))
    out_specs = (heads, heads, tr(4 * KV_COLS), tr(2 * KV_COLS), tr4(HEAD_DIM), tr4(2 * HEAD_DIM),
                 row(2 * KV_COLS), row(D_RNN), row(D_RNN), row(N_KV_HEADS * GATE_LANES))
    return pl.pallas_call(
        _proj_kernel, grid=(rows // tm,),
        in_specs=[row(D_MODEL), full(nm), full(wq), full(wkv), full(wr), tab_r, tab_r, tab_r, tab_t, tab_t],
        out_specs=out_specs, out_shape=out_shape, compiler_params=_params("parallel"), name="proj",
    )(x, nm, wq, wkv, wr, cosr, sina, sinb, cost, sint)


def _expm1(x):
    return jnp.tanh(0.5 * x) * (jnp.exp(x) + 1.0)


def _softplus(x):
    return jnp.maximum(x, 0.0) + jnp.log1p(jnp.exp(-jnp.abs(x)))


def _lru_coeffs(xc, wra_ref, wri_ref, bra_ref, bri_ref, lam_ref):
    xb = xc.astype(BF16)
    r = jax.nn.sigmoid(_nn(xb, wra_ref[...]) + bra_ref[...])
    i = jax.nn.sigmoid(_nn(xb, wri_ref[...]) + bri_ref[...])
    log_a = -LRU_C * r * _softplus(-lam_ref[...])
    a = jnp.exp(log_a)
    b = jnp.sqrt(-_expm1(2.0 * log_a)) * (i * xc)
    return a, b


def _rnn_out(rg, h, g):
    return _rms(jax.nn.gelu(rg) * h, g).astype(BF16)


def _rglru_prompt_kernel(rx_ref, rg_ref, cw_ref, cb_ref, wra_ref, wri_ref, bra_ref, bri_ref, lam_ref, g_ref,
                         y_ref, hl_ref, prev_sc, h_sc):
    tt = rx_ref.shape[0]

    @pl.when(pl.program_id(1) == 0)
    def _():
        prev_sc[...] = jnp.zeros_like(prev_sc)
        h_sc[...] = jnp.zeros_like(h_sc)

    rx = rx_ref[...]
    ext = jnp.concatenate([prev_sc[...], rx], axis=0)
    shifted = lambda d: pltpu.roll(ext, d, 0)[SUBLANES:SUBLANES + tt]
    cw = cw_ref[...]
    xc = cb_ref[...] + cw[0:1] * shifted(3)
    xc = xc + cw[1:2] * shifted(2)
    xc = xc + cw[2:3] * shifted(1)
    xc = xc + cw[3:4] * rx
    prev_sc[...] = rx[tt - SUBLANES:tt]

    a, b = _lru_coeffs(xc, wra_ref, wri_ref, bra_ref, bri_ref, lam_ref)
    row = lax.broadcasted_iota(jnp.int32, (tt, 1), 0)
    s = 1
    while s < tt:
        if s < SUBLANES:
            a_sh = jnp.where(row >= s, pltpu.roll(a, s, 0), 1.0)
            b_sh = jnp.where(row >= s, pltpu.roll(b, s, 0), 0.0)
        else:
            a_sh = jnp.concatenate([jnp.ones((s, a.shape[1]), F32), a[:tt - s]], axis=0)
            b_sh = jnp.concatenate([jnp.zeros((s, b.shape[1]), F32), b[:tt - s]], axis=0)
        b = a * b_sh + b
        a = a * a_sh
        s *= 2
    h = a * h_sc[...] + b
    h_sc[...] = h[tt - 1:tt]
    hl_ref[0] = h[tt - 1:tt]
    y_ref[...] = _rnn_out(rg_ref[...], h, g_ref[...])


def _rglru_sample_kernel(rx_ref, rg_ref, cp_ref, h0_ref, cw_ref, cb_ref, wra_ref, wri_ref, bra_ref, bri_ref,
                         lam_ref, g_ref, y_ref, cs_ref, hl_ref):
    nb = h0_ref.shape[0]
    steps = rx_ref.shape[0] // nb
    xp = [cp_ref[k] for k in range(CONV_WIDTH - 1)] + [rx_ref[t * nb:(t + 1) * nb] for t in range(steps)]
    cw, cb = cw_ref[...], cb_ref[...]
    xcs = []
    for t in range(steps):
        xc = cb + cw[0:1] * xp[t]
        for tap in range(1, CONV_WIDTH):
            xc = xc + cw[tap:tap + 1] * xp[t + tap]
        xcs.append(xc)
    a, b = _lru_coeffs(jnp.concatenate(xcs, axis=0), wra_ref, wri_ref, bra_ref, bri_ref, lam_ref)
    h = h0_ref[...]
    for t in range(steps):
        sl = slice(t * nb, (t + 1) * nb)
        h = a[sl] * h + b[sl]
        y_ref[sl] = _rnn_out(rg_ref[sl], h, g_ref[...])
    for k in range(CONV_WIDTH - 1):
        cs_ref[k] = xp[steps + k]
    hl_ref[...] = h


def _rglru_prompt(rx, rg, weights, nb, tt, chunk):
    nc = tt // chunk
    row = pl.BlockSpec((chunk, D_RNN), lambda n, c: (n * nc + c, 0))
    full = lambda a: pl.BlockSpec(a.shape, lambda n, c: (0,) * a.ndim)
    return pl.pallas_call(
        _rglru_prompt_kernel, grid=(nb, nc),
        in_specs=[row, row] + [full(w) for w in weights],
        out_specs=(row, pl.BlockSpec((1, 1, D_RNN), lambda n, c: (n, 0, 0))),
        out_shape=(jax.ShapeDtypeStruct((nb * tt, D_RNN), BF16), jax.ShapeDtypeStruct((nb, 1, D_RNN), F32)),
        scratch_shapes=[pltpu.VMEM((SUBLANES, D_RNN), F32), pltpu.VMEM((1, D_RNN), F32)],
        compiler_params=_params("parallel", "arbitrary"), name="rglru_prompt",
    )(rx, rg, *weights)


def _rglru_sample(rx, rg, conv_prev, h0, weights):
    nb = h0.shape[0]
    return pl.pallas_call(
        _rglru_sample_kernel,
        out_shape=(jax.ShapeDtypeStruct(rx.shape, BF16), jax.ShapeDtypeStruct(conv_prev.shape, F32),
                   jax.ShapeDtypeStruct((nb, D_RNN), F32)),
        compiler_params=pltpu.CompilerParams(vmem_limit_bytes=VMEM_LIMIT), name="rglru_sample",
    )(rx, rg, conv_prev, h0, *weights)


def _compress_kernel(pt_ref, *refs, n_pages):
    del pt_ref
    page_refs = refs[:n_pages]
    pe_ref, w1_ref, w2_ref, out_ref = refs[n_pages:]
    for cs in range(2 * N_KV_HEADS):
        slot = cs // N_KV_HEADS
        x = jnp.concatenate([r[0, cs] for r in page_refs], axis=0)
        p0 = _nn((x + pe_ref[slot, 0]).astype(BF16), w1_ref[slot, 0])
        p1 = _nn((x + pe_ref[slot, 1]).astype(BF16), w1_ref[slot, 1])
        hid = jax.nn.gelu(p0 + pltpu.roll(p1, N_CMP_PAD - 1, 0))
        out_ref[0, cs] = _nn(hid.astype(BF16), w2_ref[slot]).astype(BF16)


def _compress(pages, page_table, pe, w1, w2, nb):
    n_pages = page_table.shape[1]
    rows = pages.shape[2]
    assert n_pages * rows == N_CMP_PAD
    page_spec = lambda k: pl.BlockSpec((1, 2 * N_KV_HEADS, rows, pages.shape[3]),
                                       lambda n, pt: (pt[n, k], 0, 0, 0))
    full = lambda a: pl.BlockSpec(a.shape, lambda n, pt: (0,) * a.ndim)
    grid_spec = pltpu.PrefetchScalarGridSpec(
        num_scalar_prefetch=1, grid=(nb,),
        in_specs=[page_spec(k) for k in range(n_pages)] + [full(pe), full(w1), full(w2)],
        out_specs=pl.BlockSpec((1, 2 * N_KV_HEADS, N_CMP_PAD, HEAD_DIM), lambda n, pt: (n, 0, 0, 0)))
    return pl.pallas_call(
        functools.partial(_compress_kernel, n_pages=n_pages), grid_spec=grid_spec,
        out_shape=jax.ShapeDtypeStruct((nb, 2 * N_KV_HEADS, N_CMP_PAD, HEAD_DIM), BF16),
        compiler_params=_params("parallel"), name="compress",
    )(page_table, *([pages] * n_pages), pe, w1, w2)


def _overlap_matrix(n_sel):
    n_cmp = N_CMP_PAD - 1
    c0 = np.arange(N_CMP_PAD)[:, None] * CMP_STRIDE
    j0 = np.arange(LANES)[None, :] * SEL_BLOCK
    ov = (c0 < j0 + SEL_BLOCK) & (c0 + CMP_BLOCK > j0)
    ov &= (np.arange(N_CMP_PAD)[:, None] < n_cmp) & (np.arange(LANES)[None, :] < n_sel)
    return ov.astype(np.float32)


def _topk_member(score, idx, n, axis):
    rank = jnp.zeros(score.shape, jnp.int32)
    for i in range(n):
        si = score[i:i + 1, :] if axis == 0 else score[:, i:i + 1]
        beats = (si > score) | ((si == score) & (i < idx))
        rank = rank + beats.astype(jnp.int32)
    return rank < TOP_N


def _sel_score(imp, j, cur):
    valid = j <= cur
    forced = (j == 0) | (j == cur) | (j == cur - 1)
    return jnp.where(valid, imp, -SEL_BONUS) + jnp.where(forced, SEL_BONUS, 0.0)


def _nsa_prompt_kernel(q_ref, qr_ref, kb_ref, vb_ref, cmp_ref, gates_ref, ovt_ref, e_ref,
                       o_ref, m_sc, acc_sc, *, tq, slc_chunk, row_block, n_sel):
    heads = range(N_KV_HEADS)
    q0 = pl.program_id(1) * tq
    qpos = q0 + lax.broadcasted_iota(jnp.int32, (tq, 1), 0)
    group = lambda ref, h: ref[0, h * GQA:(h + 1) * GQA].reshape(GQA * tq, HEAD_DIM)

    cidx = lax.broadcasted_iota(jnp.int32, (1, N_CMP_PAD), 1)
    real = cidx < N_CMP_PAD - 1
    cmask = (cidx * CMP_STRIDE + CMP_BLOCK - 1 <= qpos) & real
    qpos_t = q0 + lax.broadcasted_iota(jnp.int32, (1, tq), 1)
    j_t = lax.broadcasted_iota(jnp.int32, (n_sel, 1), 0)
    eye = (j_t == lax.broadcasted_iota(jnp.int32, (1, n_sel), 1)).astype(BF16)
    o_cmp, sel = [], []
    for h in heads:
        s = _nt(group(q_ref, h), cmp_ref[0, h]).reshape(GQA, tq, N_CMP_PAD)
        s = jnp.where(real, jnp.where(cmask, s, NEG), -jnp.inf)
        e = jnp.exp(s - jnp.max(s, axis=-1, keepdims=True))
        p = (e / jnp.sum(e, axis=-1, keepdims=True)) * cmask.astype(F32)
        pb = p.astype(BF16)
        o_cmp.append(_nn(pb.reshape(GQA * tq, N_CMP_PAD), cmp_ref[0, N_KV_HEADS + h]).reshape(GQA, tq, HEAD_DIM))
        imp_t = _nt(ovt_ref[...], pb[0])
        for g in range(1, GQA):
            imp_t = imp_t + _nt(ovt_ref[...], pb[g])
        score_t = _sel_score(imp_t, j_t, qpos_t // SEL_BLOCK)
        sel_t = _topk_member(score_t, j_t, n_sel, axis=0).astype(BF16)
        sel.append(lax.dot_general(sel_t, eye, TN_DIMS, preferred_element_type=F32).astype(BF16))

    qr4 = [group(qr_ref, h) for h in heads]

    def attend(h, k, v, allowed, first):
        n = k.shape[1]
        sc = _nn(qr4[h], k)
        bias = jnp.where(allowed, 0.0, NEG)
        pes, ms, alphas = [], [], []
        for r0 in range(0, GQA * tq, row_block):
            t0 = r0 % tq
            x = sc[r0:r0 + row_block] + bias[t0:t0 + row_block]
            m = jnp.max(x, axis=-1, keepdims=True)
            if first:
                ms.append(jnp.broadcast_to(m, (row_block, LANES)))
            else:
                m_prev = m_sc[h, r0:r0 + row_block]
                m_wide = jnp.maximum(m_prev, m)
                alphas.append(jnp.exp(m_prev - m_wide))
                ms.append(m_wide)
                m = jnp.concatenate([m_wide] * (n // LANES), axis=1)
            pes.append(jnp.exp(x - m).astype(BF16))
        pv = _nt(jnp.concatenate(pes, axis=0), v)
        m_sc[h] = jnp.concatenate(ms, axis=0)
        acc_sc[h] = pv if first else jnp.concatenate(alphas, axis=0) * acc_sc[h] + pv

    def normalised(h):
        acc = acc_sc[h].reshape(GQA, tq, 2 * HEAD_DIM)
        return acc[:, :, 0:HEAD_DIM] / acc[:, :, HEAD_DIM:HEAD_DIM + 1]

    wn = WINDOW + tq
    w0 = pl.multiple_of(jnp.maximum(q0 - WINDOW, 0), LANES)
    kpos = w0 + lax.broadcasted_iota(jnp.int32, (1, wn), 1)
    in_window = (kpos <= qpos) & (kpos > qpos - WINDOW)
    for h in heads:
        attend(h, kb_ref[0, N_KV_HEADS + h, :, pl.ds(w0, wn)], vb_ref[0, N_KV_HEADS + h, :, pl.ds(w0, wn)],
               in_window, True)
    o_win = [normalised(h) for h in heads]

    for c0 in range(0, kb_ref.shape[3], slc_chunk):
        def chunk(c0=c0):
            kpos = c0 + lax.broadcasted_iota(jnp.int32, (1, slc_chunk), 1)
            for h in heads:
                picked = _nn(sel[h], e_ref[:, c0:c0 + slc_chunk])
                attend(h, kb_ref[0, h, :, c0:c0 + slc_chunk], vb_ref[0, h, :, c0:c0 + slc_chunk],
                       (picked > 0.5) & (kpos <= qpos), c0 == 0)

        if c0 == 0:
            chunk()
        else:
            pl.when(q0 + tq > c0)(chunk)
    o_slc = [normalised(h) for h in heads]

    gt = gates_ref[...]
    for h in heads:
        for g in range(GQA):
            col = lambda b: gt[:, h * GATE_LANES + b * GQA + g:h * GATE_LANES + b * GQA + g + 1]
            o = col(0) * o_cmp[h][g] + col(1) * o_slc[h][g]
            o = o + col(2) * o_win[h][g]
            o_ref[:, (h * GQA + g) * HEAD_DIM:(h * GQA + g + 1) * HEAD_DIM] = o


def _nsa_prompt(q, qr, kb, vb, cmp, gates, nb, tt, tq, slc_chunk):
    n_sel = -(-tt // SEL_BLOCK)
    nq = tt // tq
    ovt = jnp.asarray(_overlap_matrix(n_sel)[:, :n_sel].T, BF16)
    expand = jnp.asarray(np.arange(tt)[None, :] // SEL_BLOCK == np.arange(n_sel)[:, None], BF16)
    qspec = pl.BlockSpec((1, N_HEADS, tq, HEAD_DIM), lambda n, i: (n, 0, i, 0))
    seq = lambda a: pl.BlockSpec((1,) + a.shape[1:], lambda n, i: (n,) + (0,) * (a.ndim - 1))
    full = lambda a: pl.BlockSpec(a.shape, lambda n, i: (0,) * a.ndim)
    rows = lambda w: pl.BlockSpec((tq, w), lambda n, i: (n * nq + i, 0))
    return pl.pallas_call(
        functools.partial(_nsa_prompt_kernel, tq=tq, slc_chunk=slc_chunk, row_block=32, n_sel=n_sel),
        grid=(nb, nq),
        in_specs=[qspec, qspec, seq(kb), seq(vb), seq(cmp), rows(N_KV_HEADS * GATE_LANES), full(ovt), full(expand)],
        out_specs=rows(ATTN_WIDTH), out_shape=jax.ShapeDtypeStruct((nb * tt, ATTN_WIDTH), F32),
        scratch_shapes=[pltpu.VMEM((N_KV_HEADS, GQA * tq, LANES), F32),
                        pltpu.VMEM((N_KV_HEADS, GQA * tq, 2 * HEAD_DIM), F32)],
        compiler_params=_params("parallel", "arbitrary"), name="nsa_prompt",
    )(q, qr, kb, vb, cmp, gates, ovt, expand)


def _nsa_sample_kernel(pt_ref, *refs, n_pages, steps):
    del pt_ref
    q_ref, qr_ref = refs[0:2]
    page_refs = refs[2:2 + n_pages]
    cmp_ref, cw_ref, new_ref, gates_ref, ov_ref, o_ref, wout_ref = refs[2 + n_pages:]
    rows = GQA * steps
    past = n_pages * PAGE_SIZE
    wc = cw_ref.shape[-1]
    new0 = LANES - steps
    trow = lax.broadcasted_iota(jnp.int32, (rows, 1), 0) % steps
    lane = lax.broadcasted_iota(jnp.int32, (1, LANES), 1)
    new_ok = (lane >= new0) & (lane - new0 <= trow)
    n_sel = -(-(past + steps) // SEL_BLOCK)
    cur = (past + trow) // SEL_BLOCK

    for h in range(N_KV_HEADS):
        q, qr = q_ref[0, h], qr_ref[0, h]
        row0 = h * HEAD_DIM
        new_rows = lambda slot: new_ref[0, slot * KV_COLS + row0:slot * KV_COLS + row0 + HEAD_DIM, :]

        real = lane < N_CMP_PAD - 1
        s = jnp.where(real, _nt(q, cmp_ref[0, h]), -jnp.inf)
        e = jnp.exp(s - jnp.max(s, axis=-1, keepdims=True))
        pb = (e / jnp.sum(e, axis=-1, keepdims=True)).astype(BF16)
        o_cmp = _nn(pb, cmp_ref[0, N_KV_HEADS + h])

        part = _nn(pb, ov_ref[...])
        imp = part
        for g in range(1, GQA):
            imp = imp + pltpu.roll(part, g * steps, 0)
        score = jnp.where(lane < n_sel, _sel_score(imp, lane, cur), -jnp.inf)
        sel = _topk_member(score, lane, n_sel, axis=1) & (lane < n_sel)
        sel_f = sel.astype(F32)

        ks = jnp.concatenate([page_refs[k][0, 0, h].astype(BF16) for k in range(n_pages)]
                             + [new_rows(0).astype(BF16)], axis=1)
        vs = jnp.concatenate([page_refs[k][0, 1, h].astype(BF16) for k in range(n_pages)]
                             + [new_rows(1).astype(BF16)], axis=1)
        per_tile = PAGE_SIZE // SEL_BLOCK
        tiles = []
        for k in range(n_pages):
            m = sel_f[:, per_tile * k:per_tile * k + 1]
            for b in range(1, per_tile):
                m = jnp.where(lane < b * SEL_BLOCK, m, sel_f[:, per_tile * k + b:per_tile * k + b + 1])
            tiles.append(m > 0.5)
        tiles.append((sel_f[:, n_sel - 1:n_sel] > 0.5) & new_ok)
        msk = jnp.concatenate(tiles, axis=1)
        sc = jnp.where(msk, _nn(qr, ks), NEG)
        pe = jnp.where(msk, jnp.exp(sc - jnp.max(sc, axis=-1, keepdims=True)), 0.0)
        o_slc = _nt(pe.astype(BF16), vs) / jnp.sum(pe, axis=-1, keepdims=True)

        kw = jnp.concatenate([cw_ref[0, 0, h].astype(BF16), new_rows(2).astype(BF16)], axis=1)
        vw = jnp.concatenate([cw_ref[0, 1, h].astype(BF16), new_rows(3).astype(BF16)], axis=1)
        wlane = lax.broadcasted_iota(jnp.int32, (1, wc), 1)
        wmsk = jnp.concatenate([jnp.broadcast_to(wlane + (WINDOW - wc) > trow, (rows, wc)),
                                jnp.broadcast_to(new_ok, (rows, LANES))], axis=1)
        sw = jnp.where(wmsk, _nn(qr, kw), NEG)
        pw = jnp.where(wmsk, jnp.exp(sw - jnp.max(sw, axis=-1, keepdims=True)), 0.0)
        o_win = _nt(pw.astype(BF16), vw) / jnp.sum(pw, axis=-1, keepdims=True)

        gt = gates_ref[0, h]
        o_ref[0, h] = gt[:, 0:1] * o_cmp + gt[:, 1:2] * o_slc + gt[:, 2:3] * o_win

        for slot in range(2):
            shifted = pltpu.roll(cw_ref[0, slot, h], wc - steps, 1)
            wout_ref[0, slot, h, :, 0:wc - LANES] = shifted[:, 0:wc - LANES]
            wout_ref[0, slot, h, :, wc - LANES:wc] = jnp.where(lane >= new0, new_rows(2 + slot),
                                                               shifted[:, wc - LANES:wc])


def _nsa_sample(q, qr, pages, page_table, cmp, cache_win_t, new_t, gates, steps):
    nb, n_pages = page_table.shape
    wc = cache_win_t.shape[-1]
    n_sel = -(-(n_pages * PAGE_SIZE + steps) // SEL_BLOCK)
    ov = jnp.asarray(_overlap_matrix(n_sel), BF16)
    rows = GQA * steps
    lead = lambda a: pl.BlockSpec((1,) + a.shape[1:], lambda n, pt: (n,) + (0,) * (a.ndim - 1))
    page_spec = lambda k: pl.BlockSpec((1, 2, N_KV_HEADS, HEAD_DIM, PAGE_SIZE),
                                       lambda n, pt: (pt[n, k], 1, 0, 0, 0))
    grid_spec = pltpu.PrefetchScalarGridSpec(
        num_scalar_prefetch=1, grid=(nb,),
        in_specs=[lead(q), lead(qr)] + [page_spec(k) for k in range(n_pages)]
        + [lead(cmp), lead(cache_win_t), lead(new_t), lead(gates), pl.BlockSpec(ov.shape, lambda n, pt: (0, 0))],
        out_specs=(pl.BlockSpec((1, N_KV_HEADS, rows, HEAD_DIM), lambda n, pt: (n, 0, 0, 0)), lead(cache_win_t)))
    return pl.pallas_call(
        functools.partial(_nsa_sample_kernel, n_pages=n_pages, steps=steps), grid_spec=grid_spec,
        out_shape=(jax.ShapeDtypeStruct((nb, N_KV_HEADS, rows, HEAD_DIM), F32),
                   jax.ShapeDtypeStruct(cache_win_t.shape, F32)),
        compiler_params=_params("parallel"), name="nsa_sample",
    )(page_table, q, qr, *([pages] * n_pages), cmp, cache_win_t, new_t, gates, ov)


def _mlp_kernel(x_ref, oa_ref, yr_ref, ga_ref, wo_ref, nmlp_ref, wup_ref, wdn_ref, nfin_ref, y_ref, *, ff_chunk):
    a = _rms(oa_ref[...], ga_ref[...]).astype(BF16)
    x1 = x_ref[...] + (_nn(a, wo_ref[0:ATTN_WIDTH, :]) + _nn(yr_ref[...], wo_ref[ATTN_WIDTH:, :]))
    v = _rms(x1, nmlp_ref[...]).astype(BF16)
    acc = jnp.zeros_like(x1)
    for c in range(D_FF // ff_chunk):
        sl = slice(c * ff_chunk, (c + 1) * ff_chunk)
        hid = jnp.square(jnp.maximum(_nn(v, wup_ref[:, sl]), 0.0)).astype(BF16)
        acc = acc + _nn(hid, wdn_ref[sl, :])
    y_ref[...] = _rms(x1 + acc, nfin_ref[...])


def _mlp(x, o_attn, y_rnn, g_attn, w_out, norm_mlp, w_up, w_down, norm_final, tm, ff_chunk):
    rows = x.shape[0]
    row = lambda w: pl.BlockSpec((tm, w), lambda i: (i, 0))
    full = lambda a: pl.BlockSpec(a.shape, lambda i: (0,) * a.ndim)
    once = lambda a: pl.BlockSpec(a.shape, lambda i: (0,) * a.ndim, pipeline_mode=pl.Buffered(1))
    return pl.pallas_call(
        functools.partial(_mlp_kernel, ff_chunk=ff_chunk), grid=(rows // tm,),
        in_specs=[row(D_MODEL), row(ATTN_WIDTH), row(D_RNN), full(g_attn), once(w_out), full(norm_mlp),
                  once(w_up), once(w_down), full(norm_final)],
        out_specs=row(D_MODEL), out_shape=jax.ShapeDtypeStruct((rows, D_MODEL), F32),
        compiler_params=_params("parallel"), name="mlp",
    )(x, o_attn, y_rnn, g_attn, w_out, norm_mlp, w_up, w_down, norm_final)


def _block_diag(w):
    nb, bs, _ = w.shape
    eye = jnp.eye(nb, dtype=w.dtype)
    return (eye[:, None, :, None] * w[:, :, None, :]).reshape(nb * bs, nb * bs)


def _cmp_rows16(x, lead):
    t = x.shape[1]
    x = x.reshape(lead, t // CMP_STRIDE, CMP_STRIDE, 2 * N_KV_HEADS, HEAD_DIM)
    return x.transpose(0, 3, 1, 2, 4).reshape(lead, 2 * N_KV_HEADS, t // CMP_STRIDE, CMP_STRIDE * HEAD_DIM)


def kernel(x_prompt, x_sample, cache_kv, cache_win, state_conv, state_rnn, page_table, w_in, pe_ck, w_ck1, w_ck2,
           pe_cv, w_cv1, w_cv2, g_attn, g_rnn, conv_w, conv_b, w_ra, b_ra, w_ri, b_ri, lam, w_out, norm_mix,
           norm_mlp, w_up, w_down, norm_final):
    assert w_in.shape[0] == 1, "single layer"
    nbp, tp, _ = x_prompt.shape
    nbs, steps, _ = x_sample.shape
    n_pages = page_table.shape[1]

    wt = jnp.transpose(w_in[0]).astype(BF16)
    c_kv, c_g = ATTN_WIDTH, ATTN_WIDTH + 6 * KV_COLS
    c_rg = c_g + 3 * N_HEADS
    wq, wkv = wt[0:c_kv], wt[c_kv:c_g]
    wg = wt[c_g:c_rg].reshape(3, N_KV_HEADS, GQA, D_MODEL).transpose(1, 0, 2, 3).reshape(N_KV_HEADS, 3 * GQA, D_MODEL)
    wg = jnp.pad(wg, ((0, 0), (0, GATE_LANES - 3 * GQA), (0, 0))).reshape(N_KV_HEADS * GATE_LANES, D_MODEL)
    wr = jnp.concatenate([wt[c_rg:c_rg + 2 * D_RNN], wg], axis=0)
    row2 = lambda a: a.reshape(1, -1)
    rnn_w = (conv_w[0], row2(conv_b[0]), _block_diag(w_ra[0]).astype(BF16), _block_diag(w_ri[0]).astype(BF16),
             row2(b_ra[0]), row2(b_ri[0]), row2(lam[0]), row2(g_rnn[0]))
    half = CMP_BLOCK // 2
    cmp_pe = jnp.stack([pe_ck[0], pe_cv[0]]).reshape(2, 2, 1, half * HEAD_DIM)
    cmp_w1 = jnp.stack([w_ck1[0], w_cv1[0]]).reshape(2, 2, half * HEAD_DIM, HEAD_DIM).astype(BF16)
    cmp_w2 = jnp.stack([w_ck2[0], w_cv2[0]]).astype(BF16)
    mlp_w = (row2(g_attn[0]), w_out[0].astype(BF16), row2(norm_mlp[0]), w_up[0].astype(BF16),
             w_down[0].astype(BF16), row2(norm_final))
    nm = row2(norm_mix[0])

    tm = 512
    xp = x_prompt.reshape(nbp * tp, D_MODEL)
    q, qr, kvt, wint, kb, vb, xc, rg, rx, gates = _proj(xp, jnp.arange(tp), nm, wq, wkv, wr, nbp, tp, tm)
    y_rnn, h_last = _rglru_prompt(rx, rg, rnn_w, nbp, tp, 256)
    seq_pages = jnp.arange(nbp, dtype=jnp.int32)[:, None]
    cmp = _compress(_cmp_rows16(xc.reshape(nbp, tp, 2 * N_KV_HEADS, HEAD_DIM), nbp), seq_pages,
                    cmp_pe, cmp_w1, cmp_w2, nbp)
    o_attn = _nsa_prompt(q, qr, kb, vb, cmp, gates, nbp, tp, 128, 512)
    y_prompt = _mlp(xp, o_attn, y_rnn, *mlp_w, tm, 1024).reshape(nbp, tp, D_MODEL)
    wlen = min(WINDOW, tp)
    kv_prompt = kvt.reshape(nbp, 4, N_KV_HEADS, HEAD_DIM, tp).transpose(0, 4, 1, 2, 3)[None]
    win_prompt = wint[:, :, tp - wlen:].reshape(nbp, 2, N_KV_HEADS, HEAD_DIM, wlen).transpose(0, 4, 1, 2, 3)[None]
    conv_prompt = rx.reshape(nbp, tp, D_RNN)[:, tp - (CONV_WIDTH - 1):][None]
    h_prompt = h_last.reshape(1, nbp, D_RNN)

    rows_s = nbs * steps
    xs = x_sample.transpose(1, 0, 2).reshape(rows_s, D_MODEL)
    pos_s = PAST_LEN + jnp.arange(rows_s) // nbs
    q, qr, kvt, wint, _, _, _, rg, rx, gates = _proj(xs, pos_s, nm, wq, wkv, wr, 1, rows_s, rows_s)
    y_rnn, conv_s, h_s = _rglru_sample(rx, rg, state_conv[0].transpose(1, 0, 2), state_rnn[0], rnn_w)
    pool = cache_kv.shape[1]
    pages_t = cache_kv[0].transpose(0, 2, 3, 4, 1)
    pages16 = pages_t[:, 0:2].reshape(pool, 2 * N_KV_HEADS, HEAD_DIM, PAGE_SIZE // CMP_STRIDE, CMP_STRIDE)
    pages16 = pages16.transpose(0, 1, 3, 4, 2).reshape(pool, 2 * N_KV_HEADS, PAGE_SIZE // CMP_STRIDE,
                                                       CMP_STRIDE * HEAD_DIM)
    cmp = _compress(pages16, page_table, cmp_pe, cmp_w1, cmp_w2, nbs)
    by_seq = lambda a: a.reshape(N_KV_HEADS, GQA, steps, nbs, HEAD_DIM).transpose(3, 0, 1, 2, 4).reshape(
        nbs, N_KV_HEADS, GQA * steps, HEAD_DIM)
    new_t = jnp.concatenate([kvt[0, 2 * KV_COLS:], wint[0]], axis=0).reshape(4 * KV_COLS, steps, nbs)
    new_t = jnp.pad(new_t.transpose(2, 0, 1), ((0, 0), (0, 0), (LANES - steps, 0)))
    gates_s = gates.reshape(steps, nbs, N_KV_HEADS, GATE_LANES)[..., :3 * GQA].reshape(steps, nbs, N_KV_HEADS, 3, GQA)
    gates_s = gates_s.transpose(1, 2, 4, 0, 3).reshape(nbs, N_KV_HEADS, GQA * steps, 3)
    cache_win_t = cache_win[0].transpose(0, 2, 3, 4, 1)
    o_s, win_t = _nsa_sample(by_seq(q[0]), by_seq(qr[0]), pages_t, page_table, cmp, cache_win_t, new_t, gates_s, steps)
    o_attn = o_s.reshape(nbs, N_KV_HEADS, GQA, steps, HEAD_DIM).transpose(3, 0, 1, 2, 4).reshape(rows_s, ATTN_WIDTH)
    y_sample = _mlp(xs, o_attn, y_rnn, *mlp_w, rows_s, 1024).reshape(steps, nbs, D_MODEL).transpose(1, 0, 2)
    kv_sample = kvt[0].reshape(4, N_KV_HEADS, HEAD_DIM, steps, nbs).transpose(4, 3, 0, 1, 2)[None]
    win_sample = win_t.transpose(0, 4, 1, 2, 3)[None]
    conv_sample = conv_s.transpose(1, 0, 2)[None]
    h_sample = h_s[None]

    return (y_prompt, y_sample, kv_prompt, kv_sample, win_prompt, win_sample, conv_prompt, conv_sample,
            h_prompt, h_sample)
```

```python
import functools

import jax
import jax.numpy as jnp
import numpy as np
from jax import lax
from jax.experimental import pallas as pl
from jax.experimental.pallas import tpu as pltpu

D_MODEL = 1024
PAST_LEN = 2048
PAGE_SIZE = 128
HEAD_DIM = 64
N_HEADS = 8
N_KV_HEADS = 2
GQA = N_HEADS // N_KV_HEADS
ATTN_WIDTH = N_HEADS * HEAD_DIM
D_RNN = D_MODEL - ATTN_WIDTH
RNN_BLOCKS = 8
CONV_WIDTH = 4
LRU_C = 8.0
D_FF = 4 * D_MODEL
ROT_DIM = HEAD_DIM // 4
ROT_HALF = ROT_DIM // 2
ROPE_THETA = 500000.0
CMP_BLOCK = 32
CMP_STRIDE = 16
SEL_BLOCK = 64
TOP_N = 16
WINDOW = 512
KV_COLS = N_KV_HEADS * HEAD_DIM
EPS = 1e-6
NEG = -1e30
SEL_BONUS = 1e4
SCALE = HEAD_DIM ** -0.5

SUBLANES = 8
LANES = 128
VMEM_LIMIT = 48 * 1024 * 1024

N_CMP_PAD = 128
GATE_LANES = LANES

F32 = jnp.float32
BF16 = jnp.bfloat16
NT_DIMS = (((1,), (1,)), ((), ()))
TN_DIMS = (((0,), (0,)), ((), ()))


def _nt(a, b):
    return lax.dot_general(a, b, NT_DIMS, preferred_element_type=F32)


def _nn(a, b):
    return jnp.dot(a, b, preferred_element_type=F32)


def _rms(x, g):
    return x * lax.rsqrt(jnp.mean(x * x, axis=-1, keepdims=True) + EPS) * g


def _params(*sem):
    return pltpu.CompilerParams(dimension_semantics=sem, vmem_limit_bytes=VMEM_LIMIT)


def _proj_kernel(x_ref, nm_ref, wq_ref, wkv_ref, wr_ref, cosr_ref, sina_ref, sinb_ref, cost_ref, sint_ref,
                 q_ref, qr_ref, kvt_ref, wint_ref, kb_ref, vb_ref, rg_ref, rx_ref, gates_ref):
    u = _rms(x_ref[...], nm_ref[...]).astype(BF16)
    q = _nt(u, wq_ref[...])
    cosr, sina, sinb = cosr_ref[...], sina_ref[...], sinb_ref[...]
    for k in range(ATTN_WIDTH // LANES):
        qg = q[:, k * LANES:(k + 1) * LANES]
        qrg = (qg * cosr + pltpu.roll(qg, LANES - ROT_HALF, 1) * sina
               + pltpu.roll(qg, ROT_HALF, 1) * sinb)
        for hh in range(LANES // HEAD_DIM):
            h = k * (LANES // HEAD_DIM) + hh
            sl = slice(hh * HEAD_DIM, (hh + 1) * HEAD_DIM)
            q_ref[0, h] = (qg[:, sl] * SCALE).astype(BF16)
            qr_ref[0, h] = (qrg[:, sl] * SCALE).astype(BF16)

    kvt = _nt(wkv_ref[...], u)
    cost, sint = cost_ref[...], sint_ref[...]
    kvt_ref[0] = kvt[0:4 * KV_COLS]
    wint_ref[0] = kvt[4 * KV_COLS:6 * KV_COLS]
    for out_ref, src0, dst0 in ((kvt_ref, 2 * KV_COLS, 2 * KV_COLS), (wint_ref, 4 * KV_COLS, 0)):
        for h in range(N_KV_HEADS):
            s, d = src0 + h * HEAD_DIM, dst0 + h * HEAD_DIM
            x1, x2 = kvt[s:s + ROT_HALF], kvt[s + ROT_HALF:s + ROT_DIM]
            out_ref[0, d:d + ROT_HALF] = x1 * cost - x2 * sint
            out_ref[0, d + ROT_HALF:d + ROT_DIM] = x2 * cost + x1 * sint
    tm = x_ref.shape[0]
    ones_row = (lax.broadcasted_iota(jnp.int32, (HEAD_DIM, tm), 0) == 0).astype(BF16)
    for branch, src_ref, k0, v0 in ((0, kvt_ref, 2 * KV_COLS, 3 * KV_COLS), (1, wint_ref, 0, KV_COLS)):
        for h in range(N_KV_HEADS):
            i, r = branch * N_KV_HEADS + h, h * HEAD_DIM
            kb_ref[0, i] = src_ref[0, k0 + r:k0 + r + HEAD_DIM].astype(BF16)
            vb_ref[0, i, 0:HEAD_DIM] = src_ref[0, v0 + r:v0 + r + HEAD_DIM].astype(BF16)
            vb_ref[0, i, HEAD_DIM:2 * HEAD_DIM] = ones_row

    r = _nt(u, wr_ref[...])
    rg_ref[...] = r[:, 0:D_RNN]
    rx_ref[...] = r[:, D_RNN:2 * D_RNN]
    gates_ref[...] = jax.nn.sigmoid(r[:, 2 * D_RNN:])


def _rope_tables(pos):
    inv = ROPE_THETA ** (-jnp.arange(ROT_HALF, dtype=F32) / ROT_HALF)
    ang = pos.astype(F32)[:, None] * inv
    cos, sin = jnp.cos(ang), jnp.sin(ang)
    d = np.arange(LANES) % HEAD_DIM
    f = d % ROT_HALF
    cosr = jnp.where(d < ROT_DIM, cos[:, f], 1.0)
    sina = jnp.where(d < ROT_HALF, -sin[:, f], 0.0)
    sinb = jnp.where((d >= ROT_HALF) & (d < ROT_DIM), sin[:, f], 0.0)
    return cosr, sina, sinb, cos.T, sin.T


def _proj(x, pos, nm, wq, wkv, wr, nb, tt, tm):
    rows = nb * tt
    nt = tt // tm
    cosr, sina, sinb, cost, sint = _rope_tables(pos)
    row = lambda w: pl.BlockSpec((tm, w), lambda i: (i, 0))
    full = lambda a: pl.BlockSpec(a.shape, lambda i: (0,) * a.ndim)
    tab_r = pl.BlockSpec((tm, LANES), lambda i: (i % nt, 0))
    tab_t = pl.BlockSpec((ROT_HALF, tm), lambda i: (0, i % nt))
    heads = pl.BlockSpec((1, N_HEADS, tm, HEAD_DIM), lambda i: (i // nt, 0, i % nt, 0))
    tr = lambda r: pl.BlockSpec((1, r, tm), lambda i: (i // nt, 0, i % nt))
    out_shape = (
        jax.ShapeDtypeStruct((nb, N_HEADS, tt, HEAD_DIM), BF16),
        jax.ShapeDtypeStruct((nb, N_HEADS, tt, HEAD_DIM), BF16),
        jax.ShapeDtypeStruct((nb, 4 * KV_COLS, tt), F32),
        jax.ShapeDtypeStruct((nb, 2 * KV_COLS, tt), F32),
        jax.ShapeDtypeStruct((nb, 2 * N_KV_HEADS, HEAD_DIM, tt), BF16),
        jax.ShapeDtypeStruct((nb, 2 * N_KV_HEADS, 2 * HEAD_DIM, tt), BF16),
        jax.ShapeDtypeStruct((rows, D_RNN), F32),
        jax.ShapeDtypeStruct((rows, D_RNN), F32),
        jax.ShapeDtypeStruct((rows, N_KV_HEADS * GATE_LANES), F32),
    )
    tr4 = lambda r: pl.BlockSpec((1, 2 * N_KV_HEADS, r, tm), lambda i: (i // nt, 0, 0, i % nt))
    out_specs = (heads, heads, tr(4 * KV_COLS), tr(2 * KV_COLS), tr4(HEAD_DIM), tr4(2 * HEAD_DIM),
                 row(D_RNN), row(D_RNN), row(N_KV_HEADS * GATE_LANES))
    return pl.pallas_call(
        _proj_kernel, grid=(rows // tm,),
        in_specs=[row(D_MODEL), full(nm), full(wq), full(wkv), full(wr), tab_r, tab_r, tab_r, tab_t, tab_t],
        out_specs=out_specs, out_shape=out_shape, compiler_params=_params("parallel"), name="proj",
    )(x, nm, wq, wkv, wr, cosr, sina, sinb, cost, sint)


def _expm1(x):
    return jnp.tanh(0.5 * x) * (jnp.exp(x) + 1.0)


def _softplus(x):
    return jnp.maximum(x, 0.0) + jnp.log1p(jnp.exp(-jnp.abs(x)))


def _lru_coeffs(xc, wra_ref, wri_ref, bra_ref, bri_ref, lam_ref):
    xb = xc.astype(BF16)
    r = jax.nn.sigmoid(_nn(xb, wra_ref[...]) + bra_ref[...])
    i = jax.nn.sigmoid(_nn(xb, wri_ref[...]) + bri_ref[...])
    log_a = -LRU_C * r * _softplus(-lam_ref[...])
    a = jnp.exp(log_a)
    b = jnp.sqrt(-_expm1(2.0 * log_a)) * (i * xc)
    return a, b


def _rnn_out(rg, h, g):
    return _rms(jax.nn.gelu(rg) * h, g).astype(BF16)


def _rglru_prompt_kernel(rx_ref, rg_ref, cw_ref, cb_ref, wra_ref, wri_ref, bra_ref, bri_ref, lam_ref, g_ref,
                         y_ref, hl_ref, prev_sc, h_sc):
    tt = rx_ref.shape[0]

    @pl.when(pl.program_id(1) == 0)
    def _():
        prev_sc[...] = jnp.zeros_like(prev_sc)
        h_sc[...] = jnp.zeros_like(h_sc)

    rx = rx_ref[...]
    ext = jnp.concatenate([prev_sc[...], rx], axis=0)
    shifted = lambda d: pltpu.roll(ext, d, 0)[SUBLANES:SUBLANES + tt]
    cw = cw_ref[...]
    xc = cb_ref[...] + cw[0:1] * shifted(3)
    xc = xc + cw[1:2] * shifted(2)
    xc = xc + cw[2:3] * shifted(1)
    xc = xc + cw[3:4] * rx
    prev_sc[...] = rx[tt - SUBLANES:tt]

    a, b = _lru_coeffs(xc, wra_ref, wri_ref, bra_ref, bri_ref, lam_ref)
    row = lax.broadcasted_iota(jnp.int32, (tt, 1), 0)
    s = 1
    while s < tt:
        if s < SUBLANES:
            a_sh = jnp.where(row >= s, pltpu.roll(a, s, 0), 1.0)
            b_sh = jnp.where(row >= s, pltpu.roll(b, s, 0), 0.0)
        else:
            a_sh = jnp.concatenate([jnp.ones((s, a.shape[1]), F32), a[:tt - s]], axis=0)
            b_sh = jnp.concatenate([jnp.zeros((s, b.shape[1]), F32), b[:tt - s]], axis=0)
        b = a * b_sh + b
        a = a * a_sh
        s *= 2
    h = a * h_sc[...] + b
    h_sc[...] = h[tt - 1:tt]
    hl_ref[0] = h[tt - 1:tt]
    y_ref[...] = _rnn_out(rg_ref[...], h, g_ref[...])


def _rglru_sample_kernel(rx_ref, rg_ref, cp_ref, h0_ref, cw_ref, cb_ref, wra_ref, wri_ref, bra_ref, bri_ref,
                         lam_ref, g_ref, y_ref, cs_ref, hl_ref):
    nb = h0_ref.shape[0]
    steps = rx_ref.shape[0] // nb
    xp = [cp_ref[k] for k in range(CONV_WIDTH - 1)] + [rx_ref[t * nb:(t + 1) * nb] for t in range(steps)]
    cw, cb = cw_ref[...], cb_ref[...]
    xcs = []
    for t in range(steps):
        xc = cb + cw[0:1] * xp[t]
        for tap in range(1, CONV_WIDTH):
            xc = xc + cw[tap:tap + 1] * xp[t + tap]
        xcs.append(xc)
    a, b = _lru_coeffs(jnp.concatenate(xcs, axis=0), wra_ref, wri_ref, bra_ref, bri_ref, lam_ref)
    h = h0_ref[...]
    for t in range(steps):
        sl = slice(t * nb, (t + 1) * nb)
        h = a[sl] * h + b[sl]
        y_ref[sl] = _rnn_out(rg_ref[sl], h, g_ref[...])
    for k in range(CONV_WIDTH - 1):
        cs_ref[k] = xp[steps + k]
    hl_ref[...] = h


def _rglru_prompt(rx, rg, weights, nb, tt, chunk):
    nc = tt // chunk
    row = pl.BlockSpec((chunk, D_RNN), lambda n, c: (n * nc + c, 0))
    full = lambda a: pl.BlockSpec(a.shape, lambda n, c: (0,) * a.ndim)
    return pl.pallas_call(
        _rglru_prompt_kernel, grid=(nb, nc),
        in_specs=[row, row] + [full(w) for w in weights],
        out_specs=(row, pl.BlockSpec((1, 1, D_RNN), lambda n, c: (n, 0, 0))),
        out_shape=(jax.ShapeDtypeStruct((nb * tt, D_RNN), BF16), jax.ShapeDtypeStruct((nb, 1, D_RNN), F32)),
        scratch_shapes=[pltpu.VMEM((SUBLANES, D_RNN), F32), pltpu.VMEM((1, D_RNN), F32)],
        compiler_params=_params("parallel", "arbitrary"), name="rglru_prompt",
    )(rx, rg, *weights)


def _rglru_sample(rx, rg, conv_prev, h0, weights):
    nb = h0.shape[0]
    return pl.pallas_call(
        _rglru_sample_kernel,
        out_shape=(jax.ShapeDtypeStruct(rx.shape, BF16), jax.ShapeDtypeStruct(conv_prev.shape, F32),
                   jax.ShapeDtypeStruct((nb, D_RNN), F32)),
        compiler_params=pltpu.CompilerParams(vmem_limit_bytes=VMEM_LIMIT), name="rglru_sample",
    )(rx, rg, conv_prev, h0, *weights)


CMP_COLS = 2 * KV_COLS
HALF = CMP_BLOCK // 2
BLOCKS_PER_PAGE = PAGE_SIZE // CMP_STRIDE


def _compress_kernel(pt_ref, *refs, n_pages):
    del pt_ref
    page_refs = refs[:n_pages]
    pet_ref, perm_ref, w1_ref, w2_ref, out_ref, z_sc = refs[n_pages:]
    for k in range(n_pages):
        xt = page_refs[k][0].reshape(CMP_COLS, PAGE_SIZE)
        for half in range(2):
            a = (xt + pet_ref[half]).astype(BF16)
            z_sc[half, k] = _nt(perm_ref[...], a)
    pre = []
    for half in range(2):
        acc = jnp.zeros((N_CMP_PAD, CMP_COLS), F32)
        for j in range(CMP_STRIDE):
            rows = z_sc[half, :, j * BLOCKS_PER_PAGE:(j + 1) * BLOCKS_PER_PAGE, :]
            acc = acc + _nn(rows.reshape(N_CMP_PAD, CMP_COLS).astype(BF16), w1_ref[half, j])
        pre.append(acc)
    hid = jax.nn.gelu(pre[0] + pltpu.roll(pre[1], N_CMP_PAD - 1, 0))
    out_ref[0] = _nn(hid.astype(BF16), w2_ref[...]).astype(BF16)


def _compress(pages, page_spec, page_table, pet, w1, w2, nb):
    n_pages = page_table.shape[1]
    assert n_pages * BLOCKS_PER_PAGE == N_CMP_PAD
    pos = np.arange(PAGE_SIZE)
    perm = jnp.asarray((pos % CMP_STRIDE * BLOCKS_PER_PAGE + pos // CMP_STRIDE)[None, :]
                       == np.arange(PAGE_SIZE)[:, None], BF16)
    full = lambda a: pl.BlockSpec(a.shape, lambda n, pt: (0,) * a.ndim)
    once = lambda a: pl.BlockSpec(a.shape, lambda n, pt: (0,) * a.ndim, pipeline_mode=pl.Buffered(1))
    grid_spec = pltpu.PrefetchScalarGridSpec(
        num_scalar_prefetch=1, grid=(nb,),
        in_specs=[page_spec(k) for k in range(n_pages)] + [full(pet), full(perm), once(w1), full(w2)],
        out_specs=pl.BlockSpec((1, N_CMP_PAD, CMP_COLS), lambda n, pt: (n, 0, 0)),
        scratch_shapes=[pltpu.VMEM((2, n_pages, PAGE_SIZE, CMP_COLS), F32)])
    return pl.pallas_call(
        functools.partial(_compress_kernel, n_pages=n_pages), grid_spec=grid_spec,
        out_shape=jax.ShapeDtypeStruct((nb, N_CMP_PAD, CMP_COLS), BF16),
        compiler_params=_params("parallel"), name="compress",
    )(page_table, *([pages] * n_pages), pet, perm, w1, w2)


def _cmp_cols(cmp_ref, slot, h):
    c0 = slot * KV_COLS + h * HEAD_DIM
    return cmp_ref[0, :, c0:c0 + HEAD_DIM]


def _overlap_matrix(n_sel):
    n_cmp = N_CMP_PAD - 1
    c0 = np.arange(N_CMP_PAD)[:, None] * CMP_STRIDE
    j0 = np.arange(LANES)[None, :] * SEL_BLOCK
    ov = (c0 < j0 + SEL_BLOCK) & (c0 + CMP_BLOCK > j0)
    ov &= (np.arange(N_CMP_PAD)[:, None] < n_cmp) & (np.arange(LANES)[None, :] < n_sel)
    return ov.astype(np.float32)


def _topk_member(score, idx, n, axis):
    rank = jnp.zeros(score.shape, jnp.int32)
    for i in range(n):
        si = score[i:i + 1, :] if axis == 0 else score[:, i:i + 1]
        beats = (si > score) | ((si == score) & (i < idx))
        rank = rank + beats.astype(jnp.int32)
    return rank < TOP_N


def _sel_score(imp, j, cur):
    valid = j <= cur
    forced = (j == 0) | (j == cur) | (j == cur - 1)
    return jnp.where(valid, imp, -SEL_BONUS) + jnp.where(forced, SEL_BONUS, 0.0)


def _nsa_prompt_kernel(q_ref, qr_ref, kb_ref, vb_ref, cmp_ref, gates_ref, ovt_ref, e_ref,
                       o_ref, m_sc, acc_sc, *, tq, slc_chunk, row_block, n_sel):
    heads = range(N_KV_HEADS)
    q0 = pl.program_id(1) * tq
    qpos = q0 + lax.broadcasted_iota(jnp.int32, (tq, 1), 0)
    group = lambda ref, h: ref[0, h * GQA:(h + 1) * GQA].reshape(GQA * tq, HEAD_DIM)

    cidx = lax.broadcasted_iota(jnp.int32, (1, N_CMP_PAD), 1)
    real = cidx < N_CMP_PAD - 1
    cmask = (cidx * CMP_STRIDE + CMP_BLOCK - 1 <= qpos) & real
    qpos_t = q0 + lax.broadcasted_iota(jnp.int32, (1, tq), 1)
    j_t = lax.broadcasted_iota(jnp.int32, (n_sel, 1), 0)
    eye = (j_t == lax.broadcasted_iota(jnp.int32, (1, n_sel), 1)).astype(BF16)
    o_cmp, sel = [], []
    for h in heads:
        s = _nt(group(q_ref, h), _cmp_cols(cmp_ref, 0, h)).reshape(GQA, tq, N_CMP_PAD)
        s = jnp.where(real, jnp.where(cmask, s, NEG), -jnp.inf)
        e = jnp.exp(s - jnp.max(s, axis=-1, keepdims=True))
        p = (e / jnp.sum(e, axis=-1, keepdims=True)) * cmask.astype(F32)
        pb = p.astype(BF16)
        o_cmp.append(_nn(pb.reshape(GQA * tq, N_CMP_PAD), _cmp_cols(cmp_ref, 1, h)).reshape(GQA, tq, HEAD_DIM))
        imp_t = _nt(ovt_ref[...], pb[0])
        for g in range(1, GQA):
            imp_t = imp_t + _nt(ovt_ref[...], pb[g])
        score_t = _sel_score(imp_t, j_t, qpos_t // SEL_BLOCK)
        sel_t = _topk_member(score_t, j_t, n_sel, axis=0).astype(BF16)
        sel.append(lax.dot_general(sel_t, eye, TN_DIMS, preferred_element_type=F32).astype(BF16))

    qr4 = [group(qr_ref, h) for h in heads]

    def attend(h, k, v, allowed, first):
        n = k.shape[1]
        sc = _nn(qr4[h], k)
        bias = jnp.where(allowed, 0.0, NEG)
        pes, ms, alphas = [], [], []
        for r0 in range(0, GQA * tq, row_block):
            t0 = r0 % tq
            x = sc[r0:r0 + row_block] + bias[t0:t0 + row_block]
            m = jnp.max(x, axis=-1, keepdims=True)
            if first:
                ms.append(jnp.broadcast_to(m, (row_block, LANES)))
            else:
                m_prev = m_sc[h, r0:r0 + row_block]
                m_wide = jnp.maximum(m_prev, m)
                alphas.append(jnp.exp(m_prev - m_wide))
                ms.append(m_wide)
                m = jnp.concatenate([m_wide] * (n // LANES), axis=1)
            pes.append(jnp.exp(x - m).astype(BF16))
        pv = _nt(jnp.concatenate(pes, axis=0), v)
        m_sc[h] = jnp.concatenate(ms, axis=0)
        acc_sc[h] = pv if first else jnp.concatenate(alphas, axis=0) * acc_sc[h] + pv

    def normalised(h):
        acc = acc_sc[h].reshape(GQA, tq, 2 * HEAD_DIM)
        return acc[:, :, 0:HEAD_DIM] / acc[:, :, HEAD_DIM:HEAD_DIM + 1]

    wn = WINDOW + tq
    w0 = pl.multiple_of(jnp.maximum(q0 - WINDOW, 0), LANES)
    kpos = w0 + lax.broadcasted_iota(jnp.int32, (1, wn), 1)
    in_window = (kpos <= qpos) & (kpos > qpos - WINDOW)
    for h in heads:
        attend(h, kb_ref[0, N_KV_HEADS + h, :, pl.ds(w0, wn)], vb_ref[0, N_KV_HEADS + h, :, pl.ds(w0, wn)],
               in_window, True)
    o_win = [normalised(h) for h in heads]

    for c0 in range(0, kb_ref.shape[3], slc_chunk):
        def chunk(c0=c0):
            kpos = c0 + lax.broadcasted_iota(jnp.int32, (1, slc_chunk), 1)
            for h in heads:
                picked = _nn(sel[h], e_ref[:, c0:c0 + slc_chunk])
                attend(h, kb_ref[0, h, :, c0:c0 + slc_chunk], vb_ref[0, h, :, c0:c0 + slc_chunk],
                       (picked > 0.5) & (kpos <= qpos), c0 == 0)

        if c0 == 0:
            chunk()
        else:
            pl.when(q0 + tq > c0)(chunk)
    o_slc = [normalised(h) for h in heads]

    gt = gates_ref[...]
    for h in heads:
        for g in range(GQA):
            col = lambda b: gt[:, h * GATE_LANES + b * GQA + g:h * GATE_LANES + b * GQA + g + 1]
            o = col(0) * o_cmp[h][g] + col(1) * o_slc[h][g]
            o = o + col(2) * o_win[h][g]
            o_ref[:, (h * GQA + g) * HEAD_DIM:(h * GQA + g + 1) * HEAD_DIM] = o


def _nsa_prompt(q, qr, kb, vb, cmp, gates, nb, tt, tq, slc_chunk):
    n_sel = -(-tt // SEL_BLOCK)
    nq = tt // tq
    ovt = jnp.asarray(_overlap_matrix(n_sel)[:, :n_sel].T, BF16)
    expand = jnp.asarray(np.arange(tt)[None, :] // SEL_BLOCK == np.arange(n_sel)[:, None], BF16)
    qspec = pl.BlockSpec((1, N_HEADS, tq, HEAD_DIM), lambda n, i: (n, 0, i, 0))
    seq = lambda a: pl.BlockSpec((1,) + a.shape[1:], lambda n, i: (n,) + (0,) * (a.ndim - 1))
    full = lambda a: pl.BlockSpec(a.shape, lambda n, i: (0,) * a.ndim)
    rows = lambda w: pl.BlockSpec((tq, w), lambda n, i: (n * nq + i, 0))
    return pl.pallas_call(
        functools.partial(_nsa_prompt_kernel, tq=tq, slc_chunk=slc_chunk, row_block=32, n_sel=n_sel),
        grid=(nb, nq),
        in_specs=[qspec, qspec, seq(kb), seq(vb), seq(cmp), rows(N_KV_HEADS * GATE_LANES), full(ovt), full(expand)],
        out_specs=rows(ATTN_WIDTH), out_shape=jax.ShapeDtypeStruct((nb * tt, ATTN_WIDTH), F32),
        scratch_shapes=[pltpu.VMEM((N_KV_HEADS, GQA * tq, LANES), F32),
                        pltpu.VMEM((N_KV_HEADS, GQA * tq, 2 * HEAD_DIM), F32)],
        compiler_params=_params("parallel", "arbitrary"), name="nsa_prompt",
    )(q, qr, kb, vb, cmp, gates, ovt, expand)


def _nsa_sample_kernel(pt_ref, *refs, n_pages, steps):
    del pt_ref
    q_ref, qr_ref = refs[0:2]
    page_refs = refs[2:2 + n_pages]
    cmp_ref, cw_ref, new_ref, gates_ref, ov_ref, o_ref, wout_ref = refs[2 + n_pages:]
    rows = GQA * steps
    past = n_pages * PAGE_SIZE
    wc = cw_ref.shape[-1]
    new0 = LANES - steps
    trow = lax.broadcasted_iota(jnp.int32, (rows, 1), 0) % steps
    lane = lax.broadcasted_iota(jnp.int32, (1, LANES), 1)
    new_ok = (lane >= new0) & (lane - new0 <= trow)
    n_sel = -(-(past + steps) // SEL_BLOCK)
    cur = (past + trow) // SEL_BLOCK

    for h in range(N_KV_HEADS):
        q, qr = q_ref[0, h], qr_ref[0, h]
        row0 = h * HEAD_DIM
        new_rows = lambda slot: new_ref[0, slot * KV_COLS + row0:slot * KV_COLS + row0 + HEAD_DIM, :]

        real = lane < N_CMP_PAD - 1
        s = jnp.where(real, _nt(q, _cmp_cols(cmp_ref, 0, h)), -jnp.inf)
        e = jnp.exp(s - jnp.max(s, axis=-1, keepdims=True))
        pb = (e / jnp.sum(e, axis=-1, keepdims=True)).astype(BF16)
        o_cmp = _nn(pb, _cmp_cols(cmp_ref, 1, h))

        part = _nn(pb, ov_ref[...])
        imp = part
        for g in range(1, GQA):
            imp = imp + pltpu.roll(part, g * steps, 0)
        score = jnp.where(lane < n_sel, _sel_score(imp, lane, cur), -jnp.inf)
        sel = _topk_member(score, lane, n_sel, axis=1) & (lane < n_sel)
        sel_f = sel.astype(F32)

        ks = jnp.concatenate([page_refs[k][0, 0, h].astype(BF16) for k in range(n_pages)]
                             + [new_rows(0).astype(BF16)], axis=1)
        vs = jnp.concatenate([page_refs[k][0, 1, h].astype(BF16) for k in range(n_pages)]
                             + [new_rows(1).astype(BF16)], axis=1)
        per_tile = PAGE_SIZE // SEL_BLOCK
        tiles = []
        for k in range(n_pages):
            m = sel_f[:, per_tile * k:per_tile * k + 1]
            for b in range(1, per_tile):
                m = jnp.where(lane < b * SEL_BLOCK, m, sel_f[:, per_tile * k + b:per_tile * k + b + 1])
            tiles.append(m > 0.5)
        tiles.append((sel_f[:, n_sel - 1:n_sel] > 0.5) & new_ok)
        msk = jnp.concatenate(tiles, axis=1)
        sc = jnp.where(msk, _nn(qr, ks), NEG)
        pe = jnp.where(msk, jnp.exp(sc - jnp.max(sc, axis=-1, keepdims=True)), 0.0)
        o_slc = _nt(pe.astype(BF16), vs) / jnp.sum(pe, axis=-1, keepdims=True)

        kw = jnp.concatenate([cw_ref[0, 0, h].astype(BF16), new_rows(2).astype(BF16)], axis=1)
        vw = jnp.concatenate([cw_ref[0, 1, h].astype(BF16), new_rows(3).astype(BF16)], axis=1)
        wlane = lax.broadcasted_iota(jnp.int32, (1, wc), 1)
        wmsk = jnp.concatenate([jnp.broadcast_to(wlane + (WINDOW - wc) > trow, (rows, wc)),
                                jnp.broadcast_to(new_ok, (rows, LANES))], axis=1)
        sw = jnp.where(wmsk, _nn(qr, kw), NEG)
        pw = jnp.where(wmsk, jnp.exp(sw - jnp.max(sw, axis=-1, keepdims=True)), 0.0)
        o_win = _nt(pw.astype(BF16), vw) / jnp.sum(pw, axis=-1, keepdims=True)

        gt = gates_ref[0, h]
        o_ref[0, h] = gt[:, 0:1] * o_cmp + gt[:, 1:2] * o_slc + gt[:, 2:3] * o_win

        for slot in range(2):
            shifted = pltpu.roll(cw_ref[0, slot, h], wc - steps, 1)
            wout_ref[0, slot, h, :, 0:wc - LANES] = shifted[:, 0:wc - LANES]
            wout_ref[0, slot, h, :, wc - LANES:wc] = jnp.where(lane >= new0, new_rows(2 + slot),
                                                               shifted[:, wc - LANES:wc])


def _nsa_sample(q, qr, pages, page_table, cmp, cache_win_t, new_t, gates, steps):
    nb, n_pages = page_table.shape
    wc = cache_win_t.shape[-1]
    n_sel = -(-(n_pages * PAGE_SIZE + steps) // SEL_BLOCK)
    ov = jnp.asarray(_overlap_matrix(n_sel), BF16)
    rows = GQA * steps
    lead = lambda a: pl.BlockSpec((1,) + a.shape[1:], lambda n, pt: (n,) + (0,) * (a.ndim - 1))
    page_spec = lambda k: pl.BlockSpec((1, 2, N_KV_HEADS, HEAD_DIM, PAGE_SIZE),
                                       lambda n, pt: (pt[n, k], 1, 0, 0, 0))
    grid_spec = pltpu.PrefetchScalarGridSpec(
        num_scalar_prefetch=1, grid=(nb,),
        in_specs=[lead(q), lead(qr)] + [page_spec(k) for k in range(n_pages)]
        + [lead(cmp), lead(cache_win_t), lead(new_t), lead(gates), pl.BlockSpec(ov.shape, lambda n, pt: (0, 0))],
        out_specs=(pl.BlockSpec((1, N_KV_HEADS, rows, HEAD_DIM), lambda n, pt: (n, 0, 0, 0)), lead(cache_win_t)))
    return pl.pallas_call(
        functools.partial(_nsa_sample_kernel, n_pages=n_pages, steps=steps), grid_spec=grid_spec,
        out_shape=(jax.ShapeDtypeStruct((nb, N_KV_HEADS, rows, HEAD_DIM), F32),
                   jax.ShapeDtypeStruct(cache_win_t.shape, F32)),
        compiler_params=_params("parallel"), name="nsa_sample",
    )(page_table, q, qr, *([pages] * n_pages), cmp, cache_win_t, new_t, gates, ov)


def _mlp_kernel(x_ref, oa_ref, yr_ref, ga_ref, wo_ref, nmlp_ref, wup_ref, wdn_ref, nfin_ref, y_ref, *, ff_chunk):
    a = _rms(oa_ref[...], ga_ref[...]).astype(BF16)
    x1 = x_ref[...] + (_nn(a, wo_ref[0:ATTN_WIDTH, :]) + _nn(yr_ref[...], wo_ref[ATTN_WIDTH:, :]))
    v = _rms(x1, nmlp_ref[...]).astype(BF16)
    acc = jnp.zeros_like(x1)
    for c in range(D_FF // ff_chunk):
        sl = slice(c * ff_chunk, (c + 1) * ff_chunk)
        hid = jnp.square(jnp.maximum(_nn(v, wup_ref[:, sl]), 0.0)).astype(BF16)
        acc = acc + _nn(hid, wdn_ref[sl, :])
    y_ref[...] = _rms(x1 + acc, nfin_ref[...])


def _mlp(x, o_attn, y_rnn, g_attn, w_out, norm_mlp, w_up, w_down, norm_final, tm, ff_chunk):
    rows = x.shape[0]
    row = lambda w: pl.BlockSpec((tm, w), lambda i: (i, 0))
    full = lambda a: pl.BlockSpec(a.shape, lambda i: (0,) * a.ndim)
    once = lambda a: pl.BlockSpec(a.shape, lambda i: (0,) * a.ndim, pipeline_mode=pl.Buffered(1))
    return pl.pallas_call(
        functools.partial(_mlp_kernel, ff_chunk=ff_chunk), grid=(rows // tm,),
        in_specs=[row(D_MODEL), row(ATTN_WIDTH), row(D_RNN), full(g_attn), once(w_out), full(norm_mlp),
                  once(w_up), once(w_down), full(norm_final)],
        out_specs=row(D_MODEL), out_shape=jax.ShapeDtypeStruct((rows, D_MODEL), F32),
        compiler_params=_params("parallel"), name="mlp",
    )(x, o_attn, y_rnn, g_attn, w_out, norm_mlp, w_up, w_down, norm_final)


def _block_diag(w):
    nb, bs, _ = w.shape
    eye = jnp.eye(nb, dtype=w.dtype)
    return (eye[:, None, :, None] * w[:, :, None, :]).reshape(nb * bs, nb * bs)


def kernel(x_prompt, x_sample, cache_kv, cache_win, state_conv, state_rnn, page_table, w_in, pe_ck, w_ck1, w_ck2,
           pe_cv, w_cv1, w_cv2, g_attn, g_rnn, conv_w, conv_b, w_ra, b_ra, w_ri, b_ri, lam, w_out, norm_mix,
           norm_mlp, w_up, w_down, norm_final):
    assert w_in.shape[0] == 1, "single layer"
    nbp, tp, _ = x_prompt.shape
    nbs, steps, _ = x_sample.shape
    n_pages = page_table.shape[1]

    wt = jnp.transpose(w_in[0]).astype(BF16)
    c_kv, c_g = ATTN_WIDTH, ATTN_WIDTH + 6 * KV_COLS
    c_rg = c_g + 3 * N_HEADS
    wq, wkv = wt[0:c_kv], wt[c_kv:c_g]
    wg = wt[c_g:c_rg].reshape(3, N_KV_HEADS, GQA, D_MODEL).transpose(1, 0, 2, 3).reshape(N_KV_HEADS, 3 * GQA, D_MODEL)
    wg = jnp.pad(wg, ((0, 0), (0, GATE_LANES - 3 * GQA), (0, 0))).reshape(N_KV_HEADS * GATE_LANES, D_MODEL)
    wr = jnp.concatenate([wt[c_rg:c_rg + 2 * D_RNN], wg], axis=0)
    row2 = lambda a: a.reshape(1, -1)
    rnn_w = (conv_w[0], row2(conv_b[0]), _block_diag(w_ra[0]).astype(BF16), _block_diag(w_ri[0]).astype(BF16),
             row2(b_ra[0]), row2(b_ri[0]), row2(lam[0]), row2(g_rnn[0]))
    per_cs = lambda k, v: jnp.repeat(jnp.stack([k, v]), N_KV_HEADS, axis=0)
    pe_cs = per_cs(pe_ck[0], pe_cv[0]).reshape(2 * N_KV_HEADS, 2, HALF, HEAD_DIM)
    cmp_pet = jnp.tile(pe_cs.transpose(1, 0, 3, 2), (1, 1, 1, BLOCKS_PER_PAGE)).reshape(2, CMP_COLS, PAGE_SIZE)
    w1_cs = per_cs(w_ck1[0], w_cv1[0]).transpose(1, 0, 2, 3)
    cmp_w1 = jax.vmap(_block_diag)(w1_cs).reshape(2, HALF, CMP_COLS, CMP_COLS).astype(BF16)
    cmp_w2 = _block_diag(per_cs(w_ck2[0], w_cv2[0])).astype(BF16)
    mlp_w = (row2(g_attn[0]), w_out[0].astype(BF16), row2(norm_mlp[0]), w_up[0].astype(BF16),
             w_down[0].astype(BF16), row2(norm_final))
    nm = row2(norm_mix[0])

    tm = 512
    xp = x_prompt.reshape(nbp * tp, D_MODEL)
    q, qr, kvt, wint, kb, vb, rg, rx, gates = _proj(xp, jnp.arange(tp), nm, wq, wkv, wr, nbp, tp, tm)
    y_rnn, h_last = _rglru_prompt(rx, rg, rnn_w, nbp, tp, 256)
    own_pages = jnp.zeros((nbp, tp // PAGE_SIZE), jnp.int32)
    cmp = _compress(kvt, lambda k: pl.BlockSpec((1, CMP_COLS, PAGE_SIZE), lambda n, pt: (n, 0, k)), own_pages,
                    cmp_pet, cmp_w1, cmp_w2, nbp)
    o_attn = _nsa_prompt(q, qr, kb, vb, cmp, gates, nbp, tp, 128, 512)
    y_prompt = _mlp(xp, o_attn, y_rnn, *mlp_w, tm, 1024).reshape(nbp, tp, D_MODEL)
    wlen = min(WINDOW, tp)
    kv_prompt = kvt.reshape(nbp, 4, N_KV_HEADS, HEAD_DIM, tp).transpose(0, 4, 1, 2, 3)[None]
    win_prompt = wint[:, :, tp - wlen:].reshape(nbp, 2, N_KV_HEADS, HEAD_DIM, wlen).transpose(0, 4, 1, 2, 3)[None]
    conv_prompt = rx.reshape(nbp, tp, D_RNN)[:, tp - (CONV_WIDTH - 1):][None]
    h_prompt = h_last.reshape(1, nbp, D_RNN)

    rows_s = nbs * steps
    xs = x_sample.transpose(1, 0, 2).reshape(rows_s, D_MODEL)
    pos_s = PAST_LEN + jnp.arange(rows_s) // nbs
    q, qr, kvt, wint, _, _, rg, rx, gates = _proj(xs, pos_s, nm, wq, wkv, wr, 1, rows_s, rows_s)
    y_rnn, conv_s, h_s = _rglru_sample(rx, rg, state_conv[0].transpose(1, 0, 2), state_rnn[0], rnn_w)
    pages_t = cache_kv[0].transpose(0, 2, 3, 4, 1)
    cmp_page = lambda k: pl.BlockSpec((1, 2, N_KV_HEADS, HEAD_DIM, PAGE_SIZE), lambda n, pt: (pt[n, k], 0, 0, 0, 0))
    cmp = _compress(pages_t, cmp_page, page_table, cmp_pet, cmp_w1, cmp_w2, nbs)
    by_seq = lambda a: a.reshape(N_KV_HEADS, GQA, steps, nbs, HEAD_DIM).transpose(3, 0, 1, 2, 4).reshape(
        nbs, N_KV_HEADS, GQA * steps, HEAD_DIM)
    new_t = jnp.concatenate([kvt[0, 2 * KV_COLS:], wint[0]], axis=0).reshape(4 * KV_COLS, steps, nbs)
    new_t = jnp.pad(new_t.transpose(2, 0, 1), ((0, 0), (0, 0), (LANES - steps, 0)))
    gates_s = gates.reshape(steps, nbs, N_KV_HEADS, GATE_LANES)[..., :3 * GQA].reshape(steps, nbs, N_KV_HEADS, 3, GQA)
    gates_s = gates_s.transpose(1, 2, 4, 0, 3).reshape(nbs, N_KV_HEADS, GQA * steps, 3)
    cache_win_t = cache_win[0].transpose(0, 2, 3, 4, 1)
    o_s, win_t = _nsa_sample(by_seq(q[0]), by_seq(qr[0]), pages_t, page_table, cmp, cache_win_t, new_t, gates_s, steps)
    o_attn = o_s.reshape(nbs, N_KV_HEADS, GQA, steps, HEAD_DIM).transpose(3, 0, 1, 2, 4).reshape(rows_s, ATTN_WIDTH)
    y_sample = _mlp(xs, o_attn, y_rnn, *mlp_w, rows_s, 1024).reshape(steps, nbs, D_MODEL).transpose(1, 0, 2)
    kv_sample = kvt[0].reshape(4, N_KV_HEADS, HEAD_DIM, steps, nbs).transpose(4, 3, 0, 1, 2)[None]
    win_sample = win_t.transpose(0, 4, 1, 2, 3)[None]
    conv_sample = conv_s.transpose(1, 0, 2)[None]
    h_sample = h_s[None]

    return (y_prompt, y_sample, kv_prompt, kv_sample, win_prompt, win_sample, conv_prompt, conv_sample,
            h_prompt, h_sample)
```

```python
import functools

import jax
import jax.numpy as jnp
import numpy as np
from jax import lax
from jax.experimental import pallas as pl
from jax.experimental.pallas import tpu as pltpu

D_MODEL = 1024
PAST_LEN = 2048
PAGE_SIZE = 128
HEAD_DIM = 64
N_HEADS = 8
N_KV_HEADS = 2
GQA = N_HEADS // N_KV_HEADS
ATTN_WIDTH = N_HEADS * HEAD_DIM
D_RNN = D_MODEL - ATTN_WIDTH
RNN_BLOCKS = 8
CONV_WIDTH = 4
LRU_C = 8.0
D_FF = 4 * D_MODEL
ROT_DIM = HEAD_DIM // 4
ROT_HALF = ROT_DIM // 2
ROPE_THETA = 500000.0
CMP_BLOCK = 32
CMP_STRIDE = 16
SEL_BLOCK = 64
TOP_N = 16
WINDOW = 512
KV_COLS = N_KV_HEADS * HEAD_DIM
EPS = 1e-6
NEG = -1e30
SEL_BONUS = 1e4
SEL_MASK = 2.0 ** 100
SCALE = HEAD_DIM ** -0.5

SUBLANES = 8
LANES = 128
VMEM_LIMIT = 48 * 1024 * 1024

N_CMP_PAD = 128
GATE_LANES = LANES

F32 = jnp.float32
BF16 = jnp.bfloat16
NT_DIMS = (((1,), (1,)), ((), ()))
TN_DIMS = (((0,), (0,)), ((), ()))


def _nt(a, b):
    return lax.dot_general(a, b, NT_DIMS, preferred_element_type=F32)


def _nn(a, b):
    return jnp.dot(a, b, preferred_element_type=F32)


def _rms(x, g):
    return x * lax.rsqrt(jnp.mean(x * x, axis=-1, keepdims=True) + EPS) * g


def _params(*sem):
    return pltpu.CompilerParams(dimension_semantics=sem, vmem_limit_bytes=VMEM_LIMIT)


def _proj_kernel(x_ref, nm_ref, wq_ref, wkv_ref, wr_ref, cosr_ref, sina_ref, sinb_ref, cost_ref, sint_ref,
                 q_ref, qr_ref, kvt_ref, wint_ref, kb_ref, vb_ref, rg_ref, rx_ref, gates_ref):
    u = _rms(x_ref[...], nm_ref[...]).astype(BF16)
    q = _nt(u, wq_ref[...])
    cosr, sina, sinb = cosr_ref[...], sina_ref[...], sinb_ref[...]
    for k in range(ATTN_WIDTH // LANES):
        qg = q[:, k * LANES:(k + 1) * LANES]
        qrg = (qg * cosr + pltpu.roll(qg, LANES - ROT_HALF, 1) * sina
               + pltpu.roll(qg, ROT_HALF, 1) * sinb)
        for hh in range(LANES // HEAD_DIM):
            h = k * (LANES // HEAD_DIM) + hh
            sl = slice(hh * HEAD_DIM, (hh + 1) * HEAD_DIM)
            q_ref[0, h] = (qg[:, sl] * SCALE).astype(BF16)
            qr_ref[0, h] = (qrg[:, sl] * SCALE).astype(BF16)

    kvt = _nt(wkv_ref[...], u)
    cost, sint = cost_ref[...], sint_ref[...]
    kvt_ref[0] = kvt[0:4 * KV_COLS]
    wint_ref[0] = kvt[4 * KV_COLS:6 * KV_COLS]
    for out_ref, src0, dst0 in ((kvt_ref, 2 * KV_COLS, 2 * KV_COLS), (wint_ref, 4 * KV_COLS, 0)):
        for h in range(N_KV_HEADS):
            s, d = src0 + h * HEAD_DIM, dst0 + h * HEAD_DIM
            x1, x2 = kvt[s:s + ROT_HALF], kvt[s + ROT_HALF:s + ROT_DIM]
            out_ref[0, d:d + ROT_HALF] = x1 * cost - x2 * sint
            out_ref[0, d + ROT_HALF:d + ROT_DIM] = x2 * cost + x1 * sint
    tm = x_ref.shape[0]
    ones_row = (lax.broadcasted_iota(jnp.int32, (HEAD_DIM, tm), 0) == 0).astype(BF16)
    for branch, src_ref, k0, v0 in ((0, kvt_ref, 2 * KV_COLS, 3 * KV_COLS), (1, wint_ref, 0, KV_COLS)):
        for h in range(N_KV_HEADS):
            i, r = branch * N_KV_HEADS + h, h * HEAD_DIM
            kb_ref[0, i] = src_ref[0, k0 + r:k0 + r + HEAD_DIM].astype(BF16)
            vb_ref[0, i, 0:HEAD_DIM] = src_ref[0, v0 + r:v0 + r + HEAD_DIM].astype(BF16)
            vb_ref[0, i, HEAD_DIM:2 * HEAD_DIM] = ones_row

    r = _nt(u, wr_ref[...])
    rg_ref[...] = r[:, 0:D_RNN]
    rx_ref[...] = r[:, D_RNN:2 * D_RNN]
    gates_ref[...] = jax.nn.sigmoid(r[:, 2 * D_RNN:])


def _rope_tables(pos):
    inv = ROPE_THETA ** (-jnp.arange(ROT_HALF, dtype=F32) / ROT_HALF)
    ang = pos.astype(F32)[:, None] * inv
    cos, sin = jnp.cos(ang), jnp.sin(ang)
    d = np.arange(LANES) % HEAD_DIM
    f = d % ROT_HALF
    cosr = jnp.where(d < ROT_DIM, cos[:, f], 1.0)
    sina = jnp.where(d < ROT_HALF, -sin[:, f], 0.0)
    sinb = jnp.where((d >= ROT_HALF) & (d < ROT_DIM), sin[:, f], 0.0)
    return cosr, sina, sinb, cos.T, sin.T


def _proj(x, pos, nm, wq, wkv, wr, nb, tt, tm):
    rows = nb * tt
    nt = tt // tm
    cosr, sina, sinb, cost, sint = _rope_tables(pos)
    row = lambda w: pl.BlockSpec((tm, w), lambda i: (i, 0))
    full = lambda a: pl.BlockSpec(a.shape, lambda i: (0,) * a.ndim)
    tab_r = pl.BlockSpec((tm, LANES), lambda i: (i % nt, 0))
    tab_t = pl.BlockSpec((ROT_HALF, tm), lambda i: (0, i % nt))
    heads = pl.BlockSpec((1, N_HEADS, tm, HEAD_DIM), lambda i: (i // nt, 0, i % nt, 0))
    tr = lambda r: pl.BlockSpec((1, r, tm), lambda i: (i // nt, 0, i % nt))
    out_shape = (
        jax.ShapeDtypeStruct((nb, N_HEADS, tt, HEAD_DIM), BF16),
        jax.ShapeDtypeStruct((nb, N_HEADS, tt, HEAD_DIM), BF16),
        jax.ShapeDtypeStruct((nb, 4 * KV_COLS, tt), F32),
        jax.ShapeDtypeStruct((nb, 2 * KV_COLS, tt), F32),
        jax.ShapeDtypeStruct((nb, 2 * N_KV_HEADS, HEAD_DIM, tt), BF16),
        jax.ShapeDtypeStruct((nb, 2 * N_KV_HEADS, 2 * HEAD_DIM, tt), BF16),
        jax.ShapeDtypeStruct((rows, D_RNN), F32),
        jax.ShapeDtypeStruct((rows, D_RNN), F32),
        jax.ShapeDtypeStruct((rows, N_KV_HEADS * GATE_LANES), F32),
    )
    tr4 = lambda r: pl.BlockSpec((1, 2 * N_KV_HEADS, r, tm), lambda i: (i // nt, 0, 0, i % nt))
    out_specs = (heads, heads, tr(4 * KV_COLS), tr(2 * KV_COLS), tr4(HEAD_DIM), tr4(2 * HEAD_DIM),
                 row(D_RNN), row(D_RNN), row(N_KV_HEADS * GATE_LANES))
    return pl.pallas_call(
        _proj_kernel, grid=(rows // tm,),
        in_specs=[row(D_MODEL), full(nm), full(wq), full(wkv), full(wr), tab_r, tab_r, tab_r, tab_t, tab_t],
        out_specs=out_specs, out_shape=out_shape, compiler_params=_params("parallel"), name="proj",
    )(x, nm, wq, wkv, wr, cosr, sina, sinb, cost, sint)


def _expm1(x):
    return jnp.tanh(0.5 * x) * (jnp.exp(x) + 1.0)


def _softplus(x):
    return jnp.maximum(x, 0.0) + jnp.log1p(jnp.exp(-jnp.abs(x)))


def _lru_coeffs(xc, wra_ref, wri_ref, bra_ref, bri_ref, lam_ref):
    xb = xc.astype(BF16)
    r = jax.nn.sigmoid(_nn(xb, wra_ref[...]) + bra_ref[...])
    i = jax.nn.sigmoid(_nn(xb, wri_ref[...]) + bri_ref[...])
    log_a = -LRU_C * r * _softplus(-lam_ref[...])
    a = jnp.exp(log_a)
    b = jnp.sqrt(-_expm1(2.0 * log_a)) * (i * xc)
    return a, b


def _rnn_out(rg, h, g):
    return _rms(jax.nn.gelu(rg) * h, g).astype(BF16)


def _rglru_prompt_kernel(rx_ref, rg_ref, cw_ref, cb_ref, wra_ref, wri_ref, bra_ref, bri_ref, lam_ref, g_ref,
                         y_ref, hl_ref, prev_sc, h_sc):
    tt = rx_ref.shape[0]

    @pl.when(pl.program_id(1) == 0)
    def _():
        prev_sc[...] = jnp.zeros_like(prev_sc)
        h_sc[...] = jnp.zeros_like(h_sc)

    rx = rx_ref[...]
    ext = jnp.concatenate([prev_sc[...], rx], axis=0)
    shifted = lambda d: pltpu.roll(ext, d, 0)[SUBLANES:SUBLANES + tt]
    cw = cw_ref[...]
    xc = cb_ref[...] + cw[0:1] * shifted(3)
    xc = xc + cw[1:2] * shifted(2)
    xc = xc + cw[2:3] * shifted(1)
    xc = xc + cw[3:4] * rx
    prev_sc[...] = rx[tt - SUBLANES:tt]

    a, b = _lru_coeffs(xc, wra_ref, wri_ref, bra_ref, bri_ref, lam_ref)
    row = lax.broadcasted_iota(jnp.int32, (tt, 1), 0)
    s = 1
    while s < tt:
        if s < SUBLANES:
            a_sh = jnp.where(row >= s, pltpu.roll(a, s, 0), 1.0)
            b_sh = jnp.where(row >= s, pltpu.roll(b, s, 0), 0.0)
        else:
            a_sh = jnp.concatenate([jnp.ones((s, a.shape[1]), F32), a[:tt - s]], axis=0)
            b_sh = jnp.concatenate([jnp.zeros((s, b.shape[1]), F32), b[:tt - s]], axis=0)
        b = a * b_sh + b
        a = a * a_sh
        s *= 2
    h = a * h_sc[...] + b
    h_sc[...] = h[tt - 1:tt]
    hl_ref[0] = h[tt - 1:tt]
    y_ref[...] = _rnn_out(rg_ref[...], h, g_ref[...])


def _rglru_sample_kernel(rx_ref, rg_ref, cp_ref, h0_ref, cw_ref, cb_ref, wra_ref, wri_ref, bra_ref, bri_ref,
                         lam_ref, g_ref, y_ref, cs_ref, hl_ref):
    nb = h0_ref.shape[0]
    steps = rx_ref.shape[0] // nb
    xp = [cp_ref[k] for k in range(CONV_WIDTH - 1)] + [rx_ref[t * nb:(t + 1) * nb] for t in range(steps)]
    cw, cb = cw_ref[...], cb_ref[...]
    xcs = []
    for t in range(steps):
        xc = cb + cw[0:1] * xp[t]
        for tap in range(1, CONV_WIDTH):
            xc = xc + cw[tap:tap + 1] * xp[t + tap]
        xcs.append(xc)
    a, b = _lru_coeffs(jnp.concatenate(xcs, axis=0), wra_ref, wri_ref, bra_ref, bri_ref, lam_ref)
    h = h0_ref[...]
    for t in range(steps):
        sl = slice(t * nb, (t + 1) * nb)
        h = a[sl] * h + b[sl]
        y_ref[sl] = _rnn_out(rg_ref[sl], h, g_ref[...])
    for k in range(CONV_WIDTH - 1):
        cs_ref[k] = xp[steps + k]
    hl_ref[...] = h


def _rglru_prompt(rx, rg, weights, nb, tt, chunk):
    nc = tt // chunk
    row = pl.BlockSpec((chunk, D_RNN), lambda n, c: (n * nc + c, 0))
    full = lambda a: pl.BlockSpec(a.shape, lambda n, c: (0,) * a.ndim)
    return pl.pallas_call(
        _rglru_prompt_kernel, grid=(nb, nc),
        in_specs=[row, row] + [full(w) for w in weights],
        out_specs=(row, pl.BlockSpec((1, 1, D_RNN), lambda n, c: (n, 0, 0))),
        out_shape=(jax.ShapeDtypeStruct((nb * tt, D_RNN), BF16), jax.ShapeDtypeStruct((nb, 1, D_RNN), F32)),
        scratch_shapes=[pltpu.VMEM((SUBLANES, D_RNN), F32), pltpu.VMEM((1, D_RNN), F32)],
        compiler_params=_params("parallel", "arbitrary"), name="rglru_prompt",
    )(rx, rg, *weights)


def _rglru_sample(rx, rg, conv_prev, h0, weights):
    nb = h0.shape[0]
    return pl.pallas_call(
        _rglru_sample_kernel,
        out_shape=(jax.ShapeDtypeStruct(rx.shape, BF16), jax.ShapeDtypeStruct(conv_prev.shape, F32),
                   jax.ShapeDtypeStruct((nb, D_RNN), F32)),
        compiler_params=pltpu.CompilerParams(vmem_limit_bytes=VMEM_LIMIT), name="rglru_sample",
    )(rx, rg, conv_prev, h0, *weights)


CMP_COLS = 2 * KV_COLS
HALF = CMP_BLOCK // 2
BLOCKS_PER_PAGE = PAGE_SIZE // CMP_STRIDE


def _compress_kernel(pt_ref, *refs, n_pages):
    del pt_ref
    page_refs = refs[:n_pages]
    pet_ref, perm_ref, w1_ref, w2_ref, out_ref, z_sc = refs[n_pages:]
    for k in range(n_pages):
        xt = page_refs[k][0].reshape(CMP_COLS, PAGE_SIZE)
        for half in range(2):
            a = (xt + pet_ref[half]).astype(BF16)
            z_sc[half, k] = _nt(perm_ref[...], a)
    pre = []
    for half in range(2):
        acc = jnp.zeros((N_CMP_PAD, CMP_COLS), F32)
        for j in range(CMP_STRIDE):
            rows = z_sc[half, :, j * BLOCKS_PER_PAGE:(j + 1) * BLOCKS_PER_PAGE, :]
            acc = acc + _nn(rows.reshape(N_CMP_PAD, CMP_COLS).astype(BF16), w1_ref[half, j])
        pre.append(acc)
    hid = jax.nn.gelu(pre[0] + pltpu.roll(pre[1], N_CMP_PAD - 1, 0))
    out_ref[0] = _nn(hid.astype(BF16), w2_ref[...]).astype(BF16)


def _compress(pages, page_spec, page_table, pet, w1, w2, nb):
    n_pages = page_table.shape[1]
    assert n_pages * BLOCKS_PER_PAGE == N_CMP_PAD
    pos = np.arange(PAGE_SIZE)
    perm = jnp.asarray((pos % CMP_STRIDE * BLOCKS_PER_PAGE + pos // CMP_STRIDE)[None, :]
                       == np.arange(PAGE_SIZE)[:, None], BF16)
    full = lambda a: pl.BlockSpec(a.shape, lambda n, pt: (0,) * a.ndim)
    once = lambda a: pl.BlockSpec(a.shape, lambda n, pt: (0,) * a.ndim, pipeline_mode=pl.Buffered(1))
    grid_spec = pltpu.PrefetchScalarGridSpec(
        num_scalar_prefetch=1, grid=(nb,),
        in_specs=[page_spec(k) for k in range(n_pages)] + [full(pet), full(perm), once(w1), full(w2)],
        out_specs=pl.BlockSpec((1, N_CMP_PAD, CMP_COLS), lambda n, pt: (n, 0, 0)),
        scratch_shapes=[pltpu.VMEM((2, n_pages, PAGE_SIZE, CMP_COLS), F32)])
    return pl.pallas_call(
        functools.partial(_compress_kernel, n_pages=n_pages), grid_spec=grid_spec,
        out_shape=jax.ShapeDtypeStruct((nb, N_CMP_PAD, CMP_COLS), BF16),
        compiler_params=_params("parallel"), name="compress",
    )(page_table, *([pages] * n_pages), pet, perm, w1, w2)


def _cmp_cols(cmp_ref, i, slot, h):
    c0 = slot * KV_COLS + h * HEAD_DIM
    return cmp_ref[i, :, c0:c0 + HEAD_DIM]


def _overlap_matrix(n_sel):
    n_cmp = N_CMP_PAD - 1
    c0 = np.arange(N_CMP_PAD)[:, None] * CMP_STRIDE
    j0 = np.arange(LANES)[None, :] * SEL_BLOCK
    ov = (c0 < j0 + SEL_BLOCK) & (c0 + CMP_BLOCK > j0)
    ov &= (np.arange(N_CMP_PAD)[:, None] < n_cmp) & (np.arange(LANES)[None, :] < n_sel)
    return ov.astype(np.float32)


def _topk_member(score, idx, n, axis):
    rank = jnp.zeros(score.shape, jnp.int32)
    for i in range(n):
        si = score[i:i + 1, :] if axis == 0 else score[:, i:i + 1]
        beats = (si > score) | ((si == score) & (i < idx))
        rank = rank + beats.astype(jnp.int32)
    return rank < TOP_N


def _sel_score(imp, j, cur):
    valid = j <= cur
    forced = (j == 0) | (j == cur) | (j == cur - 1)
    return jnp.where(valid, imp, -SEL_BONUS) + jnp.where(forced, SEL_BONUS, 0.0)


def _nsa_prompt_kernel(q_ref, qr_ref, kb_ref, vb_ref, cmp_ref, gates_ref, ovt_ref, e_ref,
                       o_ref, m_sc, acc_sc, *, tq, slc_chunk, row_block, n_sel):
    heads = range(N_KV_HEADS)
    q0 = pl.program_id(1) * tq
    qpos = q0 + lax.broadcasted_iota(jnp.int32, (tq, 1), 0)
    group = lambda ref, h: ref[0, h * GQA:(h + 1) * GQA].reshape(GQA * tq, HEAD_DIM)

    cidx = lax.broadcasted_iota(jnp.int32, (1, N_CMP_PAD), 1)
    real = cidx < N_CMP_PAD - 1
    cmask = (cidx * CMP_STRIDE + CMP_BLOCK - 1 <= qpos) & real
    qpos_t = q0 + lax.broadcasted_iota(jnp.int32, (1, tq), 1)
    j_t = lax.broadcasted_iota(jnp.int32, (n_sel, 1), 0)
    eye = (j_t == lax.broadcasted_iota(jnp.int32, (1, n_sel), 1)).astype(BF16)
    o_cmp, sel = [], []
    for h in heads:
        s = _nt(group(q_ref, h), _cmp_cols(cmp_ref, 0, 0, h)).reshape(GQA, tq, N_CMP_PAD)
        s = jnp.where(real, jnp.where(cmask, s, NEG), -jnp.inf)
        e = jnp.exp(s - jnp.max(s, axis=-1, keepdims=True))
        p = (e / jnp.sum(e, axis=-1, keepdims=True)) * cmask.astype(F32)
        pb = p.astype(BF16)
        o_cmp.append(_nn(pb.reshape(GQA * tq, N_CMP_PAD), _cmp_cols(cmp_ref, 0, 1, h)).reshape(GQA, tq, HEAD_DIM))
        imp_t = _nt(ovt_ref[...], pb[0])
        for g in range(1, GQA):
            imp_t = imp_t + _nt(ovt_ref[...], pb[g])
        score_t = _sel_score(imp_t, j_t, qpos_t // SEL_BLOCK)
        sel_t = _topk_member(score_t, j_t, n_sel, axis=0).astype(BF16)
        sel.append(lax.dot_general(sel_t, eye, TN_DIMS, preferred_element_type=F32))

    qr4 = [group(qr_ref, h) for h in heads]
    pad_cols = jnp.zeros((GQA * tq, HEAD_DIM - n_sel), BF16)
    unpicked = [jnp.where(s > 0.5, 0.0, -SEL_MASK).astype(BF16) for s in sel]
    qsel4 = [jnp.concatenate([qr4[h], jnp.concatenate([unpicked[h]] * GQA, axis=0), pad_cols], axis=1)
             for h in heads]

    def attend(h, qx, k, v, allowed, first):
        n = k.shape[1]
        sc = _nn(qx, k)
        bias = jnp.where(allowed, 0.0, NEG)
        pes, ms, alphas = [], [], []
        for r0 in range(0, GQA * tq, row_block):
            t0 = r0 % tq
            x = sc[r0:r0 + row_block] + bias[t0:t0 + row_block]
            m = jnp.max(x, axis=-1, keepdims=True)
            if first:
                ms.append(jnp.broadcast_to(m, (row_block, LANES)))
            else:
                m_prev = m_sc[h, r0:r0 + row_block]
                m_wide = jnp.maximum(m_prev, m)
                alphas.append(jnp.exp(m_prev - m_wide))
                ms.append(m_wide)
                m = jnp.concatenate([m_wide] * (n // LANES), axis=1)
            pes.append(jnp.exp(x - m).astype(BF16))
        pv = _nt(jnp.concatenate(pes, axis=0), v)
        m_sc[h] = jnp.concatenate(ms, axis=0)
        acc_sc[h] = pv if first else jnp.concatenate(alphas, axis=0) * acc_sc[h] + pv

    def normalised(h):
        acc = acc_sc[h].reshape(GQA, tq, 2 * HEAD_DIM)
        return acc[:, :, 0:HEAD_DIM] / acc[:, :, HEAD_DIM:HEAD_DIM + 1]

    wn = WINDOW + tq
    w0 = pl.multiple_of(jnp.maximum(q0 - WINDOW, 0), LANES)
    kpos = w0 + lax.broadcasted_iota(jnp.int32, (1, wn), 1)
    in_window = (kpos <= qpos) & (kpos > qpos - WINDOW)
    for h in heads:
        attend(h, qr4[h], kb_ref[0, N_KV_HEADS + h, :, pl.ds(w0, wn)], vb_ref[0, N_KV_HEADS + h, :, pl.ds(w0, wn)],
               in_window, True)
    o_win = [normalised(h) for h in heads]

    for c0 in range(0, kb_ref.shape[3], slc_chunk):
        def chunk(c0=c0):
            kpos = c0 + lax.broadcasted_iota(jnp.int32, (1, slc_chunk), 1)
            member = jnp.concatenate([e_ref[:, c0:c0 + slc_chunk],
                                      jnp.zeros((HEAD_DIM - n_sel, slc_chunk), BF16)], axis=0)
            for h in heads:
                k = jnp.concatenate([kb_ref[0, h, :, c0:c0 + slc_chunk], member], axis=0)
                attend(h, qsel4[h], k, vb_ref[0, h, :, c0:c0 + slc_chunk], kpos <= qpos, c0 == 0)

        if c0 == 0:
            chunk()
        else:
            pl.when(q0 + tq > c0)(chunk)
    o_slc = [normalised(h) for h in heads]

    gt = gates_ref[...]
    for h in heads:
        for g in range(GQA):
            col = lambda b: gt[:, h * GATE_LANES + b * GQA + g:h * GATE_LANES + b * GQA + g + 1]
            o = col(0) * o_cmp[h][g] + col(1) * o_slc[h][g]
            o = o + col(2) * o_win[h][g]
            o_ref[:, (h * GQA + g) * HEAD_DIM:(h * GQA + g + 1) * HEAD_DIM] = o


def _nsa_prompt(q, qr, kb, vb, cmp, gates, nb, tt, tq, slc_chunk):
    n_sel = -(-tt // SEL_BLOCK)
    nq = tt // tq
    ovt = jnp.asarray(_overlap_matrix(n_sel)[:, :n_sel].T, BF16)
    expand = jnp.asarray(np.arange(tt)[None, :] // SEL_BLOCK == np.arange(n_sel)[:, None], BF16)
    qspec = pl.BlockSpec((1, N_HEADS, tq, HEAD_DIM), lambda n, i: (n, 0, i, 0))
    seq = lambda a: pl.BlockSpec((1,) + a.shape[1:], lambda n, i: (n,) + (0,) * (a.ndim - 1))
    full = lambda a: pl.BlockSpec(a.shape, lambda n, i: (0,) * a.ndim)
    rows = lambda w: pl.BlockSpec((tq, w), lambda n, i: (n * nq + i, 0))
    return pl.pallas_call(
        functools.partial(_nsa_prompt_kernel, tq=tq, slc_chunk=slc_chunk, row_block=32, n_sel=n_sel),
        grid=(nb, nq),
        in_specs=[qspec, qspec, seq(kb), seq(vb), seq(cmp), rows(N_KV_HEADS * GATE_LANES), full(ovt), full(expand)],
        out_specs=rows(ATTN_WIDTH), out_shape=jax.ShapeDtypeStruct((nb * tt, ATTN_WIDTH), F32),
        scratch_shapes=[pltpu.VMEM((N_KV_HEADS, GQA * tq, LANES), F32),
                        pltpu.VMEM((N_KV_HEADS, GQA * tq, 2 * HEAD_DIM), F32)],
        compiler_params=_params("parallel", "arbitrary"), name="nsa_prompt",
    )(q, qr, kb, vb, cmp, gates, ovt, expand)


def _nsa_sample_kernel(pt_ref, *refs, n_pages, steps, seqs):
    del pt_ref
    q_ref, qr_ref = refs[0:2]
    page_refs = refs[2:2 + seqs * n_pages]
    cmp_ref, cw_ref, newk_ref, newv_ref, gates_ref, ov_ref, o_ref, wout_ref = refs[2 + seqs * n_pages:]
    rows = GQA * steps
    past = n_pages * PAGE_SIZE
    wc = cw_ref.shape[-1]
    trow = lax.broadcasted_iota(jnp.int32, (rows, 1), 0) % steps
    lane = lax.broadcasted_iota(jnp.int32, (1, LANES), 1)
    n_sel = -(-(past + steps) // SEL_BLOCK)
    cur = (past + trow) // SEL_BLOCK
    real = lane < N_CMP_PAD - 1
    wlane = lax.broadcasted_iota(jnp.int32, (1, wc), 1)
    in_window = jnp.broadcast_to(wlane + (WINDOW - wc) > trow, (rows, wc))
    per_tile = PAGE_SIZE // SEL_BLOCK

    for i in range(seqs):
        newk, newv = newk_ref[i], newv_ref[i]
        newk_b, newv_b = newk.astype(BF16), newv.astype(BF16)
        for h in range(N_KV_HEADS):
            q, qr = q_ref[i, h], qr_ref[i, h]
            pages = page_refs[i * n_pages:(i + 1) * n_pages]

            def new_ok(branch):
                base = (branch * N_KV_HEADS + h) * steps
                return (lane >= base) & (lane < base + steps) & (lane - base <= trow)

            s = jnp.where(real, _nt(q, _cmp_cols(cmp_ref, i, 0, h)), -jnp.inf)
            e = jnp.exp(s - jnp.max(s, axis=-1, keepdims=True))
            pb = (e / jnp.sum(e, axis=-1, keepdims=True)).astype(BF16)
            o_cmp = _nn(pb, _cmp_cols(cmp_ref, i, 1, h))

            part = _nn(pb, ov_ref[...])
            imp = part
            for g in range(1, GQA):
                imp = imp + pltpu.roll(part, g * steps, 0)
            score = jnp.where(lane < n_sel, _sel_score(imp, lane, cur), -jnp.inf)
            sel = _topk_member(score, lane, n_sel, axis=1) & (lane < n_sel)
            sel_f = sel.astype(F32)

            ks = jnp.concatenate([p[0, 0, h].astype(BF16) for p in pages] + [newk_b], axis=1)
            vs = jnp.concatenate([p[0, 1, h].astype(BF16) for p in pages] + [newv_b], axis=1)
            tiles = []
            for k in range(n_pages):
                m = sel_f[:, per_tile * k:per_tile * k + 1]
                for b in range(1, per_tile):
                    m = jnp.where(lane < b * SEL_BLOCK, m, sel_f[:, per_tile * k + b:per_tile * k + b + 1])
                tiles.append(m > 0.5)
            tiles.append((sel_f[:, n_sel - 1:n_sel] > 0.5) & new_ok(0))
            msk = jnp.concatenate(tiles, axis=1)
            sc = jnp.where(msk, _nn(qr, ks), NEG)
            pe = jnp.where(msk, jnp.exp(sc - jnp.max(sc, axis=-1, keepdims=True)), 0.0)
            o_slc = _nt(pe.astype(BF16), vs) / jnp.sum(pe, axis=-1, keepdims=True)

            kw = jnp.concatenate([cw_ref[i, 0, h].astype(BF16), newk_b], axis=1)
            vw = jnp.concatenate([cw_ref[i, 1, h].astype(BF16), newv_b], axis=1)
            wmsk = jnp.concatenate([in_window, jnp.broadcast_to(new_ok(1), (rows, LANES))], axis=1)
            sw = jnp.where(wmsk, _nn(qr, kw), NEG)
            pw = jnp.where(wmsk, jnp.exp(sw - jnp.max(sw, axis=-1, keepdims=True)), 0.0)
            o_win = _nt(pw.astype(BF16), vw) / jnp.sum(pw, axis=-1, keepdims=True)

            gt = gates_ref[i, h]
            o_ref[i, h] = gt[:, 0:1] * o_cmp + gt[:, 1:2] * o_slc + gt[:, 2:3] * o_win

            to_end = (LANES - steps - (N_KV_HEADS + h) * steps) % LANES
            for slot, new in ((0, newk), (1, newv)):
                shifted = pltpu.roll(cw_ref[i, slot, h], wc - steps, 1)
                wout_ref[i, slot, h, :, 0:wc - LANES] = shifted[:, 0:wc - LANES]
                wout_ref[i, slot, h, :, wc - LANES:wc] = jnp.where(lane >= LANES - steps, pltpu.roll(new, to_end, 1),
                                                                   shifted[:, wc - LANES:wc])


def _nsa_sample(q, qr, pages, page_table, cmp, cache_win_t, new_k, new_v, gates, steps, seqs):
    nb, n_pages = page_table.shape
    n_sel = -(-(n_pages * PAGE_SIZE + steps) // SEL_BLOCK)
    ov = jnp.asarray(_overlap_matrix(n_sel), BF16)
    rows = GQA * steps
    lead = lambda a: pl.BlockSpec((seqs,) + a.shape[1:], lambda n, pt: (n,) + (0,) * (a.ndim - 1))
    page_spec = lambda i, k: pl.BlockSpec((1, 2, N_KV_HEADS, HEAD_DIM, PAGE_SIZE),
                                          lambda n, pt: (pt[n * seqs + i, k], 1, 0, 0, 0))
    page_specs = [page_spec(i, k) for i in range(seqs) for k in range(n_pages)]
    grid_spec = pltpu.PrefetchScalarGridSpec(
        num_scalar_prefetch=1, grid=(nb // seqs,),
        in_specs=[lead(q), lead(qr)] + page_specs
        + [lead(cmp), lead(cache_win_t), lead(new_k), lead(new_v), lead(gates),
           pl.BlockSpec(ov.shape, lambda n, pt: (0, 0))],
        out_specs=(pl.BlockSpec((seqs, N_KV_HEADS, rows, HEAD_DIM), lambda n, pt: (n, 0, 0, 0)), lead(cache_win_t)))
    return pl.pallas_call(
        functools.partial(_nsa_sample_kernel, n_pages=n_pages, steps=steps, seqs=seqs), grid_spec=grid_spec,
        out_shape=(jax.ShapeDtypeStruct((nb, N_KV_HEADS, rows, HEAD_DIM), F32),
                   jax.ShapeDtypeStruct(cache_win_t.shape, F32)),
        compiler_params=_params("parallel"), name="nsa_sample",
    )(page_table, q, qr, *([pages] * len(page_specs)), cmp, cache_win_t, new_k, new_v, gates, ov)


def _mlp_kernel(x_ref, oa_ref, yr_ref, ga_ref, wo_ref, nmlp_ref, wup_ref, wdn_ref, nfin_ref, y_ref, *, ff_chunk):
    a = _rms(oa_ref[...], ga_ref[...]).astype(BF16)
    x1 = x_ref[...] + (_nn(a, wo_ref[0:ATTN_WIDTH, :]) + _nn(yr_ref[...], wo_ref[ATTN_WIDTH:, :]))
    v = _rms(x1, nmlp_ref[...]).astype(BF16)
    acc = jnp.zeros_like(x1)
    for c in range(D_FF // ff_chunk):
        sl = slice(c * ff_chunk, (c + 1) * ff_chunk)
        hid = jnp.square(jnp.maximum(_nn(v, wup_ref[:, sl]), 0.0)).astype(BF16)
        acc = acc + _nn(hid, wdn_ref[sl, :])
    y_ref[...] = _rms(x1 + acc, nfin_ref[...])


def _mlp(x, o_attn, y_rnn, g_attn, w_out, norm_mlp, w_up, w_down, norm_final, tm, ff_chunk):
    rows = x.shape[0]
    row = lambda w: pl.BlockSpec((tm, w), lambda i: (i, 0))
    full = lambda a: pl.BlockSpec(a.shape, lambda i: (0,) * a.ndim)
    once = lambda a: pl.BlockSpec(a.shape, lambda i: (0,) * a.ndim, pipeline_mode=pl.Buffered(1))
    return pl.pallas_call(
        functools.partial(_mlp_kernel, ff_chunk=ff_chunk), grid=(rows // tm,),
        in_specs=[row(D_MODEL), row(ATTN_WIDTH), row(D_RNN), full(g_attn), once(w_out), full(norm_mlp),
                  once(w_up), once(w_down), full(norm_final)],
        out_specs=row(D_MODEL), out_shape=jax.ShapeDtypeStruct((rows, D_MODEL), F32),
        compiler_params=_params("parallel"), name="mlp",
    )(x, o_attn, y_rnn, g_attn, w_out, norm_mlp, w_up, w_down, norm_final)


def _block_diag(w):
    nb, bs, _ = w.shape
    eye = jnp.eye(nb, dtype=w.dtype)
    return (eye[:, None, :, None] * w[:, :, None, :]).reshape(nb * bs, nb * bs)


def kernel(x_prompt, x_sample, cache_kv, cache_win, state_conv, state_rnn, page_table, w_in, pe_ck, w_ck1, w_ck2,
           pe_cv, w_cv1, w_cv2, g_attn, g_rnn, conv_w, conv_b, w_ra, b_ra, w_ri, b_ri, lam, w_out, norm_mix,
           norm_mlp, w_up, w_down, norm_final):
    assert w_in.shape[0] == 1, "single layer"
    nbp, tp, _ = x_prompt.shape
    nbs, steps, _ = x_sample.shape
    n_pages = page_table.shape[1]

    wt = jnp.transpose(w_in[0]).astype(BF16)
    c_kv, c_g = ATTN_WIDTH, ATTN_WIDTH + 6 * KV_COLS
    c_rg = c_g + 3 * N_HEADS
    wq, wkv = wt[0:c_kv], wt[c_kv:c_g]
    wg = wt[c_g:c_rg].reshape(3, N_KV_HEADS, GQA, D_MODEL).transpose(1, 0, 2, 3).reshape(N_KV_HEADS, 3 * GQA, D_MODEL)
    wg = jnp.pad(wg, ((0, 0), (0, GATE_LANES - 3 * GQA), (0, 0))).reshape(N_KV_HEADS * GATE_LANES, D_MODEL)
    wr = jnp.concatenate([wt[c_rg:c_rg + 2 * D_RNN], wg], axis=0)
    row2 = lambda a: a.reshape(1, -1)
    rnn_w = (conv_w[0], row2(conv_b[0]), _block_diag(w_ra[0].astype(BF16)), _block_diag(w_ri[0].astype(BF16)),
             row2(b_ra[0]), row2(b_ri[0]), row2(lam[0]), row2(g_rnn[0]))
    per_cs = lambda k, v: jnp.repeat(jnp.stack([k, v]), N_KV_HEADS, axis=0)
    pe_cs = per_cs(pe_ck[0], pe_cv[0]).reshape(2 * N_KV_HEADS, 2, HALF, HEAD_DIM)
    cmp_pet = jnp.tile(pe_cs.transpose(1, 0, 3, 2), (1, 1, 1, BLOCKS_PER_PAGE)).reshape(2, CMP_COLS, PAGE_SIZE)
    w1_cs = per_cs(w_ck1[0], w_cv1[0]).transpose(1, 0, 2, 3)
    cmp_w1 = jax.vmap(_block_diag)(w1_cs.astype(BF16)).reshape(2, HALF, CMP_COLS, CMP_COLS)
    cmp_w2 = _block_diag(per_cs(w_ck2[0], w_cv2[0]).astype(BF16))
    mlp_w = (row2(g_attn[0]), w_out[0].astype(BF16), row2(norm_mlp[0]), w_up[0].astype(BF16),
             w_down[0].astype(BF16), row2(norm_final))
    nm = row2(norm_mix[0])

    tm = 512
    xp = x_prompt.reshape(nbp * tp, D_MODEL)
    q, qr, kvt, wint, kb, vb, rg, rx, gates = _proj(xp, jnp.arange(tp), nm, wq, wkv, wr, nbp, tp, tm)
    y_rnn, h_last = _rglru_prompt(rx, rg, rnn_w, nbp, tp, 256)
    own_pages = jnp.zeros((nbp, tp // PAGE_SIZE), jnp.int32)
    cmp = _compress(kvt, lambda k: pl.BlockSpec((1, CMP_COLS, PAGE_SIZE), lambda n, pt: (n, 0, k)), own_pages,
                    cmp_pet, cmp_w1, cmp_w2, nbp)
    o_attn = _nsa_prompt(q, qr, kb, vb, cmp, gates, nbp, tp, 128, 512)
    y_prompt = _mlp(xp, o_attn, y_rnn, *mlp_w, tm, 1024).reshape(nbp, tp, D_MODEL)
    wlen = min(WINDOW, tp)
    kv_prompt = kvt.reshape(nbp, 4, N_KV_HEADS, HEAD_DIM, tp).transpose(0, 4, 1, 2, 3)[None]
    win_prompt = wint[:, :, tp - wlen:].reshape(nbp, 2, N_KV_HEADS, HEAD_DIM, wlen).transpose(0, 4, 1, 2, 3)[None]
    conv_prompt = rx.reshape(nbp, tp, D_RNN)[:, tp - (CONV_WIDTH - 1):][None]
    h_prompt = h_last.reshape(1, nbp, D_RNN)

    rows_s = nbs * steps
    xs = x_sample.transpose(1, 0, 2).reshape(rows_s, D_MODEL)
    pos_s = PAST_LEN + jnp.arange(rows_s) // nbs
    q, qr, kvt, wint, _, _, rg, rx, gates = _proj(xs, pos_s, nm, wq, wkv, wr, 1, rows_s, rows_s)
    y_rnn, conv_s, h_s = _rglru_sample(rx, rg, state_conv[0].transpose(1, 0, 2), state_rnn[0], rnn_w)
    pages_t = cache_kv[0].transpose(0, 2, 3, 4, 1)
    cmp_page = lambda k: pl.BlockSpec((1, 2, N_KV_HEADS, HEAD_DIM, PAGE_SIZE), lambda n, pt: (pt[n, k], 0, 0, 0, 0))
    cmp = _compress(pages_t, cmp_page, page_table, cmp_pet, cmp_w1, cmp_w2, nbs)
    by_seq = lambda a: a.reshape(N_KV_HEADS, GQA, steps, nbs, HEAD_DIM).transpose(3, 0, 1, 2, 4).reshape(
        nbs, N_KV_HEADS, GQA * steps, HEAD_DIM)
    def new_tile(slc_rows, win_rows):
        t = jnp.stack([slc_rows, win_rows]).reshape(2 * N_KV_HEADS, HEAD_DIM, steps, nbs)
        t = t.transpose(3, 1, 0, 2).reshape(nbs, HEAD_DIM, 2 * N_KV_HEADS * steps)
        return jnp.pad(t, ((0, 0), (0, 0), (0, LANES - 2 * N_KV_HEADS * steps)))
    new_k = new_tile(kvt[0, 2 * KV_COLS:3 * KV_COLS], wint[0, 0:KV_COLS])
    new_v = new_tile(kvt[0, 3 * KV_COLS:4 * KV_COLS], wint[0, KV_COLS:2 * KV_COLS])
    gates_s = gates.reshape(steps, nbs, N_KV_HEADS, GATE_LANES)[..., :3 * GQA].reshape(steps, nbs, N_KV_HEADS, 3, GQA)
    gates_s = gates_s.transpose(1, 2, 4, 0, 3).reshape(nbs, N_KV_HEADS, GQA * steps, 3)
    cache_win_t = cache_win[0].transpose(0, 2, 3, 4, 1)
    o_s, win_t = _nsa_sample(by_seq(q[0]), by_seq(qr[0]), pages_t, page_table, cmp, cache_win_t, new_k, new_v,
                             gates_s, steps, 2)
    o_attn = o_s.reshape(nbs, N_KV_HEADS, GQA, steps, HEAD_DIM).transpose(3, 0, 1, 2, 4).reshape(rows_s, ATTN_WIDTH)
    y_sample = _mlp(xs, o_attn, y_rnn, *mlp_w, rows_s, 1024).reshape(steps, nbs, D_MODEL).transpose(1, 0, 2)
    kv_sample = kvt[0].reshape(4, N_KV_HEADS, HEAD_DIM, steps, nbs).transpose(4, 3, 0, 1, 2)[None]
    win_sample = win_t.transpose(0, 4, 1, 2, 3)[None]
    conv_sample = conv_s.transpose(1, 0, 2)[None]
    h_sample = h_s[None]

    return (y_prompt, y_sample, kv_prompt, kv_sample, win_prompt, win_sample, conv_prompt, conv_sample,
            h_prompt, h_sample)
```

```python
import functools

import jax
import jax.numpy as jnp
import numpy as np
from jax import lax
from jax.experimental import pallas as pl
from jax.experimental.pallas import tpu as pltpu

D_MODEL = 1024
PAST_LEN = 2048
PAGE_SIZE = 128
HEAD_DIM = 64
N_HEADS = 8
N_KV_HEADS = 2
GQA = N_HEADS // N_KV_HEADS
ATTN_WIDTH = N_HEADS * HEAD_DIM
D_RNN = D_MODEL - ATTN_WIDTH
RNN_BLOCKS = 8
CONV_WIDTH = 4
LRU_C = 8.0
D_FF = 4 * D_MODEL
ROT_DIM = HEAD_DIM // 4
ROT_HALF = ROT_DIM // 2
ROPE_THETA = 500000.0
CMP_BLOCK = 32
CMP_STRIDE = 16
SEL_BLOCK = 64
TOP_N = 16
WINDOW = 512
KV_COLS = N_KV_HEADS * HEAD_DIM
EPS = 1e-6
NEG = -1e30
SEL_BONUS = 1e4
SEL_MASK = 2.0 ** 100
SCALE = HEAD_DIM ** -0.5

SUBLANES = 8
LANES = 128
VMEM_LIMIT = 48 * 1024 * 1024

N_CMP_PAD = 128
GATE_LANES = LANES

F32 = jnp.float32
BF16 = jnp.bfloat16
NT_DIMS = (((1,), (1,)), ((), ()))
TN_DIMS = (((0,), (0,)), ((), ()))


def _nt(a, b):
    return lax.dot_general(a, b, NT_DIMS, preferred_element_type=F32)


def _nn(a, b):
    return jnp.dot(a, b, preferred_element_type=F32)


def _rms(x, g):
    return x * lax.rsqrt(jnp.mean(x * x, axis=-1, keepdims=True) + EPS) * g


def _params(*sem):
    return pltpu.CompilerParams(dimension_semantics=sem, vmem_limit_bytes=VMEM_LIMIT)


def _proj_kernel(x_ref, nm_ref, wq_ref, wkv_ref, wr_ref, cosr_ref, sina_ref, sinb_ref, cost_ref, sint_ref,
                 q_ref, qr_ref, kvt_ref, wint_ref, kb_ref, vb_ref, rg_ref, rx_ref, gates_ref):
    u = _rms(x_ref[...], nm_ref[...]).astype(BF16)
    q = _nt(u, wq_ref[...])
    cosr, sina, sinb = cosr_ref[...], sina_ref[...], sinb_ref[...]
    for k in range(ATTN_WIDTH // LANES):
        qg = q[:, k * LANES:(k + 1) * LANES]
        qrg = (qg * cosr + pltpu.roll(qg, LANES - ROT_HALF, 1) * sina
               + pltpu.roll(qg, ROT_HALF, 1) * sinb)
        for hh in range(LANES // HEAD_DIM):
            h = k * (LANES // HEAD_DIM) + hh
            sl = slice(hh * HEAD_DIM, (hh + 1) * HEAD_DIM)
            q_ref[0, h] = (qg[:, sl] * SCALE).astype(BF16)
            qr_ref[0, h] = (qrg[:, sl] * SCALE).astype(BF16)

    kvt = _nt(wkv_ref[...], u)
    cost, sint = cost_ref[...], sint_ref[...]
    kvt_ref[0] = kvt[0:4 * KV_COLS]
    wint_ref[0] = kvt[4 * KV_COLS:6 * KV_COLS]
    for out_ref, src0, dst0 in ((kvt_ref, 2 * KV_COLS, 2 * KV_COLS), (wint_ref, 4 * KV_COLS, 0)):
        for h in range(N_KV_HEADS):
            s, d = src0 + h * HEAD_DIM, dst0 + h * HEAD_DIM
            x1, x2 = kvt[s:s + ROT_HALF], kvt[s + ROT_HALF:s + ROT_DIM]
            out_ref[0, d:d + ROT_HALF] = x1 * cost - x2 * sint
            out_ref[0, d + ROT_HALF:d + ROT_DIM] = x2 * cost + x1 * sint
    tm = x_ref.shape[0]
    ones_row = (lax.broadcasted_iota(jnp.int32, (HEAD_DIM, tm), 0) == 0).astype(BF16)
    for branch, src_ref, k0, v0 in ((0, kvt_ref, 2 * KV_COLS, 3 * KV_COLS), (1, wint_ref, 0, KV_COLS)):
        for h in range(N_KV_HEADS):
            i, r = branch * N_KV_HEADS + h, h * HEAD_DIM
            kb_ref[0, i] = src_ref[0, k0 + r:k0 + r + HEAD_DIM].astype(BF16)
            vb_ref[0, i, 0:HEAD_DIM] = src_ref[0, v0 + r:v0 + r + HEAD_DIM].astype(BF16)
            vb_ref[0, i, HEAD_DIM:2 * HEAD_DIM] = ones_row

    r = _nt(u, wr_ref[...])
    rg_ref[...] = r[:, 0:D_RNN]
    rx_ref[...] = r[:, D_RNN:2 * D_RNN]
    gates_ref[...] = jax.nn.sigmoid(r[:, 2 * D_RNN:])


def _rope_tables(pos):
    inv = ROPE_THETA ** (-jnp.arange(ROT_HALF, dtype=F32) / ROT_HALF)
    ang = pos.astype(F32)[:, None] * inv
    cos, sin = jnp.cos(ang), jnp.sin(ang)
    d = np.arange(LANES) % HEAD_DIM
    f = d % ROT_HALF
    cosr = jnp.where(d < ROT_DIM, cos[:, f], 1.0)
    sina = jnp.where(d < ROT_HALF, -sin[:, f], 0.0)
    sinb = jnp.where((d >= ROT_HALF) & (d < ROT_DIM), sin[:, f], 0.0)
    return cosr, sina, sinb, cos.T, sin.T


def _proj(x, pos, nm, wq, wkv, wr, nb, tt, tm):
    rows = nb * tt
    nt = tt // tm
    cosr, sina, sinb, cost, sint = _rope_tables(pos)
    row = lambda w: pl.BlockSpec((tm, w), lambda i: (i, 0))
    full = lambda a: pl.BlockSpec(a.shape, lambda i: (0,) * a.ndim)
    tab_r = pl.BlockSpec((tm, LANES), lambda i: (i % nt, 0))
    tab_t = pl.BlockSpec((ROT_HALF, tm), lambda i: (0, i % nt))
    heads = pl.BlockSpec((1, N_HEADS, tm, HEAD_DIM), lambda i: (i // nt, 0, i % nt, 0))
    tr = lambda r: pl.BlockSpec((1, r, tm), lambda i: (i // nt, 0, i % nt))
    out_shape = (
        jax.ShapeDtypeStruct((nb, N_HEADS, tt, HEAD_DIM), BF16),
        jax.ShapeDtypeStruct((nb, N_HEADS, tt, HEAD_DIM), BF16),
        jax.ShapeDtypeStruct((nb, 4 * KV_COLS, tt), F32),
        jax.ShapeDtypeStruct((nb, 2 * KV_COLS, tt), F32),
        jax.ShapeDtypeStruct((nb, 2 * N_KV_HEADS, HEAD_DIM, tt), BF16),
        jax.ShapeDtypeStruct((nb, 2 * N_KV_HEADS, 2 * HEAD_DIM, tt), BF16),
        jax.ShapeDtypeStruct((rows, D_RNN), F32),
        jax.ShapeDtypeStruct((rows, D_RNN), F32),
        jax.ShapeDtypeStruct((rows, N_KV_HEADS * GATE_LANES), F32),
    )
    tr4 = lambda r: pl.BlockSpec((1, 2 * N_KV_HEADS, r, tm), lambda i: (i // nt, 0, 0, i % nt))
    out_specs = (heads, heads, tr(4 * KV_COLS), tr(2 * KV_COLS), tr4(HEAD_DIM), tr4(2 * HEAD_DIM),
                 row(D_RNN), row(D_RNN), row(N_KV_HEADS * GATE_LANES))
    return pl.pallas_call(
        _proj_kernel, grid=(rows // tm,),
        in_specs=[row(D_MODEL), full(nm), full(wq), full(wkv), full(wr), tab_r, tab_r, tab_r, tab_t, tab_t],
        out_specs=out_specs, out_shape=out_shape, compiler_params=_params("parallel"), name="proj",
    )(x, nm, wq, wkv, wr, cosr, sina, sinb, cost, sint)


def _expm1(x):
    return jnp.tanh(0.5 * x) * (jnp.exp(x) + 1.0)


def _softplus(x):
    return jnp.maximum(x, 0.0) + jnp.log1p(jnp.exp(-jnp.abs(x)))


def _lru_coeffs(xc, wra_ref, wri_ref, bra_ref, bri_ref, lam_ref):
    xb = xc.astype(BF16)
    r = jax.nn.sigmoid(_nn(xb, wra_ref[...]) + bra_ref[...])
    i = jax.nn.sigmoid(_nn(xb, wri_ref[...]) + bri_ref[...])
    log_a = -LRU_C * r * _softplus(-lam_ref[...])
    a = jnp.exp(log_a)
    b = jnp.sqrt(-_expm1(2.0 * log_a)) * (i * xc)
    return a, b


def _rnn_out(rg, h, g):
    return _rms(jax.nn.gelu(rg) * h, g).astype(BF16)


def _rglru_prompt_kernel(rx_ref, rg_ref, cw_ref, cb_ref, wra_ref, wri_ref, bra_ref, bri_ref, lam_ref, g_ref,
                         y_ref, hl_ref, prev_sc, h_sc):
    tt = rx_ref.shape[0]

    @pl.when(pl.program_id(1) == 0)
    def _():
        prev_sc[...] = jnp.zeros_like(prev_sc)
        h_sc[...] = jnp.zeros_like(h_sc)

    rx = rx_ref[...]
    ext = jnp.concatenate([prev_sc[...], rx], axis=0)
    shifted = lambda d: pltpu.roll(ext, d, 0)[SUBLANES:SUBLANES + tt]
    cw = cw_ref[...]
    xc = cb_ref[...] + cw[0:1] * shifted(3)
    xc = xc + cw[1:2] * shifted(2)
    xc = xc + cw[2:3] * shifted(1)
    xc = xc + cw[3:4] * rx
    prev_sc[...] = rx[tt - SUBLANES:tt]

    a, b = _lru_coeffs(xc, wra_ref, wri_ref, bra_ref, bri_ref, lam_ref)
    row = lax.broadcasted_iota(jnp.int32, (tt, 1), 0)
    s = 1
    while s < tt:
        if s < SUBLANES:
            a_sh = jnp.where(row >= s, pltpu.roll(a, s, 0), 1.0)
            b_sh = jnp.where(row >= s, pltpu.roll(b, s, 0), 0.0)
        else:
            a_sh = jnp.concatenate([jnp.ones((s, a.shape[1]), F32), a[:tt - s]], axis=0)
            b_sh = jnp.concatenate([jnp.zeros((s, b.shape[1]), F32), b[:tt - s]], axis=0)
        b = a * b_sh + b
        a = a * a_sh
        s *= 2
    h = a * h_sc[...] + b
    h_sc[...] = h[tt - 1:tt]
    hl_ref[0] = h[tt - 1:tt]
    y_ref[...] = _rnn_out(rg_ref[...], h, g_ref[...])


def _rglru_sample_kernel(rx_ref, rg_ref, cp_ref, h0_ref, cw_ref, cb_ref, wra_ref, wri_ref, bra_ref, bri_ref,
                         lam_ref, g_ref, y_ref, cs_ref, hl_ref):
    nb = h0_ref.shape[0]
    steps = rx_ref.shape[0] // nb
    xp = [cp_ref[k] for k in range(CONV_WIDTH - 1)] + [rx_ref[t * nb:(t + 1) * nb] for t in range(steps)]
    cw, cb = cw_ref[...], cb_ref[...]
    xcs = []
    for t in range(steps):
        xc = cb + cw[0:1] * xp[t]
        for tap in range(1, CONV_WIDTH):
            xc = xc + cw[tap:tap + 1] * xp[t + tap]
        xcs.append(xc)
    a, b = _lru_coeffs(jnp.concatenate(xcs, axis=0), wra_ref, wri_ref, bra_ref, bri_ref, lam_ref)
    h = h0_ref[...]
    for t in range(steps):
        sl = slice(t * nb, (t + 1) * nb)
        h = a[sl] * h + b[sl]
        y_ref[sl] = _rnn_out(rg_ref[sl], h, g_ref[...])
    for k in range(CONV_WIDTH - 1):
        cs_ref[k] = xp[steps + k]
    hl_ref[...] = h


def _rglru_prompt(rx, rg, weights, nb, tt, chunk):
    nc = tt // chunk
    row = pl.BlockSpec((chunk, D_RNN), lambda n, c: (n * nc + c, 0))
    full = lambda a: pl.BlockSpec(a.shape, lambda n, c: (0,) * a.ndim)
    return pl.pallas_call(
        _rglru_prompt_kernel, grid=(nb, nc),
        in_specs=[row, row] + [full(w) for w in weights],
        out_specs=(row, pl.BlockSpec((1, 1, D_RNN), lambda n, c: (n, 0, 0))),
        out_shape=(jax.ShapeDtypeStruct((nb * tt, D_RNN), BF16), jax.ShapeDtypeStruct((nb, 1, D_RNN), F32)),
        scratch_shapes=[pltpu.VMEM((SUBLANES, D_RNN), F32), pltpu.VMEM((1, D_RNN), F32)],
        compiler_params=_params("parallel", "arbitrary"), name="rglru_prompt",
    )(rx, rg, *weights)


def _rglru_sample(rx, rg, conv_prev, h0, weights):
    nb = h0.shape[0]
    return pl.pallas_call(
        _rglru_sample_kernel,
        out_shape=(jax.ShapeDtypeStruct(rx.shape, BF16), jax.ShapeDtypeStruct(conv_prev.shape, F32),
                   jax.ShapeDtypeStruct((nb, D_RNN), F32)),
        compiler_params=pltpu.CompilerParams(vmem_limit_bytes=VMEM_LIMIT), name="rglru_sample",
    )(rx, rg, conv_prev, h0, *weights)


CMP_COLS = 2 * KV_COLS
HALF = CMP_BLOCK // 2
BLOCKS_PER_PAGE = PAGE_SIZE // CMP_STRIDE


def _compress_kernel(pt_ref, *refs, n_pages):
    del pt_ref
    page_refs = refs[:n_pages]
    pet_ref, perm_ref, w1_ref, w2_ref, out_ref, z_sc = refs[n_pages:]
    for k in range(n_pages):
        xt = page_refs[k][0].reshape(CMP_COLS, PAGE_SIZE)
        for half in range(2):
            a = (xt + pet_ref[half]).astype(BF16)
            z_sc[half, k] = _nt(perm_ref[...], a)
    pre = []
    for half in range(2):
        acc = jnp.zeros((N_CMP_PAD, CMP_COLS), F32)
        for j in range(CMP_STRIDE):
            rows = z_sc[half, :, j * BLOCKS_PER_PAGE:(j + 1) * BLOCKS_PER_PAGE, :]
            acc = acc + _nn(rows.reshape(N_CMP_PAD, CMP_COLS).astype(BF16), w1_ref[half, j])
        pre.append(acc)
    hid = jax.nn.gelu(pre[0] + pltpu.roll(pre[1], N_CMP_PAD - 1, 0))
    out_ref[0] = _nn(hid.astype(BF16), w2_ref[...]).astype(BF16)


def _compress(pages, page_spec, page_table, pet, w1, w2, nb):
    n_pages = page_table.shape[1]
    assert n_pages * BLOCKS_PER_PAGE == N_CMP_PAD
    pos = np.arange(PAGE_SIZE)
    perm = jnp.asarray((pos % CMP_STRIDE * BLOCKS_PER_PAGE + pos // CMP_STRIDE)[None, :]
                       == np.arange(PAGE_SIZE)[:, None], BF16)
    full = lambda a: pl.BlockSpec(a.shape, lambda n, pt: (0,) * a.ndim)
    once = lambda a: pl.BlockSpec(a.shape, lambda n, pt: (0,) * a.ndim, pipeline_mode=pl.Buffered(1))
    grid_spec = pltpu.PrefetchScalarGridSpec(
        num_scalar_prefetch=1, grid=(nb,),
        in_specs=[page_spec(k) for k in range(n_pages)] + [full(pet), full(perm), once(w1), full(w2)],
        out_specs=pl.BlockSpec((1, N_CMP_PAD, CMP_COLS), lambda n, pt: (n, 0, 0)),
        scratch_shapes=[pltpu.VMEM((2, n_pages, PAGE_SIZE, CMP_COLS), F32)])
    return pl.pallas_call(
        functools.partial(_compress_kernel, n_pages=n_pages), grid_spec=grid_spec,
        out_shape=jax.ShapeDtypeStruct((nb, N_CMP_PAD, CMP_COLS), BF16),
        compiler_params=_params("parallel"), name="compress",
    )(page_table, *([pages] * n_pages), pet, perm, w1, w2)


def _cmp_cols(cmp_ref, i, slot, h):
    c0 = slot * KV_COLS + h * HEAD_DIM
    return cmp_ref[i, :, c0:c0 + HEAD_DIM]


def _overlap_matrix(n_sel):
    n_cmp = N_CMP_PAD - 1
    c0 = np.arange(N_CMP_PAD)[:, None] * CMP_STRIDE
    j0 = np.arange(LANES)[None, :] * SEL_BLOCK
    ov = (c0 < j0 + SEL_BLOCK) & (c0 + CMP_BLOCK > j0)
    ov &= (np.arange(N_CMP_PAD)[:, None] < n_cmp) & (np.arange(LANES)[None, :] < n_sel)
    return ov.astype(np.float32)


def _topk_member(score, idx, n, axis):
    rank = jnp.zeros(score.shape, jnp.int32)
    for i in range(n):
        si = score[i:i + 1, :] if axis == 0 else score[:, i:i + 1]
        beats = (si > score) | ((si == score) & (i < idx))
        rank = rank + beats.astype(jnp.int32)
    return rank < TOP_N


def _sel_score(imp, j, cur):
    valid = j <= cur
    forced = (j == 0) | (j == cur) | (j == cur - 1)
    return jnp.where(valid, imp, -SEL_BONUS) + jnp.where(forced, SEL_BONUS, 0.0)


def _nsa_prompt_kernel(q_ref, qr_ref, kb_ref, vb_ref, cmp_ref, gates_ref, ovt_ref, e_ref,
                       o_ref, m_sc, acc_sc, *, tq, slc_chunk, row_block, n_sel):
    heads = range(N_KV_HEADS)
    q0 = pl.program_id(1) * tq
    qpos = q0 + lax.broadcasted_iota(jnp.int32, (tq, 1), 0)
    group = lambda ref, h: ref[0, h * GQA:(h + 1) * GQA].reshape(GQA * tq, HEAD_DIM)

    cidx = lax.broadcasted_iota(jnp.int32, (1, N_CMP_PAD), 1)
    real = cidx < N_CMP_PAD - 1
    cmask = (cidx * CMP_STRIDE + CMP_BLOCK - 1 <= qpos) & real
    qpos_t = q0 + lax.broadcasted_iota(jnp.int32, (1, tq), 1)
    j_t = lax.broadcasted_iota(jnp.int32, (n_sel, 1), 0)
    eye = (j_t == lax.broadcasted_iota(jnp.int32, (1, n_sel), 1)).astype(BF16)
    o_cmp, sel = [], []
    for h in heads:
        s = _nt(group(q_ref, h), _cmp_cols(cmp_ref, 0, 0, h)).reshape(GQA, tq, N_CMP_PAD)
        s = jnp.where(real, jnp.where(cmask, s, NEG), -jnp.inf)
        e = jnp.exp(s - jnp.max(s, axis=-1, keepdims=True))
        p = (e / jnp.sum(e, axis=-1, keepdims=True)) * cmask.astype(F32)
        pb = p.astype(BF16)
        o_cmp.append(_nn(pb.reshape(GQA * tq, N_CMP_PAD), _cmp_cols(cmp_ref, 0, 1, h)).reshape(GQA, tq, HEAD_DIM))
        imp_t = _nt(ovt_ref[...], pb[0])
        for g in range(1, GQA):
            imp_t = imp_t + _nt(ovt_ref[...], pb[g])
        score_t = _sel_score(imp_t, j_t, qpos_t // SEL_BLOCK)
        sel_t = _topk_member(score_t, j_t, n_sel, axis=0).astype(BF16)
        sel.append(lax.dot_general(sel_t, eye, TN_DIMS, preferred_element_type=F32))

    qr4 = [group(qr_ref, h) for h in heads]
    pad_cols = jnp.zeros((GQA * tq, HEAD_DIM - n_sel), BF16)
    unpicked = [jnp.where(s > 0.5, 0.0, -SEL_MASK).astype(BF16) for s in sel]
    qsel4 = [jnp.concatenate([qr4[h], jnp.concatenate([unpicked[h]] * GQA, axis=0), pad_cols], axis=1)
             for h in heads]

    def attend(h, qx, k, v, allowed, first):
        n = k.shape[1]
        sc = _nn(qx, k)
        bias = jnp.where(allowed, 0.0, NEG)
        pes, ms, alphas = [], [], []
        for r0 in range(0, GQA * tq, row_block):
            t0 = r0 % tq
            x = sc[r0:r0 + row_block] + bias[t0:t0 + row_block]
            m = jnp.max(x, axis=-1, keepdims=True)
            if first:
                ms.append(jnp.broadcast_to(m, (row_block, LANES)))
            else:
                m_prev = m_sc[h, r0:r0 + row_block]
                m_wide = jnp.maximum(m_prev, m)
                alphas.append(jnp.exp(m_prev - m_wide))
                ms.append(m_wide)
                m = jnp.concatenate([m_wide] * (n // LANES), axis=1)
            pes.append(jnp.exp(x - m).astype(BF16))
        pv = _nt(jnp.concatenate(pes, axis=0), v)
        m_sc[h] = jnp.concatenate(ms, axis=0)
        acc_sc[h] = pv if first else jnp.concatenate(alphas, axis=0) * acc_sc[h] + pv

    def normalised(h):
        acc = acc_sc[h].reshape(GQA, tq, 2 * HEAD_DIM)
        return acc[:, :, 0:HEAD_DIM] / acc[:, :, HEAD_DIM:HEAD_DIM + 1]

    wn = WINDOW + tq
    w0 = pl.multiple_of(jnp.maximum(q0 - WINDOW, 0), LANES)
    kpos = w0 + lax.broadcasted_iota(jnp.int32, (1, wn), 1)
    in_window = (kpos <= qpos) & (kpos > qpos - WINDOW)
    for h in heads:
        attend(h, qr4[h], kb_ref[0, N_KV_HEADS + h, :, pl.ds(w0, wn)], vb_ref[0, N_KV_HEADS + h, :, pl.ds(w0, wn)],
               in_window, True)
    o_win = [normalised(h) for h in heads]

    for c0 in range(0, kb_ref.shape[3], slc_chunk):
        def chunk(c0=c0):
            kpos = c0 + lax.broadcasted_iota(jnp.int32, (1, slc_chunk), 1)
            member = jnp.concatenate([e_ref[:, c0:c0 + slc_chunk],
                                      jnp.zeros((HEAD_DIM - n_sel, slc_chunk), BF16)], axis=0)
            for h in heads:
                k = jnp.concatenate([kb_ref[0, h, :, c0:c0 + slc_chunk], member], axis=0)
                attend(h, qsel4[h], k, vb_ref[0, h, :, c0:c0 + slc_chunk], kpos <= qpos, c0 == 0)

        if c0 == 0:
            chunk()
        else:
            pl.when(q0 + tq > c0)(chunk)
    o_slc = [normalised(h) for h in heads]

    gt = gates_ref[...]
    for h in heads:
        for g in range(GQA):
            col = lambda b: gt[:, h * GATE_LANES + b * GQA + g:h * GATE_LANES + b * GQA + g + 1]
            o = col(0) * o_cmp[h][g] + col(1) * o_slc[h][g]
            o = o + col(2) * o_win[h][g]
            o_ref[:, (h * GQA + g) * HEAD_DIM:(h * GQA + g + 1) * HEAD_DIM] = o


def _nsa_prompt(q, qr, kb, vb, cmp, gates, nb, tt, tq, slc_chunk):
    n_sel = -(-tt // SEL_BLOCK)
    nq = tt // tq
    ovt = jnp.asarray(_overlap_matrix(n_sel)[:, :n_sel].T, BF16)
    expand = jnp.asarray(np.arange(tt)[None, :] // SEL_BLOCK == np.arange(n_sel)[:, None], BF16)
    qspec = pl.BlockSpec((1, N_HEADS, tq, HEAD_DIM), lambda n, i: (n, 0, i, 0))
    seq = lambda a: pl.BlockSpec((1,) + a.shape[1:], lambda n, i: (n,) + (0,) * (a.ndim - 1))
    full = lambda a: pl.BlockSpec(a.shape, lambda n, i: (0,) * a.ndim)
    rows = lambda w: pl.BlockSpec((tq, w), lambda n, i: (n * nq + i, 0))
    return pl.pallas_call(
        functools.partial(_nsa_prompt_kernel, tq=tq, slc_chunk=slc_chunk, row_block=32, n_sel=n_sel),
        grid=(nb, nq),
        in_specs=[qspec, qspec, seq(kb), seq(vb), seq(cmp), rows(N_KV_HEADS * GATE_LANES), full(ovt), full(expand)],
        out_specs=rows(ATTN_WIDTH), out_shape=jax.ShapeDtypeStruct((nb * tt, ATTN_WIDTH), F32),
        scratch_shapes=[pltpu.VMEM((N_KV_HEADS, GQA * tq, LANES), F32),
                        pltpu.VMEM((N_KV_HEADS, GQA * tq, 2 * HEAD_DIM), F32)],
        compiler_params=_params("parallel", "arbitrary"), name="nsa_prompt",
    )(q, qr, kb, vb, cmp, gates, ovt, expand)


def _nsa_sample_kernel(pt_ref, *refs, n_pages, steps, seqs):
    del pt_ref
    q_ref, qr_ref = refs[0:2]
    page_refs = refs[2:2 + seqs * n_pages]
    cmp_ref, cw_ref, newk_ref, newv_ref, gates_ref, ov_ref, o_ref, wout_ref = refs[2 + seqs * n_pages:]
    rows = GQA * steps
    past = n_pages * PAGE_SIZE
    wc = cw_ref.shape[-1]
    trow = lax.broadcasted_iota(jnp.int32, (rows, 1), 0) % steps
    lane = lax.broadcasted_iota(jnp.int32, (1, LANES), 1)
    n_sel = -(-(past + steps) // SEL_BLOCK)
    cur = (past + trow) // SEL_BLOCK
    real = lane < N_CMP_PAD - 1
    wlane = lax.broadcasted_iota(jnp.int32, (1, wc), 1)
    in_window = jnp.broadcast_to(wlane + (WINDOW - wc) > trow, (rows, wc))
    per_tile = PAGE_SIZE // SEL_BLOCK

    for i in range(seqs):
        newk, newv = newk_ref[i], newv_ref[i]
        newk_b, newv_b = newk.astype(BF16), newv.astype(BF16)
        for h in range(N_KV_HEADS):
            q, qr = q_ref[i, h], qr_ref[i, h]
            pages = page_refs[i * n_pages:(i + 1) * n_pages]

            def new_ok(branch):
                base = (branch * N_KV_HEADS + h) * steps
                return (lane >= base) & (lane < base + steps) & (lane - base <= trow)

            s = jnp.where(real, _nt(q, _cmp_cols(cmp_ref, i, 0, h)), -jnp.inf)
            e = jnp.exp(s - jnp.max(s, axis=-1, keepdims=True))
            pb = (e / jnp.sum(e, axis=-1, keepdims=True)).astype(BF16)
            o_cmp = _nn(pb, _cmp_cols(cmp_ref, i, 1, h))

            part = _nn(pb, ov_ref[...])
            imp = part
            for g in range(1, GQA):
                imp = imp + pltpu.roll(part, g * steps, 0)
            score = jnp.where(lane < n_sel, _sel_score(imp, lane, cur), -jnp.inf)
            sel = _topk_member(score, lane, n_sel, axis=1) & (lane < n_sel)
            sel_f = sel.astype(F32)

            ks = jnp.concatenate([p[0, 0, h].astype(BF16) for p in pages] + [newk_b], axis=1)
            vs = jnp.concatenate([p[0, 1, h].astype(BF16) for p in pages] + [newv_b], axis=1)
            tiles = []
            for k in range(n_pages):
                m = sel_f[:, per_tile * k:per_tile * k + 1]
                for b in range(1, per_tile):
                    m = jnp.where(lane < b * SEL_BLOCK, m, sel_f[:, per_tile * k + b:per_tile * k + b + 1])
                tiles.append(m > 0.5)
            tiles.append((sel_f[:, n_sel - 1:n_sel] > 0.5) & new_ok(0))
            msk = jnp.concatenate(tiles, axis=1)
            sc = jnp.where(msk, _nn(qr, ks), NEG)
            pe = jnp.where(msk, jnp.exp(sc - jnp.max(sc, axis=-1, keepdims=True)), 0.0)
            o_slc = _nt(pe.astype(BF16), vs) / jnp.sum(pe, axis=-1, keepdims=True)

            kw = jnp.concatenate([cw_ref[i, 0, h].astype(BF16), newk_b], axis=1)
            vw = jnp.concatenate([cw_ref[i, 1, h].astype(BF16), newv_b], axis=1)
            wmsk = jnp.concatenate([in_window, jnp.broadcast_to(new_ok(1), (rows, LANES))], axis=1)
            sw = jnp.where(wmsk, _nn(qr, kw), NEG)
            pw = jnp.where(wmsk, jnp.exp(sw - jnp.max(sw, axis=-1, keepdims=True)), 0.0)
            o_win = _nt(pw.astype(BF16), vw) / jnp.sum(pw, axis=-1, keepdims=True)

            gt = gates_ref[i, h]
            o_ref[i, h] = gt[:, 0:1] * o_cmp + gt[:, 1:2] * o_slc + gt[:, 2:3] * o_win

            to_end = (LANES - steps - (N_KV_HEADS + h) * steps) % LANES
            for slot, new in ((0, newk), (1, newv)):
                shifted = pltpu.roll(cw_ref[i, slot, h], wc - steps, 1)
                wout_ref[i, slot, h, :, 0:wc - LANES] = shifted[:, 0:wc - LANES]
                wout_ref[i, slot, h, :, wc - LANES:wc] = jnp.where(lane >= LANES - steps, pltpu.roll(new, to_end, 1),
                                                                   shifted[:, wc - LANES:wc])


def _nsa_sample(q, qr, pages, page_table, cmp, cache_win_t, new_k, new_v, gates, steps, seqs):
    nb, n_pages = page_table.shape
    n_sel = -(-(n_pages * PAGE_SIZE + steps) // SEL_BLOCK)
    ov = jnp.asarray(_overlap_matrix(n_sel), BF16)
    rows = GQA * steps
    lead = lambda a: pl.BlockSpec((seqs,) + a.shape[1:], lambda n, pt: (n,) + (0,) * (a.ndim - 1))
    page_spec = lambda i, k: pl.BlockSpec((1, 2, N_KV_HEADS, HEAD_DIM, PAGE_SIZE),
                                          lambda n, pt: (pt[n * seqs + i, k], 1, 0, 0, 0))
    page_specs = [page_spec(i, k) for i in range(seqs) for k in range(n_pages)]
    grid_spec = pltpu.PrefetchScalarGridSpec(
        num_scalar_prefetch=1, grid=(nb // seqs,),
        in_specs=[lead(q), lead(qr)] + page_specs
        + [lead(cmp), lead(cache_win_t), lead(new_k), lead(new_v), lead(gates),
           pl.BlockSpec(ov.shape, lambda n, pt: (0, 0))],
        out_specs=(pl.BlockSpec((seqs, N_KV_HEADS, rows, HEAD_DIM), lambda n, pt: (n, 0, 0, 0)), lead(cache_win_t)))
    return pl.pallas_call(
        functools.partial(_nsa_sample_kernel, n_pages=n_pages, steps=steps, seqs=seqs), grid_spec=grid_spec,
        out_shape=(jax.ShapeDtypeStruct((nb, N_KV_HEADS, rows, HEAD_DIM), F32),
                   jax.ShapeDtypeStruct(cache_win_t.shape, F32)),
        compiler_params=_params("parallel"), name="nsa_sample",
    )(page_table, q, qr, *([pages] * len(page_specs)), cmp, cache_win_t, new_k, new_v, gates, ov)


def _mlp_kernel(x_ref, oa_ref, yr_ref, ga_ref, wo_ref, nmlp_ref, wup_ref, wdn_ref, nfin_ref, y_ref, *, ff_chunk):
    a = _rms(oa_ref[...], ga_ref[...]).astype(BF16)
    x1 = x_ref[...] + (_nn(a, wo_ref[0:ATTN_WIDTH, :]) + _nn(yr_ref[...], wo_ref[ATTN_WIDTH:, :]))
    v = _rms(x1, nmlp_ref[...]).astype(BF16)
    acc = jnp.zeros_like(x1)
    for c in range(D_FF // ff_chunk):
        sl = slice(c * ff_chunk, (c + 1) * ff_chunk)
        hid = jnp.square(jnp.maximum(_nn(v, wup_ref[:, sl]), 0.0)).astype(BF16)
        acc = acc + _nn(hid, wdn_ref[sl, :])
    y_ref[...] = _rms(x1 + acc, nfin_ref[...])


def _mlp(x, o_attn, y_rnn, g_attn, w_out, norm_mlp, w_up, w_down, norm_final, tm, ff_chunk):
    rows = x.shape[0]
    row = lambda w: pl.BlockSpec((tm, w), lambda i: (i, 0))
    full = lambda a: pl.BlockSpec(a.shape, lambda i: (0,) * a.ndim)
    once = lambda a: pl.BlockSpec(a.shape, lambda i: (0,) * a.ndim, pipeline_mode=pl.Buffered(1))
    return pl.pallas_call(
        functools.partial(_mlp_kernel, ff_chunk=ff_chunk), grid=(rows // tm,),
        in_specs=[row(D_MODEL), row(ATTN_WIDTH), row(D_RNN), full(g_attn), once(w_out), full(norm_mlp),
                  once(w_up), once(w_down), full(norm_final)],
        out_specs=row(D_MODEL), out_shape=jax.ShapeDtypeStruct((rows, D_MODEL), F32),
        compiler_params=_params("parallel"), name="mlp",
    )(x, o_attn, y_rnn, g_attn, w_out, norm_mlp, w_up, w_down, norm_final)


def _block_diag(w):
    nb, bs, _ = w.shape
    eye = jnp.eye(nb, dtype=w.dtype)
    return (eye[:, None, :, None] * w[:, :, None, :]).reshape(nb * bs, nb * bs)


def kernel(x_prompt, x_sample, cache_kv, cache_win, state_conv, state_rnn, page_table, w_in, pe_ck, w_ck1, w_ck2,
           pe_cv, w_cv1, w_cv2, g_attn, g_rnn, conv_w, conv_b, w_ra, b_ra, w_ri, b_ri, lam, w_out, norm_mix,
           norm_mlp, w_up, w_down, norm_final):
    assert w_in.shape[0] == 1, "single layer"
    nbp, tp, _ = x_prompt.shape
    nbs, steps, _ = x_sample.shape
    n_pages = page_table.shape[1]

    wt = jnp.transpose(w_in[0]).astype(BF16)
    c_kv, c_g = ATTN_WIDTH, ATTN_WIDTH + 6 * KV_COLS
    c_rg = c_g + 3 * N_HEADS
    wq, wkv = wt[0:c_kv], wt[c_kv:c_g]
    wg = wt[c_g:c_rg].reshape(3, N_KV_HEADS, GQA, D_MODEL).transpose(1, 0, 2, 3).reshape(N_KV_HEADS, 3 * GQA, D_MODEL)
    wg = jnp.pad(wg, ((0, 0), (0, GATE_LANES - 3 * GQA), (0, 0))).reshape(N_KV_HEADS * GATE_LANES, D_MODEL)
    wr = jnp.concatenate([wt[c_rg:c_rg + 2 * D_RNN], wg], axis=0)
    row2 = lambda a: a.reshape(1, -1)
    rnn_w = (conv_w[0], row2(conv_b[0]), _block_diag(w_ra[0].astype(BF16)), _block_diag(w_ri[0].astype(BF16)),
             row2(b_ra[0]), row2(b_ri[0]), row2(lam[0]), row2(g_rnn[0]))
    per_cs = lambda k, v: jnp.repeat(jnp.stack([k, v]), N_KV_HEADS, axis=0)
    pe_cs = per_cs(pe_ck[0], pe_cv[0]).reshape(2 * N_KV_HEADS, 2, HALF, HEAD_DIM)
    cmp_pet = jnp.tile(pe_cs.transpose(1, 0, 3, 2), (1, 1, 1, BLOCKS_PER_PAGE)).reshape(2, CMP_COLS, PAGE_SIZE)
    w1_cs = per_cs(w_ck1[0], w_cv1[0]).transpose(1, 0, 2, 3)
    cmp_w1 = jax.vmap(_block_diag)(w1_cs.astype(BF16)).reshape(2, HALF, CMP_COLS, CMP_COLS)
    cmp_w2 = _block_diag(per_cs(w_ck2[0], w_cv2[0]).astype(BF16))
    mlp_w = (row2(g_attn[0]), w_out[0].astype(BF16), row2(norm_mlp[0]), w_up[0].astype(BF16),
             w_down[0].astype(BF16), row2(norm_final))
    nm = row2(norm_mix[0])

    tm = 512
    xp = x_prompt.reshape(nbp * tp, D_MODEL)
    q, qr, kvt, wint, kb, vb, rg, rx, gates = _proj(xp, jnp.arange(tp), nm, wq, wkv, wr, nbp, tp, tm)
    y_rnn, h_last = _rglru_prompt(rx, rg, rnn_w, nbp, tp, 256)
    own_pages = jnp.zeros((nbp, tp // PAGE_SIZE), jnp.int32)
    cmp = _compress(kvt, lambda k: pl.BlockSpec((1, CMP_COLS, PAGE_SIZE), lambda n, pt: (n, 0, k)), own_pages,
                    cmp_pet, cmp_w1, cmp_w2, nbp)
    o_attn = _nsa_prompt(q, qr, kb, vb, cmp, gates, nbp, tp, 256, 512)
    y_prompt = _mlp(xp, o_attn, y_rnn, *mlp_w, tm, 1024).reshape(nbp, tp, D_MODEL)
    wlen = min(WINDOW, tp)
    kv_prompt = kvt.reshape(nbp, 4, N_KV_HEADS, HEAD_DIM, tp).transpose(0, 4, 1, 2, 3)[None]
    win_prompt = wint[:, :, tp - wlen:].reshape(nbp, 2, N_KV_HEADS, HEAD_DIM, wlen).transpose(0, 4, 1, 2, 3)[None]
    conv_prompt = rx.reshape(nbp, tp, D_RNN)[:, tp - (CONV_WIDTH - 1):][None]
    h_prompt = h_last.reshape(1, nbp, D_RNN)

    rows_s = nbs * steps
    xs = x_sample.transpose(1, 0, 2).reshape(rows_s, D_MODEL)
    pos_s = PAST_LEN + jnp.arange(rows_s) // nbs
    q, qr, kvt, wint, _, _, rg, rx, gates = _proj(xs, pos_s, nm, wq, wkv, wr, 1, rows_s, rows_s)
    y_rnn, conv_s, h_s = _rglru_sample(rx, rg, state_conv[0].transpose(1, 0, 2), state_rnn[0], rnn_w)
    pages_t = cache_kv[0].transpose(0, 2, 3, 4, 1)
    cmp_page = lambda k: pl.BlockSpec((1, 2, N_KV_HEADS, HEAD_DIM, PAGE_SIZE), lambda n, pt: (pt[n, k], 0, 0, 0, 0))
    cmp = _compress(pages_t, cmp_page, page_table, cmp_pet, cmp_w1, cmp_w2, nbs)
    by_seq = lambda a: a.reshape(N_KV_HEADS, GQA, steps, nbs, HEAD_DIM).transpose(3, 0, 1, 2, 4).reshape(
        nbs, N_KV_HEADS, GQA * steps, HEAD_DIM)
    def new_tile(slc_rows, win_rows):
        t = jnp.stack([slc_rows, win_rows]).reshape(2 * N_KV_HEADS, HEAD_DIM, steps, nbs)
        t = t.transpose(3, 1, 0, 2).reshape(nbs, HEAD_DIM, 2 * N_KV_HEADS * steps)
        return jnp.pad(t, ((0, 0), (0, 0), (0, LANES - 2 * N_KV_HEADS * steps)))
    new_k = new_tile(kvt[0, 2 * KV_COLS:3 * KV_COLS], wint[0, 0:KV_COLS])
    new_v = new_tile(kvt[0, 3 * KV_COLS:4 * KV_COLS], wint[0, KV_COLS:2 * KV_COLS])
    gates_s = gates.reshape(steps, nbs, N_KV_HEADS, GATE_LANES)[..., :3 * GQA].reshape(steps, nbs, N_KV_HEADS, 3, GQA)
    gates_s = gates_s.transpose(1, 2, 4, 0, 3).reshape(nbs, N_KV_HEADS, GQA * steps, 3)
    cache_win_t = cache_win[0].transpose(0, 2, 3, 4, 1)
    o_s, win_t = _nsa_sample(by_seq(q[0]), by_seq(qr[0]), pages_t, page_table, cmp, cache_win_t, new_k, new_v,
                             gates_s, steps, 2)
    o_attn = o_s.reshape(nbs, N_KV_HEADS, GQA, steps, HEAD_DIM).transpose(3, 0, 1, 2, 4).reshape(rows_s, ATTN_WIDTH)
    y_sample = _mlp(xs, o_attn, y_rnn, *mlp_w, rows_s, 1024).reshape(steps, nbs, D_MODEL).transpose(1, 0, 2)
    kv_sample = kvt[0].reshape(4, N_KV_HEADS, HEAD_DIM, steps, nbs).transpose(4, 3, 0, 1, 2)[None]
    win_sample = win_t.transpose(0, 4, 1, 2, 3)[None]
    conv_sample = conv_s.transpose(1, 0, 2)[None]
    h_sample = h_s[None]

    return (y_prompt, y_sample, kv_prompt, kv_sample, win_prompt, win_sample, conv_prompt, conv_sample,
            h_prompt, h_sample)
```

```python
import functools

import jax
import jax.numpy as jnp
import numpy as np
from jax import lax
from jax.experimental import pallas as pl
from jax.experimental.pallas import tpu as pltpu

D_MODEL = 1024
PAST_LEN = 2048
PAGE_SIZE = 128
HEAD_DIM = 64
N_HEADS = 8
N_KV_HEADS = 2
GQA = N_HEADS // N_KV_HEADS
ATTN_WIDTH = N_HEADS * HEAD_DIM
D_RNN = D_MODEL - ATTN_WIDTH
RNN_BLOCKS = 8
CONV_WIDTH = 4
LRU_C = 8.0
D_FF = 4 * D_MODEL
ROT_DIM = HEAD_DIM // 4
ROT_HALF = ROT_DIM // 2
ROPE_THETA = 500000.0
CMP_BLOCK = 32
CMP_STRIDE = 16
SEL_BLOCK = 64
TOP_N = 16
WINDOW = 512
KV_COLS = N_KV_HEADS * HEAD_DIM
EPS = 1e-6
NEG = -1e30
SEL_BONUS = 1e4
SEL_MASK = 2.0 ** 100
SCALE = HEAD_DIM ** -0.5

SUBLANES = 8
LANES = 128
VMEM_LIMIT = 48 * 1024 * 1024

N_CMP_PAD = 128
GATE_LANES = LANES

F32 = jnp.float32
BF16 = jnp.bfloat16
NT_DIMS = (((1,), (1,)), ((), ()))
TN_DIMS = (((0,), (0,)), ((), ()))


def _nt(a, b):
    return lax.dot_general(a, b, NT_DIMS, preferred_element_type=F32)


def _nn(a, b):
    return jnp.dot(a, b, preferred_element_type=F32)


def _rms(x, g):
    return x * lax.rsqrt(jnp.mean(x * x, axis=-1, keepdims=True) + EPS) * g


def _params(*sem):
    return pltpu.CompilerParams(dimension_semantics=sem, vmem_limit_bytes=VMEM_LIMIT)


def _proj_kernel(x_ref, nm_ref, wq_ref, wkv_ref, wr_ref, cosr_ref, sina_ref, sinb_ref, cost_ref, sint_ref,
                 q_ref, qr_ref, kvt_ref, wint_ref, kb_ref, vb_ref, rg_ref, rx_ref, gates_ref):
    u = _rms(x_ref[...], nm_ref[...]).astype(BF16)
    q = _nt(u, wq_ref[...])
    cosr, sina, sinb = cosr_ref[...], sina_ref[...], sinb_ref[...]
    for k in range(ATTN_WIDTH // LANES):
        qg = q[:, k * LANES:(k + 1) * LANES]
        qrg = (qg * cosr + pltpu.roll(qg, LANES - ROT_HALF, 1) * sina
               + pltpu.roll(qg, ROT_HALF, 1) * sinb)
        for hh in range(LANES // HEAD_DIM):
            h = k * (LANES // HEAD_DIM) + hh
            sl = slice(hh * HEAD_DIM, (hh + 1) * HEAD_DIM)
            q_ref[0, h] = (qg[:, sl] * SCALE).astype(BF16)
            qr_ref[0, h] = (qrg[:, sl] * SCALE).astype(BF16)

    kvt = _nt(wkv_ref[...], u)
    cost, sint = cost_ref[...], sint_ref[...]
    kvt_ref[0] = kvt[0:4 * KV_COLS]
    wint_ref[0] = kvt[4 * KV_COLS:6 * KV_COLS]
    for out_ref, src0, dst0 in ((kvt_ref, 2 * KV_COLS, 2 * KV_COLS), (wint_ref, 4 * KV_COLS, 0)):
        for h in range(N_KV_HEADS):
            s, d = src0 + h * HEAD_DIM, dst0 + h * HEAD_DIM
            x1, x2 = kvt[s:s + ROT_HALF], kvt[s + ROT_HALF:s + ROT_DIM]
            out_ref[0, d:d + ROT_HALF] = x1 * cost - x2 * sint
            out_ref[0, d + ROT_HALF:d + ROT_DIM] = x2 * cost + x1 * sint
    tm = x_ref.shape[0]
    ones_row = (lax.broadcasted_iota(jnp.int32, (HEAD_DIM, tm), 0) == 0).astype(BF16)
    for branch, src_ref, k0, v0 in ((0, kvt_ref, 2 * KV_COLS, 3 * KV_COLS), (1, wint_ref, 0, KV_COLS)):
        for h in range(N_KV_HEADS):
            i, r = branch * N_KV_HEADS + h, h * HEAD_DIM
            kb_ref[0, i] = src_ref[0, k0 + r:k0 + r + HEAD_DIM].astype(BF16)
            vb_ref[0, i, 0:HEAD_DIM] = src_ref[0, v0 + r:v0 + r + HEAD_DIM].astype(BF16)
            vb_ref[0, i, HEAD_DIM:2 * HEAD_DIM] = ones_row

    r = _nt(u, wr_ref[...])
    rg_ref[...] = r[:, 0:D_RNN]
    rx_ref[...] = r[:, D_RNN:2 * D_RNN]
    gates_ref[...] = jax.nn.sigmoid(r[:, 2 * D_RNN:])


def _rope_tables(pos):
    inv = ROPE_THETA ** (-jnp.arange(ROT_HALF, dtype=F32) / ROT_HALF)
    ang = pos.astype(F32)[:, None] * inv
    cos, sin = jnp.cos(ang), jnp.sin(ang)
    d = np.arange(LANES) % HEAD_DIM
    f = d % ROT_HALF
    cosr = jnp.where(d < ROT_DIM, cos[:, f], 1.0)
    sina = jnp.where(d < ROT_HALF, -sin[:, f], 0.0)
    sinb = jnp.where((d >= ROT_HALF) & (d < ROT_DIM), sin[:, f], 0.0)
    return cosr, sina, sinb, cos.T, sin.T


def _proj(x, pos, nm, wq, wkv, wr, nb, tt, tm):
    rows = nb * tt
    nt = tt // tm
    cosr, sina, sinb, cost, sint = _rope_tables(pos)
    row = lambda w: pl.BlockSpec((tm, w), lambda i: (i, 0))
    full = lambda a: pl.BlockSpec(a.shape, lambda i: (0,) * a.ndim)
    tab_r = pl.BlockSpec((tm, LANES), lambda i: (i % nt, 0))
    tab_t = pl.BlockSpec((ROT_HALF, tm), lambda i: (0, i % nt))
    heads = pl.BlockSpec((1, N_HEADS, tm, HEAD_DIM), lambda i: (i // nt, 0, i % nt, 0))
    tr = lambda r: pl.BlockSpec((1, r, tm), lambda i: (i // nt, 0, i % nt))
    out_shape = (
        jax.ShapeDtypeStruct((nb, N_HEADS, tt, HEAD_DIM), BF16),
        jax.ShapeDtypeStruct((nb, N_HEADS, tt, HEAD_DIM), BF16),
        jax.ShapeDtypeStruct((nb, 4 * KV_COLS, tt), F32),
        jax.ShapeDtypeStruct((nb, 2 * KV_COLS, tt), F32),
        jax.ShapeDtypeStruct((nb, 2 * N_KV_HEADS, HEAD_DIM, tt), BF16),
        jax.ShapeDtypeStruct((nb, 2 * N_KV_HEADS, 2 * HEAD_DIM, tt), BF16),
        jax.ShapeDtypeStruct((rows, D_RNN), F32),
        jax.ShapeDtypeStruct((rows, D_RNN), F32),
        jax.ShapeDtypeStruct((rows, N_KV_HEADS * GATE_LANES), F32),
    )
    tr4 = lambda r: pl.BlockSpec((1, 2 * N_KV_HEADS, r, tm), lambda i: (i // nt, 0, 0, i % nt))
    out_specs = (heads, heads, tr(4 * KV_COLS), tr(2 * KV_COLS), tr4(HEAD_DIM), tr4(2 * HEAD_DIM),
                 row(D_RNN), row(D_RNN), row(N_KV_HEADS * GATE_LANES))
    return pl.pallas_call(
        _proj_kernel, grid=(rows // tm,),
        in_specs=[row(D_MODEL), full(nm), full(wq), full(wkv), full(wr), tab_r, tab_r, tab_r, tab_t, tab_t],
        out_specs=out_specs, out_shape=out_shape, compiler_params=_params("parallel"), name="proj",
    )(x, nm, wq, wkv, wr, cosr, sina, sinb, cost, sint)


def _expm1(x):
    return jnp.tanh(0.5 * x) * (jnp.exp(x) + 1.0)


def _softplus(x):
    return jnp.maximum(x, 0.0) + jnp.log1p(jnp.exp(-jnp.abs(x)))


def _lru_coeffs(xc, wra_ref, wri_ref, bra_ref, bri_ref, lam_ref):
    xb = xc.astype(BF16)
    r = jax.nn.sigmoid(_nn(xb, wra_ref[...]) + bra_ref[...])
    i = jax.nn.sigmoid(_nn(xb, wri_ref[...]) + bri_ref[...])
    log_a = -LRU_C * r * _softplus(-lam_ref[...])
    a = jnp.exp(log_a)
    b = jnp.sqrt(-_expm1(2.0 * log_a)) * (i * xc)
    return a, b


def _rnn_out(rg, h, g):
    return _rms(jax.nn.gelu(rg) * h, g).astype(BF16)


def _rglru_prompt_kernel(rx_ref, rg_ref, cw_ref, cb_ref, wra_ref, wri_ref, bra_ref, bri_ref, lam_ref, g_ref,
                         y_ref, hl_ref, prev_sc, h_sc):
    tt = rx_ref.shape[0]

    @pl.when(pl.program_id(1) == 0)
    def _():
        prev_sc[...] = jnp.zeros_like(prev_sc)
        h_sc[...] = jnp.zeros_like(h_sc)

    rx = rx_ref[...]
    ext = jnp.concatenate([prev_sc[...], rx], axis=0)
    shifted = lambda d: pltpu.roll(ext, d, 0)[SUBLANES:SUBLANES + tt]
    cw = cw_ref[...]
    xc = cb_ref[...] + cw[0:1] * shifted(3)
    xc = xc + cw[1:2] * shifted(2)
    xc = xc + cw[2:3] * shifted(1)
    xc = xc + cw[3:4] * rx
    prev_sc[...] = rx[tt - SUBLANES:tt]

    a, b = _lru_coeffs(xc, wra_ref, wri_ref, bra_ref, bri_ref, lam_ref)
    row = lax.broadcasted_iota(jnp.int32, (tt, 1), 0)
    s = 1
    while s < tt:
        if s < SUBLANES:
            a_sh = jnp.where(row >= s, pltpu.roll(a, s, 0), 1.0)
            b_sh = jnp.where(row >= s, pltpu.roll(b, s, 0), 0.0)
        else:
            a_sh = jnp.concatenate([jnp.ones((s, a.shape[1]), F32), a[:tt - s]], axis=0)
            b_sh = jnp.concatenate([jnp.zeros((s, b.shape[1]), F32), b[:tt - s]], axis=0)
        b = a * b_sh + b
        a = a * a_sh
        s *= 2
    h = a * h_sc[...] + b
    h_sc[...] = h[tt - 1:tt]
    hl_ref[0] = h[tt - 1:tt]
    y_ref[...] = _rnn_out(rg_ref[...], h, g_ref[...])


def _rglru_sample_kernel(rx_ref, rg_ref, cp_ref, h0_ref, cw_ref, cb_ref, wra_ref, wri_ref, bra_ref, bri_ref,
                         lam_ref, g_ref, y_ref, cs_ref, hl_ref):
    nb = h0_ref.shape[0]
    steps = rx_ref.shape[0] // nb
    xp = [cp_ref[k] for k in range(CONV_WIDTH - 1)] + [rx_ref[t * nb:(t + 1) * nb] for t in range(steps)]
    cw, cb = cw_ref[...], cb_ref[...]
    xcs = []
    for t in range(steps):
        xc = cb + cw[0:1] * xp[t]
        for tap in range(1, CONV_WIDTH):
            xc = xc + cw[tap:tap + 1] * xp[t + tap]
        xcs.append(xc)
    a, b = _lru_coeffs(jnp.concatenate(xcs, axis=0), wra_ref, wri_ref, bra_ref, bri_ref, lam_ref)
    h = h0_ref[...]
    for t in range(steps):
        sl = slice(t * nb, (t + 1) * nb)
        h = a[sl] * h + b[sl]
        y_ref[sl] = _rnn_out(rg_ref[sl], h, g_ref[...])
    for k in range(CONV_WIDTH - 1):
        cs_ref[k] = xp[steps + k]
    hl_ref[...] = h


def _rglru_prompt(rx, rg, weights, nb, tt, chunk):
    nc = tt // chunk
    row = pl.BlockSpec((chunk, D_RNN), lambda n, c: (n * nc + c, 0))
    full = lambda a: pl.BlockSpec(a.shape, lambda n, c: (0,) * a.ndim)
    return pl.pallas_call(
        _rglru_prompt_kernel, grid=(nb, nc),
        in_specs=[row, row] + [full(w) for w in weights],
        out_specs=(row, pl.BlockSpec((1, 1, D_RNN), lambda n, c: (n, 0, 0))),
        out_shape=(jax.ShapeDtypeStruct((nb * tt, D_RNN), BF16), jax.ShapeDtypeStruct((nb, 1, D_RNN), F32)),
        scratch_shapes=[pltpu.VMEM((SUBLANES, D_RNN), F32), pltpu.VMEM((1, D_RNN), F32)],
        compiler_params=_params("parallel", "arbitrary"), name="rglru_prompt",
    )(rx, rg, *weights)


def _rglru_sample(rx, rg, conv_prev, h0, weights):
    nb = h0.shape[0]
    return pl.pallas_call(
        _rglru_sample_kernel,
        out_shape=(jax.ShapeDtypeStruct(rx.shape, BF16), jax.ShapeDtypeStruct(conv_prev.shape, F32),
                   jax.ShapeDtypeStruct((nb, D_RNN), F32)),
        compiler_params=pltpu.CompilerParams(vmem_limit_bytes=VMEM_LIMIT), name="rglru_sample",
    )(rx, rg, conv_prev, h0, *weights)


CMP_COLS = 2 * KV_COLS
HALF = CMP_BLOCK // 2
BLOCKS_PER_PAGE = PAGE_SIZE // CMP_STRIDE


def _compress_kernel(pt_ref, *refs, n_pages):
    del pt_ref
    page_refs = refs[:n_pages]
    pet_ref, perm_ref, w1_ref, w2_ref, out_ref, z_sc = refs[n_pages:]
    for k in range(n_pages):
        xt = page_refs[k][0].reshape(CMP_COLS, PAGE_SIZE)
        for half in range(2):
            a = (xt + pet_ref[half]).astype(BF16)
            z_sc[half, k] = _nt(perm_ref[...], a)
    pre = []
    for half in range(2):
        acc = jnp.zeros((N_CMP_PAD, CMP_COLS), F32)
        for j in range(CMP_STRIDE):
            rows = z_sc[half, :, j * BLOCKS_PER_PAGE:(j + 1) * BLOCKS_PER_PAGE, :]
            acc = acc + _nn(rows.reshape(N_CMP_PAD, CMP_COLS).astype(BF16), w1_ref[half, j])
        pre.append(acc)
    hid = jax.nn.gelu(pre[0] + pltpu.roll(pre[1], N_CMP_PAD - 1, 0))
    out_ref[0] = _nn(hid.astype(BF16), w2_ref[...]).astype(BF16)


def _compress(pages, page_spec, page_table, pet, w1, w2, nb):
    n_pages = page_table.shape[1]
    assert n_pages * BLOCKS_PER_PAGE == N_CMP_PAD
    pos = np.arange(PAGE_SIZE)
    perm = jnp.asarray((pos % CMP_STRIDE * BLOCKS_PER_PAGE + pos // CMP_STRIDE)[None, :]
                       == np.arange(PAGE_SIZE)[:, None], BF16)
    full = lambda a: pl.BlockSpec(a.shape, lambda n, pt: (0,) * a.ndim)
    once = lambda a: pl.BlockSpec(a.shape, lambda n, pt: (0,) * a.ndim, pipeline_mode=pl.Buffered(1))
    grid_spec = pltpu.PrefetchScalarGridSpec(
        num_scalar_prefetch=1, grid=(nb,),
        in_specs=[page_spec(k) for k in range(n_pages)] + [full(pet), full(perm), once(w1), full(w2)],
        out_specs=pl.BlockSpec((1, N_CMP_PAD, CMP_COLS), lambda n, pt: (n, 0, 0)),
        scratch_shapes=[pltpu.VMEM((2, n_pages, PAGE_SIZE, CMP_COLS), F32)])
    return pl.pallas_call(
        functools.partial(_compress_kernel, n_pages=n_pages), grid_spec=grid_spec,
        out_shape=jax.ShapeDtypeStruct((nb, N_CMP_PAD, CMP_COLS), BF16),
        compiler_params=_params("parallel"), name="compress",
    )(page_table, *([pages] * n_pages), pet, perm, w1, w2)


def _cmp_cols(cmp_ref, i, slot, h):
    c0 = slot * KV_COLS + h * HEAD_DIM
    return cmp_ref[i, :, c0:c0 + HEAD_DIM]


def _overlap_matrix(n_sel):
    n_cmp = N_CMP_PAD - 1
    c0 = np.arange(N_CMP_PAD)[:, None] * CMP_STRIDE
    j0 = np.arange(LANES)[None, :] * SEL_BLOCK
    ov = (c0 < j0 + SEL_BLOCK) & (c0 + CMP_BLOCK > j0)
    ov &= (np.arange(N_CMP_PAD)[:, None] < n_cmp) & (np.arange(LANES)[None, :] < n_sel)
    return ov.astype(np.float32)


def _topk_member(score, idx, n, axis):
    rank = jnp.zeros(score.shape, jnp.int32)
    for i in range(n):
        si = score[i:i + 1, :] if axis == 0 else score[:, i:i + 1]
        beats = (si > score) | ((si == score) & (i < idx))
        rank = rank + beats.astype(jnp.int32)
    return rank < TOP_N


def _sel_score(imp, j, cur):
    valid = j <= cur
    forced = (j == 0) | (j == cur) | (j == cur - 1)
    return jnp.where(valid, imp, -SEL_BONUS) + jnp.where(forced, SEL_BONUS, 0.0)


def _nsa_prompt_kernel(q_ref, qr_ref, kb_ref, vb_ref, cmp_ref, gates_ref, ovt_ref, e_ref,
                       o_ref, m_sc, acc_sc, *, tq, slc_chunk, row_block, n_sel):
    heads = range(N_KV_HEADS)
    q0 = pl.program_id(1) * tq
    qpos = q0 + lax.broadcasted_iota(jnp.int32, (tq, 1), 0)
    group = lambda ref, h: ref[0, h * GQA:(h + 1) * GQA].reshape(GQA * tq, HEAD_DIM)

    cidx = lax.broadcasted_iota(jnp.int32, (1, N_CMP_PAD), 1)
    real = cidx < N_CMP_PAD - 1
    cmask = (cidx * CMP_STRIDE + CMP_BLOCK - 1 <= qpos) & real
    qpos_t = q0 + lax.broadcasted_iota(jnp.int32, (1, N_KV_HEADS * tq), 1) % tq
    j_t = lax.broadcasted_iota(jnp.int32, (n_sel, 1), 0)
    eye = (j_t == lax.broadcasted_iota(jnp.int32, (1, n_sel), 1)).astype(BF16)
    s = jnp.concatenate([_nt(group(q_ref, h), _cmp_cols(cmp_ref, 0, 0, h)) for h in heads], axis=0)
    s = s.reshape(N_HEADS, tq, N_CMP_PAD)
    s = jnp.where(real, jnp.where(cmask, s, NEG), -jnp.inf)
    e = jnp.exp(s - jnp.max(s, axis=-1, keepdims=True))
    p = (e / jnp.sum(e, axis=-1, keepdims=True)) * cmask.astype(F32)
    pb = p.astype(BF16)
    o_cmp = [_nn(pb[h * GQA:(h + 1) * GQA].reshape(GQA * tq, N_CMP_PAD),
                 _cmp_cols(cmp_ref, 0, 1, h)).reshape(GQA, tq, HEAD_DIM) for h in heads]
    imp = []
    for h in heads:
        imp_h = _nt(ovt_ref[...], pb[h * GQA])
        for g in range(1, GQA):
            imp_h = imp_h + _nt(ovt_ref[...], pb[h * GQA + g])
        imp.append(imp_h)
    score_t = _sel_score(jnp.concatenate(imp, axis=1), j_t, qpos_t // SEL_BLOCK)
    sel_t = _topk_member(score_t, j_t, n_sel, axis=0).astype(BF16)
    sel_all = lax.dot_general(sel_t, eye, TN_DIMS, preferred_element_type=F32)
    sel = [sel_all[h * tq:(h + 1) * tq] for h in heads]

    qr4 = [group(qr_ref, h) for h in heads]
    pad_cols = jnp.zeros((GQA * tq, HEAD_DIM - n_sel), BF16)
    unpicked = [jnp.where(s > 0.5, 0.0, -SEL_MASK).astype(BF16) for s in sel]
    qsel4 = [jnp.concatenate([qr4[h], jnp.concatenate([unpicked[h]] * GQA, axis=0), pad_cols], axis=1)
             for h in heads]

    def attend(h, qx, k, v, allowed, first):
        n = k.shape[1]
        sc = _nn(qx, k)
        bias = jnp.where(allowed, 0.0, NEG)
        pes, ms, alphas = [], [], []
        for r0 in range(0, GQA * tq, row_block):
            t0 = r0 % tq
            x = sc[r0:r0 + row_block] + bias[t0:t0 + row_block]
            m = jnp.max(x, axis=-1, keepdims=True)
            if first:
                ms.append(jnp.broadcast_to(m, (row_block, LANES)))
            else:
                m_prev = m_sc[h, r0:r0 + row_block]
                m_wide = jnp.maximum(m_prev, m)
                alphas.append(jnp.exp(m_prev - m_wide))
                ms.append(m_wide)
                m = jnp.concatenate([m_wide] * (n // LANES), axis=1)
            pes.append(jnp.exp(x - m).astype(BF16))
        pv = _nt(jnp.concatenate(pes, axis=0), v)
        m_sc[h] = jnp.concatenate(ms, axis=0)
        acc_sc[h] = pv if first else jnp.concatenate(alphas, axis=0) * acc_sc[h] + pv

    def normalised(h):
        acc = acc_sc[h].reshape(GQA, tq, 2 * HEAD_DIM)
        return acc[:, :, 0:HEAD_DIM] / acc[:, :, HEAD_DIM:HEAD_DIM + 1]

    wn = WINDOW + tq
    w0 = pl.multiple_of(jnp.maximum(q0 - WINDOW, 0), LANES)
    kpos = w0 + lax.broadcasted_iota(jnp.int32, (1, wn), 1)
    in_window = (kpos <= qpos) & (kpos > qpos - WINDOW)
    for h in heads:
        attend(h, qr4[h], kb_ref[0, N_KV_HEADS + h, :, pl.ds(w0, wn)], vb_ref[0, N_KV_HEADS + h, :, pl.ds(w0, wn)],
               in_window, True)
    o_win = [normalised(h) for h in heads]

    for c0 in range(0, kb_ref.shape[3], slc_chunk):
        def chunk(c0=c0):
            kpos = c0 + lax.broadcasted_iota(jnp.int32, (1, slc_chunk), 1)
            member = jnp.concatenate([e_ref[:, c0:c0 + slc_chunk],
                                      jnp.zeros((HEAD_DIM - n_sel, slc_chunk), BF16)], axis=0)
            for h in heads:
                k = jnp.concatenate([kb_ref[0, h, :, c0:c0 + slc_chunk], member], axis=0)
                attend(h, qsel4[h], k, vb_ref[0, h, :, c0:c0 + slc_chunk], kpos <= qpos, c0 == 0)

        if c0 == 0:
            chunk()
        else:
            pl.when(q0 + tq > c0)(chunk)
    o_slc = [normalised(h) for h in heads]

    gt = gates_ref[...]
    for h in heads:
        for g in range(GQA):
            col = lambda b: gt[:, h * GATE_LANES + b * GQA + g:h * GATE_LANES + b * GQA + g + 1]
            o = col(0) * o_cmp[h][g] + col(1) * o_slc[h][g]
            o = o + col(2) * o_win[h][g]
            o_ref[:, (h * GQA + g) * HEAD_DIM:(h * GQA + g + 1) * HEAD_DIM] = o


def _nsa_prompt(q, qr, kb, vb, cmp, gates, nb, tt, tq, slc_chunk):
    n_sel = -(-tt // SEL_BLOCK)
    nq = tt // tq
    ovt = jnp.asarray(_overlap_matrix(n_sel)[:, :n_sel].T, BF16)
    expand = jnp.asarray(np.arange(tt)[None, :] // SEL_BLOCK == np.arange(n_sel)[:, None], BF16)
    qspec = pl.BlockSpec((1, N_HEADS, tq, HEAD_DIM), lambda n, i: (n, 0, i, 0))
    seq = lambda a: pl.BlockSpec((1,) + a.shape[1:], lambda n, i: (n,) + (0,) * (a.ndim - 1))
    full = lambda a: pl.BlockSpec(a.shape, lambda n, i: (0,) * a.ndim)
    rows = lambda w: pl.BlockSpec((tq, w), lambda n, i: (n * nq + i, 0))
    return pl.pallas_call(
        functools.partial(_nsa_prompt_kernel, tq=tq, slc_chunk=slc_chunk, row_block=32, n_sel=n_sel),
        grid=(nb, nq),
        in_specs=[qspec, qspec, seq(kb), seq(vb), seq(cmp), rows(N_KV_HEADS * GATE_LANES), full(ovt), full(expand)],
        out_specs=rows(ATTN_WIDTH), out_shape=jax.ShapeDtypeStruct((nb * tt, ATTN_WIDTH), F32),
        scratch_shapes=[pltpu.VMEM((N_KV_HEADS, GQA * tq, LANES), F32),
                        pltpu.VMEM((N_KV_HEADS, GQA * tq, 2 * HEAD_DIM), F32)],
        compiler_params=_params("parallel", "arbitrary"), name="nsa_prompt",
    )(q, qr, kb, vb, cmp, gates, ovt, expand)


def _nsa_sample_kernel(pt_ref, *refs, n_pages, steps, seqs):
    del pt_ref
    q_ref, qr_ref = refs[0:2]
    page_refs = refs[2:2 + seqs * n_pages]
    cmp_ref, cw_ref, newk_ref, newv_ref, gates_ref, ov_ref, o_ref, wout_ref = refs[2 + seqs * n_pages:]
    rows = GQA * steps
    past = n_pages * PAGE_SIZE
    wc = cw_ref.shape[-1]
    trow = lax.broadcasted_iota(jnp.int32, (rows, 1), 0) % steps
    lane = lax.broadcasted_iota(jnp.int32, (1, LANES), 1)
    n_sel = -(-(past + steps) // SEL_BLOCK)
    cur = (past + trow) // SEL_BLOCK
    real = lane < N_CMP_PAD - 1
    wlane = lax.broadcasted_iota(jnp.int32, (1, wc), 1)
    in_window = jnp.broadcast_to(wlane + (WINDOW - wc) > trow, (rows, wc))
    per_tile = PAGE_SIZE // SEL_BLOCK

    chains = [(i, h) for i in range(seqs) for h in range(N_KV_HEADS)]
    assert len(chains) * steps == rows, "the stacked top-k below holds one chain per query-group slot"
    s = jnp.concatenate([_nt(q_ref[i, h], _cmp_cols(cmp_ref, i, 0, h)) for i, h in chains], axis=0)
    s = jnp.where(real, s, -jnp.inf)
    e = jnp.exp(s - jnp.max(s, axis=-1, keepdims=True))
    pb_all = (e / jnp.sum(e, axis=-1, keepdims=True)).astype(BF16)
    group = lax.broadcasted_iota(jnp.int32, (rows, 1), 0) // steps
    o_cmps, imp_all = [], None
    for c, (i, h) in enumerate(chains):
        pb = pb_all[c * rows:(c + 1) * rows]
        o_cmps.append(_nn(pb, _cmp_cols(cmp_ref, i, 1, h)))
        part = _nn(pb, ov_ref[...])
        imp = part
        for g in range(1, GQA):
            imp = imp + pltpu.roll(part, g * steps, 0)
        imp_all = imp if c == 0 else jnp.where(group == c, imp, imp_all)
    score = jnp.where(lane < n_sel, _sel_score(imp_all, lane, cur), -jnp.inf)
    sel_all = (_topk_member(score, lane, n_sel, axis=1) & (lane < n_sel)).astype(F32)
    rolled = [sel_all] + [pltpu.roll(sel_all, k * steps, 0) for k in range(1, GQA)]

    for c, (i, h) in enumerate(chains):
        newk, newv = newk_ref[i], newv_ref[i]
        newk_b, newv_b = newk.astype(BF16), newv.astype(BF16)
        qr = qr_ref[i, h]
        pages = page_refs[i * n_pages:(i + 1) * n_pages]

        def new_ok(branch, h=h):
            base = (branch * N_KV_HEADS + h) * steps
            return (lane >= base) & (lane < base + steps) & (lane - base <= trow)

        sel_f = rolled[(0 - c) % GQA]
        for g in range(1, GQA):
            sel_f = jnp.where(group == g, rolled[(g - c) % GQA], sel_f)

        ks = jnp.concatenate([p[0, 0, h].astype(BF16) for p in pages] + [newk_b], axis=1)
        vs = jnp.concatenate([p[0, 1, h].astype(BF16) for p in pages] + [newv_b], axis=1)
        tiles = []
        for k in range(n_pages):
            m = sel_f[:, per_tile * k:per_tile * k + 1]
            for b in range(1, per_tile):
                m = jnp.where(lane < b * SEL_BLOCK, m, sel_f[:, per_tile * k + b:per_tile * k + b + 1])
            tiles.append(m > 0.5)
        tiles.append((sel_f[:, n_sel - 1:n_sel] > 0.5) & new_ok(0))
        msk = jnp.concatenate(tiles, axis=1)
        sc = jnp.where(msk, _nn(qr, ks), NEG)
        pe = jnp.where(msk, jnp.exp(sc - jnp.max(sc, axis=-1, keepdims=True)), 0.0)
        o_slc = _nt(pe.astype(BF16), vs) / jnp.sum(pe, axis=-1, keepdims=True)

        kw = jnp.concatenate([cw_ref[i, 0, h].astype(BF16), newk_b], axis=1)
        vw = jnp.concatenate([cw_ref[i, 1, h].astype(BF16), newv_b], axis=1)
        wmsk = jnp.concatenate([in_window, jnp.broadcast_to(new_ok(1), (rows, LANES))], axis=1)
        sw = jnp.where(wmsk, _nn(qr, kw), NEG)
        pw = jnp.where(wmsk, jnp.exp(sw - jnp.max(sw, axis=-1, keepdims=True)), 0.0)
        o_win = _nt(pw.astype(BF16), vw) / jnp.sum(pw, axis=-1, keepdims=True)

        gt = gates_ref[i, h]
        o_ref[i, h] = gt[:, 0:1] * o_cmps[c] + gt[:, 1:2] * o_slc + gt[:, 2:3] * o_win

        to_end = (LANES - steps - (N_KV_HEADS + h) * steps) % LANES
        for slot, new in ((0, newk), (1, newv)):
            shifted = pltpu.roll(cw_ref[i, slot, h], wc - steps, 1)
            wout_ref[i, slot, h, :, 0:wc - LANES] = shifted[:, 0:wc - LANES]
            wout_ref[i, slot, h, :, wc - LANES:wc] = jnp.where(lane >= LANES - steps, pltpu.roll(new, to_end, 1),
                                                               shifted[:, wc - LANES:wc])


def _nsa_sample(q, qr, pages, page_table, cmp, cache_win_t, new_k, new_v, gates, steps, seqs):
    nb, n_pages = page_table.shape
    n_sel = -(-(n_pages * PAGE_SIZE + steps) // SEL_BLOCK)
    ov = jnp.asarray(_overlap_matrix(n_sel), BF16)
    rows = GQA * steps
    lead = lambda a: pl.BlockSpec((seqs,) + a.shape[1:], lambda n, pt: (n,) + (0,) * (a.ndim - 1))
    page_spec = lambda i, k: pl.BlockSpec((1, 2, N_KV_HEADS, HEAD_DIM, PAGE_SIZE),
                                          lambda n, pt: (pt[n * seqs + i, k], 1, 0, 0, 0))
    page_specs = [page_spec(i, k) for i in range(seqs) for k in range(n_pages)]
    grid_spec = pltpu.PrefetchScalarGridSpec(
        num_scalar_prefetch=1, grid=(nb // seqs,),
        in_specs=[lead(q), lead(qr)] + page_specs
        + [lead(cmp), lead(cache_win_t), lead(new_k), lead(new_v), lead(gates),
           pl.BlockSpec(ov.shape, lambda n, pt: (0, 0))],
        out_specs=(pl.BlockSpec((seqs, N_KV_HEADS, rows, HEAD_DIM), lambda n, pt: (n, 0, 0, 0)), lead(cache_win_t)))
    return pl.pallas_call(
        functools.partial(_nsa_sample_kernel, n_pages=n_pages, steps=steps, seqs=seqs), grid_spec=grid_spec,
        out_shape=(jax.ShapeDtypeStruct((nb, N_KV_HEADS, rows, HEAD_DIM), F32),
                   jax.ShapeDtypeStruct(cache_win_t.shape, F32)),
        compiler_params=_params("parallel"), name="nsa_sample",
    )(page_table, q, qr, *([pages] * len(page_specs)), cmp, cache_win_t, new_k, new_v, gates, ov)


def _mlp_kernel(x_ref, oa_ref, yr_ref, ga_ref, wo_ref, nmlp_ref, wup_ref, wdn_ref, nfin_ref, y_ref, *, ff_chunk):
    a = _rms(oa_ref[...], ga_ref[...]).astype(BF16)
    x1 = x_ref[...] + (_nn(a, wo_ref[0:ATTN_WIDTH, :]) + _nn(yr_ref[...], wo_ref[ATTN_WIDTH:, :]))
    v = _rms(x1, nmlp_ref[...]).astype(BF16)
    acc = jnp.zeros_like(x1)
    for c in range(D_FF // ff_chunk):
        sl = slice(c * ff_chunk, (c + 1) * ff_chunk)
        hid = jnp.square(jnp.maximum(_nn(v, wup_ref[:, sl]), 0.0)).astype(BF16)
        acc = acc + _nn(hid, wdn_ref[sl, :])
    y_ref[...] = _rms(x1 + acc, nfin_ref[...])


def _mlp(x, o_attn, y_rnn, g_attn, w_out, norm_mlp, w_up, w_down, norm_final, tm, ff_chunk):
    rows = x.shape[0]
    row = lambda w: pl.BlockSpec((tm, w), lambda i: (i, 0))
    full = lambda a: pl.BlockSpec(a.shape, lambda i: (0,) * a.ndim)
    once = lambda a: pl.BlockSpec(a.shape, lambda i: (0,) * a.ndim, pipeline_mode=pl.Buffered(1))
    return pl.pallas_call(
        functools.partial(_mlp_kernel, ff_chunk=ff_chunk), grid=(rows // tm,),
        in_specs=[row(D_MODEL), row(ATTN_WIDTH), row(D_RNN), full(g_attn), once(w_out), full(norm_mlp),
                  once(w_up), once(w_down), full(norm_final)],
        out_specs=row(D_MODEL), out_shape=jax.ShapeDtypeStruct((rows, D_MODEL), F32),
        compiler_params=_params("parallel"), name="mlp",
    )(x, o_attn, y_rnn, g_attn, w_out, norm_mlp, w_up, w_down, norm_final)


def _block_diag(w):
    nb, bs, _ = w.shape
    eye = jnp.eye(nb, dtype=w.dtype)
    return (eye[:, None, :, None] * w[:, :, None, :]).reshape(nb * bs, nb * bs)


def kernel(x_prompt, x_sample, cache_kv, cache_win, state_conv, state_rnn, page_table, w_in, pe_ck, w_ck1, w_ck2,
           pe_cv, w_cv1, w_cv2, g_attn, g_rnn, conv_w, conv_b, w_ra, b_ra, w_ri, b_ri, lam, w_out, norm_mix,
           norm_mlp, w_up, w_down, norm_final):
    assert w_in.shape[0] == 1, "single layer"
    nbp, tp, _ = x_prompt.shape
    nbs, steps, _ = x_sample.shape
    n_pages = page_table.shape[1]

    wt = jnp.transpose(w_in[0]).astype(BF16)
    c_kv, c_g = ATTN_WIDTH, ATTN_WIDTH + 6 * KV_COLS
    c_rg = c_g + 3 * N_HEADS
    wq, wkv = wt[0:c_kv], wt[c_kv:c_g]
    wg = wt[c_g:c_rg].reshape(3, N_KV_HEADS, GQA, D_MODEL).transpose(1, 0, 2, 3).reshape(N_KV_HEADS, 3 * GQA, D_MODEL)
    wg = jnp.pad(wg, ((0, 0), (0, GATE_LANES - 3 * GQA), (0, 0))).reshape(N_KV_HEADS * GATE_LANES, D_MODEL)
    wr = jnp.concatenate([wt[c_rg:c_rg + 2 * D_RNN], wg], axis=0)
    row2 = lambda a: a.reshape(1, -1)
    rnn_w = (conv_w[0], row2(conv_b[0]), _block_diag(w_ra[0].astype(BF16)), _block_diag(w_ri[0].astype(BF16)),
             row2(b_ra[0]), row2(b_ri[0]), row2(lam[0]), row2(g_rnn[0]))
    per_cs = lambda k, v: jnp.repeat(jnp.stack([k, v]), N_KV_HEADS, axis=0)
    pe_cs = per_cs(pe_ck[0], pe_cv[0]).reshape(2 * N_KV_HEADS, 2, HALF, HEAD_DIM)
    cmp_pet = jnp.tile(pe_cs.transpose(1, 0, 3, 2), (1, 1, 1, BLOCKS_PER_PAGE)).reshape(2, CMP_COLS, PAGE_SIZE)
    w1_cs = per_cs(w_ck1[0], w_cv1[0]).transpose(1, 0, 2, 3)
    cmp_w1 = jax.vmap(_block_diag)(w1_cs.astype(BF16)).reshape(2, HALF, CMP_COLS, CMP_COLS)
    cmp_w2 = _block_diag(per_cs(w_ck2[0], w_cv2[0]).astype(BF16))
    mlp_w = (row2(g_attn[0]), w_out[0].astype(BF16), row2(norm_mlp[0]), w_up[0].astype(BF16),
             w_down[0].astype(BF16), row2(norm_final))
    nm = row2(norm_mix[0])

    tm = 512
    xp = x_prompt.reshape(nbp * tp, D_MODEL)
    q, qr, kvt, wint, kb, vb, rg, rx, gates = _proj(xp, jnp.arange(tp), nm, wq, wkv, wr, nbp, tp, tm)
    y_rnn, h_last = _rglru_prompt(rx, rg, rnn_w, nbp, tp, 256)
    own_pages = jnp.zeros((nbp, tp // PAGE_SIZE), jnp.int32)
    cmp = _compress(kvt, lambda k: pl.BlockSpec((1, CMP_COLS, PAGE_SIZE), lambda n, pt: (n, 0, k)), own_pages,
                    cmp_pet, cmp_w1, cmp_w2, nbp)
    o_attn = _nsa_prompt(q, qr, kb, vb, cmp, gates, nbp, tp, 256, 512)
    y_prompt = _mlp(xp, o_attn, y_rnn, *mlp_w, tm, 1024).reshape(nbp, tp, D_MODEL)
    wlen = min(WINDOW, tp)
    kv_prompt = kvt.reshape(nbp, 4, N_KV_HEADS, HEAD_DIM, tp).transpose(0, 4, 1, 2, 3)[None]
    win_prompt = wint[:, :, tp - wlen:].reshape(nbp, 2, N_KV_HEADS, HEAD_DIM, wlen).transpose(0, 4, 1, 2, 3)[None]
    conv_prompt = rx.reshape(nbp, tp, D_RNN)[:, tp - (CONV_WIDTH - 1):][None]
    h_prompt = h_last.reshape(1, nbp, D_RNN)

    rows_s = nbs * steps
    xs = x_sample.transpose(1, 0, 2).reshape(rows_s, D_MODEL)
    pos_s = PAST_LEN + jnp.arange(rows_s) // nbs
    q, qr, kvt, wint, _, _, rg, rx, gates = _proj(xs, pos_s, nm, wq, wkv, wr, 1, rows_s, rows_s)
    y_rnn, conv_s, h_s = _rglru_sample(rx, rg, state_conv[0].transpose(1, 0, 2), state_rnn[0], rnn_w)
    pages_t = cache_kv[0].transpose(0, 2, 3, 4, 1)
    cmp_page = lambda k: pl.BlockSpec((1, 2, N_KV_HEADS, HEAD_DIM, PAGE_SIZE), lambda n, pt: (pt[n, k], 0, 0, 0, 0))
    cmp = _compress(pages_t, cmp_page, page_table, cmp_pet, cmp_w1, cmp_w2, nbs)
    by_seq = lambda a: a.reshape(N_KV_HEADS, GQA, steps, nbs, HEAD_DIM).transpose(3, 0, 1, 2, 4).reshape(
        nbs, N_KV_HEADS, GQA * steps, HEAD_DIM)
    def new_tile(slc_rows, win_rows):
        t = jnp.stack([slc_rows, win_rows]).reshape(2 * N_KV_HEADS, HEAD_DIM, steps, nbs)
        t = t.transpose(3, 1, 0, 2).reshape(nbs, HEAD_DIM, 2 * N_KV_HEADS * steps)
        return jnp.pad(t, ((0, 0), (0, 0), (0, LANES - 2 * N_KV_HEADS * steps)))
    new_k = new_tile(kvt[0, 2 * KV_COLS:3 * KV_COLS], wint[0, 0:KV_COLS])
    new_v = new_tile(kvt[0, 3 * KV_COLS:4 * KV_COLS], wint[0, KV_COLS:2 * KV_COLS])
    gates_s = gates.reshape(steps, nbs, N_KV_HEADS, GATE_LANES)[..., :3 * GQA].reshape(steps, nbs, N_KV_HEADS, 3, GQA)
    gates_s = gates_s.transpose(1, 2, 4, 0, 3).reshape(nbs, N_KV_HEADS, GQA * steps, 3)
    cache_win_t = cache_win[0].transpose(0, 2, 3, 4, 1)
    o_s, win_t = _nsa_sample(by_seq(q[0]), by_seq(qr[0]), pages_t, page_table, cmp, cache_win_t, new_k, new_v,
                             gates_s, steps, 2)
    o_attn = o_s.reshape(nbs, N_KV_HEADS, GQA, steps, HEAD_DIM).transpose(3, 0, 1, 2, 4).reshape(rows_s, ATTN_WIDTH)
    y_sample = _mlp(xs, o_attn, y_rnn, *mlp_w, rows_s, 1024).reshape(steps, nbs, D_MODEL).transpose(1, 0, 2)
    kv_sample = kvt[0].reshape(4, N_KV_HEADS, HEAD_DIM, steps, nbs).transpose(4, 3, 0, 1, 2)[None]
    win_sample = win_t.transpose(0, 4, 1, 2, 3)[None]
    conv_sample = conv_s.transpose(1, 0, 2)[None]
    h_sample = h_s[None]

    return (y_prompt, y_sample, kv_prompt, kv_sample, win_prompt, win_sample, conv_prompt, conv_sample,
            h_prompt, h_sample)
```

```python
import functools

import jax
import jax.numpy as jnp
import numpy as np
from jax import lax
from jax.experimental import pallas as pl
from jax.experimental.pallas import tpu as pltpu

D_MODEL = 1024
PAST_LEN = 2048
PAGE_SIZE = 128
HEAD_DIM = 64
N_HEADS = 8
N_KV_HEADS = 2
GQA = N_HEADS // N_KV_HEADS
ATTN_WIDTH = N_HEADS * HEAD_DIM
D_RNN = D_MODEL - ATTN_WIDTH
RNN_BLOCKS = 8
CONV_WIDTH = 4
LRU_C = 8.0
D_FF = 4 * D_MODEL
ROT_DIM = HEAD_DIM // 4
ROT_HALF = ROT_DIM // 2
ROPE_THETA = 500000.0
CMP_BLOCK = 32
CMP_STRIDE = 16
SEL_BLOCK = 64
TOP_N = 16
WINDOW = 512
KV_COLS = N_KV_HEADS * HEAD_DIM
EPS = 1e-6
NEG = -1e30
SEL_BONUS = 1e4
SEL_MASK = 2.0 ** 100
SCALE = HEAD_DIM ** -0.5

SUBLANES = 8
LANES = 128
VMEM_LIMIT = 48 * 1024 * 1024

N_CMP_PAD = 128
GATE_ROWS = 16

F32 = jnp.float32
BF16 = jnp.bfloat16
NT_DIMS = (((1,), (1,)), ((), ()))
TN_DIMS = (((0,), (0,)), ((), ()))


def _nt(a, b):
    return lax.dot_general(a, b, NT_DIMS, preferred_element_type=F32)


def _nn(a, b):
    return jnp.dot(a, b, preferred_element_type=F32)


def _rms(x, g):
    return x * lax.rsqrt(jnp.mean(x * x, axis=-1, keepdims=True) + EPS) * g


def _params(*sem):
    return pltpu.CompilerParams(dimension_semantics=sem, vmem_limit_bytes=VMEM_LIMIT)


def _proj_kernel(x_ref, nm_ref, wq_ref, wkv_ref, wr_ref, wg_ref, cos_ref, sin_ref,
                 qt_ref, qrt_ref, kvt_ref, wint_ref, kb_ref, vb_ref, rg_ref, rx_ref, gt_ref):
    u = _rms(x_ref[...], nm_ref[...]).astype(BF16)
    cos, sin = cos_ref[...], sin_ref[...]

    def rotated(a, r0):
        x1, x2 = a[r0:r0 + ROT_HALF], a[r0 + ROT_HALF:r0 + ROT_DIM]
        return x1 * cos - x2 * sin, x2 * cos + x1 * sin

    qt = _nt(wq_ref[...], u)
    qt_ref[0] = (qt * SCALE).astype(BF16)
    parts = []
    for head in range(N_HEADS):
        r0 = head * HEAD_DIM
        parts += [*rotated(qt, r0), qt[r0 + ROT_DIM:r0 + HEAD_DIM]]
    qrt_ref[0] = (jnp.concatenate(parts, axis=0) * SCALE).astype(BF16)

    kvt = _nt(wkv_ref[...], u)
    kvt_ref[0] = kvt[0:4 * KV_COLS]
    wint_ref[0] = kvt[4 * KV_COLS:6 * KV_COLS]
    for out_ref, src0, dst0 in ((kvt_ref, 2 * KV_COLS, 2 * KV_COLS), (wint_ref, 4 * KV_COLS, 0)):
        for h in range(N_KV_HEADS):
            d = dst0 + h * HEAD_DIM
            out_ref[0, d:d + ROT_HALF], out_ref[0, d + ROT_HALF:d + ROT_DIM] = rotated(kvt, src0 + h * HEAD_DIM)
    tm = x_ref.shape[0]
    ones_row = (lax.broadcasted_iota(jnp.int32, (HEAD_DIM, tm), 0) == 0).astype(BF16)
    for branch, src_ref, k0, v0 in ((0, kvt_ref, 2 * KV_COLS, 3 * KV_COLS), (1, wint_ref, 0, KV_COLS)):
        for h in range(N_KV_HEADS):
            i, r = branch * N_KV_HEADS + h, h * HEAD_DIM
            kb_ref[0, i] = src_ref[0, k0 + r:k0 + r + HEAD_DIM].astype(BF16)
            vb_ref[0, i, 0:HEAD_DIM] = src_ref[0, v0 + r:v0 + r + HEAD_DIM].astype(BF16)
            vb_ref[0, i, HEAD_DIM:2 * HEAD_DIM] = ones_row

    r = _nt(u, wr_ref[...])
    rg_ref[...] = r[:, 0:D_RNN]
    rx_ref[...] = r[:, D_RNN:2 * D_RNN]
    gt_ref[0] = jax.nn.sigmoid(_nt(wg_ref[...], u))


def _rope_tables(pos):
    inv = ROPE_THETA ** (-jnp.arange(ROT_HALF, dtype=F32) / ROT_HALF)
    ang = pos.astype(F32)[:, None] * inv
    return jnp.cos(ang).T, jnp.sin(ang).T


def _proj(x, pos, nm, wq, wkv, wr, wg, nb, tt, tm):
    rows = nb * tt
    nt = tt // tm
    cos, sin = _rope_tables(pos)
    row = lambda w: pl.BlockSpec((tm, w), lambda i: (i, 0))
    full = lambda a: pl.BlockSpec(a.shape, lambda i: (0,) * a.ndim)
    tab = pl.BlockSpec((ROT_HALF, tm), lambda i: (0, i % nt))
    tr = lambda r: pl.BlockSpec((1, r, tm), lambda i: (i // nt, 0, i % nt))
    out_shape = (
        jax.ShapeDtypeStruct((nb, ATTN_WIDTH, tt), BF16),
        jax.ShapeDtypeStruct((nb, ATTN_WIDTH, tt), BF16),
        jax.ShapeDtypeStruct((nb, 4 * KV_COLS, tt), F32),
        jax.ShapeDtypeStruct((nb, 2 * KV_COLS, tt), F32),
        jax.ShapeDtypeStruct((nb, 2 * N_KV_HEADS, HEAD_DIM, tt), BF16),
        jax.ShapeDtypeStruct((nb, 2 * N_KV_HEADS, 2 * HEAD_DIM, tt), BF16),
        jax.ShapeDtypeStruct((rows, D_RNN), F32),
        jax.ShapeDtypeStruct((rows, D_RNN), F32),
        jax.ShapeDtypeStruct((nb, N_KV_HEADS * GATE_ROWS, tt), F32),
    )
    tr4 = lambda r: pl.BlockSpec((1, 2 * N_KV_HEADS, r, tm), lambda i: (i // nt, 0, 0, i % nt))
    out_specs = (tr(ATTN_WIDTH), tr(ATTN_WIDTH), tr(4 * KV_COLS), tr(2 * KV_COLS), tr4(HEAD_DIM), tr4(2 * HEAD_DIM),
                 row(D_RNN), row(D_RNN), tr(N_KV_HEADS * GATE_ROWS))
    return pl.pallas_call(
        _proj_kernel, grid=(rows // tm,),
        in_specs=[row(D_MODEL), full(nm), full(wq), full(wkv), full(wr), full(wg), tab, tab],
        out_specs=out_specs, out_shape=out_shape, compiler_params=_params("parallel"), name="proj",
    )(x, nm, wq, wkv, wr, wg, cos, sin)


def _expm1(x):
    return jnp.tanh(0.5 * x) * (jnp.exp(x) + 1.0)


def _softplus(x):
    return jnp.maximum(x, 0.0) + jnp.log1p(jnp.exp(-jnp.abs(x)))


def _lru_coeffs(xc, wra_ref, wri_ref, bra_ref, bri_ref, lam_ref):
    xb = xc.astype(BF16)
    r = jax.nn.sigmoid(_nn(xb, wra_ref[...]) + bra_ref[...])
    i = jax.nn.sigmoid(_nn(xb, wri_ref[...]) + bri_ref[...])
    log_a = -LRU_C * r * _softplus(-lam_ref[...])
    a = jnp.exp(log_a)
    b = jnp.sqrt(-_expm1(2.0 * log_a)) * (i * xc)
    return a, b


def _rnn_out(rg, h, g):
    return _rms(jax.nn.gelu(rg) * h, g).astype(BF16)


def _rglru_prompt_kernel(rx_ref, rg_ref, cw_ref, cb_ref, wra_ref, wri_ref, bra_ref, bri_ref, lam_ref, g_ref,
                         y_ref, hl_ref, prev_sc, h_sc):
    tt = rx_ref.shape[0]

    @pl.when(pl.program_id(1) == 0)
    def _():
        prev_sc[...] = jnp.zeros_like(prev_sc)
        h_sc[...] = jnp.zeros_like(h_sc)

    rx = rx_ref[...]
    ext = jnp.concatenate([prev_sc[...], rx], axis=0)
    shifted = lambda d: pltpu.roll(ext, d, 0)[SUBLANES:SUBLANES + tt]
    cw = cw_ref[...]
    xc = cb_ref[...] + cw[0:1] * shifted(3)
    xc = xc + cw[1:2] * shifted(2)
    xc = xc + cw[2:3] * shifted(1)
    xc = xc + cw[3:4] * rx
    prev_sc[...] = rx[tt - SUBLANES:tt]

    a, b = _lru_coeffs(xc, wra_ref, wri_ref, bra_ref, bri_ref, lam_ref)
    row = lax.broadcasted_iota(jnp.int32, (tt, 1), 0)
    s = 1
    while s < tt:
        if s < SUBLANES:
            a_sh = jnp.where(row >= s, pltpu.roll(a, s, 0), 1.0)
            b_sh = jnp.where(row >= s, pltpu.roll(b, s, 0), 0.0)
        else:
            a_sh = jnp.concatenate([jnp.ones((s, a.shape[1]), F32), a[:tt - s]], axis=0)
            b_sh = jnp.concatenate([jnp.zeros((s, b.shape[1]), F32), b[:tt - s]], axis=0)
        b = a * b_sh + b
        a = a * a_sh
        s *= 2
    h = a * h_sc[...] + b
    h_sc[...] = h[tt - 1:tt]
    hl_ref[0] = h[tt - 1:tt]
    y_ref[...] = _rnn_out(rg_ref[...], h, g_ref[...])


def _rglru_sample_kernel(rx_ref, rg_ref, cp_ref, h0_ref, cw_ref, cb_ref, wra_ref, wri_ref, bra_ref, bri_ref,
                         lam_ref, g_ref, y_ref, cs_ref, hl_ref):
    nb = h0_ref.shape[0]
    steps = rx_ref.shape[0] // nb
    xp = [cp_ref[k] for k in range(CONV_WIDTH - 1)] + [rx_ref[t * nb:(t + 1) * nb] for t in range(steps)]
    cw, cb = cw_ref[...], cb_ref[...]
    xcs = []
    for t in range(steps):
        xc = cb + cw[0:1] * xp[t]
        for tap in range(1, CONV_WIDTH):
            xc = xc + cw[tap:tap + 1] * xp[t + tap]
        xcs.append(xc)
    a, b = _lru_coeffs(jnp.concatenate(xcs, axis=0), wra_ref, wri_ref, bra_ref, bri_ref, lam_ref)
    h = h0_ref[...]
    for t in range(steps):
        sl = slice(t * nb, (t + 1) * nb)
        h = a[sl] * h + b[sl]
        y_ref[sl] = _rnn_out(rg_ref[sl], h, g_ref[...])
    for k in range(CONV_WIDTH - 1):
        cs_ref[k] = xp[steps + k]
    hl_ref[...] = h


def _rglru_prompt(rx, rg, weights, nb, tt, chunk):
    nc = tt // chunk
    row = pl.BlockSpec((chunk, D_RNN), lambda n, c: (n * nc + c, 0))
    full = lambda a: pl.BlockSpec(a.shape, lambda n, c: (0,) * a.ndim)
    return pl.pallas_call(
        _rglru_prompt_kernel, grid=(nb, nc),
        in_specs=[row, row] + [full(w) for w in weights],
        out_specs=(row, pl.BlockSpec((1, 1, D_RNN), lambda n, c: (n, 0, 0))),
        out_shape=(jax.ShapeDtypeStruct((nb * tt, D_RNN), BF16), jax.ShapeDtypeStruct((nb, 1, D_RNN), F32)),
        scratch_shapes=[pltpu.VMEM((SUBLANES, D_RNN), F32), pltpu.VMEM((1, D_RNN), F32)],
        compiler_params=_params("parallel", "arbitrary"), name="rglru_prompt",
    )(rx, rg, *weights)


def _rglru_sample(rx, rg, conv_prev, h0, weights):
    nb = h0.shape[0]
    return pl.pallas_call(
        _rglru_sample_kernel,
        out_shape=(jax.ShapeDtypeStruct(rx.shape, BF16), jax.ShapeDtypeStruct(conv_prev.shape, F32),
                   jax.ShapeDtypeStruct((nb, D_RNN), F32)),
        compiler_params=pltpu.CompilerParams(vmem_limit_bytes=VMEM_LIMIT), name="rglru_sample",
    )(rx, rg, conv_prev, h0, *weights)


CMP_COLS = 2 * KV_COLS
HALF = CMP_BLOCK // 2
BLOCKS_PER_PAGE = PAGE_SIZE // CMP_STRIDE


def _compress_kernel(pt_ref, *refs, n_pages):
    del pt_ref
    page_refs = refs[:n_pages]
    pet_ref, perm_ref, w1_ref, w2_ref, out_ref, z_sc = refs[n_pages:]
    for k in range(n_pages):
        xt = page_refs[k][0].reshape(CMP_COLS, PAGE_SIZE)
        for half in range(2):
            a = (xt + pet_ref[half]).astype(BF16)
            z_sc[half, k] = _nt(perm_ref[...], a)
    pre = []
    for half in range(2):
        acc = jnp.zeros((N_CMP_PAD, CMP_COLS), F32)
        for j in range(CMP_STRIDE):
            rows = z_sc[half, :, j * BLOCKS_PER_PAGE:(j + 1) * BLOCKS_PER_PAGE, :]
            acc = acc + _nn(rows.reshape(N_CMP_PAD, CMP_COLS).astype(BF16), w1_ref[half, j])
        pre.append(acc)
    hid = jax.nn.gelu(pre[0] + pltpu.roll(pre[1], N_CMP_PAD - 1, 0))
    out_ref[0] = _nn(hid.astype(BF16), w2_ref[...]).astype(BF16)


def _compress(pages, page_spec, page_table, pet, w1, w2, nb):
    n_pages = page_table.shape[1]
    assert n_pages * BLOCKS_PER_PAGE == N_CMP_PAD
    pos = np.arange(PAGE_SIZE)
    perm = jnp.asarray((pos % CMP_STRIDE * BLOCKS_PER_PAGE + pos // CMP_STRIDE)[None, :]
                       == np.arange(PAGE_SIZE)[:, None], BF16)
    full = lambda a: pl.BlockSpec(a.shape, lambda n, pt: (0,) * a.ndim)
    once = lambda a: pl.BlockSpec(a.shape, lambda n, pt: (0,) * a.ndim, pipeline_mode=pl.Buffered(1))
    grid_spec = pltpu.PrefetchScalarGridSpec(
        num_scalar_prefetch=1, grid=(nb,),
        in_specs=[page_spec(k) for k in range(n_pages)] + [full(pet), full(perm), once(w1), full(w2)],
        out_specs=pl.BlockSpec((1, N_CMP_PAD, CMP_COLS), lambda n, pt: (n, 0, 0)),
        scratch_shapes=[pltpu.VMEM((2, n_pages, PAGE_SIZE, CMP_COLS), F32)])
    return pl.pallas_call(
        functools.partial(_compress_kernel, n_pages=n_pages), grid_spec=grid_spec,
        out_shape=jax.ShapeDtypeStruct((nb, N_CMP_PAD, CMP_COLS), BF16),
        compiler_params=_params("parallel"), name="compress",
    )(page_table, *([pages] * n_pages), pet, perm, w1, w2)


def _cmp_cols(cmp_ref, i, slot, h):
    c0 = slot * KV_COLS + h * HEAD_DIM
    return cmp_ref[i, :, c0:c0 + HEAD_DIM]


def _overlap_matrix(n_sel):
    n_cmp = N_CMP_PAD - 1
    c0 = np.arange(N_CMP_PAD)[:, None] * CMP_STRIDE
    j0 = np.arange(LANES)[None, :] * SEL_BLOCK
    ov = (c0 < j0 + SEL_BLOCK) & (c0 + CMP_BLOCK > j0)
    ov &= (np.arange(N_CMP_PAD)[:, None] < n_cmp) & (np.arange(LANES)[None, :] < n_sel)
    return ov.astype(np.float32)


def _topk_member(score, idx, n, axis):
    rank = jnp.zeros(score.shape, jnp.int32)
    for i in range(n):
        si = score[i:i + 1, :] if axis == 0 else score[:, i:i + 1]
        beats = (si > score) | ((si == score) & (i < idx))
        rank = rank + beats.astype(jnp.int32)
    return rank < TOP_N


def _sel_score(imp, j, cur):
    valid = j <= cur
    forced = (j == 0) | (j == cur) | (j == cur - 1)
    return jnp.where(valid, imp, -SEL_BONUS) + jnp.where(forced, SEL_BONUS, 0.0)


def _nsa_prompt_kernel(qt_ref, qrt_ref, kb_ref, vb_ref, cmp_ref, gt_ref, ovt_ref, e_ref,
                       o_ref, m_sc, acc_sc, *, tq, slc_chunk, col_block, n_sel):
    heads = range(N_KV_HEADS)
    q0 = pl.program_id(1) * tq
    wide = GQA * tq
    qpos = q0 + lax.broadcasted_iota(jnp.int32, (1, tq), 1)
    tile = lambda a, n: jnp.concatenate([a] * n, axis=1)
    group = lambda ref, h: jnp.concatenate(
        [ref[0, (h * GQA + g) * HEAD_DIM:(h * GQA + g + 1) * HEAD_DIM, :] for g in range(GQA)], axis=1)

    cidx = lax.broadcasted_iota(jnp.int32, (N_CMP_PAD, 1), 0)
    real = cidx < N_CMP_PAD - 1
    qpos_all = q0 + lax.broadcasted_iota(jnp.int32, (1, N_HEADS * tq), 1) % tq
    cmask = (cidx * CMP_STRIDE + CMP_BLOCK - 1 <= qpos_all) & real
    s = jnp.concatenate([_nn(_cmp_cols(cmp_ref, 0, 0, h), group(qt_ref, h)) for h in heads], axis=1)
    s = jnp.where(real, jnp.where(cmask, s, NEG), -jnp.inf)
    e = jnp.exp(s - jnp.max(s, axis=0, keepdims=True))
    p = (e / jnp.sum(e, axis=0, keepdims=True)) * cmask.astype(F32)
    pb = p.astype(BF16)
    o_cmp = [lax.dot_general(_cmp_cols(cmp_ref, 0, 1, h), pb[:, h * wide:(h + 1) * wide], TN_DIMS,
                             preferred_element_type=F32) for h in heads]

    imp = []
    for h in heads:
        imp_h = _nn(ovt_ref[...], pb[:, h * wide:h * wide + tq])
        for g in range(1, GQA):
            imp_h = imp_h + _nn(ovt_ref[...], pb[:, h * wide + g * tq:h * wide + (g + 1) * tq])
        imp.append(imp_h)
    j = lax.broadcasted_iota(jnp.int32, (n_sel, 1), 0)
    qpos_2 = q0 + lax.broadcasted_iota(jnp.int32, (1, N_KV_HEADS * tq), 1) % tq
    picked = _topk_member(_sel_score(jnp.concatenate(imp, axis=1), j, qpos_2 // SEL_BLOCK), j, n_sel, axis=0)
    unpicked = jnp.where(picked, 0.0, -SEL_MASK).astype(BF16)
    pad_rows = jnp.zeros((HEAD_DIM - n_sel, wide), BF16)
    qr = [group(qrt_ref, h) for h in heads]
    qsel = [jnp.concatenate([qr[h], tile(unpicked[:, h * tq:(h + 1) * tq], GQA), pad_rows], axis=0)
            for h in heads]

    def attend(h, qx, k, v, allowed, first):
        sc = lax.dot_general(k, qx, TN_DIMS, preferred_element_type=F32)
        bias = jnp.where(allowed, 0.0, NEG)
        pes, ms, alphas = [], [], []
        for c0 in range(0, wide, col_block):
            t0 = c0 % tq
            x = sc[:, c0:c0 + col_block] + bias[:, t0:t0 + col_block]
            m = jnp.max(x, axis=0, keepdims=True)
            if not first:
                m_prev = m_sc[h, :, c0:c0 + col_block]
                m = jnp.maximum(m_prev, m)
                alphas.append(jnp.exp(m_prev - m))
            ms.append(m)
            pes.append(jnp.exp(x - m).astype(BF16))
        pv = _nn(v, jnp.concatenate(pes, axis=1))
        m_sc[h] = jnp.concatenate(ms, axis=1)
        acc_sc[h] = pv if first else jnp.concatenate(alphas, axis=1) * acc_sc[h] + pv

    def normalised(h):
        acc = acc_sc[h]
        return acc[0:HEAD_DIM] / acc[HEAD_DIM:HEAD_DIM + 1]

    wn = WINDOW + tq
    w0 = pl.multiple_of(jnp.maximum(q0 - WINDOW, 0), LANES)
    kpos = w0 + lax.broadcasted_iota(jnp.int32, (wn, 1), 0)
    in_window = (kpos <= qpos) & (kpos > qpos - WINDOW)
    for h in heads:
        attend(h, qr[h], kb_ref[0, N_KV_HEADS + h, :, pl.ds(w0, wn)], vb_ref[0, N_KV_HEADS + h, :, pl.ds(w0, wn)],
               in_window, True)
    o_win = [normalised(h) for h in heads]

    for c0 in range(0, kb_ref.shape[3], slc_chunk):
        def chunk(c0=c0):
            kpos = c0 + lax.broadcasted_iota(jnp.int32, (slc_chunk, 1), 0)
            member = jnp.concatenate([e_ref[:, c0:c0 + slc_chunk],
                                      jnp.zeros((HEAD_DIM - n_sel, slc_chunk), BF16)], axis=0)
            for h in heads:
                k = jnp.concatenate([kb_ref[0, h, :, c0:c0 + slc_chunk], member], axis=0)
                attend(h, qsel[h], k, vb_ref[0, h, :, c0:c0 + slc_chunk], kpos <= qpos, c0 == 0)

        if c0 == 0:
            chunk()
        else:
            pl.when(q0 + tq > c0)(chunk)
    o_slc = [normalised(h) for h in heads]

    gt = gt_ref[0]
    for h in heads:
        for g in range(GQA):
            gate = lambda b: gt[h * GATE_ROWS + b * GQA + g:h * GATE_ROWS + b * GQA + g + 1, :]
            cols = slice(g * tq, (g + 1) * tq)
            o = gate(0) * o_cmp[h][:, cols] + gate(1) * o_slc[h][:, cols]
            o = o + gate(2) * o_win[h][:, cols]
            o_ref[(h * GQA + g) * HEAD_DIM:(h * GQA + g + 1) * HEAD_DIM, :] = o


def _nsa_prompt(qt, qrt, kb, vb, cmp, gt, nb, tt, tq, slc_chunk):
    n_sel = -(-tt // SEL_BLOCK)
    nq = tt // tq
    ovt = jnp.asarray(_overlap_matrix(n_sel)[:, :n_sel].T, BF16)
    expand = jnp.asarray(np.arange(tt)[None, :] // SEL_BLOCK == np.arange(n_sel)[:, None], BF16)
    cols = lambda a: pl.BlockSpec((1, a.shape[1], tq), lambda n, i: (n, 0, i))
    seq = lambda a: pl.BlockSpec((1,) + a.shape[1:], lambda n, i: (n,) + (0,) * (a.ndim - 1))
    full = lambda a: pl.BlockSpec(a.shape, lambda n, i: (0,) * a.ndim)
    return pl.pallas_call(
        functools.partial(_nsa_prompt_kernel, tq=tq, slc_chunk=slc_chunk, col_block=LANES, n_sel=n_sel),
        grid=(nb, nq),
        in_specs=[cols(qt), cols(qrt), seq(kb), seq(vb), seq(cmp), cols(gt), full(ovt), full(expand)],
        out_specs=pl.BlockSpec((ATTN_WIDTH, tq), lambda n, i: (0, n * nq + i)),
        out_shape=jax.ShapeDtypeStruct((ATTN_WIDTH, nb * tt), F32),
        scratch_shapes=[pltpu.VMEM((N_KV_HEADS, 1, GQA * tq), F32),
                        pltpu.VMEM((N_KV_HEADS, 2 * HEAD_DIM, GQA * tq), F32)],
        compiler_params=_params("parallel", "arbitrary"), name="nsa_prompt",
    )(qt, qrt, kb, vb, cmp, gt, ovt, expand)


def _nsa_sample_kernel(pt_ref, *refs, n_pages, steps, seqs):
    del pt_ref
    q_ref, qr_ref = refs[0:2]
    page_refs = refs[2:2 + seqs * n_pages]
    cmp_ref, cw_ref, newk_ref, newv_ref, gates_ref, ov_ref, o_ref, wout_ref = refs[2 + seqs * n_pages:]
    rows = GQA * steps
    past = n_pages * PAGE_SIZE
    wc = cw_ref.shape[-1]
    trow = lax.broadcasted_iota(jnp.int32, (rows, 1), 0) % steps
    lane = lax.broadcasted_iota(jnp.int32, (1, LANES), 1)
    n_sel = -(-(past + steps) // SEL_BLOCK)
    cur = (past + trow) // SEL_BLOCK
    real = lane < N_CMP_PAD - 1
    wlane = lax.broadcasted_iota(jnp.int32, (1, wc), 1)
    in_window = jnp.broadcast_to(wlane + (WINDOW - wc) > trow, (rows, wc))
    per_tile = PAGE_SIZE // SEL_BLOCK

    chains = [(i, h) for i in range(seqs) for h in range(N_KV_HEADS)]
    assert len(chains) * steps == rows, "the stacked top-k below holds one chain per query-group slot"
    s = jnp.concatenate([_nt(q_ref[i, h], _cmp_cols(cmp_ref, i, 0, h)) for i, h in chains], axis=0)
    s = jnp.where(real, s, -jnp.inf)
    e = jnp.exp(s - jnp.max(s, axis=-1, keepdims=True))
    pb_all = (e / jnp.sum(e, axis=-1, keepdims=True)).astype(BF16)
    group = lax.broadcasted_iota(jnp.int32, (rows, 1), 0) // steps
    o_cmps, imp_all = [], None
    for c, (i, h) in enumerate(chains):
        pb = pb_all[c * rows:(c + 1) * rows]
        o_cmps.append(_nn(pb, _cmp_cols(cmp_ref, i, 1, h)))
        part = _nn(pb, ov_ref[...])
        imp = part
        for g in range(1, GQA):
            imp = imp + pltpu.roll(part, g * steps, 0)
        imp_all = imp if c == 0 else jnp.where(group == c, imp, imp_all)
    score = jnp.where(lane < n_sel, _sel_score(imp_all, lane, cur), -jnp.inf)
    sel_all = (_topk_member(score, lane, n_sel, axis=1) & (lane < n_sel)).astype(F32)
    rolled = [sel_all] + [pltpu.roll(sel_all, k * steps, 0) for k in range(1, GQA)]

    for c, (i, h) in enumerate(chains):
        newk, newv = newk_ref[i], newv_ref[i]
        newk_b, newv_b = newk.astype(BF16), newv.astype(BF16)
        qr = qr_ref[i, h]
        pages = page_refs[i * n_pages:(i + 1) * n_pages]

        def new_ok(branch, h=h):
            base = (branch * N_KV_HEADS + h) * steps
            return (lane >= base) & (lane < base + steps) & (lane - base <= trow)

        sel_f = rolled[(0 - c) % GQA]
        for g in range(1, GQA):
            sel_f = jnp.where(group == g, rolled[(g - c) % GQA], sel_f)

        ks = jnp.concatenate([p[0, 0, h].astype(BF16) for p in pages] + [newk_b], axis=1)
        vs = jnp.concatenate([p[0, 1, h].astype(BF16) for p in pages] + [newv_b], axis=1)
        tiles = []
        for k in range(n_pages):
            m = sel_f[:, per_tile * k:per_tile * k + 1]
            for b in range(1, per_tile):
                m = jnp.where(lane < b * SEL_BLOCK, m, sel_f[:, per_tile * k + b:per_tile * k + b + 1])
            tiles.append(m > 0.5)
        tiles.append((sel_f[:, n_sel - 1:n_sel] > 0.5) & new_ok(0))
        msk = jnp.concatenate(tiles, axis=1)
        sc = jnp.where(msk, _nn(qr, ks), NEG)
        pe = jnp.where(msk, jnp.exp(sc - jnp.max(sc, axis=-1, keepdims=True)), 0.0)
        o_slc = _nt(pe.astype(BF16), vs) / jnp.sum(pe, axis=-1, keepdims=True)

        kw = jnp.concatenate([cw_ref[i, 0, h].astype(BF16), newk_b], axis=1)
        vw = jnp.concatenate([cw_ref[i, 1, h].astype(BF16), newv_b], axis=1)
        wmsk = jnp.concatenate([in_window, jnp.broadcast_to(new_ok(1), (rows, LANES))], axis=1)
        sw = jnp.where(wmsk, _nn(qr, kw), NEG)
        pw = jnp.where(wmsk, jnp.exp(sw - jnp.max(sw, axis=-1, keepdims=True)), 0.0)
        o_win = _nt(pw.astype(BF16), vw) / jnp.sum(pw, axis=-1, keepdims=True)

        gt = gates_ref[i, h]
        o_ref[i, h] = gt[:, 0:1] * o_cmps[c] + gt[:, 1:2] * o_slc + gt[:, 2:3] * o_win

        to_end = (LANES - steps - (N_KV_HEADS + h) * steps) % LANES
        for slot, new in ((0, newk), (1, newv)):
            shifted = pltpu.roll(cw_ref[i, slot, h], wc - steps, 1)
            wout_ref[i, slot, h, :, 0:wc - LANES] = shifted[:, 0:wc - LANES]
            wout_ref[i, slot, h, :, wc - LANES:wc] = jnp.where(lane >= LANES - steps, pltpu.roll(new, to_end, 1),
                                                               shifted[:, wc - LANES:wc])


def _nsa_sample(q, qr, pages, page_table, cmp, cache_win_t, new_k, new_v, gates, steps, seqs):
    nb, n_pages = page_table.shape
    n_sel = -(-(n_pages * PAGE_SIZE + steps) // SEL_BLOCK)
    ov = jnp.asarray(_overlap_matrix(n_sel), BF16)
    rows = GQA * steps
    lead = lambda a: pl.BlockSpec((seqs,) + a.shape[1:], lambda n, pt: (n,) + (0,) * (a.ndim - 1))
    page_spec = lambda i, k: pl.BlockSpec((1, 2, N_KV_HEADS, HEAD_DIM, PAGE_SIZE),
                                          lambda n, pt: (pt[n * seqs + i, k], 1, 0, 0, 0))
    page_specs = [page_spec(i, k) for i in range(seqs) for k in range(n_pages)]
    grid_spec = pltpu.PrefetchScalarGridSpec(
        num_scalar_prefetch=1, grid=(nb // seqs,),
        in_specs=[lead(q), lead(qr)] + page_specs
        + [lead(cmp), lead(cache_win_t), lead(new_k), lead(new_v), lead(gates),
           pl.BlockSpec(ov.shape, lambda n, pt: (0, 0))],
        out_specs=(pl.BlockSpec((seqs, N_KV_HEADS, rows, HEAD_DIM), lambda n, pt: (n, 0, 0, 0)), lead(cache_win_t)))
    return pl.pallas_call(
        functools.partial(_nsa_sample_kernel, n_pages=n_pages, steps=steps, seqs=seqs), grid_spec=grid_spec,
        out_shape=(jax.ShapeDtypeStruct((nb, N_KV_HEADS, rows, HEAD_DIM), F32),
                   jax.ShapeDtypeStruct(cache_win_t.shape, F32)),
        compiler_params=_params("parallel"), name="nsa_sample",
    )(page_table, q, qr, *([pages] * len(page_specs)), cmp, cache_win_t, new_k, new_v, gates, ov)


def _mlp_kernel(x_ref, oat_ref, yr_ref, ga_ref, wo_ref, nmlp_ref, wup_ref, wdn_ref, nfin_ref, y_ref, *, ff_chunk):
    oat = oat_ref[...]
    at = (oat * lax.rsqrt(jnp.mean(oat * oat, axis=0, keepdims=True) + EPS) * ga_ref[...]).astype(BF16)
    attn = lax.dot_general(at, wo_ref[0:ATTN_WIDTH, :], TN_DIMS, preferred_element_type=F32)
    x1 = x_ref[...] + (attn + _nn(yr_ref[...], wo_ref[ATTN_WIDTH:, :]))
    v = _rms(x1, nmlp_ref[...]).astype(BF16)
    acc = jnp.zeros_like(x1)
    for c in range(D_FF // ff_chunk):
        sl = slice(c * ff_chunk, (c + 1) * ff_chunk)
        hid = jnp.square(jnp.maximum(_nn(v, wup_ref[:, sl]), 0.0)).astype(BF16)
        acc = acc + _nn(hid, wdn_ref[sl, :])
    y_ref[...] = _rms(x1 + acc, nfin_ref[...])


def _mlp(x, o_attn_t, y_rnn, g_attn, w_out, norm_mlp, w_up, w_down, norm_final, tm, ff_chunk):
    rows = x.shape[0]
    row = lambda w: pl.BlockSpec((tm, w), lambda i: (i, 0))
    full = lambda a: pl.BlockSpec(a.shape, lambda i: (0,) * a.ndim)
    once = lambda a: pl.BlockSpec(a.shape, lambda i: (0,) * a.ndim, pipeline_mode=pl.Buffered(1))
    return pl.pallas_call(
        functools.partial(_mlp_kernel, ff_chunk=ff_chunk), grid=(rows // tm,),
        in_specs=[row(D_MODEL), pl.BlockSpec((ATTN_WIDTH, tm), lambda i: (0, i)), row(D_RNN), full(g_attn),
                  once(w_out), full(norm_mlp),
                  once(w_up), once(w_down), full(norm_final)],
        out_specs=row(D_MODEL), out_shape=jax.ShapeDtypeStruct((rows, D_MODEL), F32),
        compiler_params=_params("parallel"), name="mlp",
    )(x, o_attn_t, y_rnn, g_attn, w_out, norm_mlp, w_up, w_down, norm_final)


def _block_diag(w):
    nb, bs, _ = w.shape
    eye = jnp.eye(nb, dtype=w.dtype)
    return (eye[:, None, :, None] * w[:, :, None, :]).reshape(nb * bs, nb * bs)


def kernel(x_prompt, x_sample, cache_kv, cache_win, state_conv, state_rnn, page_table, w_in, pe_ck, w_ck1, w_ck2,
           pe_cv, w_cv1, w_cv2, g_attn, g_rnn, conv_w, conv_b, w_ra, b_ra, w_ri, b_ri, lam, w_out, norm_mix,
           norm_mlp, w_up, w_down, norm_final):
    assert w_in.shape[0] == 1, "single layer"
    nbp, tp, _ = x_prompt.shape
    nbs, steps, _ = x_sample.shape
    n_pages = page_table.shape[1]

    wt = jnp.transpose(w_in[0]).astype(BF16)
    c_kv, c_g = ATTN_WIDTH, ATTN_WIDTH + 6 * KV_COLS
    c_rg = c_g + 3 * N_HEADS
    wq, wkv = wt[0:c_kv], wt[c_kv:c_g]
    wg = wt[c_g:c_rg].reshape(3, N_KV_HEADS, GQA, D_MODEL).transpose(1, 0, 2, 3).reshape(N_KV_HEADS, 3 * GQA, D_MODEL)
    wg = jnp.pad(wg, ((0, 0), (0, GATE_ROWS - 3 * GQA), (0, 0))).reshape(N_KV_HEADS * GATE_ROWS, D_MODEL)
    wr = wt[c_rg:c_rg + 2 * D_RNN]
    row2 = lambda a: a.reshape(1, -1)
    rnn_w = (conv_w[0], row2(conv_b[0]), _block_diag(w_ra[0].astype(BF16)), _block_diag(w_ri[0].astype(BF16)),
             row2(b_ra[0]), row2(b_ri[0]), row2(lam[0]), row2(g_rnn[0]))
    per_cs = lambda k, v: jnp.repeat(jnp.stack([k, v]), N_KV_HEADS, axis=0)
    pe_cs = per_cs(pe_ck[0], pe_cv[0]).reshape(2 * N_KV_HEADS, 2, HALF, HEAD_DIM)
    cmp_pet = jnp.tile(pe_cs.transpose(1, 0, 3, 2), (1, 1, 1, BLOCKS_PER_PAGE)).reshape(2, CMP_COLS, PAGE_SIZE)
    w1_cs = per_cs(w_ck1[0], w_cv1[0]).transpose(1, 0, 2, 3)
    cmp_w1 = jax.vmap(_block_diag)(w1_cs.astype(BF16)).reshape(2, HALF, CMP_COLS, CMP_COLS)
    cmp_w2 = _block_diag(per_cs(w_ck2[0], w_cv2[0]).astype(BF16))
    mlp_w = (g_attn[0].reshape(-1, 1), w_out[0].astype(BF16), row2(norm_mlp[0]), w_up[0].astype(BF16),
             w_down[0].astype(BF16), row2(norm_final))
    nm = row2(norm_mix[0])

    tm = 512
    xp = x_prompt.reshape(nbp * tp, D_MODEL)
    qt, qrt, kvt, wint, kb, vb, rg, rx, gt = _proj(xp, jnp.arange(tp), nm, wq, wkv, wr, wg, nbp, tp, tm)
    y_rnn, h_last = _rglru_prompt(rx, rg, rnn_w, nbp, tp, 256)
    own_pages = jnp.zeros((nbp, tp // PAGE_SIZE), jnp.int32)
    cmp = _compress(kvt, lambda k: pl.BlockSpec((1, CMP_COLS, PAGE_SIZE), lambda n, pt: (n, 0, k)), own_pages,
                    cmp_pet, cmp_w1, cmp_w2, nbp)
    o_attn_t = _nsa_prompt(qt, qrt, kb, vb, cmp, gt, nbp, tp, 256, 512)
    y_prompt = _mlp(xp, o_attn_t, y_rnn, *mlp_w, tm, 1024).reshape(nbp, tp, D_MODEL)
    wlen = min(WINDOW, tp)
    kv_prompt = kvt.reshape(nbp, 4, N_KV_HEADS, HEAD_DIM, tp).transpose(0, 4, 1, 2, 3)[None]
    win_prompt = wint[:, :, tp - wlen:].reshape(nbp, 2, N_KV_HEADS, HEAD_DIM, wlen).transpose(0, 4, 1, 2, 3)[None]
    conv_prompt = rx.reshape(nbp, tp, D_RNN)[:, tp - (CONV_WIDTH - 1):][None]
    h_prompt = h_last.reshape(1, nbp, D_RNN)

    rows_s = nbs * steps
    xs = x_sample.transpose(1, 0, 2).reshape(rows_s, D_MODEL)
    pos_s = PAST_LEN + jnp.arange(rows_s) // nbs
    qt, qrt, kvt, wint, _, _, rg, rx, gt = _proj(xs, pos_s, nm, wq, wkv, wr, wg, 1, rows_s, rows_s)
    y_rnn, conv_s, h_s = _rglru_sample(rx, rg, state_conv[0].transpose(1, 0, 2), state_rnn[0], rnn_w)
    pages_t = cache_kv[0].transpose(0, 2, 3, 4, 1)
    cmp_page = lambda k: pl.BlockSpec((1, 2, N_KV_HEADS, HEAD_DIM, PAGE_SIZE), lambda n, pt: (pt[n, k], 0, 0, 0, 0))
    cmp = _compress(pages_t, cmp_page, page_table, cmp_pet, cmp_w1, cmp_w2, nbs)
    by_seq = lambda a: a.reshape(N_KV_HEADS, GQA, HEAD_DIM, steps, nbs).transpose(4, 0, 1, 3, 2).reshape(
        nbs, N_KV_HEADS, GQA * steps, HEAD_DIM)
    def new_tile(slc_rows, win_rows):
        t = jnp.stack([slc_rows, win_rows]).reshape(2 * N_KV_HEADS, HEAD_DIM, steps, nbs)
        t = t.transpose(3, 1, 0, 2).reshape(nbs, HEAD_DIM, 2 * N_KV_HEADS * steps)
        return jnp.pad(t, ((0, 0), (0, 0), (0, LANES - 2 * N_KV_HEADS * steps)))
    new_k = new_tile(kvt[0, 2 * KV_COLS:3 * KV_COLS], wint[0, 0:KV_COLS])
    new_v = new_tile(kvt[0, 3 * KV_COLS:4 * KV_COLS], wint[0, KV_COLS:2 * KV_COLS])
    gates_s = gt[0].reshape(N_KV_HEADS, GATE_ROWS, steps, nbs)[:, :3 * GQA].reshape(N_KV_HEADS, 3, GQA, steps, nbs)
    gates_s = gates_s.transpose(4, 0, 2, 3, 1).reshape(nbs, N_KV_HEADS, GQA * steps, 3)
    cache_win_t = cache_win[0].transpose(0, 2, 3, 4, 1)
    o_s, win_t = _nsa_sample(by_seq(qt[0]), by_seq(qrt[0]), pages_t, page_table, cmp, cache_win_t, new_k, new_v,
                             gates_s, steps, 2)
    o_attn_t = o_s.reshape(nbs, N_KV_HEADS, GQA, steps, HEAD_DIM).transpose(1, 2, 4, 3, 0).reshape(ATTN_WIDTH, rows_s)
    y_sample = _mlp(xs, o_attn_t, y_rnn, *mlp_w, rows_s, 1024).reshape(steps, nbs, D_MODEL).transpose(1, 0, 2)
    kv_sample = kvt[0].reshape(4, N_KV_HEADS, HEAD_DIM, steps, nbs).transpose(4, 3, 0, 1, 2)[None]
    win_sample = win_t.transpose(0, 4, 1, 2, 3)[None]
    conv_sample = conv_s.transpose(1, 0, 2)[None]
    h_sample = h_s[None]

    return (y_prompt, y_sample, kv_prompt, kv_sample, win_prompt, win_sample, conv_prompt, conv_sample,
            h_prompt, h_sample)
```

```python
import functools

import jax
import jax.numpy as jnp
import numpy as np
from jax import lax
from jax.experimental import pallas as pl
from jax.experimental.pallas import tpu as pltpu

D_MODEL = 1024
PAST_LEN = 2048
PAGE_SIZE = 128
HEAD_DIM = 64
N_HEADS = 8
N_KV_HEADS = 2
GQA = N_HEADS // N_KV_HEADS
ATTN_WIDTH = N_HEADS * HEAD_DIM
D_RNN = D_MODEL - ATTN_WIDTH
RNN_BLOCKS = 8
CONV_WIDTH = 4
LRU_C = 8.0
D_FF = 4 * D_MODEL
ROT_DIM = HEAD_DIM // 4
ROT_HALF = ROT_DIM // 2
ROPE_THETA = 500000.0
CMP_BLOCK = 32
CMP_STRIDE = 16
SEL_BLOCK = 64
TOP_N = 16
WINDOW = 512
KV_COLS = N_KV_HEADS * HEAD_DIM
EPS = 1e-6
NEG = -1e30
SEL_BONUS = 1e4
SEL_MASK = 2.0 ** 100
SCALE = HEAD_DIM ** -0.5

SUBLANES = 8
LANES = 128
VMEM_LIMIT = 48 * 1024 * 1024

N_CMP_PAD = 128
GATE_ROWS = 16

F32 = jnp.float32
BF16 = jnp.bfloat16
NT_DIMS = (((1,), (1,)), ((), ()))
TN_DIMS = (((0,), (0,)), ((), ()))


def _nt(a, b):
    return lax.dot_general(a, b, NT_DIMS, preferred_element_type=F32)


def _nn(a, b):
    return jnp.dot(a, b, preferred_element_type=F32)


def _rms(x, g):
    return x * lax.rsqrt(jnp.mean(x * x, axis=-1, keepdims=True) + EPS) * g


def _params(*sem):
    return pltpu.CompilerParams(dimension_semantics=sem, vmem_limit_bytes=VMEM_LIMIT)


def _proj_kernel(x_ref, nm_ref, wq_ref, wkv_ref, wr_ref, wg_ref, cos_ref, sin_ref,
                 qt_ref, qrt_ref, kvt_ref, wint_ref, kb_ref, vb_ref, rg_ref, rx_ref, gt_ref):
    u = _rms(x_ref[...], nm_ref[...]).astype(BF16)
    cos, sin = cos_ref[...], sin_ref[...]

    def rotated(a, r0):
        x1, x2 = a[r0:r0 + ROT_HALF], a[r0 + ROT_HALF:r0 + ROT_DIM]
        return x1 * cos - x2 * sin, x2 * cos + x1 * sin

    qt = _nt(wq_ref[...], u)
    qt_ref[0] = (qt * SCALE).astype(BF16)
    parts = []
    for head in range(N_HEADS):
        r0 = head * HEAD_DIM
        parts += [*rotated(qt, r0), qt[r0 + ROT_DIM:r0 + HEAD_DIM]]
    qrt_ref[0] = (jnp.concatenate(parts, axis=0) * SCALE).astype(BF16)

    kvt = _nt(wkv_ref[...], u)
    kvt_ref[0] = kvt[0:4 * KV_COLS]
    wint_ref[0] = kvt[4 * KV_COLS:6 * KV_COLS]
    for out_ref, src0, dst0 in ((kvt_ref, 2 * KV_COLS, 2 * KV_COLS), (wint_ref, 4 * KV_COLS, 0)):
        for h in range(N_KV_HEADS):
            d = dst0 + h * HEAD_DIM
            out_ref[0, d:d + ROT_HALF], out_ref[0, d + ROT_HALF:d + ROT_DIM] = rotated(kvt, src0 + h * HEAD_DIM)
    tm = x_ref.shape[0]
    ones_row = (lax.broadcasted_iota(jnp.int32, (HEAD_DIM, tm), 0) == 0).astype(BF16)
    for branch, src_ref, k0, v0 in ((0, kvt_ref, 2 * KV_COLS, 3 * KV_COLS), (1, wint_ref, 0, KV_COLS)):
        for h in range(N_KV_HEADS):
            i, r = branch * N_KV_HEADS + h, h * HEAD_DIM
            kb_ref[0, i] = src_ref[0, k0 + r:k0 + r + HEAD_DIM].astype(BF16)
            vb_ref[0, i, 0:HEAD_DIM] = src_ref[0, v0 + r:v0 + r + HEAD_DIM].astype(BF16)
            vb_ref[0, i, HEAD_DIM:2 * HEAD_DIM] = ones_row

    r = _nt(u, wr_ref[...])
    rg_ref[...] = r[:, 0:D_RNN]
    rx_ref[...] = r[:, D_RNN:2 * D_RNN]
    gt_ref[0] = jax.nn.sigmoid(_nt(wg_ref[...], u))


def _rope_tables(pos):
    inv = ROPE_THETA ** (-jnp.arange(ROT_HALF, dtype=F32) / ROT_HALF)
    ang = pos.astype(F32)[:, None] * inv
    return jnp.cos(ang).T, jnp.sin(ang).T


def _proj(x, pos, nm, wq, wkv, wr, wg, nb, tt, tm):
    rows = nb * tt
    nt = tt // tm
    cos, sin = _rope_tables(pos)
    row = lambda w: pl.BlockSpec((tm, w), lambda i: (i, 0))
    full = lambda a: pl.BlockSpec(a.shape, lambda i: (0,) * a.ndim)
    tab = pl.BlockSpec((ROT_HALF, tm), lambda i: (0, i % nt))
    tr = lambda r: pl.BlockSpec((1, r, tm), lambda i: (i // nt, 0, i % nt))
    out_shape = (
        jax.ShapeDtypeStruct((nb, ATTN_WIDTH, tt), BF16),
        jax.ShapeDtypeStruct((nb, ATTN_WIDTH, tt), BF16),
        jax.ShapeDtypeStruct((nb, 4 * KV_COLS, tt), F32),
        jax.ShapeDtypeStruct((nb, 2 * KV_COLS, tt), F32),
        jax.ShapeDtypeStruct((nb, 2 * N_KV_HEADS, HEAD_DIM, tt), BF16),
        jax.ShapeDtypeStruct((nb, 2 * N_KV_HEADS, 2 * HEAD_DIM, tt), BF16),
        jax.ShapeDtypeStruct((rows, D_RNN), F32),
        jax.ShapeDtypeStruct((rows, D_RNN), F32),
        jax.ShapeDtypeStruct((nb, N_KV_HEADS * GATE_ROWS, tt), F32),
    )
    tr4 = lambda r: pl.BlockSpec((1, 2 * N_KV_HEADS, r, tm), lambda i: (i // nt, 0, 0, i % nt))
    out_specs = (tr(ATTN_WIDTH), tr(ATTN_WIDTH), tr(4 * KV_COLS), tr(2 * KV_COLS), tr4(HEAD_DIM), tr4(2 * HEAD_DIM),
                 row(D_RNN), row(D_RNN), tr(N_KV_HEADS * GATE_ROWS))
    return pl.pallas_call(
        _proj_kernel, grid=(rows // tm,),
        in_specs=[row(D_MODEL), full(nm), full(wq), full(wkv), full(wr), full(wg), tab, tab],
        out_specs=out_specs, out_shape=out_shape, compiler_params=_params("parallel"), name="proj",
    )(x, nm, wq, wkv, wr, wg, cos, sin)


def _expm1_from_exp(x, u):
    near = jnp.where(u == 1.0, x, (u - 1.0) * x / jnp.log(u))
    return jnp.where(x < -30.0, -1.0, near)


def _softplus(x):
    return jnp.maximum(x, 0.0) + jnp.log1p(jnp.exp(-jnp.abs(x)))


def _lru_coeffs(xc, wra_ref, wri_ref, bra_ref, bri_ref, lam_ref):
    xb = xc.astype(BF16)
    r = jax.nn.sigmoid(_nn(xb, wra_ref[...]) + bra_ref[...])
    i = jax.nn.sigmoid(_nn(xb, wri_ref[...]) + bri_ref[...])
    log_a = -LRU_C * r * _softplus(-lam_ref[...])
    a = jnp.exp(log_a)
    b = jnp.sqrt(-_expm1_from_exp(2.0 * log_a, a * a)) * (i * xc)
    return a, b


def _rnn_out(rg, h, g):
    return _rms(jax.nn.gelu(rg) * h, g).astype(BF16)


def _rglru_prompt_kernel(rx_ref, rg_ref, cw_ref, cb_ref, wra_ref, wri_ref, bra_ref, bri_ref, lam_ref, g_ref,
                         y_ref, hl_ref, prev_sc, h_sc):
    tt = rx_ref.shape[0]

    @pl.when(pl.program_id(1) == 0)
    def _():
        prev_sc[...] = jnp.zeros_like(prev_sc)
        h_sc[...] = jnp.zeros_like(h_sc)

    rx = rx_ref[...]
    ext = jnp.concatenate([prev_sc[...], rx], axis=0)
    shifted = lambda d: pltpu.roll(ext, d, 0)[SUBLANES:SUBLANES + tt]
    cw = cw_ref[...]
    xc = cb_ref[...] + cw[0:1] * shifted(3)
    xc = xc + cw[1:2] * shifted(2)
    xc = xc + cw[2:3] * shifted(1)
    xc = xc + cw[3:4] * rx
    prev_sc[...] = rx[tt - SUBLANES:tt]

    a, b = _lru_coeffs(xc, wra_ref, wri_ref, bra_ref, bri_ref, lam_ref)
    n_tiles, width = tt // SUBLANES, a.shape[1]
    a = a.reshape(n_tiles, SUBLANES, width)
    b = b.reshape(n_tiles, SUBLANES, width)
    sub = lax.broadcasted_iota(jnp.int32, (1, SUBLANES, 1), 1)
    s = 1
    while s < SUBLANES:
        a_sh = jnp.where(sub >= s, pltpu.roll(a, s, 1), 1.0)
        b_sh = jnp.where(sub >= s, pltpu.roll(b, s, 1), 0.0)
        b = a * b_sh + b
        a = a * a_sh
        s *= 2
    state, tiles = h_sc[...], []
    for t in range(n_tiles):
        h_t = a[t] * state + b[t]
        tiles.append(h_t)
        state = h_t[SUBLANES - 1:SUBLANES]
    h = jnp.concatenate(tiles, axis=0)
    h_sc[...] = state
    hl_ref[0] = state
    y_ref[...] = _rnn_out(rg_ref[...], h, g_ref[...])


def _rglru_sample_kernel(rx_ref, rg_ref, cp_ref, h0_ref, cw_ref, cb_ref, wra_ref, wri_ref, bra_ref, bri_ref,
                         lam_ref, g_ref, y_ref, cs_ref, hl_ref):
    nb = h0_ref.shape[0]
    steps = rx_ref.shape[0] // nb
    xp = [cp_ref[k] for k in range(CONV_WIDTH - 1)] + [rx_ref[t * nb:(t + 1) * nb] for t in range(steps)]
    cw, cb = cw_ref[...], cb_ref[...]
    xcs = []
    for t in range(steps):
        xc = cb + cw[0:1] * xp[t]
        for tap in range(1, CONV_WIDTH):
            xc = xc + cw[tap:tap + 1] * xp[t + tap]
        xcs.append(xc)
    a, b = _lru_coeffs(jnp.concatenate(xcs, axis=0), wra_ref, wri_ref, bra_ref, bri_ref, lam_ref)
    h = h0_ref[...]
    for t in range(steps):
        sl = slice(t * nb, (t + 1) * nb)
        h = a[sl] * h + b[sl]
        y_ref[sl] = _rnn_out(rg_ref[sl], h, g_ref[...])
    for k in range(CONV_WIDTH - 1):
        cs_ref[k] = xp[steps + k]
    hl_ref[...] = h


def _rglru_prompt(rx, rg, weights, nb, tt, chunk):
    nc = tt // chunk
    row = pl.BlockSpec((chunk, D_RNN), lambda n, c: (n * nc + c, 0))
    full = lambda a: pl.BlockSpec(a.shape, lambda n, c: (0,) * a.ndim)
    return pl.pallas_call(
        _rglru_prompt_kernel, grid=(nb, nc),
        in_specs=[row, row] + [full(w) for w in weights],
        out_specs=(row, pl.BlockSpec((1, 1, D_RNN), lambda n, c: (n, 0, 0))),
        out_shape=(jax.ShapeDtypeStruct((nb * tt, D_RNN), BF16), jax.ShapeDtypeStruct((nb, 1, D_RNN), F32)),
        scratch_shapes=[pltpu.VMEM((SUBLANES, D_RNN), F32), pltpu.VMEM((1, D_RNN), F32)],
        compiler_params=_params("parallel", "arbitrary"), name="rglru_prompt",
    )(rx, rg, *weights)


def _rglru_sample(rx, rg, conv_prev, h0, weights):
    nb = h0.shape[0]
    return pl.pallas_call(
        _rglru_sample_kernel,
        out_shape=(jax.ShapeDtypeStruct(rx.shape, BF16), jax.ShapeDtypeStruct(conv_prev.shape, F32),
                   jax.ShapeDtypeStruct((nb, D_RNN), F32)),
        compiler_params=pltpu.CompilerParams(vmem_limit_bytes=VMEM_LIMIT), name="rglru_sample",
    )(rx, rg, conv_prev, h0, *weights)


CMP_COLS = 2 * KV_COLS
HALF = CMP_BLOCK // 2
BLOCKS_PER_PAGE = PAGE_SIZE // CMP_STRIDE


def _compress_kernel(pt_ref, *refs, n_pages):
    del pt_ref
    page_refs = refs[:n_pages]
    pet_ref, perm_ref, w1_ref, w2_ref, out_ref, z_sc = refs[n_pages:]
    seqs = out_ref.shape[0]
    for k in range(n_pages):
        xt = page_refs[k][0].reshape(CMP_COLS, PAGE_SIZE)
        for half in range(2):
            a = (xt + pet_ref[half]).astype(BF16)
            z_sc[half, k] = _nt(perm_ref[...], a)
    pre = []
    for half in range(2):
        acc = jnp.zeros((seqs * N_CMP_PAD, CMP_COLS), F32)
        for j in range(CMP_STRIDE):
            rows = z_sc[half, :, j * BLOCKS_PER_PAGE:(j + 1) * BLOCKS_PER_PAGE, :]
            acc = acc + _nn(rows.reshape(seqs * N_CMP_PAD, CMP_COLS).astype(BF16), w1_ref[half, j])
        pre.append(acc)
    hid = jax.nn.gelu(pre[0] + pltpu.roll(pre[1], seqs * N_CMP_PAD - 1, 0))
    out_ref[...] = _nn(hid.astype(BF16), w2_ref[...]).astype(BF16).reshape(seqs, N_CMP_PAD, CMP_COLS)


def _compress(pages, page_spec, page_table, pet, w1, w2, nb, seqs):
    n_pages = page_table.shape[1]
    assert n_pages * BLOCKS_PER_PAGE == N_CMP_PAD
    pos = np.arange(PAGE_SIZE)
    perm = jnp.asarray((pos % CMP_STRIDE * BLOCKS_PER_PAGE + pos // CMP_STRIDE)[None, :]
                       == np.arange(PAGE_SIZE)[:, None], BF16)
    full = lambda a: pl.BlockSpec(a.shape, lambda n, pt: (0,) * a.ndim)
    once = lambda a: pl.BlockSpec(a.shape, lambda n, pt: (0,) * a.ndim, pipeline_mode=pl.Buffered(1))
    page_specs = [page_spec(i, k) for i in range(seqs) for k in range(n_pages)]
    grid_spec = pltpu.PrefetchScalarGridSpec(
        num_scalar_prefetch=1, grid=(nb // seqs,),
        in_specs=page_specs + [full(pet), full(perm), once(w1), full(w2)],
        out_specs=pl.BlockSpec((seqs, N_CMP_PAD, CMP_COLS), lambda n, pt: (n, 0, 0)),
        scratch_shapes=[pltpu.VMEM((2, len(page_specs), PAGE_SIZE, CMP_COLS), F32)])
    return pl.pallas_call(
        functools.partial(_compress_kernel, n_pages=len(page_specs)), grid_spec=grid_spec,
        out_shape=jax.ShapeDtypeStruct((nb, N_CMP_PAD, CMP_COLS), BF16),
        compiler_params=_params("parallel"), name="compress",
    )(page_table, *([pages] * len(page_specs)), pet, perm, w1, w2)


def _cmp_cols(cmp_ref, i, slot, h):
    c0 = slot * KV_COLS + h * HEAD_DIM
    return cmp_ref[i, :, c0:c0 + HEAD_DIM]


def _overlap_matrix(n_sel):
    n_cmp = N_CMP_PAD - 1
    c0 = np.arange(N_CMP_PAD)[:, None] * CMP_STRIDE
    j0 = np.arange(LANES)[None, :] * SEL_BLOCK
    ov = (c0 < j0 + SEL_BLOCK) & (c0 + CMP_BLOCK > j0)
    ov &= (np.arange(N_CMP_PAD)[:, None] < n_cmp) & (np.arange(LANES)[None, :] < n_sel)
    return ov.astype(np.float32)


def _topk_member(score, idx, n, axis):
    rank = jnp.zeros(score.shape, jnp.int32)
    for i in range(n):
        si = score[i:i + 1, :] if axis == 0 else score[:, i:i + 1]
        beats = (si > score) | ((si == score) & (i < idx))
        rank = rank + beats.astype(jnp.int32)
    return rank < TOP_N


def _sel_score(imp, j, cur):
    valid = j <= cur
    forced = (j == 0) | (j == cur) | (j == cur - 1)
    return jnp.where(valid, imp, -SEL_BONUS) + jnp.where(forced, SEL_BONUS, 0.0)


def _nsa_prompt_kernel(qt_ref, qrt_ref, kb_ref, vb_ref, cmp_ref, gt_ref, ovt_ref, e_ref,
                       o_ref, m_sc, acc_sc, *, tq, slc_chunk, col_block, n_sel):
    heads = range(N_KV_HEADS)
    q0 = pl.program_id(1) * tq
    wide = GQA * tq
    qpos = q0 + lax.broadcasted_iota(jnp.int32, (1, tq), 1)
    tile = lambda a, n: jnp.concatenate([a] * n, axis=1)
    group = lambda ref, h: jnp.concatenate(
        [ref[0, (h * GQA + g) * HEAD_DIM:(h * GQA + g + 1) * HEAD_DIM, :] for g in range(GQA)], axis=1)

    cidx = lax.broadcasted_iota(jnp.int32, (N_CMP_PAD, 1), 0)
    real = cidx < N_CMP_PAD - 1
    qpos_all = q0 + lax.broadcasted_iota(jnp.int32, (1, N_HEADS * tq), 1) % tq
    cmask = (cidx * CMP_STRIDE + CMP_BLOCK - 1 <= qpos_all) & real
    s = jnp.concatenate([_nn(_cmp_cols(cmp_ref, 0, 0, h), group(qt_ref, h)) for h in heads], axis=1)
    s = jnp.where(real, jnp.where(cmask, s, NEG), -jnp.inf)
    e = jnp.exp(s - jnp.max(s, axis=0, keepdims=True))
    p = (e / jnp.sum(e, axis=0, keepdims=True)) * cmask.astype(F32)
    pb = p.astype(BF16)
    o_cmp = [lax.dot_general(_cmp_cols(cmp_ref, 0, 1, h), pb[:, h * wide:(h + 1) * wide], TN_DIMS,
                             preferred_element_type=F32) for h in heads]

    imp = []
    for h in heads:
        imp_h = _nn(ovt_ref[...], pb[:, h * wide:h * wide + tq])
        for g in range(1, GQA):
            imp_h = imp_h + _nn(ovt_ref[...], pb[:, h * wide + g * tq:h * wide + (g + 1) * tq])
        imp.append(imp_h)
    j = lax.broadcasted_iota(jnp.int32, (n_sel, 1), 0)
    qpos_2 = q0 + lax.broadcasted_iota(jnp.int32, (1, N_KV_HEADS * tq), 1) % tq
    picked = _topk_member(_sel_score(jnp.concatenate(imp, axis=1), j, qpos_2 // SEL_BLOCK), j, n_sel, axis=0)
    unpicked = jnp.where(picked, 0.0, -SEL_MASK).astype(BF16)
    pad_rows = jnp.zeros((HEAD_DIM - n_sel, wide), BF16)
    qr = [group(qrt_ref, h) for h in heads]
    qsel = [jnp.concatenate([qr[h], tile(unpicked[:, h * tq:(h + 1) * tq], GQA), pad_rows], axis=0)
            for h in heads]

    def attend(h, qx, k, v, allowed, first):
        sc = lax.dot_general(k, qx, TN_DIMS, preferred_element_type=F32)
        bias = jnp.where(allowed, 0.0, NEG)
        pes, ms, alphas = [], [], []
        for c0 in range(0, wide, col_block):
            t0 = c0 % tq
            x = sc[:, c0:c0 + col_block] + bias[:, t0:t0 + col_block]
            m = jnp.max(x, axis=0, keepdims=True)
            if not first:
                m_prev = m_sc[h, :, c0:c0 + col_block]
                m = jnp.maximum(m_prev, m)
                alphas.append(jnp.exp(m_prev - m))
            ms.append(m)
            pes.append(jnp.exp(x - m).astype(BF16))
        pv = _nn(v, jnp.concatenate(pes, axis=1))
        m_sc[h] = jnp.concatenate(ms, axis=1)
        acc_sc[h] = pv if first else jnp.concatenate(alphas, axis=1) * acc_sc[h] + pv

    def normalised(h):
        acc = acc_sc[h]
        return acc[0:HEAD_DIM] / acc[HEAD_DIM:HEAD_DIM + 1]

    wn = WINDOW + tq
    w0 = pl.multiple_of(jnp.maximum(q0 - WINDOW, 0), LANES)
    kpos = w0 + lax.broadcasted_iota(jnp.int32, (wn, 1), 0)
    in_window = (kpos <= qpos) & (kpos > qpos - WINDOW)
    for h in heads:
        attend(h, qr[h], kb_ref[0, N_KV_HEADS + h, :, pl.ds(w0, wn)], vb_ref[0, N_KV_HEADS + h, :, pl.ds(w0, wn)],
               in_window, True)
    o_win = [normalised(h) for h in heads]

    for c0 in range(0, kb_ref.shape[3], slc_chunk):
        def chunk(c0=c0):
            kpos = c0 + lax.broadcasted_iota(jnp.int32, (slc_chunk, 1), 0)
            member = jnp.concatenate([e_ref[:, c0:c0 + slc_chunk],
                                      jnp.zeros((HEAD_DIM - n_sel, slc_chunk), BF16)], axis=0)
            for h in heads:
                k = jnp.concatenate([kb_ref[0, h, :, c0:c0 + slc_chunk], member], axis=0)
                attend(h, qsel[h], k, vb_ref[0, h, :, c0:c0 + slc_chunk], kpos <= qpos, c0 == 0)

        if c0 == 0:
            chunk()
        else:
            pl.when(q0 + tq > c0)(chunk)
    o_slc = [normalised(h) for h in heads]

    gt = gt_ref[0]
    for h in heads:
        for g in range(GQA):
            gate = lambda b: gt[h * GATE_ROWS + b * GQA + g:h * GATE_ROWS + b * GQA + g + 1, :]
            cols = slice(g * tq, (g + 1) * tq)
            o = gate(0) * o_cmp[h][:, cols] + gate(1) * o_slc[h][:, cols]
            o = o + gate(2) * o_win[h][:, cols]
            o_ref[(h * GQA + g) * HEAD_DIM:(h * GQA + g + 1) * HEAD_DIM, :] = o


def _nsa_prompt(qt, qrt, kb, vb, cmp, gt, nb, tt, tq, slc_chunk):
    n_sel = -(-tt // SEL_BLOCK)
    nq = tt // tq
    ovt = jnp.asarray(_overlap_matrix(n_sel)[:, :n_sel].T, BF16)
    expand = jnp.asarray(np.arange(tt)[None, :] // SEL_BLOCK == np.arange(n_sel)[:, None], BF16)
    cols = lambda a: pl.BlockSpec((1, a.shape[1], tq), lambda n, i: (n, 0, i))
    seq = lambda a: pl.BlockSpec((1,) + a.shape[1:], lambda n, i: (n,) + (0,) * (a.ndim - 1))
    full = lambda a: pl.BlockSpec(a.shape, lambda n, i: (0,) * a.ndim)
    return pl.pallas_call(
        functools.partial(_nsa_prompt_kernel, tq=tq, slc_chunk=slc_chunk, col_block=LANES, n_sel=n_sel),
        grid=(nb, nq),
        in_specs=[cols(qt), cols(qrt), seq(kb), seq(vb), seq(cmp), cols(gt), full(ovt), full(expand)],
        out_specs=pl.BlockSpec((ATTN_WIDTH, tq), lambda n, i: (0, n * nq + i)),
        out_shape=jax.ShapeDtypeStruct((ATTN_WIDTH, nb * tt), F32),
        scratch_shapes=[pltpu.VMEM((N_KV_HEADS, 1, GQA * tq), F32),
                        pltpu.VMEM((N_KV_HEADS, 2 * HEAD_DIM, GQA * tq), F32)],
        compiler_params=_params("parallel", "arbitrary"), name="nsa_prompt",
    )(qt, qrt, kb, vb, cmp, gt, ovt, expand)


def _nsa_sample_kernel(pt_ref, *refs, n_pages, steps, seqs):
    del pt_ref
    q_ref, qr_ref = refs[0:2]
    page_refs = refs[2:2 + seqs * n_pages]
    cmp_ref, cw_ref, newk_ref, newv_ref, gates_ref, ov_ref, o_ref, wout_ref = refs[2 + seqs * n_pages:]
    rows = GQA * steps
    past = n_pages * PAGE_SIZE
    wc = cw_ref.shape[-1]
    trow = lax.broadcasted_iota(jnp.int32, (rows, 1), 0) % steps
    lane = lax.broadcasted_iota(jnp.int32, (1, LANES), 1)
    n_sel = -(-(past + steps) // SEL_BLOCK)
    cur = (past + trow) // SEL_BLOCK
    real = lane < N_CMP_PAD - 1
    wlane = lax.broadcasted_iota(jnp.int32, (1, wc), 1)
    in_window = jnp.broadcast_to(wlane + (WINDOW - wc) > trow, (rows, wc))
    per_tile = PAGE_SIZE // SEL_BLOCK

    chains = [(i, h) for i in range(seqs) for h in range(N_KV_HEADS)]
    assert len(chains) * steps == rows, "the stacked top-k below holds one chain per query-group slot"
    s = jnp.concatenate([_nt(q_ref[i, h], _cmp_cols(cmp_ref, i, 0, h)) for i, h in chains], axis=0)
    s = jnp.where(real, s, -jnp.inf)
    e = jnp.exp(s - jnp.max(s, axis=-1, keepdims=True))
    pb_all = (e / jnp.sum(e, axis=-1, keepdims=True)).astype(BF16)
    group = lax.broadcasted_iota(jnp.int32, (rows, 1), 0) // steps
    o_cmps, imp_all = [], None
    for c, (i, h) in enumerate(chains):
        pb = pb_all[c * rows:(c + 1) * rows]
        o_cmps.append(_nn(pb, _cmp_cols(cmp_ref, i, 1, h)))
        part = _nn(pb, ov_ref[...])
        imp = part
        for g in range(1, GQA):
            imp = imp + pltpu.roll(part, g * steps, 0)
        imp_all = imp if c == 0 else jnp.where(group == c, imp, imp_all)
    score = jnp.where(lane < n_sel, _sel_score(imp_all, lane, cur), -jnp.inf)
    sel_all = (_topk_member(score, lane, n_sel, axis=1) & (lane < n_sel)).astype(F32)
    rolled = [sel_all] + [pltpu.roll(sel_all, k * steps, 0) for k in range(1, GQA)]

    for c, (i, h) in enumerate(chains):
        newk, newv = newk_ref[i], newv_ref[i]
        newk_b, newv_b = newk.astype(BF16), newv.astype(BF16)
        qr = qr_ref[i, h]
        pages = page_refs[i * n_pages:(i + 1) * n_pages]

        def new_ok(branch, h=h):
            base = (branch * N_KV_HEADS + h) * steps
            return (lane >= base) & (lane < base + steps) & (lane - base <= trow)

        sel_f = rolled[(0 - c) % GQA]
        for g in range(1, GQA):
            sel_f = jnp.where(group == g, rolled[(g - c) % GQA], sel_f)

        ks = jnp.concatenate([p[0, 0, h].astype(BF16) for p in pages] + [newk_b], axis=1)
        vs = jnp.concatenate([p[0, 1, h].astype(BF16) for p in pages] + [newv_b], axis=1)
        tiles = []
        for k in range(n_pages):
            m = sel_f[:, per_tile * k:per_tile * k + 1]
            for b in range(1, per_tile):
                m = jnp.where(lane < b * SEL_BLOCK, m, sel_f[:, per_tile * k + b:per_tile * k + b + 1])
            tiles.append(m > 0.5)
        tiles.append((sel_f[:, n_sel - 1:n_sel] > 0.5) & new_ok(0))
        msk = jnp.concatenate(tiles, axis=1)
        sc = jnp.where(msk, _nn(qr, ks), NEG)
        pe = jnp.where(msk, jnp.exp(sc - jnp.max(sc, axis=-1, keepdims=True)), 0.0)
        o_slc = _nt(pe.astype(BF16), vs) / jnp.sum(pe, axis=-1, keepdims=True)

        kw = jnp.concatenate([cw_ref[i, 0, h].astype(BF16), newk_b], axis=1)
        vw = jnp.concatenate([cw_ref[i, 1, h].astype(BF16), newv_b], axis=1)
        wmsk = jnp.concatenate([in_window, jnp.broadcast_to(new_ok(1), (rows, LANES))], axis=1)
        sw = jnp.where(wmsk, _nn(qr, kw), NEG)
        pw = jnp.where(wmsk, jnp.exp(sw - jnp.max(sw, axis=-1, keepdims=True)), 0.0)
        o_win = _nt(pw.astype(BF16), vw) / jnp.sum(pw, axis=-1, keepdims=True)

        gt = gates_ref[i, h]
        o_ref[i, h] = gt[:, 0:1] * o_cmps[c] + gt[:, 1:2] * o_slc + gt[:, 2:3] * o_win

        to_end = (LANES - steps - (N_KV_HEADS + h) * steps) % LANES
        for slot, new in ((0, newk), (1, newv)):
            shifted = pltpu.roll(cw_ref[i, slot, h], wc - steps, 1)
            wout_ref[i, slot, h, :, 0:wc - LANES] = shifted[:, 0:wc - LANES]
            wout_ref[i, slot, h, :, wc - LANES:wc] = jnp.where(lane >= LANES - steps, pltpu.roll(new, to_end, 1),
                                                               shifted[:, wc - LANES:wc])


def _nsa_sample(q, qr, pages, page_table, cmp, cache_win_t, new_k, new_v, gates, steps, seqs):
    nb, n_pages = page_table.shape
    n_sel = -(-(n_pages * PAGE_SIZE + steps) // SEL_BLOCK)
    ov = jnp.asarray(_overlap_matrix(n_sel), BF16)
    rows = GQA * steps
    lead = lambda a: pl.BlockSpec((seqs,) + a.shape[1:], lambda n, pt: (n,) + (0,) * (a.ndim - 1))
    page_spec = lambda i, k: pl.BlockSpec((1, 2, N_KV_HEADS, HEAD_DIM, PAGE_SIZE),
                                          lambda n, pt: (pt[n * seqs + i, k], 1, 0, 0, 0))
    page_specs = [page_spec(i, k) for i in range(seqs) for k in range(n_pages)]
    grid_spec = pltpu.PrefetchScalarGridSpec(
        num_scalar_prefetch=1, grid=(nb // seqs,),
        in_specs=[lead(q), lead(qr)] + page_specs
        + [lead(cmp), lead(cache_win_t), lead(new_k), lead(new_v), lead(gates),
           pl.BlockSpec(ov.shape, lambda n, pt: (0, 0))],
        out_specs=(pl.BlockSpec((seqs, N_KV_HEADS, rows, HEAD_DIM), lambda n, pt: (n, 0, 0, 0)), lead(cache_win_t)))
    return pl.pallas_call(
        functools.partial(_nsa_sample_kernel, n_pages=n_pages, steps=steps, seqs=seqs), grid_spec=grid_spec,
        out_shape=(jax.ShapeDtypeStruct((nb, N_KV_HEADS, rows, HEAD_DIM), F32),
                   jax.ShapeDtypeStruct(cache_win_t.shape, F32)),
        compiler_params=_params("parallel"), name="nsa_sample",
    )(page_table, q, qr, *([pages] * len(page_specs)), cmp, cache_win_t, new_k, new_v, gates, ov)


def _mlp_kernel(x_ref, oat_ref, yr_ref, ga_ref, wo_ref, nmlp_ref, wup_ref, wdn_ref, nfin_ref, y_ref, *, ff_chunk):
    oat = oat_ref[...]
    at = (oat * lax.rsqrt(jnp.mean(oat * oat, axis=0, keepdims=True) + EPS) * ga_ref[...]).astype(BF16)
    attn = lax.dot_general(at, wo_ref[0:ATTN_WIDTH, :], TN_DIMS, preferred_element_type=F32)
    x1 = x_ref[...] + (attn + _nn(yr_ref[...], wo_ref[ATTN_WIDTH:, :]))
    v = _rms(x1, nmlp_ref[...]).astype(BF16)
    acc = jnp.zeros_like(x1)
    for c in range(D_FF // ff_chunk):
        sl = slice(c * ff_chunk, (c + 1) * ff_chunk)
        hid = jnp.square(jnp.maximum(_nn(v, wup_ref[:, sl]), 0.0)).astype(BF16)
        acc = acc + _nn(hid, wdn_ref[sl, :])
    y_ref[...] = _rms(x1 + acc, nfin_ref[...])


def _mlp(x, o_attn_t, y_rnn, g_attn, w_out, norm_mlp, w_up, w_down, norm_final, tm, ff_chunk):
    rows = x.shape[0]
    row = lambda w: pl.BlockSpec((tm, w), lambda i: (i, 0))
    full = lambda a: pl.BlockSpec(a.shape, lambda i: (0,) * a.ndim)
    once = lambda a: pl.BlockSpec(a.shape, lambda i: (0,) * a.ndim, pipeline_mode=pl.Buffered(1))
    return pl.pallas_call(
        functools.partial(_mlp_kernel, ff_chunk=ff_chunk), grid=(rows // tm,),
        in_specs=[row(D_MODEL), pl.BlockSpec((ATTN_WIDTH, tm), lambda i: (0, i)), row(D_RNN), full(g_attn),
                  once(w_out), full(norm_mlp),
                  once(w_up), once(w_down), full(norm_final)],
        out_specs=row(D_MODEL), out_shape=jax.ShapeDtypeStruct((rows, D_MODEL), F32),
        compiler_params=_params("parallel"), name="mlp",
    )(x, o_attn_t, y_rnn, g_attn, w_out, norm_mlp, w_up, w_down, norm_final)


def _block_diag(w):
    nb, bs, _ = w.shape
    eye = jnp.eye(nb, dtype=w.dtype)
    return (eye[:, None, :, None] * w[:, :, None, :]).reshape(nb * bs, nb * bs)


def kernel(x_prompt, x_sample, cache_kv, cache_win, state_conv, state_rnn, page_table, w_in, pe_ck, w_ck1, w_ck2,
           pe_cv, w_cv1, w_cv2, g_attn, g_rnn, conv_w, conv_b, w_ra, b_ra, w_ri, b_ri, lam, w_out, norm_mix,
           norm_mlp, w_up, w_down, norm_final):
    assert w_in.shape[0] == 1, "single layer"
    nbp, tp, _ = x_prompt.shape
    nbs, steps, _ = x_sample.shape
    n_pages = page_table.shape[1]

    wt = jnp.transpose(w_in[0]).astype(BF16)
    c_kv, c_g = ATTN_WIDTH, ATTN_WIDTH + 6 * KV_COLS
    c_rg = c_g + 3 * N_HEADS
    wq, wkv = wt[0:c_kv], wt[c_kv:c_g]
    wg = wt[c_g:c_rg].reshape(3, N_KV_HEADS, GQA, D_MODEL).transpose(1, 0, 2, 3).reshape(N_KV_HEADS, 3 * GQA, D_MODEL)
    wg = jnp.pad(wg, ((0, 0), (0, GATE_ROWS - 3 * GQA), (0, 0))).reshape(N_KV_HEADS * GATE_ROWS, D_MODEL)
    wr = wt[c_rg:c_rg + 2 * D_RNN]
    row2 = lambda a: a.reshape(1, -1)
    rnn_w = (conv_w[0], row2(conv_b[0]), _block_diag(w_ra[0].astype(BF16)), _block_diag(w_ri[0].astype(BF16)),
             row2(b_ra[0]), row2(b_ri[0]), row2(lam[0]), row2(g_rnn[0]))
    per_cs = lambda k, v: jnp.repeat(jnp.stack([k, v]), N_KV_HEADS, axis=0)
    pe_cs = per_cs(pe_ck[0], pe_cv[0]).reshape(2 * N_KV_HEADS, 2, HALF, HEAD_DIM)
    cmp_pet = jnp.tile(pe_cs.transpose(1, 0, 3, 2), (1, 1, 1, BLOCKS_PER_PAGE)).reshape(2, CMP_COLS, PAGE_SIZE)
    w1_cs = per_cs(w_ck1[0], w_cv1[0]).transpose(1, 0, 2, 3)
    cmp_w1 = jax.vmap(_block_diag)(w1_cs.astype(BF16)).reshape(2, HALF, CMP_COLS, CMP_COLS)
    cmp_w2 = _block_diag(per_cs(w_ck2[0], w_cv2[0]).astype(BF16))
    mlp_w = (g_attn[0].reshape(-1, 1), w_out[0].astype(BF16), row2(norm_mlp[0]), w_up[0].astype(BF16),
             w_down[0].astype(BF16), row2(norm_final))
    nm = row2(norm_mix[0])

    tm = 512
    xp = x_prompt.reshape(nbp * tp, D_MODEL)
    qt, qrt, kvt, wint, kb, vb, rg, rx, gt = _proj(xp, jnp.arange(tp), nm, wq, wkv, wr, wg, nbp, tp, tm)
    y_rnn, h_last = _rglru_prompt(rx, rg, rnn_w, nbp, tp, 256)
    own_pages = jnp.zeros((nbp, tp // PAGE_SIZE), jnp.int32)
    cmp_seqs = 2
    own_page = lambda i, k: pl.BlockSpec((1, CMP_COLS, PAGE_SIZE), lambda n, pt: (n * cmp_seqs + i, 0, k))
    cmp = _compress(kvt, own_page, own_pages, cmp_pet, cmp_w1, cmp_w2, nbp, cmp_seqs)
    o_attn_t = _nsa_prompt(qt, qrt, kb, vb, cmp, gt, nbp, tp, 256, 512)
    y_prompt = _mlp(xp, o_attn_t, y_rnn, *mlp_w, tm, 1024).reshape(nbp, tp, D_MODEL)
    wlen = min(WINDOW, tp)
    kv_prompt = kvt.reshape(nbp, 4, N_KV_HEADS, HEAD_DIM, tp).transpose(0, 4, 1, 2, 3)[None]
    win_prompt = wint[:, :, tp - wlen:].reshape(nbp, 2, N_KV_HEADS, HEAD_DIM, wlen).transpose(0, 4, 1, 2, 3)[None]
    conv_prompt = rx.reshape(nbp, tp, D_RNN)[:, tp - (CONV_WIDTH - 1):][None]
    h_prompt = h_last.reshape(1, nbp, D_RNN)

    rows_s = nbs * steps
    xs = x_sample.transpose(1, 0, 2).reshape(rows_s, D_MODEL)
    pos_s = PAST_LEN + jnp.arange(rows_s) // nbs
    qt, qrt, kvt, wint, _, _, rg, rx, gt = _proj(xs, pos_s, nm, wq, wkv, wr, wg, 1, rows_s, rows_s)
    y_rnn, conv_s, h_s = _rglru_sample(rx, rg, state_conv[0].transpose(1, 0, 2), state_rnn[0], rnn_w)
    pages_t = cache_kv[0].transpose(0, 2, 3, 4, 1)
    cmp_page = lambda i, k: pl.BlockSpec((1, 2, N_KV_HEADS, HEAD_DIM, PAGE_SIZE),
                                         lambda n, pt: (pt[n * cmp_seqs + i, k], 0, 0, 0, 0))
    cmp = _compress(pages_t, cmp_page, page_table, cmp_pet, cmp_w1, cmp_w2, nbs, cmp_seqs)
    by_seq = lambda a: a.reshape(N_KV_HEADS, GQA, HEAD_DIM, steps, nbs).transpose(4, 0, 1, 3, 2).reshape(
        nbs, N_KV_HEADS, GQA * steps, HEAD_DIM)
    def new_tile(slc_rows, win_rows):
        t = jnp.stack([slc_rows, win_rows]).reshape(2 * N_KV_HEADS, HEAD_DIM, steps, nbs)
        t = t.transpose(3, 1, 0, 2).reshape(nbs, HEAD_DIM, 2 * N_KV_HEADS * steps)
        return jnp.pad(t, ((0, 0), (0, 0), (0, LANES - 2 * N_KV_HEADS * steps)))
    new_k = new_tile(kvt[0, 2 * KV_COLS:3 * KV_COLS], wint[0, 0:KV_COLS])
    new_v = new_tile(kvt[0, 3 * KV_COLS:4 * KV_COLS], wint[0, KV_COLS:2 * KV_COLS])
    gates_s = gt[0].reshape(N_KV_HEADS, GATE_ROWS, steps, nbs)[:, :3 * GQA].reshape(N_KV_HEADS, 3, GQA, steps, nbs)
    gates_s = gates_s.transpose(4, 0, 2, 3, 1).reshape(nbs, N_KV_HEADS, GQA * steps, 3)
    cache_win_t = cache_win[0].transpose(0, 2, 3, 4, 1)
    o_s, win_t = _nsa_sample(by_seq(qt[0]), by_seq(qrt[0]), pages_t, page_table, cmp, cache_win_t, new_k, new_v,
                             gates_s, steps, 2)
    o_attn_t = o_s.reshape(nbs, N_KV_HEADS, GQA, steps, HEAD_DIM).transpose(1, 2, 4, 3, 0).reshape(ATTN_WIDTH, rows_s)
    y_sample = _mlp(xs, o_attn_t, y_rnn, *mlp_w, rows_s, 1024).reshape(steps, nbs, D_MODEL).transpose(1, 0, 2)
    kv_sample = kvt[0].reshape(4, N_KV_HEADS, HEAD_DIM, steps, nbs).transpose(4, 3, 0, 1, 2)[None]
    win_sample = win_t.transpose(0, 4, 1, 2, 3)[None]
    conv_sample = conv_s.transpose(1, 0, 2)[None]
    h_sample = h_s[None]

    return (y_prompt, y_sample, kv_prompt, kv_sample, win_prompt, win_sample, conv_prompt, conv_sample,
            h_prompt, h_sample)
```

```python
import functools

import jax
import jax.numpy as jnp
import numpy as np
from jax import lax
from jax.experimental import pallas as pl
from jax.experimental.pallas import tpu as pltpu

D_MODEL = 1024
PAST_LEN = 2048
PAGE_SIZE = 128
HEAD_DIM = 64
N_HEADS = 8
N_KV_HEADS = 2
GQA = N_HEADS // N_KV_HEADS
ATTN_WIDTH = N_HEADS * HEAD_DIM
D_RNN = D_MODEL - ATTN_WIDTH
RNN_BLOCKS = 8
CONV_WIDTH = 4
LRU_C = 8.0
D_FF = 4 * D_MODEL
ROT_DIM = HEAD_DIM // 4
ROT_HALF = ROT_DIM // 2
ROPE_THETA = 500000.0
CMP_BLOCK = 32
CMP_STRIDE = 16
SEL_BLOCK = 64
TOP_N = 16
WINDOW = 512
KV_COLS = N_KV_HEADS * HEAD_DIM
EPS = 1e-6
NEG = -1e30
SEL_BONUS = 1e4
SEL_MASK = 2.0 ** 100
SCALE = HEAD_DIM ** -0.5

SUBLANES = 8
LANES = 128
VMEM_LIMIT = 48 * 1024 * 1024

N_CMP_PAD = 128
GATE_ROWS = 16

F32 = jnp.float32
BF16 = jnp.bfloat16
NT_DIMS = (((1,), (1,)), ((), ()))
TN_DIMS = (((0,), (0,)), ((), ()))


def _nt(a, b):
    return lax.dot_general(a, b, NT_DIMS, preferred_element_type=F32)


def _nn(a, b):
    return jnp.dot(a, b, preferred_element_type=F32)


def _rms(x, g):
    return x * lax.rsqrt(jnp.mean(x * x, axis=-1, keepdims=True) + EPS) * g


def _params(*sem):
    return pltpu.CompilerParams(dimension_semantics=sem, vmem_limit_bytes=VMEM_LIMIT)


def _proj_kernel(x_ref, nm_ref, wq_ref, wkv_ref, wr_ref, wg_ref, cos_ref, sin_ref,
                 qt_ref, qrt_ref, kvt_ref, wint_ref, kb_ref, vb_ref, rg_ref, rx_ref, gt_ref):
    u = _rms(x_ref[...], nm_ref[...]).astype(BF16)
    cos, sin = cos_ref[...], sin_ref[...]

    def rotated(a, r0):
        x1, x2 = a[r0:r0 + ROT_HALF], a[r0 + ROT_HALF:r0 + ROT_DIM]
        return x1 * cos - x2 * sin, x2 * cos + x1 * sin

    qt = _nt(wq_ref[...], u)
    qt_ref[0] = (qt * SCALE).astype(BF16)
    parts = []
    for head in range(N_HEADS):
        r0 = head * HEAD_DIM
        parts += [*rotated(qt, r0), qt[r0 + ROT_DIM:r0 + HEAD_DIM]]
    qrt_ref[0] = (jnp.concatenate(parts, axis=0) * SCALE).astype(BF16)

    kvt = _nt(wkv_ref[...], u)
    kvt_ref[0] = kvt[0:4 * KV_COLS]
    wint_ref[0] = kvt[4 * KV_COLS:6 * KV_COLS]
    for out_ref, src0, dst0 in ((kvt_ref, 2 * KV_COLS, 2 * KV_COLS), (wint_ref, 4 * KV_COLS, 0)):
        for h in range(N_KV_HEADS):
            d = dst0 + h * HEAD_DIM
            out_ref[0, d:d + ROT_HALF], out_ref[0, d + ROT_HALF:d + ROT_DIM] = rotated(kvt, src0 + h * HEAD_DIM)
    tm = x_ref.shape[0]
    ones_row = (lax.broadcasted_iota(jnp.int32, (HEAD_DIM, tm), 0) == 0).astype(BF16)
    for branch, src_ref, k0, v0 in ((0, kvt_ref, 2 * KV_COLS, 3 * KV_COLS), (1, wint_ref, 0, KV_COLS)):
        for h in range(N_KV_HEADS):
            i, r = branch * N_KV_HEADS + h, h * HEAD_DIM
            kb_ref[0, i] = src_ref[0, k0 + r:k0 + r + HEAD_DIM].astype(BF16)
            vb_ref[0, i, 0:HEAD_DIM] = src_ref[0, v0 + r:v0 + r + HEAD_DIM].astype(BF16)
            vb_ref[0, i, HEAD_DIM:2 * HEAD_DIM] = ones_row

    r = _nt(u, wr_ref[...])
    rg_ref[...] = r[:, 0:D_RNN]
    rx_ref[...] = r[:, D_RNN:2 * D_RNN]
    gt_ref[0] = jax.nn.sigmoid(_nt(wg_ref[...], u))


def _rope_tables(pos):
    inv = ROPE_THETA ** (-jnp.arange(ROT_HALF, dtype=F32) / ROT_HALF)
    ang = pos.astype(F32)[:, None] * inv
    return jnp.cos(ang).T, jnp.sin(ang).T


def _proj(x, pos, nm, wq, wkv, wr, wg, nb, tt, tm):
    rows = nb * tt
    nt = tt // tm
    cos, sin = _rope_tables(pos)
    row = lambda w: pl.BlockSpec((tm, w), lambda i: (i, 0))
    full = lambda a: pl.BlockSpec(a.shape, lambda i: (0,) * a.ndim)
    tab = pl.BlockSpec((ROT_HALF, tm), lambda i: (0, i % nt))
    tr = lambda r: pl.BlockSpec((1, r, tm), lambda i: (i // nt, 0, i % nt))
    out_shape = (
        jax.ShapeDtypeStruct((nb, ATTN_WIDTH, tt), BF16),
        jax.ShapeDtypeStruct((nb, ATTN_WIDTH, tt), BF16),
        jax.ShapeDtypeStruct((nb, 4 * KV_COLS, tt), F32),
        jax.ShapeDtypeStruct((nb, 2 * KV_COLS, tt), F32),
        jax.ShapeDtypeStruct((nb, 2 * N_KV_HEADS, HEAD_DIM, tt), BF16),
        jax.ShapeDtypeStruct((nb, 2 * N_KV_HEADS, 2 * HEAD_DIM, tt), BF16),
        jax.ShapeDtypeStruct((rows, D_RNN), F32),
        jax.ShapeDtypeStruct((rows, D_RNN), F32),
        jax.ShapeDtypeStruct((nb, N_KV_HEADS * GATE_ROWS, tt), F32),
    )
    tr4 = lambda r: pl.BlockSpec((1, 2 * N_KV_HEADS, r, tm), lambda i: (i // nt, 0, 0, i % nt))
    out_specs = (tr(ATTN_WIDTH), tr(ATTN_WIDTH), tr(4 * KV_COLS), tr(2 * KV_COLS), tr4(HEAD_DIM), tr4(2 * HEAD_DIM),
                 row(D_RNN), row(D_RNN), tr(N_KV_HEADS * GATE_ROWS))
    return pl.pallas_call(
        _proj_kernel, grid=(rows // tm,),
        in_specs=[row(D_MODEL), full(nm), full(wq), full(wkv), full(wr), full(wg), tab, tab],
        out_specs=out_specs, out_shape=out_shape, compiler_params=_params("parallel"), name="proj",
    )(x, nm, wq, wkv, wr, wg, cos, sin)


def _expm1_from_exp(x, u):
    near = jnp.where(u == 1.0, x, (u - 1.0) * x / jnp.log(u))
    return jnp.where(x < -30.0, -1.0, near)


def _softplus(x):
    return jnp.maximum(x, 0.0) + jnp.log1p(jnp.exp(-jnp.abs(x)))


def _lru_coeffs(xc, wra_ref, wri_ref, bra_ref, bri_ref, lam_ref):
    xb = xc.astype(BF16)
    r = jax.nn.sigmoid(_nn(xb, wra_ref[...]) + bra_ref[...])
    i = jax.nn.sigmoid(_nn(xb, wri_ref[...]) + bri_ref[...])
    log_a = -LRU_C * r * _softplus(-lam_ref[...])
    a = jnp.exp(log_a)
    b = jnp.sqrt(-_expm1_from_exp(2.0 * log_a, a * a)) * (i * xc)
    return a, b


def _rnn_out(rg, h, g):
    return _rms(jax.nn.gelu(rg) * h, g).astype(BF16)


def _rglru_prompt_kernel(rx_ref, rg_ref, cw_ref, cb_ref, wra_ref, wri_ref, bra_ref, bri_ref, lam_ref, g_ref,
                         y_ref, hl_ref, prev_sc, h_sc):
    tt = rx_ref.shape[0]

    @pl.when(pl.program_id(1) == 0)
    def _():
        prev_sc[...] = jnp.zeros_like(prev_sc)
        h_sc[...] = jnp.zeros_like(h_sc)

    rx = rx_ref[...]
    ext = jnp.concatenate([prev_sc[...], rx], axis=0)
    shifted = lambda d: pltpu.roll(ext, d, 0)[SUBLANES:SUBLANES + tt]
    cw = cw_ref[...]
    xc = cb_ref[...] + cw[0:1] * shifted(3)
    xc = xc + cw[1:2] * shifted(2)
    xc = xc + cw[2:3] * shifted(1)
    xc = xc + cw[3:4] * rx
    prev_sc[...] = rx[tt - SUBLANES:tt]

    a, b = _lru_coeffs(xc, wra_ref, wri_ref, bra_ref, bri_ref, lam_ref)
    n_tiles, width = tt // SUBLANES, a.shape[1]
    a = a.reshape(n_tiles, SUBLANES, width)
    b = b.reshape(n_tiles, SUBLANES, width)
    sub = lax.broadcasted_iota(jnp.int32, (1, SUBLANES, 1), 1)
    s = 1
    while s < SUBLANES:
        a_sh = jnp.where(sub >= s, pltpu.roll(a, s, 1), 1.0)
        b_sh = jnp.where(sub >= s, pltpu.roll(b, s, 1), 0.0)
        b = a * b_sh + b
        a = a * a_sh
        s *= 2
    state, tiles = h_sc[...], []
    for t in range(n_tiles):
        h_t = a[t] * state + b[t]
        tiles.append(h_t)
        state = h_t[SUBLANES - 1:SUBLANES]
    h = jnp.concatenate(tiles, axis=0)
    h_sc[...] = state
    hl_ref[0] = state
    y_ref[...] = _rnn_out(rg_ref[...], h, g_ref[...])


def _rglru_sample_kernel(rx_ref, rg_ref, cp_ref, h0_ref, cw_ref, cb_ref, wra_ref, wri_ref, bra_ref, bri_ref,
                         lam_ref, g_ref, y_ref, cs_ref, hl_ref):
    nb = h0_ref.shape[0]
    steps = rx_ref.shape[0] // nb
    xp = [cp_ref[k] for k in range(CONV_WIDTH - 1)] + [rx_ref[t * nb:(t + 1) * nb] for t in range(steps)]
    cw, cb = cw_ref[...], cb_ref[...]
    xcs = []
    for t in range(steps):
        xc = cb + cw[0:1] * xp[t]
        for tap in range(1, CONV_WIDTH):
            xc = xc + cw[tap:tap + 1] * xp[t + tap]
        xcs.append(xc)
    a, b = _lru_coeffs(jnp.concatenate(xcs, axis=0), wra_ref, wri_ref, bra_ref, bri_ref, lam_ref)
    h = h0_ref[...]
    for t in range(steps):
        sl = slice(t * nb, (t + 1) * nb)
        h = a[sl] * h + b[sl]
        y_ref[sl] = _rnn_out(rg_ref[sl], h, g_ref[...])
    for k in range(CONV_WIDTH - 1):
        cs_ref[k] = xp[steps + k]
    hl_ref[...] = h


def _rglru_prompt(rx, rg, weights, nb, tt, chunk):
    nc = tt // chunk
    row = pl.BlockSpec((chunk, D_RNN), lambda n, c: (n * nc + c, 0))
    full = lambda a: pl.BlockSpec(a.shape, lambda n, c: (0,) * a.ndim)
    return pl.pallas_call(
        _rglru_prompt_kernel, grid=(nb, nc),
        in_specs=[row, row] + [full(w) for w in weights],
        out_specs=(row, pl.BlockSpec((1, 1, D_RNN), lambda n, c: (n, 0, 0))),
        out_shape=(jax.ShapeDtypeStruct((nb * tt, D_RNN), BF16), jax.ShapeDtypeStruct((nb, 1, D_RNN), F32)),
        scratch_shapes=[pltpu.VMEM((SUBLANES, D_RNN), F32), pltpu.VMEM((1, D_RNN), F32)],
        compiler_params=_params("parallel", "arbitrary"), name="rglru_prompt",
    )(rx, rg, *weights)


def _rglru_sample(rx, rg, conv_prev, h0, weights):
    nb = h0.shape[0]
    return pl.pallas_call(
        _rglru_sample_kernel,
        out_shape=(jax.ShapeDtypeStruct(rx.shape, BF16), jax.ShapeDtypeStruct(conv_prev.shape, F32),
                   jax.ShapeDtypeStruct((nb, D_RNN), F32)),
        compiler_params=pltpu.CompilerParams(vmem_limit_bytes=VMEM_LIMIT), name="rglru_sample",
    )(rx, rg, conv_prev, h0, *weights)


CMP_COLS = 2 * KV_COLS
HALF = CMP_BLOCK // 2
BLOCKS_PER_PAGE = PAGE_SIZE // CMP_STRIDE


def _compress_kernel(pt_ref, *refs, n_pages):
    del pt_ref
    page_refs = refs[:n_pages]
    pet_ref, perm_ref, w1_ref, w2_ref, out_ref, z_sc = refs[n_pages:]
    seqs = out_ref.shape[0]
    for k in range(n_pages):
        xt = page_refs[k][0].reshape(CMP_COLS, PAGE_SIZE)
        for half in range(2):
            a = (xt + pet_ref[half]).astype(BF16)
            z_sc[half, k] = _nt(perm_ref[...], a)
    pre = []
    for half in range(2):
        acc = jnp.zeros((seqs * N_CMP_PAD, CMP_COLS), F32)
        for j in range(CMP_STRIDE):
            rows = z_sc[half, :, j * BLOCKS_PER_PAGE:(j + 1) * BLOCKS_PER_PAGE, :]
            acc = acc + _nn(rows.reshape(seqs * N_CMP_PAD, CMP_COLS).astype(BF16), w1_ref[half, j])
        pre.append(acc)
    hid = jax.nn.gelu(pre[0] + pltpu.roll(pre[1], seqs * N_CMP_PAD - 1, 0))
    out_ref[...] = _nn(hid.astype(BF16), w2_ref[...]).astype(BF16).reshape(seqs, N_CMP_PAD, CMP_COLS)


def _compress(pages, page_spec, page_table, pet, w1, w2, nb, seqs):
    n_pages = page_table.shape[1]
    assert n_pages * BLOCKS_PER_PAGE == N_CMP_PAD
    pos = np.arange(PAGE_SIZE)
    perm = jnp.asarray((pos % CMP_STRIDE * BLOCKS_PER_PAGE + pos // CMP_STRIDE)[None, :]
                       == np.arange(PAGE_SIZE)[:, None], BF16)
    full = lambda a: pl.BlockSpec(a.shape, lambda n, pt: (0,) * a.ndim)
    once = lambda a: pl.BlockSpec(a.shape, lambda n, pt: (0,) * a.ndim, pipeline_mode=pl.Buffered(1))
    page_specs = [page_spec(i, k) for i in range(seqs) for k in range(n_pages)]
    grid_spec = pltpu.PrefetchScalarGridSpec(
        num_scalar_prefetch=1, grid=(nb // seqs,),
        in_specs=page_specs + [full(pet), full(perm), once(w1), full(w2)],
        out_specs=pl.BlockSpec((seqs, N_CMP_PAD, CMP_COLS), lambda n, pt: (n, 0, 0)),
        scratch_shapes=[pltpu.VMEM((2, len(page_specs), PAGE_SIZE, CMP_COLS), F32)])
    return pl.pallas_call(
        functools.partial(_compress_kernel, n_pages=len(page_specs)), grid_spec=grid_spec,
        out_shape=jax.ShapeDtypeStruct((nb, N_CMP_PAD, CMP_COLS), BF16),
        compiler_params=_params("parallel"), name="compress",
    )(page_table, *([pages] * len(page_specs)), pet, perm, w1, w2)


def _cmp_cols(cmp_ref, i, slot, h):
    c0 = slot * KV_COLS + h * HEAD_DIM
    return cmp_ref[i, :, c0:c0 + HEAD_DIM]


def _overlap_matrix(n_sel):
    n_cmp = N_CMP_PAD - 1
    c0 = np.arange(N_CMP_PAD)[:, None] * CMP_STRIDE
    j0 = np.arange(LANES)[None, :] * SEL_BLOCK
    ov = (c0 < j0 + SEL_BLOCK) & (c0 + CMP_BLOCK > j0)
    ov &= (np.arange(N_CMP_PAD)[:, None] < n_cmp) & (np.arange(LANES)[None, :] < n_sel)
    return ov.astype(np.float32)


def _topk_member(score, idx, n, axis):
    rank = jnp.zeros(score.shape, jnp.int32)
    for i in range(n):
        si = score[i:i + 1, :] if axis == 0 else score[:, i:i + 1]
        beats = (si > score) | ((si == score) & (i < idx))
        rank = rank + beats.astype(jnp.int32)
    return rank < TOP_N


def _sel_score(imp, j, cur):
    valid = j <= cur
    forced = (j == 0) | (j == cur) | (j == cur - 1)
    return jnp.where(valid, imp, -SEL_BONUS) + jnp.where(forced, SEL_BONUS, 0.0)


def _nsa_prompt_kernel(qt_ref, qrt_ref, kb_ref, vb_ref, cmp_ref, gt_ref, ovt_ref, e_ref,
                       o_ref, m_sc, acc_sc, *, tq, slc_chunk, col_block, n_sel):
    heads = range(N_KV_HEADS)
    q0 = pl.program_id(1) * tq
    wide = GQA * tq
    qpos = q0 + lax.broadcasted_iota(jnp.int32, (1, tq), 1)
    tile = lambda a, n: jnp.concatenate([a] * n, axis=1)
    group = lambda ref, h: jnp.concatenate(
        [ref[0, (h * GQA + g) * HEAD_DIM:(h * GQA + g + 1) * HEAD_DIM, :] for g in range(GQA)], axis=1)

    cidx = lax.broadcasted_iota(jnp.int32, (N_CMP_PAD, 1), 0)
    real = cidx < N_CMP_PAD - 1
    qpos_all = q0 + lax.broadcasted_iota(jnp.int32, (1, N_HEADS * tq), 1) % tq
    cmask = (cidx * CMP_STRIDE + CMP_BLOCK - 1 <= qpos_all) & real
    s = jnp.concatenate([_nn(_cmp_cols(cmp_ref, 0, 0, h), group(qt_ref, h)) for h in heads], axis=1)
    s = jnp.where(real, jnp.where(cmask, s, NEG), -jnp.inf)
    e = jnp.exp(s - jnp.max(s, axis=0, keepdims=True))
    p = (e / jnp.sum(e, axis=0, keepdims=True)) * cmask.astype(F32)
    pb = p.astype(BF16)
    o_cmp = [lax.dot_general(_cmp_cols(cmp_ref, 0, 1, h), pb[:, h * wide:(h + 1) * wide], TN_DIMS,
                             preferred_element_type=F32) for h in heads]

    imp = []
    for h in heads:
        imp_h = _nn(ovt_ref[...], pb[:, h * wide:h * wide + tq])
        for g in range(1, GQA):
            imp_h = imp_h + _nn(ovt_ref[...], pb[:, h * wide + g * tq:h * wide + (g + 1) * tq])
        imp.append(imp_h)
    j = lax.broadcasted_iota(jnp.int32, (n_sel, 1), 0)
    qpos_2 = q0 + lax.broadcasted_iota(jnp.int32, (1, N_KV_HEADS * tq), 1) % tq
    picked = _topk_member(_sel_score(jnp.concatenate(imp, axis=1), j, qpos_2 // SEL_BLOCK), j, n_sel, axis=0)
    unpicked = jnp.where(picked, 0.0, -SEL_MASK).astype(BF16)
    pad_rows = jnp.zeros((HEAD_DIM - n_sel, wide), BF16)
    qr = [group(qrt_ref, h) for h in heads]
    qsel = [jnp.concatenate([qr[h], tile(unpicked[:, h * tq:(h + 1) * tq], GQA), pad_rows], axis=0)
            for h in heads]

    def attend(h, qx, k, v, allowed, first):
        sc = lax.dot_general(k, qx, TN_DIMS, preferred_element_type=F32)
        bias = jnp.where(allowed, 0.0, NEG)
        pes, ms, alphas = [], [], []
        for c0 in range(0, wide, col_block):
            t0 = c0 % tq
            x = sc[:, c0:c0 + col_block] + bias[:, t0:t0 + col_block]
            m = jnp.max(x, axis=0, keepdims=True)
            if not first:
                m_prev = m_sc[h, :, c0:c0 + col_block]
                m = jnp.maximum(m_prev, m)
                alphas.append(jnp.exp(m_prev - m))
            ms.append(m)
            pes.append(jnp.exp(x - m).astype(BF16))
        pv = _nn(v, jnp.concatenate(pes, axis=1))
        m_sc[h] = jnp.concatenate(ms, axis=1)
        acc_sc[h] = pv if first else jnp.concatenate(alphas, axis=1) * acc_sc[h] + pv

    def normalised(h):
        acc = acc_sc[h]
        return acc[0:HEAD_DIM] / acc[HEAD_DIM:HEAD_DIM + 1]

    wn = WINDOW + tq
    w0 = pl.multiple_of(jnp.maximum(q0 - WINDOW, 0), LANES)
    kpos = w0 + lax.broadcasted_iota(jnp.int32, (wn, 1), 0)
    in_window = (kpos <= qpos) & (kpos > qpos - WINDOW)
    for h in heads:
        attend(h, qr[h], kb_ref[0, N_KV_HEADS + h, :, pl.ds(w0, wn)], vb_ref[0, N_KV_HEADS + h, :, pl.ds(w0, wn)],
               in_window, True)
    o_win = [normalised(h) for h in heads]

    for c0 in range(0, kb_ref.shape[3], slc_chunk):
        def chunk(c0=c0):
            kpos = c0 + lax.broadcasted_iota(jnp.int32, (slc_chunk, 1), 0)
            member = jnp.concatenate([e_ref[:, c0:c0 + slc_chunk],
                                      jnp.zeros((HEAD_DIM - n_sel, slc_chunk), BF16)], axis=0)
            for h in heads:
                k = jnp.concatenate([kb_ref[0, h, :, c0:c0 + slc_chunk], member], axis=0)
                attend(h, qsel[h], k, vb_ref[0, h, :, c0:c0 + slc_chunk], kpos <= qpos, c0 == 0)

        if c0 == 0:
            chunk()
        else:
            pl.when(q0 + tq > c0)(chunk)
    o_slc = [normalised(h) for h in heads]

    gt = gt_ref[0]
    for h in heads:
        for g in range(GQA):
            gate = lambda b: gt[h * GATE_ROWS + b * GQA + g:h * GATE_ROWS + b * GQA + g + 1, :]
            cols = slice(g * tq, (g + 1) * tq)
            o = gate(0) * o_cmp[h][:, cols] + gate(1) * o_slc[h][:, cols]
            o = o + gate(2) * o_win[h][:, cols]
            o_ref[(h * GQA + g) * HEAD_DIM:(h * GQA + g + 1) * HEAD_DIM, :] = o


def _nsa_prompt(qt, qrt, kb, vb, cmp, gt, nb, tt, tq, slc_chunk):
    n_sel = -(-tt // SEL_BLOCK)
    nq = tt // tq
    ovt = jnp.asarray(_overlap_matrix(n_sel)[:, :n_sel].T, BF16)
    expand = jnp.asarray(np.arange(tt)[None, :] // SEL_BLOCK == np.arange(n_sel)[:, None], BF16)
    cols = lambda a: pl.BlockSpec((1, a.shape[1], tq), lambda n, i: (n, 0, i))
    seq = lambda a: pl.BlockSpec((1,) + a.shape[1:], lambda n, i: (n,) + (0,) * (a.ndim - 1))
    full = lambda a: pl.BlockSpec(a.shape, lambda n, i: (0,) * a.ndim)
    return pl.pallas_call(
        functools.partial(_nsa_prompt_kernel, tq=tq, slc_chunk=slc_chunk, col_block=LANES, n_sel=n_sel),
        grid=(nb, nq),
        in_specs=[cols(qt), cols(qrt), seq(kb), seq(vb), seq(cmp), cols(gt), full(ovt), full(expand)],
        out_specs=pl.BlockSpec((ATTN_WIDTH, tq), lambda n, i: (0, n * nq + i)),
        out_shape=jax.ShapeDtypeStruct((ATTN_WIDTH, nb * tt), F32),
        scratch_shapes=[pltpu.VMEM((N_KV_HEADS, 1, GQA * tq), F32),
                        pltpu.VMEM((N_KV_HEADS, 2 * HEAD_DIM, GQA * tq), F32)],
        compiler_params=_params("parallel", "arbitrary"), name="nsa_prompt",
    )(qt, qrt, kb, vb, cmp, gt, ovt, expand)


def _nsa_sample_kernel(pt_ref, *refs, n_pages, steps, seqs):
    del pt_ref
    q_ref, qr_ref = refs[0:2]
    page_refs = refs[2:2 + seqs * n_pages]
    cmp_ref, cw_ref, newk_ref, newv_ref, gates_ref, ov_ref, o_ref, wout_ref = refs[2 + seqs * n_pages:]
    rows = GQA * steps
    past = n_pages * PAGE_SIZE
    wc = cw_ref.shape[-1]
    trow = lax.broadcasted_iota(jnp.int32, (rows, 1), 0) % steps
    lane = lax.broadcasted_iota(jnp.int32, (1, LANES), 1)
    n_sel = -(-(past + steps) // SEL_BLOCK)
    cur = (past + trow) // SEL_BLOCK
    real = lane < N_CMP_PAD - 1
    wlane = lax.broadcasted_iota(jnp.int32, (1, wc), 1)
    in_window = jnp.broadcast_to(wlane + (WINDOW - wc) > trow, (rows, wc))
    per_tile = PAGE_SIZE // SEL_BLOCK

    chains = [(i, h) for i in range(seqs) for h in range(N_KV_HEADS)]
    assert len(chains) * steps == rows, "the stacked top-k below holds one chain per query-group slot"
    s = jnp.concatenate([_nt(q_ref[i, h], _cmp_cols(cmp_ref, i, 0, h)) for i, h in chains], axis=0)
    s = jnp.where(real, s, -jnp.inf)
    e = jnp.exp(s - jnp.max(s, axis=-1, keepdims=True))
    pb_all = (e / jnp.sum(e, axis=-1, keepdims=True)).astype(BF16)
    group = lax.broadcasted_iota(jnp.int32, (rows, 1), 0) // steps
    o_cmps, imp_all = [], None
    for c, (i, h) in enumerate(chains):
        pb = pb_all[c * rows:(c + 1) * rows]
        o_cmps.append(_nn(pb, _cmp_cols(cmp_ref, i, 1, h)))
        part = _nn(pb, ov_ref[...])
        imp = part
        for g in range(1, GQA):
            imp = imp + pltpu.roll(part, g * steps, 0)
        imp_all = imp if c == 0 else jnp.where(group == c, imp, imp_all)
    score = jnp.where(lane < n_sel, _sel_score(imp_all, lane, cur), -jnp.inf)
    sel_all = (_topk_member(score, lane, n_sel, axis=1) & (lane < n_sel)).astype(F32)
    rolled = [sel_all] + [pltpu.roll(sel_all, k * steps, 0) for k in range(1, GQA)]

    for c, (i, h) in enumerate(chains):
        newk, newv = newk_ref[i], newv_ref[i]
        newk_b, newv_b = newk.astype(BF16), newv.astype(BF16)
        qr = qr_ref[i, h]
        pages = page_refs[i * n_pages:(i + 1) * n_pages]

        def new_ok(branch, h=h):
            base = (branch * N_KV_HEADS + h) * steps
            return (lane >= base) & (lane < base + steps) & (lane - base <= trow)

        sel_f = rolled[(0 - c) % GQA]
        for g in range(1, GQA):
            sel_f = jnp.where(group == g, rolled[(g - c) % GQA], sel_f)

        ks = jnp.concatenate([p[0, 0, h].astype(BF16) for p in pages] + [newk_b], axis=1)
        vs = jnp.concatenate([p[0, 1, h].astype(BF16) for p in pages] + [newv_b], axis=1)
        tiles = []
        for k in range(n_pages):
            m = sel_f[:, per_tile * k:per_tile * k + 1]
            for b in range(1, per_tile):
                m = jnp.where(lane < b * SEL_BLOCK, m, sel_f[:, per_tile * k + b:per_tile * k + b + 1])
            tiles.append(m > 0.5)
        tiles.append((sel_f[:, n_sel - 1:n_sel] > 0.5) & new_ok(0))
        msk = jnp.concatenate(tiles, axis=1)
        sc = jnp.where(msk, _nn(qr, ks), NEG)
        pe = jnp.where(msk, jnp.exp(sc - jnp.max(sc, axis=-1, keepdims=True)), 0.0)
        o_slc = _nt(pe.astype(BF16), vs) / jnp.sum(pe, axis=-1, keepdims=True)

        kw = jnp.concatenate([cw_ref[i, 0, h].astype(BF16), newk_b], axis=1)
        vw = jnp.concatenate([cw_ref[i, 1, h].astype(BF16), newv_b], axis=1)
        wmsk = jnp.concatenate([in_window, jnp.broadcast_to(new_ok(1), (rows, LANES))], axis=1)
        sw = jnp.where(wmsk, _nn(qr, kw), NEG)
        pw = jnp.where(wmsk, jnp.exp(sw - jnp.max(sw, axis=-1, keepdims=True)), 0.0)
        o_win = _nt(pw.astype(BF16), vw) / jnp.sum(pw, axis=-1, keepdims=True)

        gt = gates_ref[i, h]
        o_ref[i, h] = gt[:, 0:1] * o_cmps[c] + gt[:, 1:2] * o_slc + gt[:, 2:3] * o_win

        to_end = (LANES - steps - (N_KV_HEADS + h) * steps) % LANES
        for slot, new in ((0, newk), (1, newv)):
            shifted = pltpu.roll(cw_ref[i, slot, h], wc - steps, 1)
            wout_ref[i, slot, h, :, 0:wc - LANES] = shifted[:, 0:wc - LANES]
            wout_ref[i, slot, h, :, wc - LANES:wc] = jnp.where(lane >= LANES - steps, pltpu.roll(new, to_end, 1),
                                                               shifted[:, wc - LANES:wc])


def _nsa_sample(q, qr, pages, page_table, cmp, cache_win_t, new_k, new_v, gates, steps, seqs):
    nb, n_pages = page_table.shape
    n_sel = -(-(n_pages * PAGE_SIZE + steps) // SEL_BLOCK)
    ov = jnp.asarray(_overlap_matrix(n_sel), BF16)
    rows = GQA * steps
    lead = lambda a: pl.BlockSpec((seqs,) + a.shape[1:], lambda n, pt: (n,) + (0,) * (a.ndim - 1))
    page_spec = lambda i, k: pl.BlockSpec((1, 2, N_KV_HEADS, HEAD_DIM, PAGE_SIZE),
                                          lambda n, pt: (pt[n * seqs + i, k], 1, 0, 0, 0))
    page_specs = [page_spec(i, k) for i in range(seqs) for k in range(n_pages)]
    grid_spec = pltpu.PrefetchScalarGridSpec(
        num_scalar_prefetch=1, grid=(nb // seqs,),
        in_specs=[lead(q), lead(qr)] + page_specs
        + [lead(cmp), lead(cache_win_t), lead(new_k), lead(new_v), lead(gates),
           pl.BlockSpec(ov.shape, lambda n, pt: (0, 0))],
        out_specs=(pl.BlockSpec((seqs, N_KV_HEADS, rows, HEAD_DIM), lambda n, pt: (n, 0, 0, 0)), lead(cache_win_t)))
    return pl.pallas_call(
        functools.partial(_nsa_sample_kernel, n_pages=n_pages, steps=steps, seqs=seqs), grid_spec=grid_spec,
        out_shape=(jax.ShapeDtypeStruct((nb, N_KV_HEADS, rows, HEAD_DIM), F32),
                   jax.ShapeDtypeStruct(cache_win_t.shape, F32)),
        compiler_params=_params("parallel"), name="nsa_sample",
    )(page_table, q, qr, *([pages] * len(page_specs)), cmp, cache_win_t, new_k, new_v, gates, ov)


def _mlp_kernel(x_ref, oat_ref, yr_ref, ga_ref, wo_ref, nmlp_ref, wup_ref, wdn_ref, nfin_ref, y_ref, *, ff_chunk):
    oat = oat_ref[...]
    at = (oat * lax.rsqrt(jnp.mean(oat * oat, axis=0, keepdims=True) + EPS) * ga_ref[...]).astype(BF16)
    attn = lax.dot_general(at, wo_ref[0:ATTN_WIDTH, :], TN_DIMS, preferred_element_type=F32)
    x1 = x_ref[...] + (attn + _nn(yr_ref[...], wo_ref[ATTN_WIDTH:, :]))
    v = _rms(x1, nmlp_ref[...]).astype(BF16)
    acc = jnp.zeros_like(x1)
    for c in range(D_FF // ff_chunk):
        sl = slice(c * ff_chunk, (c + 1) * ff_chunk)
        hid = jnp.square(jnp.maximum(_nn(v, wup_ref[:, sl]), 0.0)).astype(BF16)
        acc = acc + _nn(hid, wdn_ref[sl, :])
    y_ref[...] = _rms(x1 + acc, nfin_ref[...])


def _mlp(x, o_attn_t, y_rnn, g_attn, w_out, norm_mlp, w_up, w_down, norm_final, tm, ff_chunk):
    rows = x.shape[0]
    row = lambda w: pl.BlockSpec((tm, w), lambda i: (i, 0))
    full = lambda a: pl.BlockSpec(a.shape, lambda i: (0,) * a.ndim)
    once = lambda a: pl.BlockSpec(a.shape, lambda i: (0,) * a.ndim, pipeline_mode=pl.Buffered(1))
    return pl.pallas_call(
        functools.partial(_mlp_kernel, ff_chunk=ff_chunk), grid=(rows // tm,),
        in_specs=[row(D_MODEL), pl.BlockSpec((ATTN_WIDTH, tm), lambda i: (0, i)), row(D_RNN), full(g_attn),
                  once(w_out), full(norm_mlp),
                  once(w_up), once(w_down), full(norm_final)],
        out_specs=row(D_MODEL), out_shape=jax.ShapeDtypeStruct((rows, D_MODEL), F32),
        compiler_params=_params("parallel"), name="mlp",
    )(x, o_attn_t, y_rnn, g_attn, w_out, norm_mlp, w_up, w_down, norm_final)


def _block_diag(w):
    nb, bs = w.shape[-3], w.shape[-1]
    tiled = jnp.tile(w, (1,) * (w.ndim - 1) + (nb,)).reshape(w.shape[:-3] + (nb * bs, nb * bs))
    on_diagonal = np.kron(np.eye(nb, dtype=bool), np.ones((bs, bs), dtype=bool))
    return jnp.where(on_diagonal, tiled, jnp.zeros((), w.dtype))


def kernel(x_prompt, x_sample, cache_kv, cache_win, state_conv, state_rnn, page_table, w_in, pe_ck, w_ck1, w_ck2,
           pe_cv, w_cv1, w_cv2, g_attn, g_rnn, conv_w, conv_b, w_ra, b_ra, w_ri, b_ri, lam, w_out, norm_mix,
           norm_mlp, w_up, w_down, norm_final):
    assert w_in.shape[0] == 1, "single layer"
    nbp, tp, _ = x_prompt.shape
    nbs, steps, _ = x_sample.shape
    n_pages = page_table.shape[1]

    wt = jnp.transpose(w_in[0]).astype(BF16)
    c_kv, c_g = ATTN_WIDTH, ATTN_WIDTH + 6 * KV_COLS
    c_rg = c_g + 3 * N_HEADS
    wq, wkv = wt[0:c_kv], wt[c_kv:c_g]
    wg = wt[c_g:c_rg].reshape(3, N_KV_HEADS, GQA, D_MODEL).transpose(1, 0, 2, 3).reshape(N_KV_HEADS, 3 * GQA, D_MODEL)
    wg = jnp.pad(wg, ((0, 0), (0, GATE_ROWS - 3 * GQA), (0, 0))).reshape(N_KV_HEADS * GATE_ROWS, D_MODEL)
    wr = wt[c_rg:c_rg + 2 * D_RNN]
    row2 = lambda a: a.reshape(1, -1)
    rnn_w = (conv_w[0], row2(conv_b[0]), _block_diag(w_ra[0].astype(BF16)), _block_diag(w_ri[0].astype(BF16)),
             row2(b_ra[0]), row2(b_ri[0]), row2(lam[0]), row2(g_rnn[0]))
    per_cs = lambda k, v: jnp.repeat(jnp.stack([k, v]), N_KV_HEADS, axis=0)
    pe_cs = per_cs(pe_ck[0], pe_cv[0]).reshape(2 * N_KV_HEADS, 2, HALF, HEAD_DIM)
    cmp_pet = jnp.tile(pe_cs.transpose(1, 0, 3, 2), (1, 1, 1, BLOCKS_PER_PAGE)).reshape(2, CMP_COLS, PAGE_SIZE)
    w1_cs = per_cs(w_ck1[0], w_cv1[0]).transpose(1, 0, 2, 3)
    cmp_w1 = _block_diag(w1_cs.astype(BF16)).reshape(2, HALF, CMP_COLS, CMP_COLS)
    cmp_w2 = _block_diag(per_cs(w_ck2[0], w_cv2[0]).astype(BF16))
    mlp_w = (g_attn[0].reshape(-1, 1), w_out[0].astype(BF16), row2(norm_mlp[0]), w_up[0].astype(BF16),
             w_down[0].astype(BF16), row2(norm_final))
    nm = row2(norm_mix[0])

    tm = 512
    xp = x_prompt.reshape(nbp * tp, D_MODEL)
    qt, qrt, kvt, wint, kb, vb, rg, rx, gt = _proj(xp, jnp.arange(tp), nm, wq, wkv, wr, wg, nbp, tp, tm)
    y_rnn, h_last = _rglru_prompt(rx, rg, rnn_w, nbp, tp, 256)
    own_pages = jnp.zeros((nbp, tp // PAGE_SIZE), jnp.int32)
    cmp_seqs = 2
    own_page = lambda i, k: pl.BlockSpec((1, CMP_COLS, PAGE_SIZE), lambda n, pt: (n * cmp_seqs + i, 0, k))
    cmp = _compress(kvt, own_page, own_pages, cmp_pet, cmp_w1, cmp_w2, nbp, cmp_seqs)
    o_attn_t = _nsa_prompt(qt, qrt, kb, vb, cmp, gt, nbp, tp, 256, 512)
    y_prompt = _mlp(xp, o_attn_t, y_rnn, *mlp_w, tm, 1024).reshape(nbp, tp, D_MODEL)
    wlen = min(WINDOW, tp)
    kv_prompt = kvt.reshape(nbp, 4, N_KV_HEADS, HEAD_DIM, tp).transpose(0, 4, 1, 2, 3)[None]
    win_prompt = wint[:, :, tp - wlen:].reshape(nbp, 2, N_KV_HEADS, HEAD_DIM, wlen).transpose(0, 4, 1, 2, 3)[None]
    conv_prompt = rx.reshape(nbp, tp, D_RNN)[:, tp - (CONV_WIDTH - 1):][None]
    h_prompt = h_last.reshape(1, nbp, D_RNN)

    rows_s = nbs * steps
    xs = x_sample.transpose(1, 0, 2).reshape(rows_s, D_MODEL)
    pos_s = PAST_LEN + jnp.arange(rows_s) // nbs
    qt, qrt, kvt, wint, _, _, rg, rx, gt = _proj(xs, pos_s, nm, wq, wkv, wr, wg, 1, rows_s, rows_s)
    y_rnn, conv_s, h_s = _rglru_sample(rx, rg, state_conv[0].transpose(1, 0, 2), state_rnn[0], rnn_w)
    pages_t = cache_kv[0].transpose(0, 2, 3, 4, 1)
    cmp_page = lambda i, k: pl.BlockSpec((1, 2, N_KV_HEADS, HEAD_DIM, PAGE_SIZE),
                                         lambda n, pt: (pt[n * cmp_seqs + i, k], 0, 0, 0, 0))
    cmp = _compress(pages_t, cmp_page, page_table, cmp_pet, cmp_w1, cmp_w2, nbs, cmp_seqs)
    by_seq = lambda a: a.reshape(N_KV_HEADS, GQA, HEAD_DIM, steps, nbs).transpose(4, 0, 1, 3, 2).reshape(
        nbs, N_KV_HEADS, GQA * steps, HEAD_DIM)
    def new_tile(slc_rows, win_rows):
        t = jnp.stack([slc_rows, win_rows]).reshape(2 * N_KV_HEADS, HEAD_DIM, steps, nbs)
        t = t.transpose(3, 1, 0, 2).reshape(nbs, HEAD_DIM, 2 * N_KV_HEADS * steps)
        return jnp.pad(t, ((0, 0), (0, 0), (0, LANES - 2 * N_KV_HEADS * steps)))
    new_k = new_tile(kvt[0, 2 * KV_COLS:3 * KV_COLS], wint[0, 0:KV_COLS])
    new_v = new_tile(kvt[0, 3 * KV_COLS:4 * KV_COLS], wint[0, KV_COLS:2 * KV_COLS])
    gates_s = gt[0].reshape(N_KV_HEADS, GATE_ROWS, steps, nbs)[:, :3 * GQA].reshape(N_KV_HEADS, 3, GQA, steps, nbs)
    gates_s = gates_s.transpose(4, 0, 2, 3, 1).reshape(nbs, N_KV_HEADS, GQA * steps, 3)
    cache_win_t = cache_win[0].transpose(0, 2, 3, 4, 1)
    o_s, win_t = _nsa_sample(by_seq(qt[0]), by_seq(qrt[0]), pages_t, page_table, cmp, cache_win_t, new_k, new_v,
                             gates_s, steps, 2)
    o_attn_t = o_s.reshape(nbs, N_KV_HEADS, GQA, steps, HEAD_DIM).transpose(1, 2, 4, 3, 0).reshape(ATTN_WIDTH, rows_s)
    y_sample = _mlp(xs, o_attn_t, y_rnn, *mlp_w, rows_s, 1024).reshape(steps, nbs, D_MODEL).transpose(1, 0, 2)
    kv_sample = kvt[0].reshape(4, N_KV_HEADS, HEAD_DIM, steps, nbs).transpose(4, 3, 0, 1, 2)[None]
    win_sample = win_t.transpose(0, 4, 1, 2, 3)[None]
    conv_sample = conv_s.transpose(1, 0, 2)[None]
    h_sample = h_s[None]

    return (y_prompt, y_sample, kv_prompt, kv_sample, win_prompt, win_sample, conv_prompt, conv_sample,
            h_prompt, h_sample)
```

```python
import functools

import jax
import jax.numpy as jnp
import numpy as np
from jax import lax
from jax.experimental import pallas as pl
from jax.experimental.pallas import tpu as pltpu

D_MODEL = 1024
PAST_LEN = 2048
PAGE_SIZE = 128
HEAD_DIM = 64
N_HEADS = 8
N_KV_HEADS = 2
GQA = N_HEADS // N_KV_HEADS
ATTN_WIDTH = N_HEADS * HEAD_DIM
D_RNN = D_MODEL - ATTN_WIDTH
CONV_WIDTH = 4
LRU_C = 8.0
D_FF = 4 * D_MODEL
ROT_DIM = HEAD_DIM // 4
ROT_HALF = ROT_DIM // 2
ROPE_THETA = 500000.0
CMP_BLOCK = 32
CMP_STRIDE = 16
SEL_BLOCK = 64
TOP_N = 16
WINDOW = 512
KV_COLS = N_KV_HEADS * HEAD_DIM
EPS = 1e-6
NEG = -1e30
SEL_BONUS = 1e4
SEL_MASK = 2.0 ** 100
SCALE = HEAD_DIM ** -0.5

SUBLANES = 8
LANES = 128
VMEM_LIMIT = 48 * 1024 * 1024

ROW_TILE = 512
RNN_CHUNK = 256
Q_TILE = 256
SLC_CHUNK = 512
FF_CHUNK = 1024
CMP_SEQS = 2
SAMPLE_SEQS = 2
EXPM1_FLOOR = -30.0

N_CMP_PAD = 128
GATE_ROWS = 16

F32 = jnp.float32
BF16 = jnp.bfloat16
NT_DIMS = (((1,), (1,)), ((), ()))
TN_DIMS = (((0,), (0,)), ((), ()))


def _nt(a, b):
    return lax.dot_general(a, b, NT_DIMS, preferred_element_type=F32)


def _nn(a, b):
    return jnp.dot(a, b, preferred_element_type=F32)


def _rms(x, g):
    return x * lax.rsqrt(jnp.mean(x * x, axis=-1, keepdims=True) + EPS) * g


def _params(*sem):
    return pltpu.CompilerParams(dimension_semantics=sem, vmem_limit_bytes=VMEM_LIMIT)


def _proj_kernel(x_ref, nm_ref, wq_ref, wkv_ref, wr_ref, wg_ref, cos_ref, sin_ref,
                 qt_ref, qrt_ref, kvt_ref, wint_ref, kb_ref, vb_ref, rg_ref, rx_ref, gt_ref):
    u = _rms(x_ref[...], nm_ref[...]).astype(BF16)
    cos, sin = cos_ref[...], sin_ref[...]

    def rotated(a, r0):
        x1, x2 = a[r0:r0 + ROT_HALF], a[r0 + ROT_HALF:r0 + ROT_DIM]
        return x1 * cos - x2 * sin, x2 * cos + x1 * sin

    qt = _nt(wq_ref[...], u)
    qt_ref[0] = (qt * SCALE).astype(BF16)
    parts = []
    for head in range(N_HEADS):
        r0 = head * HEAD_DIM
        parts += [*rotated(qt, r0), qt[r0 + ROT_DIM:r0 + HEAD_DIM]]
    qrt_ref[0] = (jnp.concatenate(parts, axis=0) * SCALE).astype(BF16)

    kvt = _nt(wkv_ref[...], u)
    kvt_ref[0] = kvt[0:4 * KV_COLS]
    wint_ref[0] = kvt[4 * KV_COLS:6 * KV_COLS]
    for out_ref, src0, dst0 in ((kvt_ref, 2 * KV_COLS, 2 * KV_COLS), (wint_ref, 4 * KV_COLS, 0)):
        for h in range(N_KV_HEADS):
            d = dst0 + h * HEAD_DIM
            out_ref[0, d:d + ROT_HALF], out_ref[0, d + ROT_HALF:d + ROT_DIM] = rotated(kvt, src0 + h * HEAD_DIM)
    tm = x_ref.shape[0]
    ones_row = (lax.broadcasted_iota(jnp.int32, (HEAD_DIM, tm), 0) == 0).astype(BF16)
    for branch, src_ref, k0, v0 in ((0, kvt_ref, 2 * KV_COLS, 3 * KV_COLS), (1, wint_ref, 0, KV_COLS)):
        for h in range(N_KV_HEADS):
            i, r = branch * N_KV_HEADS + h, h * HEAD_DIM
            kb_ref[0, i] = src_ref[0, k0 + r:k0 + r + HEAD_DIM].astype(BF16)
            vb_ref[0, i, 0:HEAD_DIM] = src_ref[0, v0 + r:v0 + r + HEAD_DIM].astype(BF16)
            vb_ref[0, i, HEAD_DIM:2 * HEAD_DIM] = ones_row

    r = _nt(u, wr_ref[...])
    rg_ref[...] = r[:, 0:D_RNN]
    rx_ref[...] = r[:, D_RNN:2 * D_RNN]
    gt_ref[0] = jax.nn.sigmoid(_nt(wg_ref[...], u))


def _rope_tables(pos):
    inv = ROPE_THETA ** (-jnp.arange(ROT_HALF, dtype=F32) / ROT_HALF)
    ang = pos.astype(F32)[:, None] * inv
    return jnp.cos(ang).T, jnp.sin(ang).T


def _proj(x, pos, nm, wq, wkv, wr, wg, nb, tt, tm):
    rows = nb * tt
    nt = tt // tm
    cos, sin = _rope_tables(pos)
    row = lambda w: pl.BlockSpec((tm, w), lambda i: (i, 0))
    full = lambda a: pl.BlockSpec(a.shape, lambda i: (0,) * a.ndim)
    tab = pl.BlockSpec((ROT_HALF, tm), lambda i: (0, i % nt))
    tr = lambda r: pl.BlockSpec((1, r, tm), lambda i: (i // nt, 0, i % nt))
    out_shape = (
        jax.ShapeDtypeStruct((nb, ATTN_WIDTH, tt), BF16),
        jax.ShapeDtypeStruct((nb, ATTN_WIDTH, tt), BF16),
        jax.ShapeDtypeStruct((nb, 4 * KV_COLS, tt), F32),
        jax.ShapeDtypeStruct((nb, 2 * KV_COLS, tt), F32),
        jax.ShapeDtypeStruct((nb, 2 * N_KV_HEADS, HEAD_DIM, tt), BF16),
        jax.ShapeDtypeStruct((nb, 2 * N_KV_HEADS, 2 * HEAD_DIM, tt), BF16),
        jax.ShapeDtypeStruct((rows, D_RNN), F32),
        jax.ShapeDtypeStruct((rows, D_RNN), F32),
        jax.ShapeDtypeStruct((nb, N_KV_HEADS * GATE_ROWS, tt), F32),
    )
    tr4 = lambda r: pl.BlockSpec((1, 2 * N_KV_HEADS, r, tm), lambda i: (i // nt, 0, 0, i % nt))
    out_specs = (tr(ATTN_WIDTH), tr(ATTN_WIDTH), tr(4 * KV_COLS), tr(2 * KV_COLS), tr4(HEAD_DIM), tr4(2 * HEAD_DIM),
                 row(D_RNN), row(D_RNN), tr(N_KV_HEADS * GATE_ROWS))
    return pl.pallas_call(
        _proj_kernel, grid=(rows // tm,),
        in_specs=[row(D_MODEL), full(nm), full(wq), full(wkv), full(wr), full(wg), tab, tab],
        out_specs=out_specs, out_shape=out_shape, compiler_params=_params("parallel"), name="proj",
    )(x, nm, wq, wkv, wr, wg, cos, sin)


def _expm1_from_exp(x, u):
    near = jnp.where(u == 1.0, x, (u - 1.0) * x / jnp.log(u))
    return jnp.where(x < EXPM1_FLOOR, -1.0, near)


def _softplus(x):
    return jnp.maximum(x, 0.0) + jnp.log1p(jnp.exp(-jnp.abs(x)))


def _lru_coeffs(xc, wra_ref, wri_ref, bra_ref, bri_ref, lam_ref):
    xb = xc.astype(BF16)
    r = jax.nn.sigmoid(_nn(xb, wra_ref[...]) + bra_ref[...])
    i = jax.nn.sigmoid(_nn(xb, wri_ref[...]) + bri_ref[...])
    log_a = -LRU_C * r * _softplus(-lam_ref[...])
    a = jnp.exp(log_a)
    b = jnp.sqrt(-_expm1_from_exp(2.0 * log_a, a * a)) * (i * xc)
    return a, b


def _rnn_out(rg, h, g):
    return _rms(jax.nn.gelu(rg) * h, g).astype(BF16)


def _rglru_prompt_kernel(rx_ref, rg_ref, cw_ref, cb_ref, wra_ref, wri_ref, bra_ref, bri_ref, lam_ref, g_ref,
                         y_ref, hl_ref, prev_sc, h_sc):
    tt = rx_ref.shape[0]

    @pl.when(pl.program_id(1) == 0)
    def _():
        prev_sc[...] = jnp.zeros_like(prev_sc)
        h_sc[...] = jnp.zeros_like(h_sc)

    rx = rx_ref[...]
    ext = jnp.concatenate([prev_sc[...], rx], axis=0)
    shifted = lambda d: pltpu.roll(ext, d, 0)[SUBLANES:SUBLANES + tt]
    cw = cw_ref[...]
    xc = cb_ref[...] + cw[0:1] * shifted(3)
    xc = xc + cw[1:2] * shifted(2)
    xc = xc + cw[2:3] * shifted(1)
    xc = xc + cw[3:4] * rx
    prev_sc[...] = rx[tt - SUBLANES:tt]

    a, b = _lru_coeffs(xc, wra_ref, wri_ref, bra_ref, bri_ref, lam_ref)
    n_tiles, width = tt // SUBLANES, a.shape[1]
    a = a.reshape(n_tiles, SUBLANES, width)
    b = b.reshape(n_tiles, SUBLANES, width)
    sub = lax.broadcasted_iota(jnp.int32, (1, SUBLANES, 1), 1)
    s = 1
    while s < SUBLANES:
        a_sh = jnp.where(sub >= s, pltpu.roll(a, s, 1), 1.0)
        b_sh = jnp.where(sub >= s, pltpu.roll(b, s, 1), 0.0)
        b = a * b_sh + b
        a = a * a_sh
        s *= 2
    state, tiles = h_sc[...], []
    for t in range(n_tiles):
        h_t = a[t] * state + b[t]
        tiles.append(h_t)
        state = h_t[SUBLANES - 1:SUBLANES]
    h = jnp.concatenate(tiles, axis=0)
    h_sc[...] = state
    hl_ref[0] = state
    y_ref[...] = _rnn_out(rg_ref[...], h, g_ref[...])


def _rglru_sample_kernel(rx_ref, rg_ref, cp_ref, h0_ref, cw_ref, cb_ref, wra_ref, wri_ref, bra_ref, bri_ref,
                         lam_ref, g_ref, y_ref, cs_ref, hl_ref):
    nb = h0_ref.shape[0]
    steps = rx_ref.shape[0] // nb
    xp = [cp_ref[k] for k in range(CONV_WIDTH - 1)] + [rx_ref[t * nb:(t + 1) * nb] for t in range(steps)]
    cw, cb = cw_ref[...], cb_ref[...]
    xcs = []
    for t in range(steps):
        xc = cb + cw[0:1] * xp[t]
        for tap in range(1, CONV_WIDTH):
            xc = xc + cw[tap:tap + 1] * xp[t + tap]
        xcs.append(xc)
    a, b = _lru_coeffs(jnp.concatenate(xcs, axis=0), wra_ref, wri_ref, bra_ref, bri_ref, lam_ref)
    h = h0_ref[...]
    for t in range(steps):
        sl = slice(t * nb, (t + 1) * nb)
        h = a[sl] * h + b[sl]
        y_ref[sl] = _rnn_out(rg_ref[sl], h, g_ref[...])
    for k in range(CONV_WIDTH - 1):
        cs_ref[k] = xp[steps + k]
    hl_ref[...] = h


def _rglru_prompt(rx, rg, weights, nb, tt, chunk):
    nc = tt // chunk
    row = pl.BlockSpec((chunk, D_RNN), lambda n, c: (n * nc + c, 0))
    full = lambda a: pl.BlockSpec(a.shape, lambda n, c: (0,) * a.ndim)
    return pl.pallas_call(
        _rglru_prompt_kernel, grid=(nb, nc),
        in_specs=[row, row] + [full(w) for w in weights],
        out_specs=(row, pl.BlockSpec((1, 1, D_RNN), lambda n, c: (n, 0, 0))),
        out_shape=(jax.ShapeDtypeStruct((nb * tt, D_RNN), BF16), jax.ShapeDtypeStruct((nb, 1, D_RNN), F32)),
        scratch_shapes=[pltpu.VMEM((SUBLANES, D_RNN), F32), pltpu.VMEM((1, D_RNN), F32)],
        compiler_params=_params("parallel", "arbitrary"), name="rglru_prompt",
    )(rx, rg, *weights)


def _rglru_sample(rx, rg, conv_prev, h0, weights):
    nb = h0.shape[0]
    return pl.pallas_call(
        _rglru_sample_kernel,
        out_shape=(jax.ShapeDtypeStruct(rx.shape, BF16), jax.ShapeDtypeStruct(conv_prev.shape, F32),
                   jax.ShapeDtypeStruct((nb, D_RNN), F32)),
        compiler_params=pltpu.CompilerParams(vmem_limit_bytes=VMEM_LIMIT), name="rglru_sample",
    )(rx, rg, conv_prev, h0, *weights)


CMP_COLS = 2 * KV_COLS
HALF = CMP_BLOCK // 2
BLOCKS_PER_PAGE = PAGE_SIZE // CMP_STRIDE


def _compress_kernel(pt_ref, *refs, n_pages):
    del pt_ref
    page_refs = refs[:n_pages]
    pet_ref, perm_ref, w1_ref, w2_ref, out_ref, z_sc = refs[n_pages:]
    seqs = out_ref.shape[0]
    for k in range(n_pages):
        xt = page_refs[k][0].reshape(CMP_COLS, PAGE_SIZE)
        for half in range(2):
            a = (xt + pet_ref[half]).astype(BF16)
            z_sc[half, k] = _nt(perm_ref[...], a)
    pre = []
    for half in range(2):
        acc = jnp.zeros((seqs * N_CMP_PAD, CMP_COLS), F32)
        for j in range(CMP_STRIDE):
            rows = z_sc[half, :, j * BLOCKS_PER_PAGE:(j + 1) * BLOCKS_PER_PAGE, :]
            acc = acc + _nn(rows.reshape(seqs * N_CMP_PAD, CMP_COLS).astype(BF16), w1_ref[half, j])
        pre.append(acc)
    hid = jax.nn.gelu(pre[0] + pltpu.roll(pre[1], seqs * N_CMP_PAD - 1, 0))
    out_ref[...] = _nn(hid.astype(BF16), w2_ref[...]).astype(BF16).reshape(seqs, N_CMP_PAD, CMP_COLS)


def _compress(pages, page_spec, page_table, pet, w1, w2, nb, seqs):
    n_pages = page_table.shape[1]
    assert n_pages * BLOCKS_PER_PAGE == N_CMP_PAD
    pos = np.arange(PAGE_SIZE)
    perm = jnp.asarray((pos % CMP_STRIDE * BLOCKS_PER_PAGE + pos // CMP_STRIDE)[None, :]
                       == np.arange(PAGE_SIZE)[:, None], BF16)
    full = lambda a: pl.BlockSpec(a.shape, lambda n, pt: (0,) * a.ndim)
    once = lambda a: pl.BlockSpec(a.shape, lambda n, pt: (0,) * a.ndim, pipeline_mode=pl.Buffered(1))
    page_specs = [page_spec(i, k) for i in range(seqs) for k in range(n_pages)]
    grid_spec = pltpu.PrefetchScalarGridSpec(
        num_scalar_prefetch=1, grid=(nb // seqs,),
        in_specs=page_specs + [full(pet), full(perm), once(w1), full(w2)],
        out_specs=pl.BlockSpec((seqs, N_CMP_PAD, CMP_COLS), lambda n, pt: (n, 0, 0)),
        scratch_shapes=[pltpu.VMEM((2, len(page_specs), PAGE_SIZE, CMP_COLS), F32)])
    return pl.pallas_call(
        functools.partial(_compress_kernel, n_pages=len(page_specs)), grid_spec=grid_spec,
        out_shape=jax.ShapeDtypeStruct((nb, N_CMP_PAD, CMP_COLS), BF16),
        compiler_params=_params("parallel"), name="compress",
    )(page_table, *([pages] * len(page_specs)), pet, perm, w1, w2)


def _cmp_cols(cmp_ref, i, slot, h):
    c0 = slot * KV_COLS + h * HEAD_DIM
    return cmp_ref[i, :, c0:c0 + HEAD_DIM]


def _overlap_matrix(n_sel):
    n_cmp = N_CMP_PAD - 1
    c0 = np.arange(N_CMP_PAD)[:, None] * CMP_STRIDE
    j0 = np.arange(LANES)[None, :] * SEL_BLOCK
    ov = (c0 < j0 + SEL_BLOCK) & (c0 + CMP_BLOCK > j0)
    ov &= (np.arange(N_CMP_PAD)[:, None] < n_cmp) & (np.arange(LANES)[None, :] < n_sel)
    return ov.astype(np.float32)


def _topk_member(score, idx, n, axis):
    rank = jnp.zeros(score.shape, jnp.int32)
    for i in range(n):
        si = score[i:i + 1, :] if axis == 0 else score[:, i:i + 1]
        beats = (si > score) | ((si == score) & (i < idx))
        rank = rank + beats.astype(jnp.int32)
    return rank < TOP_N


def _sel_score(imp, j, cur):
    valid = j <= cur
    forced = (j == 0) | (j == cur) | (j == cur - 1)
    return jnp.where(valid, imp, -SEL_BONUS) + jnp.where(forced, SEL_BONUS, 0.0)


def _nsa_prompt_kernel(qt_ref, qrt_ref, kb_ref, vb_ref, cmp_ref, gt_ref, ovt_ref, e_ref,
                       o_ref, m_sc, acc_sc, *, tq, slc_chunk, col_block, n_sel):
    heads = range(N_KV_HEADS)
    q0 = pl.program_id(1) * tq
    wide = GQA * tq
    qpos = q0 + lax.broadcasted_iota(jnp.int32, (1, tq), 1)
    tile = lambda a, n: jnp.concatenate([a] * n, axis=1)
    group = lambda ref, h: jnp.concatenate(
        [ref[0, (h * GQA + g) * HEAD_DIM:(h * GQA + g + 1) * HEAD_DIM, :] for g in range(GQA)], axis=1)

    cidx = lax.broadcasted_iota(jnp.int32, (N_CMP_PAD, 1), 0)
    real = cidx < N_CMP_PAD - 1
    qpos_all = q0 + lax.broadcasted_iota(jnp.int32, (1, N_HEADS * tq), 1) % tq
    cmask = (cidx * CMP_STRIDE + CMP_BLOCK - 1 <= qpos_all) & real
    s = jnp.concatenate([_nn(_cmp_cols(cmp_ref, 0, 0, h), group(qt_ref, h)) for h in heads], axis=1)
    s = jnp.where(real, jnp.where(cmask, s, NEG), -jnp.inf)
    e = jnp.exp(s - jnp.max(s, axis=0, keepdims=True))
    p = (e / jnp.sum(e, axis=0, keepdims=True)) * cmask.astype(F32)
    pb = p.astype(BF16)
    o_cmp = [lax.dot_general(_cmp_cols(cmp_ref, 0, 1, h), pb[:, h * wide:(h + 1) * wide], TN_DIMS,
                             preferred_element_type=F32) for h in heads]

    imp = []
    for h in heads:
        imp_h = _nn(ovt_ref[...], pb[:, h * wide:h * wide + tq])
        for g in range(1, GQA):
            imp_h = imp_h + _nn(ovt_ref[...], pb[:, h * wide + g * tq:h * wide + (g + 1) * tq])
        imp.append(imp_h)
    j = lax.broadcasted_iota(jnp.int32, (n_sel, 1), 0)
    qpos_2 = q0 + lax.broadcasted_iota(jnp.int32, (1, N_KV_HEADS * tq), 1) % tq
    picked = _topk_member(_sel_score(jnp.concatenate(imp, axis=1), j, qpos_2 // SEL_BLOCK), j, n_sel, axis=0)
    unpicked = jnp.where(picked, 0.0, -SEL_MASK).astype(BF16)
    pad_rows = jnp.zeros((HEAD_DIM - n_sel, wide), BF16)
    qr = [group(qrt_ref, h) for h in heads]
    qsel = [jnp.concatenate([qr[h], tile(unpicked[:, h * tq:(h + 1) * tq], GQA), pad_rows], axis=0)
            for h in heads]

    def attend(h, qx, k, v, allowed, first):
        sc = lax.dot_general(k, qx, TN_DIMS, preferred_element_type=F32)
        bias = jnp.where(allowed, 0.0, NEG)
        pes, ms, alphas = [], [], []
        for c0 in range(0, wide, col_block):
            t0 = c0 % tq
            x = sc[:, c0:c0 + col_block] + bias[:, t0:t0 + col_block]
            m = jnp.max(x, axis=0, keepdims=True)
            if not first:
                m_prev = m_sc[h, :, c0:c0 + col_block]
                m = jnp.maximum(m_prev, m)
                alphas.append(jnp.exp(m_prev - m))
            ms.append(m)
            pes.append(jnp.exp(x - m).astype(BF16))
        pv = _nn(v, jnp.concatenate(pes, axis=1))
        m_sc[h] = jnp.concatenate(ms, axis=1)
        acc_sc[h] = pv if first else jnp.concatenate(alphas, axis=1) * acc_sc[h] + pv

    def normalised(h):
        acc = acc_sc[h]
        return acc[0:HEAD_DIM] / acc[HEAD_DIM:HEAD_DIM + 1]

    wn = WINDOW + tq
    w0 = pl.multiple_of(jnp.maximum(q0 - WINDOW, 0), LANES)
    kpos = w0 + lax.broadcasted_iota(jnp.int32, (wn, 1), 0)
    in_window = (kpos <= qpos) & (kpos > qpos - WINDOW)
    for h in heads:
        attend(h, qr[h], kb_ref[0, N_KV_HEADS + h, :, pl.ds(w0, wn)], vb_ref[0, N_KV_HEADS + h, :, pl.ds(w0, wn)],
               in_window, True)
    o_win = [normalised(h) for h in heads]

    for c0 in range(0, kb_ref.shape[3], slc_chunk):
        def chunk(c0=c0):
            kpos = c0 + lax.broadcasted_iota(jnp.int32, (slc_chunk, 1), 0)
            member = jnp.concatenate([e_ref[:, c0:c0 + slc_chunk],
                                      jnp.zeros((HEAD_DIM - n_sel, slc_chunk), BF16)], axis=0)
            for h in heads:
                k = jnp.concatenate([kb_ref[0, h, :, c0:c0 + slc_chunk], member], axis=0)
                attend(h, qsel[h], k, vb_ref[0, h, :, c0:c0 + slc_chunk], kpos <= qpos, c0 == 0)

        if c0 == 0:
            chunk()
        else:
            pl.when(q0 + tq > c0)(chunk)
    o_slc = [normalised(h) for h in heads]

    gt = gt_ref[0]
    for h in heads:
        for g in range(GQA):
            gate = lambda b: gt[h * GATE_ROWS + b * GQA + g:h * GATE_ROWS + b * GQA + g + 1, :]
            cols = slice(g * tq, (g + 1) * tq)
            o = gate(0) * o_cmp[h][:, cols] + gate(1) * o_slc[h][:, cols]
            o = o + gate(2) * o_win[h][:, cols]
            o_ref[(h * GQA + g) * HEAD_DIM:(h * GQA + g + 1) * HEAD_DIM, :] = o


def _nsa_prompt(qt, qrt, kb, vb, cmp, gt, nb, tt, tq, slc_chunk):
    n_sel = -(-tt // SEL_BLOCK)
    nq = tt // tq
    ovt = jnp.asarray(_overlap_matrix(n_sel)[:, :n_sel].T, BF16)
    expand = jnp.asarray(np.arange(tt)[None, :] // SEL_BLOCK == np.arange(n_sel)[:, None], BF16)
    cols = lambda a: pl.BlockSpec((1, a.shape[1], tq), lambda n, i: (n, 0, i))
    seq = lambda a: pl.BlockSpec((1,) + a.shape[1:], lambda n, i: (n,) + (0,) * (a.ndim - 1))
    full = lambda a: pl.BlockSpec(a.shape, lambda n, i: (0,) * a.ndim)
    return pl.pallas_call(
        functools.partial(_nsa_prompt_kernel, tq=tq, slc_chunk=slc_chunk, col_block=LANES, n_sel=n_sel),
        grid=(nb, nq),
        in_specs=[cols(qt), cols(qrt), seq(kb), seq(vb), seq(cmp), cols(gt), full(ovt), full(expand)],
        out_specs=pl.BlockSpec((ATTN_WIDTH, tq), lambda n, i: (0, n * nq + i)),
        out_shape=jax.ShapeDtypeStruct((ATTN_WIDTH, nb * tt), F32),
        scratch_shapes=[pltpu.VMEM((N_KV_HEADS, 1, GQA * tq), F32),
                        pltpu.VMEM((N_KV_HEADS, 2 * HEAD_DIM, GQA * tq), F32)],
        compiler_params=_params("parallel", "arbitrary"), name="nsa_prompt",
    )(qt, qrt, kb, vb, cmp, gt, ovt, expand)


def _nsa_sample_kernel(pt_ref, *refs, n_pages, steps, seqs):
    del pt_ref
    q_ref, qr_ref = refs[0:2]
    page_refs = refs[2:2 + seqs * n_pages]
    cmp_ref, cw_ref, newk_ref, newv_ref, gates_ref, ov_ref, o_ref, wout_ref = refs[2 + seqs * n_pages:]
    rows = GQA * steps
    past = n_pages * PAGE_SIZE
    wc = cw_ref.shape[-1]
    trow = lax.broadcasted_iota(jnp.int32, (rows, 1), 0) % steps
    lane = lax.broadcasted_iota(jnp.int32, (1, LANES), 1)
    n_sel = -(-(past + steps) // SEL_BLOCK)
    cur = (past + trow) // SEL_BLOCK
    real = lane < N_CMP_PAD - 1
    wlane = lax.broadcasted_iota(jnp.int32, (1, wc), 1)
    in_window = jnp.broadcast_to(wlane + (WINDOW - wc) > trow, (rows, wc))
    per_tile = PAGE_SIZE // SEL_BLOCK

    chains = [(i, h) for i in range(seqs) for h in range(N_KV_HEADS)]
    assert len(chains) * steps == rows, "the stacked top-k below holds one chain per query-group slot"
    s = jnp.concatenate([_nt(q_ref[i, h], _cmp_cols(cmp_ref, i, 0, h)) for i, h in chains], axis=0)
    s = jnp.where(real, s, -jnp.inf)
    e = jnp.exp(s - jnp.max(s, axis=-1, keepdims=True))
    pb_all = (e / jnp.sum(e, axis=-1, keepdims=True)).astype(BF16)
    group = lax.broadcasted_iota(jnp.int32, (rows, 1), 0) // steps
    o_cmps, imp_all = [], None
    for c, (i, h) in enumerate(chains):
        pb = pb_all[c * rows:(c + 1) * rows]
        o_cmps.append(_nn(pb, _cmp_cols(cmp_ref, i, 1, h)))
        part = _nn(pb, ov_ref[...])
        imp = part
        for g in range(1, GQA):
            imp = imp + pltpu.roll(part, g * steps, 0)
        imp_all = imp if c == 0 else jnp.where(group == c, imp, imp_all)
    score = jnp.where(lane < n_sel, _sel_score(imp_all, lane, cur), -jnp.inf)
    sel_all = (_topk_member(score, lane, n_sel, axis=1) & (lane < n_sel)).astype(F32)
    rolled = [sel_all] + [pltpu.roll(sel_all, k * steps, 0) for k in range(1, GQA)]

    for c, (i, h) in enumerate(chains):
        newk, newv = newk_ref[i], newv_ref[i]
        newk_b, newv_b = newk.astype(BF16), newv.astype(BF16)
        qr = qr_ref[i, h]
        pages = page_refs[i * n_pages:(i + 1) * n_pages]

        def new_ok(branch, h=h):
            base = (branch * N_KV_HEADS + h) * steps
            return (lane >= base) & (lane < base + steps) & (lane - base <= trow)

        sel_f = rolled[(0 - c) % GQA]
        for g in range(1, GQA):
            sel_f = jnp.where(group == g, rolled[(g - c) % GQA], sel_f)

        ks = jnp.concatenate([p[0, 0, h].astype(BF16) for p in pages] + [newk_b], axis=1)
        vs = jnp.concatenate([p[0, 1, h].astype(BF16) for p in pages] + [newv_b], axis=1)
        tiles = []
        for k in range(n_pages):
            m = sel_f[:, per_tile * k:per_tile * k + 1]
            for b in range(1, per_tile):
                m = jnp.where(lane < b * SEL_BLOCK, m, sel_f[:, per_tile * k + b:per_tile * k + b + 1])
            tiles.append(m > 0.5)
        tiles.append((sel_f[:, n_sel - 1:n_sel] > 0.5) & new_ok(0))
        msk = jnp.concatenate(tiles, axis=1)
        sc = jnp.where(msk, _nn(qr, ks), NEG)
        pe = jnp.where(msk, jnp.exp(sc - jnp.max(sc, axis=-1, keepdims=True)), 0.0)
        o_slc = _nt(pe.astype(BF16), vs) / jnp.sum(pe, axis=-1, keepdims=True)

        kw = jnp.concatenate([cw_ref[i, 0, h].astype(BF16), newk_b], axis=1)
        vw = jnp.concatenate([cw_ref[i, 1, h].astype(BF16), newv_b], axis=1)
        wmsk = jnp.concatenate([in_window, jnp.broadcast_to(new_ok(1), (rows, LANES))], axis=1)
        sw = jnp.where(wmsk, _nn(qr, kw), NEG)
        pw = jnp.where(wmsk, jnp.exp(sw - jnp.max(sw, axis=-1, keepdims=True)), 0.0)
        o_win = _nt(pw.astype(BF16), vw) / jnp.sum(pw, axis=-1, keepdims=True)

        gt = gates_ref[i, h]
        o_ref[i, h] = gt[:, 0:1] * o_cmps[c] + gt[:, 1:2] * o_slc + gt[:, 2:3] * o_win

        to_end = (LANES - steps - (N_KV_HEADS + h) * steps) % LANES
        for slot, new in ((0, newk), (1, newv)):
            shifted = pltpu.roll(cw_ref[i, slot, h], wc - steps, 1)
            wout_ref[i, slot, h, :, 0:wc - LANES] = shifted[:, 0:wc - LANES]
            wout_ref[i, slot, h, :, wc - LANES:wc] = jnp.where(lane >= LANES - steps, pltpu.roll(new, to_end, 1),
                                                               shifted[:, wc - LANES:wc])


def _nsa_sample(q, qr, pages, page_table, cmp, cache_win_t, new_k, new_v, gates, steps, seqs):
    nb, n_pages = page_table.shape
    n_sel = -(-(n_pages * PAGE_SIZE + steps) // SEL_BLOCK)
    ov = jnp.asarray(_overlap_matrix(n_sel), BF16)
    rows = GQA * steps
    lead = lambda a: pl.BlockSpec((seqs,) + a.shape[1:], lambda n, pt: (n,) + (0,) * (a.ndim - 1))
    page_spec = lambda i, k: pl.BlockSpec((1, 2, N_KV_HEADS, HEAD_DIM, PAGE_SIZE),
                                          lambda n, pt: (pt[n * seqs + i, k], 1, 0, 0, 0))
    page_specs = [page_spec(i, k) for i in range(seqs) for k in range(n_pages)]
    grid_spec = pltpu.PrefetchScalarGridSpec(
        num_scalar_prefetch=1, grid=(nb // seqs,),
        in_specs=[lead(q), lead(qr)] + page_specs
        + [lead(cmp), lead(cache_win_t), lead(new_k), lead(new_v), lead(gates),
           pl.BlockSpec(ov.shape, lambda n, pt: (0, 0))],
        out_specs=(pl.BlockSpec((seqs, N_KV_HEADS, rows, HEAD_DIM), lambda n, pt: (n, 0, 0, 0)), lead(cache_win_t)))
    return pl.pallas_call(
        functools.partial(_nsa_sample_kernel, n_pages=n_pages, steps=steps, seqs=seqs), grid_spec=grid_spec,
        out_shape=(jax.ShapeDtypeStruct((nb, N_KV_HEADS, rows, HEAD_DIM), F32),
                   jax.ShapeDtypeStruct(cache_win_t.shape, F32)),
        compiler_params=_params("parallel"), name="nsa_sample",
    )(page_table, q, qr, *([pages] * len(page_specs)), cmp, cache_win_t, new_k, new_v, gates, ov)


def _mlp_kernel(x_ref, oat_ref, yr_ref, ga_ref, wo_ref, nmlp_ref, wup_ref, wdn_ref, nfin_ref, y_ref, *, ff_chunk):
    oat = oat_ref[...]
    at = (oat * lax.rsqrt(jnp.mean(oat * oat, axis=0, keepdims=True) + EPS) * ga_ref[...]).astype(BF16)
    attn = lax.dot_general(at, wo_ref[0:ATTN_WIDTH, :], TN_DIMS, preferred_element_type=F32)
    x1 = x_ref[...] + (attn + _nn(yr_ref[...], wo_ref[ATTN_WIDTH:, :]))
    v = _rms(x1, nmlp_ref[...]).astype(BF16)
    acc = jnp.zeros_like(x1)
    for c in range(D_FF // ff_chunk):
        sl = slice(c * ff_chunk, (c + 1) * ff_chunk)
        hid = jnp.square(jnp.maximum(_nn(v, wup_ref[:, sl]), 0.0)).astype(BF16)
        acc = acc + _nn(hid, wdn_ref[sl, :])
    y_ref[...] = _rms(x1 + acc, nfin_ref[...])


def _mlp(x, o_attn_t, y_rnn, g_attn, w_out, norm_mlp, w_up, w_down, norm_final, tm, ff_chunk):
    rows = x.shape[0]
    row = lambda w: pl.BlockSpec((tm, w), lambda i: (i, 0))
    full = lambda a: pl.BlockSpec(a.shape, lambda i: (0,) * a.ndim)
    once = lambda a: pl.BlockSpec(a.shape, lambda i: (0,) * a.ndim, pipeline_mode=pl.Buffered(1))
    return pl.pallas_call(
        functools.partial(_mlp_kernel, ff_chunk=ff_chunk), grid=(rows // tm,),
        in_specs=[row(D_MODEL), pl.BlockSpec((ATTN_WIDTH, tm), lambda i: (0, i)), row(D_RNN), full(g_attn),
                  once(w_out), full(norm_mlp),
                  once(w_up), once(w_down), full(norm_final)],
        out_specs=row(D_MODEL), out_shape=jax.ShapeDtypeStruct((rows, D_MODEL), F32),
        compiler_params=_params("parallel"), name="mlp",
    )(x, o_attn_t, y_rnn, g_attn, w_out, norm_mlp, w_up, w_down, norm_final)


def _block_diag(w):
    nb, bs = w.shape[-3], w.shape[-1]
    tiled = jnp.concatenate([jnp.tile(w[..., b, :, :], (1,) * (w.ndim - 2) + (nb,)) for b in range(nb)], axis=-2)
    on_diagonal = np.kron(np.eye(nb, dtype=bool), np.ones((bs, bs), dtype=bool))
    return jnp.where(on_diagonal, tiled, jnp.zeros((), w.dtype))


def kernel(x_prompt, x_sample, cache_kv, cache_win, state_conv, state_rnn, page_table, w_in, pe_ck, w_ck1, w_ck2,
           pe_cv, w_cv1, w_cv2, g_attn, g_rnn, conv_w, conv_b, w_ra, b_ra, w_ri, b_ri, lam, w_out, norm_mix,
           norm_mlp, w_up, w_down, norm_final):
    assert w_in.shape[0] == 1, "single layer"
    nbp, tp, _ = x_prompt.shape
    nbs, steps, _ = x_sample.shape

    wt = jnp.transpose(w_in[0]).astype(BF16)
    c_kv, c_g = ATTN_WIDTH, ATTN_WIDTH + 6 * KV_COLS
    c_rg = c_g + 3 * N_HEADS
    wq, wkv = wt[0:c_kv], wt[c_kv:c_g]
    wg = wt[c_g:c_rg].reshape(3, N_KV_HEADS, GQA, D_MODEL).transpose(1, 0, 2, 3).reshape(N_KV_HEADS, 3 * GQA, D_MODEL)
    wg = jnp.pad(wg, ((0, 0), (0, GATE_ROWS - 3 * GQA), (0, 0))).reshape(N_KV_HEADS * GATE_ROWS, D_MODEL)
    wr = wt[c_rg:c_rg + 2 * D_RNN]
    row2 = lambda a: a.reshape(1, -1)
    rnn_w = (conv_w[0], row2(conv_b[0]), _block_diag(w_ra[0].astype(BF16)), _block_diag(w_ri[0].astype(BF16)),
             row2(b_ra[0]), row2(b_ri[0]), row2(lam[0]), row2(g_rnn[0]))
    per_cs = lambda k, v: jnp.repeat(jnp.stack([k, v]), N_KV_HEADS, axis=0)
    pe_cs = per_cs(pe_ck[0], pe_cv[0]).reshape(2 * N_KV_HEADS, 2, HALF, HEAD_DIM)
    cmp_pet = jnp.tile(pe_cs.transpose(1, 0, 3, 2), (1, 1, 1, BLOCKS_PER_PAGE)).reshape(2, CMP_COLS, PAGE_SIZE)
    w1_cs = per_cs(w_ck1[0], w_cv1[0]).transpose(1, 0, 2, 3)
    cmp_w1 = _block_diag(w1_cs.astype(BF16)).reshape(2, HALF, CMP_COLS, CMP_COLS)
    cmp_w2 = _block_diag(per_cs(w_ck2[0], w_cv2[0]).astype(BF16))
    mlp_w = (g_attn[0].reshape(-1, 1), w_out[0].astype(BF16), row2(norm_mlp[0]), w_up[0].astype(BF16),
             w_down[0].astype(BF16), row2(norm_final))
    nm = row2(norm_mix[0])

    xp = x_prompt.reshape(nbp * tp, D_MODEL)
    qt, qrt, kvt, wint, kb, vb, rg, rx, gt = _proj(xp, jnp.arange(tp), nm, wq, wkv, wr, wg, nbp, tp, ROW_TILE)
    y_rnn, h_last = _rglru_prompt(rx, rg, rnn_w, nbp, tp, RNN_CHUNK)
    own_pages = jnp.zeros((nbp, tp // PAGE_SIZE), jnp.int32)
    own_page = lambda i, k: pl.BlockSpec((1, CMP_COLS, PAGE_SIZE), lambda n, pt: (n * CMP_SEQS + i, 0, k))
    cmp = _compress(kvt, own_page, own_pages, cmp_pet, cmp_w1, cmp_w2, nbp, CMP_SEQS)
    o_attn_t = _nsa_prompt(qt, qrt, kb, vb, cmp, gt, nbp, tp, Q_TILE, SLC_CHUNK)
    y_prompt = _mlp(xp, o_attn_t, y_rnn, *mlp_w, ROW_TILE, FF_CHUNK).reshape(nbp, tp, D_MODEL)
    wlen = min(WINDOW, tp)
    kv_prompt = kvt.reshape(nbp, 4, N_KV_HEADS, HEAD_DIM, tp).transpose(0, 4, 1, 2, 3)[None]
    win_prompt = wint[:, :, tp - wlen:].reshape(nbp, 2, N_KV_HEADS, HEAD_DIM, wlen).transpose(0, 4, 1, 2, 3)[None]
    conv_prompt = rx.reshape(nbp, tp, D_RNN)[:, tp - (CONV_WIDTH - 1):][None]
    h_prompt = h_last.reshape(1, nbp, D_RNN)

    rows_s = nbs * steps
    xs = x_sample.transpose(1, 0, 2).reshape(rows_s, D_MODEL)
    pos_s = PAST_LEN + jnp.arange(rows_s) // nbs
    qt, qrt, kvt, wint, _, _, rg, rx, gt = _proj(xs, pos_s, nm, wq, wkv, wr, wg, 1, rows_s, rows_s)
    y_rnn, conv_s, h_s = _rglru_sample(rx, rg, state_conv[0].transpose(1, 0, 2), state_rnn[0], rnn_w)
    pages_t = cache_kv[0].transpose(0, 2, 3, 4, 1)
    cmp_page = lambda i, k: pl.BlockSpec((1, 2, N_KV_HEADS, HEAD_DIM, PAGE_SIZE),
                                         lambda n, pt: (pt[n * CMP_SEQS + i, k], 0, 0, 0, 0))
    cmp = _compress(pages_t, cmp_page, page_table, cmp_pet, cmp_w1, cmp_w2, nbs, CMP_SEQS)
    by_seq = lambda a: a.reshape(N_KV_HEADS, GQA, HEAD_DIM, steps, nbs).transpose(4, 0, 1, 3, 2).reshape(
        nbs, N_KV_HEADS, GQA * steps, HEAD_DIM)
    def new_tile(slc_rows, win_rows):
        t = jnp.stack([slc_rows, win_rows]).reshape(2 * N_KV_HEADS, HEAD_DIM, steps, nbs)
        t = t.transpose(3, 1, 0, 2).reshape(nbs, HEAD_DIM, 2 * N_KV_HEADS * steps)
        return jnp.pad(t, ((0, 0), (0, 0), (0, LANES - 2 * N_KV_HEADS * steps)))
    new_k = new_tile(kvt[0, 2 * KV_COLS:3 * KV_COLS], wint[0, 0:KV_COLS])
    new_v = new_tile(kvt[0, 3 * KV_COLS:4 * KV_COLS], wint[0, KV_COLS:2 * KV_COLS])
    gates_s = gt[0].reshape(N_KV_HEADS, GATE_ROWS, steps, nbs)[:, :3 * GQA].reshape(N_KV_HEADS, 3, GQA, steps, nbs)
    gates_s = gates_s.transpose(4, 0, 2, 3, 1).reshape(nbs, N_KV_HEADS, GQA * steps, 3)
    cache_win_t = cache_win[0].transpose(0, 2, 3, 4, 1)
    o_s, win_t = _nsa_sample(by_seq(qt[0]), by_seq(qrt[0]), pages_t, page_table, cmp, cache_win_t, new_k, new_v,
                             gates_s, steps, SAMPLE_SEQS)
    o_attn_t = o_s.reshape(nbs, N_KV_HEADS, GQA, steps, HEAD_DIM).transpose(1, 2, 4, 3, 0).reshape(ATTN_WIDTH, rows_s)
    y_sample = _mlp(xs, o_attn_t, y_rnn, *mlp_w, rows_s, FF_CHUNK).reshape(steps, nbs, D_MODEL).transpose(1, 0, 2)
    kv_sample = kvt[0].reshape(4, N_KV_HEADS, HEAD_DIM, steps, nbs).transpose(4, 3, 0, 1, 2)[None]
    win_sample = win_t.transpose(0, 4, 1, 2, 3)[None]
    conv_sample = conv_s.transpose(1, 0, 2)[None]
    h_sample = h_s[None]

    return (y_prompt, y_sample, kv_prompt, kv_sample, win_prompt, win_sample, conv_prompt, conv_sample,
            h_prompt, h_sample)
```

```python
import functools

import jax
import jax.numpy as jnp
import numpy as np
from jax import lax
from jax.experimental import pallas as pl
from jax.experimental.pallas import tpu as pltpu

D_MODEL = 1024
PAST_LEN = 2048
PAGE_SIZE = 128
HEAD_DIM = 64
N_HEADS = 8
N_KV_HEADS = 2
GQA = N_HEADS // N_KV_HEADS
ATTN_WIDTH = N_HEADS * HEAD_DIM
D_RNN = D_MODEL - ATTN_WIDTH
CONV_WIDTH = 4
LRU_C = 8.0
D_FF = 4 * D_MODEL
ROT_DIM = HEAD_DIM // 4
ROT_HALF = ROT_DIM // 2
ROPE_THETA = 500000.0
CMP_BLOCK = 32
CMP_STRIDE = 16
SEL_BLOCK = 64
TOP_N = 16
WINDOW = 512
KV_COLS = N_KV_HEADS * HEAD_DIM
EPS = 1e-6
NEG = -1e30
SEL_BONUS = 1e4
SEL_MASK = 2.0 ** 100
SCALE = HEAD_DIM ** -0.5

SUBLANES = 8
LANES = 128
VMEM_LIMIT = 48 * 1024 * 1024

ROW_TILE = 512
RNN_CHUNK = 512
Q_TILE = 256
SLC_CHUNK = 512
FF_CHUNK = 1024
CMP_SEQS = 4
SAMPLE_SEQS = 2
EXPM1_FLOOR = -30.0

N_CMP_PAD = 128
GATE_ROWS = 16

F32 = jnp.float32
BF16 = jnp.bfloat16
NT_DIMS = (((1,), (1,)), ((), ()))
TN_DIMS = (((0,), (0,)), ((), ()))


def _nt(a, b):
    return lax.dot_general(a, b, NT_DIMS, preferred_element_type=F32)


def _nn(a, b):
    return jnp.dot(a, b, preferred_element_type=F32)


def _rms(x, g):
    return x * lax.rsqrt(jnp.mean(x * x, axis=-1, keepdims=True) + EPS) * g


def _params(*sem):
    return pltpu.CompilerParams(dimension_semantics=sem, vmem_limit_bytes=VMEM_LIMIT)


def _proj_kernel(x_ref, nm_ref, wq_ref, wkv_ref, wr_ref, wg_ref, cos_ref, sin_ref,
                 qt_ref, qrt_ref, kvt_ref, wint_ref, kb_ref, vb_ref, rg_ref, rx_ref, gt_ref):
    u = _rms(x_ref[...], nm_ref[...]).astype(BF16)
    cos, sin = cos_ref[...], sin_ref[...]

    def rotated(a, r0):
        x1, x2 = a[r0:r0 + ROT_HALF], a[r0 + ROT_HALF:r0 + ROT_DIM]
        return x1 * cos - x2 * sin, x2 * cos + x1 * sin

    qt = _nt(wq_ref[...], u)
    qt_ref[0] = (qt * SCALE).astype(BF16)
    parts = []
    for head in range(N_HEADS):
        r0 = head * HEAD_DIM
        parts += [*rotated(qt, r0), qt[r0 + ROT_DIM:r0 + HEAD_DIM]]
    qrt_ref[0] = (jnp.concatenate(parts, axis=0) * SCALE).astype(BF16)

    kvt = _nt(wkv_ref[...], u)
    kvt_ref[0] = kvt[0:4 * KV_COLS]
    wint_ref[0] = kvt[4 * KV_COLS:6 * KV_COLS]
    for out_ref, src0, dst0 in ((kvt_ref, 2 * KV_COLS, 2 * KV_COLS), (wint_ref, 4 * KV_COLS, 0)):
        for h in range(N_KV_HEADS):
            d = dst0 + h * HEAD_DIM
            out_ref[0, d:d + ROT_HALF], out_ref[0, d + ROT_HALF:d + ROT_DIM] = rotated(kvt, src0 + h * HEAD_DIM)
    tm = x_ref.shape[0]
    ones_row = (lax.broadcasted_iota(jnp.int32, (HEAD_DIM, tm), 0) == 0).astype(BF16)
    for branch, src_ref, k0, v0 in ((0, kvt_ref, 2 * KV_COLS, 3 * KV_COLS), (1, wint_ref, 0, KV_COLS)):
        for h in range(N_KV_HEADS):
            i, r = branch * N_KV_HEADS + h, h * HEAD_DIM
            kb_ref[0, i] = src_ref[0, k0 + r:k0 + r + HEAD_DIM].astype(BF16)
            vb_ref[0, i, 0:HEAD_DIM] = src_ref[0, v0 + r:v0 + r + HEAD_DIM].astype(BF16)
            vb_ref[0, i, HEAD_DIM:2 * HEAD_DIM] = ones_row

    r = _nt(u, wr_ref[...])
    rg_ref[...] = r[:, 0:D_RNN]
    rx_ref[...] = r[:, D_RNN:2 * D_RNN]
    gt_ref[0] = jax.nn.sigmoid(_nt(wg_ref[...], u))


def _rope_tables(pos):
    inv = ROPE_THETA ** (-jnp.arange(ROT_HALF, dtype=F32) / ROT_HALF)
    ang = pos.astype(F32)[:, None] * inv
    return jnp.cos(ang).T, jnp.sin(ang).T


def _proj(x, pos, nm, wq, wkv, wr, wg, nb, tt, tm):
    rows = nb * tt
    nt = tt // tm
    cos, sin = _rope_tables(pos)
    row = lambda w: pl.BlockSpec((tm, w), lambda i: (i, 0))
    full = lambda a: pl.BlockSpec(a.shape, lambda i: (0,) * a.ndim)
    tab = pl.BlockSpec((ROT_HALF, tm), lambda i: (0, i % nt))
    tr = lambda r: pl.BlockSpec((1, r, tm), lambda i: (i // nt, 0, i % nt))
    out_shape = (
        jax.ShapeDtypeStruct((nb, ATTN_WIDTH, tt), BF16),
        jax.ShapeDtypeStruct((nb, ATTN_WIDTH, tt), BF16),
        jax.ShapeDtypeStruct((nb, 4 * KV_COLS, tt), F32),
        jax.ShapeDtypeStruct((nb, 2 * KV_COLS, tt), F32),
        jax.ShapeDtypeStruct((nb, 2 * N_KV_HEADS, HEAD_DIM, tt), BF16),
        jax.ShapeDtypeStruct((nb, 2 * N_KV_HEADS, 2 * HEAD_DIM, tt), BF16),
        jax.ShapeDtypeStruct((rows, D_RNN), F32),
        jax.ShapeDtypeStruct((rows, D_RNN), F32),
        jax.ShapeDtypeStruct((nb, N_KV_HEADS * GATE_ROWS, tt), F32),
    )
    tr4 = lambda r: pl.BlockSpec((1, 2 * N_KV_HEADS, r, tm), lambda i: (i // nt, 0, 0, i % nt))
    out_specs = (tr(ATTN_WIDTH), tr(ATTN_WIDTH), tr(4 * KV_COLS), tr(2 * KV_COLS), tr4(HEAD_DIM), tr4(2 * HEAD_DIM),
                 row(D_RNN), row(D_RNN), tr(N_KV_HEADS * GATE_ROWS))
    return pl.pallas_call(
        _proj_kernel, grid=(rows // tm,),
        in_specs=[row(D_MODEL), full(nm), full(wq), full(wkv), full(wr), full(wg), tab, tab],
        out_specs=out_specs, out_shape=out_shape, compiler_params=_params("parallel"), name="proj",
    )(x, nm, wq, wkv, wr, wg, cos, sin)


def _expm1_from_exp(x, u):
    near = jnp.where(u == 1.0, x, (u - 1.0) * x / jnp.log(u))
    return jnp.where(x < EXPM1_FLOOR, -1.0, near)


def _softplus(x):
    return jnp.maximum(x, 0.0) + jnp.log1p(jnp.exp(-jnp.abs(x)))


def _lru_coeffs(xc, wra_ref, wri_ref, bra_ref, bri_ref, lam_ref):
    xb = xc.astype(BF16)
    r = jax.nn.sigmoid(_nn(xb, wra_ref[...]) + bra_ref[...])
    i = jax.nn.sigmoid(_nn(xb, wri_ref[...]) + bri_ref[...])
    log_a = -LRU_C * r * _softplus(-lam_ref[...])
    a = jnp.exp(log_a)
    b = jnp.sqrt(-_expm1_from_exp(2.0 * log_a, a * a)) * (i * xc)
    return a, b


def _rnn_out(rg, h, g):
    return _rms(jax.nn.gelu(rg) * h, g).astype(BF16)


def _rglru_prompt_kernel(rx_ref, rg_ref, cw_ref, cb_ref, wra_ref, wri_ref, bra_ref, bri_ref, lam_ref, g_ref,
                         y_ref, hl_ref, prev_sc, h_sc):
    tt = rx_ref.shape[0]

    @pl.when(pl.program_id(1) == 0)
    def _():
        prev_sc[...] = jnp.zeros_like(prev_sc)
        h_sc[...] = jnp.zeros_like(h_sc)

    rx = rx_ref[...]
    ext = jnp.concatenate([prev_sc[...], rx], axis=0)
    shifted = lambda d: pltpu.roll(ext, d, 0)[SUBLANES:SUBLANES + tt]
    cw = cw_ref[...]
    xc = cb_ref[...] + cw[0:1] * shifted(3)
    xc = xc + cw[1:2] * shifted(2)
    xc = xc + cw[2:3] * shifted(1)
    xc = xc + cw[3:4] * rx
    prev_sc[...] = rx[tt - SUBLANES:tt]

    a, b = _lru_coeffs(xc, wra_ref, wri_ref, bra_ref, bri_ref, lam_ref)
    n_tiles, width = tt // SUBLANES, a.shape[1]
    a = a.reshape(n_tiles, SUBLANES, width)
    b = b.reshape(n_tiles, SUBLANES, width)
    sub = lax.broadcasted_iota(jnp.int32, (1, SUBLANES, 1), 1)
    s = 1
    while s < SUBLANES:
        a_sh = jnp.where(sub >= s, pltpu.roll(a, s, 1), 1.0)
        b_sh = jnp.where(sub >= s, pltpu.roll(b, s, 1), 0.0)
        b = a * b_sh + b
        a = a * a_sh
        s *= 2
    state, tiles = h_sc[...], []
    for t in range(n_tiles):
        h_t = a[t] * state + b[t]
        tiles.append(h_t)
        state = h_t[SUBLANES - 1:SUBLANES]
    h = jnp.concatenate(tiles, axis=0)
    h_sc[...] = state
    hl_ref[0] = state
    y_ref[...] = _rnn_out(rg_ref[...], h, g_ref[...])


def _rglru_sample_kernel(rx_ref, rg_ref, cp_ref, h0_ref, cw_ref, cb_ref, wra_ref, wri_ref, bra_ref, bri_ref,
                         lam_ref, g_ref, y_ref, cs_ref, hl_ref):
    nb = h0_ref.shape[0]
    steps = rx_ref.shape[0] // nb
    xp = [cp_ref[k] for k in range(CONV_WIDTH - 1)] + [rx_ref[t * nb:(t + 1) * nb] for t in range(steps)]
    cw, cb = cw_ref[...], cb_ref[...]
    xcs = []
    for t in range(steps):
        xc = cb + cw[0:1] * xp[t]
        for tap in range(1, CONV_WIDTH):
            xc = xc + cw[tap:tap + 1] * xp[t + tap]
        xcs.append(xc)
    a, b = _lru_coeffs(jnp.concatenate(xcs, axis=0), wra_ref, wri_ref, bra_ref, bri_ref, lam_ref)
    h = h0_ref[...]
    for t in range(steps):
        sl = slice(t * nb, (t + 1) * nb)
        h = a[sl] * h + b[sl]
        y_ref[sl] = _rnn_out(rg_ref[sl], h, g_ref[...])
    for k in range(CONV_WIDTH - 1):
        cs_ref[k] = xp[steps + k]
    hl_ref[...] = h


def _rglru_prompt(rx, rg, weights, nb, tt, chunk):
    nc = tt // chunk
    row = pl.BlockSpec((chunk, D_RNN), lambda n, c: (n * nc + c, 0))
    full = lambda a: pl.BlockSpec(a.shape, lambda n, c: (0,) * a.ndim)
    return pl.pallas_call(
        _rglru_prompt_kernel, grid=(nb, nc),
        in_specs=[row, row] + [full(w) for w in weights],
        out_specs=(row, pl.BlockSpec((1, 1, D_RNN), lambda n, c: (n, 0, 0))),
        out_shape=(jax.ShapeDtypeStruct((nb * tt, D_RNN), BF16), jax.ShapeDtypeStruct((nb, 1, D_RNN), F32)),
        scratch_shapes=[pltpu.VMEM((SUBLANES, D_RNN), F32), pltpu.VMEM((1, D_RNN), F32)],
        compiler_params=_params("parallel", "arbitrary"), name="rglru_prompt",
    )(rx, rg, *weights)


def _rglru_sample(rx, rg, conv_prev, h0, weights):
    nb = h0.shape[0]
    return pl.pallas_call(
        _rglru_sample_kernel,
        out_shape=(jax.ShapeDtypeStruct(rx.shape, BF16), jax.ShapeDtypeStruct(conv_prev.shape, F32),
                   jax.ShapeDtypeStruct((nb, D_RNN), F32)),
        compiler_params=pltpu.CompilerParams(vmem_limit_bytes=VMEM_LIMIT), name="rglru_sample",
    )(rx, rg, conv_prev, h0, *weights)


CMP_COLS = 2 * KV_COLS
HALF = CMP_BLOCK // 2
BLOCKS_PER_PAGE = PAGE_SIZE // CMP_STRIDE


def _compress_kernel(pt_ref, *refs, n_pages):
    del pt_ref
    page_refs = refs[:n_pages]
    pet_ref, perm_ref, w1_ref, w2_ref, out_ref, z_sc = refs[n_pages:]
    seqs = out_ref.shape[0]
    for k in range(n_pages):
        xt = page_refs[k][0].reshape(CMP_COLS, PAGE_SIZE)
        for half in range(2):
            a = (xt + pet_ref[half]).astype(BF16)
            z_sc[half, k] = _nt(perm_ref[...], a)
    pre = []
    for half in range(2):
        acc = jnp.zeros((seqs * N_CMP_PAD, CMP_COLS), F32)
        for j in range(CMP_STRIDE):
            rows = z_sc[half, :, j * BLOCKS_PER_PAGE:(j + 1) * BLOCKS_PER_PAGE, :]
            acc = acc + _nn(rows.reshape(seqs * N_CMP_PAD, CMP_COLS).astype(BF16), w1_ref[half, j])
        pre.append(acc)
    hid = jax.nn.gelu(pre[0] + pltpu.roll(pre[1], seqs * N_CMP_PAD - 1, 0))
    out_ref[...] = _nn(hid.astype(BF16), w2_ref[...]).astype(BF16).reshape(seqs, N_CMP_PAD, CMP_COLS)


def _compress(pages, page_spec, page_table, pet, w1, w2, nb, seqs):
    n_pages = page_table.shape[1]
    assert n_pages * BLOCKS_PER_PAGE == N_CMP_PAD
    pos = np.arange(PAGE_SIZE)
    perm = jnp.asarray((pos % CMP_STRIDE * BLOCKS_PER_PAGE + pos // CMP_STRIDE)[None, :]
                       == np.arange(PAGE_SIZE)[:, None], BF16)
    full = lambda a: pl.BlockSpec(a.shape, lambda n, pt: (0,) * a.ndim)
    once = lambda a: pl.BlockSpec(a.shape, lambda n, pt: (0,) * a.ndim, pipeline_mode=pl.Buffered(1))
    page_specs = [page_spec(i, k) for i in range(seqs) for k in range(n_pages)]
    grid_spec = pltpu.PrefetchScalarGridSpec(
        num_scalar_prefetch=1, grid=(nb // seqs,),
        in_specs=page_specs + [full(pet), full(perm), once(w1), full(w2)],
        out_specs=pl.BlockSpec((seqs, N_CMP_PAD, CMP_COLS), lambda n, pt: (n, 0, 0)),
        scratch_shapes=[pltpu.VMEM((2, len(page_specs), PAGE_SIZE, CMP_COLS), F32)])
    return pl.pallas_call(
        functools.partial(_compress_kernel, n_pages=len(page_specs)), grid_spec=grid_spec,
        out_shape=jax.ShapeDtypeStruct((nb, N_CMP_PAD, CMP_COLS), BF16),
        compiler_params=_params("parallel"), name="compress",
    )(page_table, *([pages] * len(page_specs)), pet, perm, w1, w2)


def _cmp_cols(cmp_ref, i, slot, h):
    c0 = slot * KV_COLS + h * HEAD_DIM
    return cmp_ref[i, :, c0:c0 + HEAD_DIM]


def _overlap_matrix(n_sel):
    n_cmp = N_CMP_PAD - 1
    c0 = np.arange(N_CMP_PAD)[:, None] * CMP_STRIDE
    j0 = np.arange(LANES)[None, :] * SEL_BLOCK
    ov = (c0 < j0 + SEL_BLOCK) & (c0 + CMP_BLOCK > j0)
    ov &= (np.arange(N_CMP_PAD)[:, None] < n_cmp) & (np.arange(LANES)[None, :] < n_sel)
    return ov.astype(np.float32)


def _topk_member(score, idx, n, axis):
    rank = jnp.zeros(score.shape, jnp.int32)
    for i in range(n):
        si = score[i:i + 1, :] if axis == 0 else score[:, i:i + 1]
        beats = (si > score) | ((si == score) & (i < idx))
        rank = rank + beats.astype(jnp.int32)
    return rank < TOP_N


def _sel_score(imp, j, cur):
    valid = j <= cur
    forced = (j == 0) | (j == cur) | (j == cur - 1)
    return jnp.where(valid, imp, -SEL_BONUS) + jnp.where(forced, SEL_BONUS, 0.0)


def _nsa_prompt_kernel(qt_ref, qrt_ref, kb_ref, vb_ref, cmp_ref, gt_ref, ovt_ref, e_ref,
                       o_ref, m_sc, acc_sc, *, tq, slc_chunk, col_block, n_sel):
    heads = range(N_KV_HEADS)
    q0 = pl.program_id(1) * tq
    wide = GQA * tq
    qpos = q0 + lax.broadcasted_iota(jnp.int32, (1, tq), 1)
    tile = lambda a, n: jnp.concatenate([a] * n, axis=1)
    group = lambda ref, h: jnp.concatenate(
        [ref[0, (h * GQA + g) * HEAD_DIM:(h * GQA + g + 1) * HEAD_DIM, :] for g in range(GQA)], axis=1)

    cidx = lax.broadcasted_iota(jnp.int32, (N_CMP_PAD, 1), 0)
    real = cidx < N_CMP_PAD - 1
    qpos_all = q0 + lax.broadcasted_iota(jnp.int32, (1, N_HEADS * tq), 1) % tq
    cmask = (cidx * CMP_STRIDE + CMP_BLOCK - 1 <= qpos_all) & real
    s = jnp.concatenate([_nn(_cmp_cols(cmp_ref, 0, 0, h), group(qt_ref, h)) for h in heads], axis=1)
    s = jnp.where(real, jnp.where(cmask, s, NEG), -jnp.inf)
    e = jnp.exp(s - jnp.max(s, axis=0, keepdims=True))
    p = (e / jnp.sum(e, axis=0, keepdims=True)) * cmask.astype(F32)
    pb = p.astype(BF16)
    o_cmp = [lax.dot_general(_cmp_cols(cmp_ref, 0, 1, h), pb[:, h * wide:(h + 1) * wide], TN_DIMS,
                             preferred_element_type=F32) for h in heads]

    imp = []
    for h in heads:
        imp_h = _nn(ovt_ref[...], pb[:, h * wide:h * wide + tq])
        for g in range(1, GQA):
            imp_h = imp_h + _nn(ovt_ref[...], pb[:, h * wide + g * tq:h * wide + (g + 1) * tq])
        imp.append(imp_h)
    j = lax.broadcasted_iota(jnp.int32, (n_sel, 1), 0)
    qpos_2 = q0 + lax.broadcasted_iota(jnp.int32, (1, N_KV_HEADS * tq), 1) % tq
    picked = _topk_member(_sel_score(jnp.concatenate(imp, axis=1), j, qpos_2 // SEL_BLOCK), j, n_sel, axis=0)
    unpicked = jnp.where(picked, 0.0, -SEL_MASK).astype(BF16)
    pad_rows = jnp.zeros((HEAD_DIM - n_sel, wide), BF16)
    qr = [group(qrt_ref, h) for h in heads]
    qsel = [jnp.concatenate([qr[h], tile(unpicked[:, h * tq:(h + 1) * tq], GQA), pad_rows], axis=0)
            for h in heads]

    def attend(h, qx, k, v, allowed, first):
        bias = jnp.where(allowed, 0.0, NEG)
        pes, ms, alphas = [], [], []
        for c0 in range(0, wide, col_block):
            t0 = c0 % tq
            sc = lax.dot_general(k, qx[:, c0:c0 + col_block], TN_DIMS, preferred_element_type=F32)
            x = sc + bias[:, t0:t0 + col_block]
            m = jnp.max(x, axis=0, keepdims=True)
            if not first:
                m_prev = m_sc[h, :, c0:c0 + col_block]
                m = jnp.maximum(m_prev, m)
                alphas.append(jnp.exp(m_prev - m))
            ms.append(m)
            pes.append(jnp.exp(x - m).astype(BF16))
        pv = _nn(v, jnp.concatenate(pes, axis=1))
        m_sc[h] = jnp.concatenate(ms, axis=1)
        acc_sc[h] = pv if first else jnp.concatenate(alphas, axis=1) * acc_sc[h] + pv

    def normalised(h):
        acc = acc_sc[h]
        return acc[0:HEAD_DIM] / acc[HEAD_DIM:HEAD_DIM + 1]

    wn = WINDOW + tq
    w0 = pl.multiple_of(jnp.maximum(q0 - WINDOW, 0), LANES)
    kpos = w0 + lax.broadcasted_iota(jnp.int32, (wn, 1), 0)
    in_window = (kpos <= qpos) & (kpos > qpos - WINDOW)
    for h in heads:
        attend(h, qr[h], kb_ref[0, N_KV_HEADS + h, :, pl.ds(w0, wn)], vb_ref[0, N_KV_HEADS + h, :, pl.ds(w0, wn)],
               in_window, True)
    o_win = [normalised(h) for h in heads]

    for c0 in range(0, kb_ref.shape[3], slc_chunk):
        def chunk(c0=c0):
            kpos = c0 + lax.broadcasted_iota(jnp.int32, (slc_chunk, 1), 0)
            member = jnp.concatenate([e_ref[:, c0:c0 + slc_chunk],
                                      jnp.zeros((HEAD_DIM - n_sel, slc_chunk), BF16)], axis=0)
            for h in heads:
                k = jnp.concatenate([kb_ref[0, h, :, c0:c0 + slc_chunk], member], axis=0)
                attend(h, qsel[h], k, vb_ref[0, h, :, c0:c0 + slc_chunk], kpos <= qpos, c0 == 0)

        if c0 == 0:
            chunk()
        else:
            pl.when(q0 + tq > c0)(chunk)
    o_slc = [normalised(h) for h in heads]

    gt = gt_ref[0]
    for h in heads:
        for g in range(GQA):
            gate = lambda b: gt[h * GATE_ROWS + b * GQA + g:h * GATE_ROWS + b * GQA + g + 1, :]
            cols = slice(g * tq, (g + 1) * tq)
            o = gate(0) * o_cmp[h][:, cols] + gate(1) * o_slc[h][:, cols]
            o = o + gate(2) * o_win[h][:, cols]
            o_ref[(h * GQA + g) * HEAD_DIM:(h * GQA + g + 1) * HEAD_DIM, :] = o


def _nsa_prompt(qt, qrt, kb, vb, cmp, gt, nb, tt, tq, slc_chunk):
    n_sel = -(-tt // SEL_BLOCK)
    nq = tt // tq
    ovt = jnp.asarray(_overlap_matrix(n_sel)[:, :n_sel].T, BF16)
    expand = jnp.asarray(np.arange(tt)[None, :] // SEL_BLOCK == np.arange(n_sel)[:, None], BF16)
    cols = lambda a: pl.BlockSpec((1, a.shape[1], tq), lambda n, i: (n, 0, i))
    seq = lambda a: pl.BlockSpec((1,) + a.shape[1:], lambda n, i: (n,) + (0,) * (a.ndim - 1))
    full = lambda a: pl.BlockSpec(a.shape, lambda n, i: (0,) * a.ndim)
    return pl.pallas_call(
        functools.partial(_nsa_prompt_kernel, tq=tq, slc_chunk=slc_chunk, col_block=LANES, n_sel=n_sel),
        grid=(nb, nq),
        in_specs=[cols(qt), cols(qrt), seq(kb), seq(vb), seq(cmp), cols(gt), full(ovt), full(expand)],
        out_specs=pl.BlockSpec((ATTN_WIDTH, tq), lambda n, i: (0, n * nq + i)),
        out_shape=jax.ShapeDtypeStruct((ATTN_WIDTH, nb * tt), F32),
        scratch_shapes=[pltpu.VMEM((N_KV_HEADS, 1, GQA * tq), F32),
                        pltpu.VMEM((N_KV_HEADS, 2 * HEAD_DIM, GQA * tq), F32)],
        compiler_params=_params("parallel", "arbitrary"), name="nsa_prompt",
    )(qt, qrt, kb, vb, cmp, gt, ovt, expand)


def _nsa_sample_kernel(pt_ref, *refs, n_pages, steps, seqs):
    del pt_ref
    q_ref, qr_ref = refs[0:2]
    page_refs = refs[2:2 + seqs * n_pages]
    cmp_ref, cw_ref, newk_ref, newv_ref, gates_ref, ov_ref, o_ref, wout_ref = refs[2 + seqs * n_pages:]
    rows = GQA * steps
    past = n_pages * PAGE_SIZE
    wc = cw_ref.shape[-1]
    trow = lax.broadcasted_iota(jnp.int32, (rows, 1), 0) % steps
    lane = lax.broadcasted_iota(jnp.int32, (1, LANES), 1)
    n_sel = -(-(past + steps) // SEL_BLOCK)
    cur = (past + trow) // SEL_BLOCK
    real = lane < N_CMP_PAD - 1
    wlane = lax.broadcasted_iota(jnp.int32, (1, wc), 1)
    in_window = jnp.broadcast_to(wlane + (WINDOW - wc) > trow, (rows, wc))
    per_tile = PAGE_SIZE // SEL_BLOCK

    chains = [(i, h) for i in range(seqs) for h in range(N_KV_HEADS)]
    assert len(chains) * steps == rows, "the stacked top-k below holds one chain per query-group slot"
    s = jnp.concatenate([_nt(q_ref[i, h], _cmp_cols(cmp_ref, i, 0, h)) for i, h in chains], axis=0)
    s = jnp.where(real, s, -jnp.inf)
    e = jnp.exp(s - jnp.max(s, axis=-1, keepdims=True))
    pb_all = (e / jnp.sum(e, axis=-1, keepdims=True)).astype(BF16)
    group = lax.broadcasted_iota(jnp.int32, (rows, 1), 0) // steps
    o_cmps, imp_all = [], None
    for c, (i, h) in enumerate(chains):
        pb = pb_all[c * rows:(c + 1) * rows]
        o_cmps.append(_nn(pb, _cmp_cols(cmp_ref, i, 1, h)))
        part = _nn(pb, ov_ref[...])
        imp = part
        for g in range(1, GQA):
            imp = imp + pltpu.roll(part, g * steps, 0)
        imp_all = imp if c == 0 else jnp.where(group == c, imp, imp_all)
    score = jnp.where(lane < n_sel, _sel_score(imp_all, lane, cur), -jnp.inf)
    sel_all = (_topk_member(score, lane, n_sel, axis=1) & (lane < n_sel)).astype(F32)
    rolled = [sel_all] + [pltpu.roll(sel_all, k * steps, 0) for k in range(1, GQA)]

    for c, (i, h) in enumerate(chains):
        newk, newv = newk_ref[i], newv_ref[i]
        newk_b, newv_b = newk.astype(BF16), newv.astype(BF16)
        qr = qr_ref[i, h]
        pages = page_refs[i * n_pages:(i + 1) * n_pages]

        def new_ok(branch, h=h):
            base = (branch * N_KV_HEADS + h) * steps
            return (lane >= base) & (lane < base + steps) & (lane - base <= trow)

        sel_f = rolled[(0 - c) % GQA]
        for g in range(1, GQA):
            sel_f = jnp.where(group == g, rolled[(g - c) % GQA], sel_f)

        ks = jnp.concatenate([p[0, 0, h].astype(BF16) for p in pages] + [newk_b], axis=1)
        vs = jnp.concatenate([p[0, 1, h].astype(BF16) for p in pages] + [newv_b], axis=1)
        tiles = []
        for k in range(n_pages):
            m = sel_f[:, per_tile * k:per_tile * k + 1]
            for b in range(1, per_tile):
                m = jnp.where(lane < b * SEL_BLOCK, m, sel_f[:, per_tile * k + b:per_tile * k + b + 1])
            tiles.append(m > 0.5)
        tiles.append((sel_f[:, n_sel - 1:n_sel] > 0.5) & new_ok(0))
        msk = jnp.concatenate(tiles, axis=1)
        sc = jnp.where(msk, _nn(qr, ks), NEG)
        pe = jnp.where(msk, jnp.exp(sc - jnp.max(sc, axis=-1, keepdims=True)), 0.0)
        o_slc = _nt(pe.astype(BF16), vs) / jnp.sum(pe, axis=-1, keepdims=True)

        kw = jnp.concatenate([cw_ref[i, 0, h].astype(BF16), newk_b], axis=1)
        vw = jnp.concatenate([cw_ref[i, 1, h].astype(BF16), newv_b], axis=1)
        wmsk = jnp.concatenate([in_window, jnp.broadcast_to(new_ok(1), (rows, LANES))], axis=1)
        sw = jnp.where(wmsk, _nn(qr, kw), NEG)
        pw = jnp.where(wmsk, jnp.exp(sw - jnp.max(sw, axis=-1, keepdims=True)), 0.0)
        o_win = _nt(pw.astype(BF16), vw) / jnp.sum(pw, axis=-1, keepdims=True)

        gt = gates_ref[i, h]
        o_ref[i, h] = gt[:, 0:1] * o_cmps[c] + gt[:, 1:2] * o_slc + gt[:, 2:3] * o_win

        to_end = (LANES - steps - (N_KV_HEADS + h) * steps) % LANES
        for slot, new in ((0, newk), (1, newv)):
            shifted = pltpu.roll(cw_ref[i, slot, h], wc - steps, 1)
            wout_ref[i, slot, h, :, 0:wc - LANES] = shifted[:, 0:wc - LANES]
            wout_ref[i, slot, h, :, wc - LANES:wc] = jnp.where(lane >= LANES - steps, pltpu.roll(new, to_end, 1),
                                                               shifted[:, wc - LANES:wc])


def _nsa_sample(q, qr, pages, page_table, cmp, cache_win_t, new_k, new_v, gates, steps, seqs):
    nb, n_pages = page_table.shape
    n_sel = -(-(n_pages * PAGE_SIZE + steps) // SEL_BLOCK)
    ov = jnp.asarray(_overlap_matrix(n_sel), BF16)
    rows = GQA * steps
    lead = lambda a: pl.BlockSpec((seqs,) + a.shape[1:], lambda n, pt: (n,) + (0,) * (a.ndim - 1))
    page_spec = lambda i, k: pl.BlockSpec((1, 2, N_KV_HEADS, HEAD_DIM, PAGE_SIZE),
                                          lambda n, pt: (pt[n * seqs + i, k], 1, 0, 0, 0))
    page_specs = [page_spec(i, k) for i in range(seqs) for k in range(n_pages)]
    grid_spec = pltpu.PrefetchScalarGridSpec(
        num_scalar_prefetch=1, grid=(nb // seqs,),
        in_specs=[lead(q), lead(qr)] + page_specs
        + [lead(cmp), lead(cache_win_t), lead(new_k), lead(new_v), lead(gates),
           pl.BlockSpec(ov.shape, lambda n, pt: (0, 0))],
        out_specs=(pl.BlockSpec((seqs, N_KV_HEADS, rows, HEAD_DIM), lambda n, pt: (n, 0, 0, 0)), lead(cache_win_t)))
    return pl.pallas_call(
        functools.partial(_nsa_sample_kernel, n_pages=n_pages, steps=steps, seqs=seqs), grid_spec=grid_spec,
        out_shape=(jax.ShapeDtypeStruct((nb, N_KV_HEADS, rows, HEAD_DIM), F32),
                   jax.ShapeDtypeStruct(cache_win_t.shape, F32)),
        compiler_params=_params("parallel"), name="nsa_sample",
    )(page_table, q, qr, *([pages] * len(page_specs)), cmp, cache_win_t, new_k, new_v, gates, ov)


def _mlp_kernel(x_ref, oat_ref, yr_ref, ga_ref, wo_ref, nmlp_ref, wup_ref, wdn_ref, nfin_ref, y_ref, *, ff_chunk):
    oat = oat_ref[...]
    at = (oat * lax.rsqrt(jnp.mean(oat * oat, axis=0, keepdims=True) + EPS) * ga_ref[...]).astype(BF16)
    attn = lax.dot_general(at, wo_ref[0:ATTN_WIDTH, :], TN_DIMS, preferred_element_type=F32)
    x1 = x_ref[...] + (attn + _nn(yr_ref[...], wo_ref[ATTN_WIDTH:, :]))
    v = _rms(x1, nmlp_ref[...]).astype(BF16)
    acc = jnp.zeros_like(x1)
    for c in range(D_FF // ff_chunk):
        sl = slice(c * ff_chunk, (c + 1) * ff_chunk)
        hid = jnp.square(jnp.maximum(_nn(v, wup_ref[:, sl]), 0.0)).astype(BF16)
        acc = acc + _nn(hid, wdn_ref[sl, :])
    y_ref[...] = _rms(x1 + acc, nfin_ref[...])


def _mlp(x, o_attn_t, y_rnn, g_attn, w_out, norm_mlp, w_up, w_down, norm_final, tm, ff_chunk):
    rows = x.shape[0]
    row = lambda w: pl.BlockSpec((tm, w), lambda i: (i, 0))
    full = lambda a: pl.BlockSpec(a.shape, lambda i: (0,) * a.ndim)
    once = lambda a: pl.BlockSpec(a.shape, lambda i: (0,) * a.ndim, pipeline_mode=pl.Buffered(1))
    return pl.pallas_call(
        functools.partial(_mlp_kernel, ff_chunk=ff_chunk), grid=(rows // tm,),
        in_specs=[row(D_MODEL), pl.BlockSpec((ATTN_WIDTH, tm), lambda i: (0, i)), row(D_RNN), full(g_attn),
                  once(w_out), full(norm_mlp),
                  once(w_up), once(w_down), full(norm_final)],
        out_specs=row(D_MODEL), out_shape=jax.ShapeDtypeStruct((rows, D_MODEL), F32),
        compiler_params=_params("parallel"), name="mlp",
    )(x, o_attn_t, y_rnn, g_attn, w_out, norm_mlp, w_up, w_down, norm_final)


def _block_diag(w):
    nb, bs = w.shape[-3], w.shape[-1]
    tiled = jnp.concatenate([jnp.tile(w[..., b, :, :], (1,) * (w.ndim - 2) + (nb,)) for b in range(nb)], axis=-2)
    on_diagonal = np.kron(np.eye(nb, dtype=bool), np.ones((bs, bs), dtype=bool))
    return jnp.where(on_diagonal, tiled, jnp.zeros((), w.dtype))


def kernel(x_prompt, x_sample, cache_kv, cache_win, state_conv, state_rnn, page_table, w_in, pe_ck, w_ck1, w_ck2,
           pe_cv, w_cv1, w_cv2, g_attn, g_rnn, conv_w, conv_b, w_ra, b_ra, w_ri, b_ri, lam, w_out, norm_mix,
           norm_mlp, w_up, w_down, norm_final):
    assert w_in.shape[0] == 1, "single layer"
    nbp, tp, _ = x_prompt.shape
    nbs, steps, _ = x_sample.shape

    wt = jnp.transpose(w_in[0]).astype(BF16)
    c_kv, c_g = ATTN_WIDTH, ATTN_WIDTH + 6 * KV_COLS
    c_rg = c_g + 3 * N_HEADS
    wq, wkv = wt[0:c_kv], wt[c_kv:c_g]
    wg = wt[c_g:c_rg].reshape(3, N_KV_HEADS, GQA, D_MODEL).transpose(1, 0, 2, 3).reshape(N_KV_HEADS, 3 * GQA, D_MODEL)
    wg = jnp.pad(wg, ((0, 0), (0, GATE_ROWS - 3 * GQA), (0, 0))).reshape(N_KV_HEADS * GATE_ROWS, D_MODEL)
    wr = wt[c_rg:c_rg + 2 * D_RNN]
    row2 = lambda a: a.reshape(1, -1)
    rnn_w = (conv_w[0], row2(conv_b[0]), _block_diag(w_ra[0].astype(BF16)), _block_diag(w_ri[0].astype(BF16)),
             row2(b_ra[0]), row2(b_ri[0]), row2(lam[0]), row2(g_rnn[0]))
    per_cs = lambda k, v: jnp.repeat(jnp.stack([k, v]), N_KV_HEADS, axis=0)
    pe_cs = per_cs(pe_ck[0], pe_cv[0]).reshape(2 * N_KV_HEADS, 2, HALF, HEAD_DIM)
    cmp_pet = jnp.tile(pe_cs.transpose(1, 0, 3, 2), (1, 1, 1, BLOCKS_PER_PAGE)).reshape(2, CMP_COLS, PAGE_SIZE)
    w1_cs = per_cs(w_ck1[0], w_cv1[0]).transpose(1, 0, 2, 3)
    cmp_w1 = _block_diag(w1_cs.astype(BF16)).reshape(2, HALF, CMP_COLS, CMP_COLS)
    cmp_w2 = _block_diag(per_cs(w_ck2[0], w_cv2[0]).astype(BF16))
    mlp_w = (g_attn[0].reshape(-1, 1), w_out[0].astype(BF16), row2(norm_mlp[0]), w_up[0].astype(BF16),
             w_down[0].astype(BF16), row2(norm_final))
    nm = row2(norm_mix[0])

    xp = x_prompt.reshape(nbp * tp, D_MODEL)
    qt, qrt, kvt, wint, kb, vb, rg, rx, gt = _proj(xp, jnp.arange(tp), nm, wq, wkv, wr, wg, nbp, tp, ROW_TILE)
    y_rnn, h_last = _rglru_prompt(rx, rg, rnn_w, nbp, tp, RNN_CHUNK)
    own_pages = jnp.zeros((nbp, tp // PAGE_SIZE), jnp.int32)
    own_page = lambda i, k: pl.BlockSpec((1, CMP_COLS, PAGE_SIZE), lambda n, pt: (n * CMP_SEQS + i, 0, k))
    cmp = _compress(kvt, own_page, own_pages, cmp_pet, cmp_w1, cmp_w2, nbp, CMP_SEQS)
    o_attn_t = _nsa_prompt(qt, qrt, kb, vb, cmp, gt, nbp, tp, Q_TILE, SLC_CHUNK)
    y_prompt = _mlp(xp, o_attn_t, y_rnn, *mlp_w, ROW_TILE, FF_CHUNK).reshape(nbp, tp, D_MODEL)
    wlen = min(WINDOW, tp)
    kv_prompt = kvt.reshape(nbp, 4, N_KV_HEADS, HEAD_DIM, tp).transpose(0, 4, 1, 2, 3)[None]
    win_prompt = wint[:, :, tp - wlen:].reshape(nbp, 2, N_KV_HEADS, HEAD_DIM, wlen).transpose(0, 4, 1, 2, 3)[None]
    conv_prompt = rx.reshape(nbp, tp, D_RNN)[:, tp - (CONV_WIDTH - 1):][None]
    h_prompt = h_last.reshape(1, nbp, D_RNN)

    rows_s = nbs * steps
    xs = x_sample.transpose(1, 0, 2).reshape(rows_s, D_MODEL)
    pos_s = PAST_LEN + jnp.arange(rows_s) // nbs
    qt, qrt, kvt, wint, _, _, rg, rx, gt = _proj(xs, pos_s, nm, wq, wkv, wr, wg, 1, rows_s, rows_s)
    y_rnn, conv_s, h_s = _rglru_sample(rx, rg, state_conv[0].transpose(1, 0, 2), state_rnn[0], rnn_w)
    pages_t = cache_kv[0].transpose(0, 2, 3, 4, 1)
    cmp_page = lambda i, k: pl.BlockSpec((1, 2, N_KV_HEADS, HEAD_DIM, PAGE_SIZE),
                                         lambda n, pt: (pt[n * CMP_SEQS + i, k], 0, 0, 0, 0))
    cmp = _compress(pages_t, cmp_page, page_table, cmp_pet, cmp_w1, cmp_w2, nbs, CMP_SEQS)
    by_seq = lambda a: a.reshape(N_KV_HEADS, GQA, HEAD_DIM, steps, nbs).transpose(4, 0, 1, 3, 2).reshape(
        nbs, N_KV_HEADS, GQA * steps, HEAD_DIM)
    def new_tile(slc_rows, win_rows):
        t = jnp.stack([slc_rows, win_rows]).reshape(2 * N_KV_HEADS, HEAD_DIM, steps, nbs)
        t = t.transpose(3, 1, 0, 2).reshape(nbs, HEAD_DIM, 2 * N_KV_HEADS * steps)
        return jnp.pad(t, ((0, 0), (0, 0), (0, LANES - 2 * N_KV_HEADS * steps)))
    new_k = new_tile(kvt[0, 2 * KV_COLS:3 * KV_COLS], wint[0, 0:KV_COLS])
    new_v = new_tile(kvt[0, 3 * KV_COLS:4 * KV_COLS], wint[0, KV_COLS:2 * KV_COLS])
    gates_s = gt[0].reshape(N_KV_HEADS, GATE_ROWS, steps, nbs)[:, :3 * GQA].reshape(N_KV_HEADS, 3, GQA, steps, nbs)
    gates_s = gates_s.transpose(4, 0, 2, 3, 1).reshape(nbs, N_KV_HEADS, GQA * steps, 3)
    cache_win_t = cache_win[0].transpose(0, 2, 3, 4, 1)
    o_s, win_t = _nsa_sample(by_seq(qt[0]), by_seq(qrt[0]), pages_t, page_table, cmp, cache_win_t, new_k, new_v,
                             gates_s, steps, SAMPLE_SEQS)
    o_attn_t = o_s.reshape(nbs, N_KV_HEADS, GQA, steps, HEAD_DIM).transpose(1, 2, 4, 3, 0).reshape(ATTN_WIDTH, rows_s)
    y_sample = _mlp(xs, o_attn_t, y_rnn, *mlp_w, rows_s, FF_CHUNK).reshape(steps, nbs, D_MODEL).transpose(1, 0, 2)
    kv_sample = kvt[0].reshape(4, N_KV_HEADS, HEAD_DIM, steps, nbs).transpose(4, 3, 0, 1, 2)[None]
    win_sample = win_t.transpose(0, 4, 1, 2, 3)[None]
    conv_sample = conv_s.transpose(1, 0, 2)[None]
    h_sample = h_s[None]

    return (y_prompt, y_sample, kv_prompt, kv_sample, win_prompt, win_sample, conv_prompt, conv_sample,
            h_prompt, h_sample)
```

```python
import functools

import jax
import jax.numpy as jnp
import numpy as np
from jax import lax
from jax.experimental import pallas as pl
from jax.experimental.pallas import tpu as pltpu

D_MODEL = 1024
PAST_LEN = 2048
PAGE_SIZE = 128
HEAD_DIM = 64
N_HEADS = 8
N_KV_HEADS = 2
GQA = N_HEADS // N_KV_HEADS
ATTN_WIDTH = N_HEADS * HEAD_DIM
D_RNN = D_MODEL - ATTN_WIDTH
CONV_WIDTH = 4
LRU_C = 8.0
D_FF = 4 * D_MODEL
ROT_DIM = HEAD_DIM // 4
ROT_HALF = ROT_DIM // 2
ROPE_THETA = 500000.0
CMP_BLOCK = 32
CMP_STRIDE = 16
SEL_BLOCK = 64
TOP_N = 16
WINDOW = 512
KV_COLS = N_KV_HEADS * HEAD_DIM
EPS = 1e-6
NEG = -1e30
SEL_BONUS = 1e4
SEL_MASK = 2.0 ** 100
SCALE = HEAD_DIM ** -0.5

SUBLANES = 8
LANES = 128
VMEM_LIMIT = 48 * 1024 * 1024

ROW_TILE = 512
RNN_CHUNK = 512
Q_TILE = 256
SLC_CHUNK = 512
FF_CHUNK = 1024
CMP_SEQS = 4
SAMPLE_SEQS = 2
EXPM1_FLOOR = -30.0

N_CMP_PAD = 128
GATE_ROWS = 16

F32 = jnp.float32
BF16 = jnp.bfloat16
NT_DIMS = (((1,), (1,)), ((), ()))
TN_DIMS = (((0,), (0,)), ((), ()))


def _nt(a, b):
    return lax.dot_general(a, b, NT_DIMS, preferred_element_type=F32)


def _nn(a, b):
    return jnp.dot(a, b, preferred_element_type=F32)


def _rms(x, g):
    return x * lax.rsqrt(jnp.mean(x * x, axis=-1, keepdims=True) + EPS) * g


def _params(*sem, **extra):
    return pltpu.CompilerParams(dimension_semantics=sem, vmem_limit_bytes=VMEM_LIMIT, **extra)


def _proj_kernel(x_ref, nm_ref, wq_ref, wkv_ref, wr_ref, wg_ref, cos_ref, sin_ref,
                 qt_ref, qrt_ref, kvt_ref, wint_ref, kb_ref, vb_ref, rg_ref, rx_ref, gt_ref):
    u = _rms(x_ref[...], nm_ref[...]).astype(BF16)
    cos, sin = cos_ref[...], sin_ref[...]

    def rotated(a, r0):
        x1, x2 = a[r0:r0 + ROT_HALF], a[r0 + ROT_HALF:r0 + ROT_DIM]
        return x1 * cos - x2 * sin, x2 * cos + x1 * sin

    qt = _nt(wq_ref[...], u)
    qt_ref[0] = (qt * SCALE).astype(BF16)
    parts = []
    for head in range(N_HEADS):
        r0 = head * HEAD_DIM
        parts += [*rotated(qt, r0), qt[r0 + ROT_DIM:r0 + HEAD_DIM]]
    qrt_ref[0] = (jnp.concatenate(parts, axis=0) * SCALE).astype(BF16)

    kvt = _nt(wkv_ref[...], u)
    kvt_ref[0] = kvt[0:4 * KV_COLS]
    wint_ref[0] = kvt[4 * KV_COLS:6 * KV_COLS]
    for out_ref, src0, dst0 in ((kvt_ref, 2 * KV_COLS, 2 * KV_COLS), (wint_ref, 4 * KV_COLS, 0)):
        for h in range(N_KV_HEADS):
            d = dst0 + h * HEAD_DIM
            out_ref[0, d:d + ROT_HALF], out_ref[0, d + ROT_HALF:d + ROT_DIM] = rotated(kvt, src0 + h * HEAD_DIM)
    tm = x_ref.shape[0]
    ones_row = (lax.broadcasted_iota(jnp.int32, (HEAD_DIM, tm), 0) == 0).astype(BF16)
    for branch, src_ref, k0, v0 in ((0, kvt_ref, 2 * KV_COLS, 3 * KV_COLS), (1, wint_ref, 0, KV_COLS)):
        for h in range(N_KV_HEADS):
            i, r = branch * N_KV_HEADS + h, h * HEAD_DIM
            kb_ref[0, i] = src_ref[0, k0 + r:k0 + r + HEAD_DIM].astype(BF16)
            vb_ref[0, i, 0:HEAD_DIM] = src_ref[0, v0 + r:v0 + r + HEAD_DIM].astype(BF16)
            vb_ref[0, i, HEAD_DIM:2 * HEAD_DIM] = ones_row

    r = _nt(u, wr_ref[...])
    rg_ref[...] = r[:, 0:D_RNN]
    rx_ref[...] = r[:, D_RNN:2 * D_RNN]
    gt_ref[0] = jax.nn.sigmoid(_nt(wg_ref[...], u))


def _rope_tables(pos):
    inv = ROPE_THETA ** (-jnp.arange(ROT_HALF, dtype=F32) / ROT_HALF)
    ang = pos.astype(F32)[:, None] * inv
    return jnp.cos(ang).T, jnp.sin(ang).T


def _proj(x, pos, nm, wq, wkv, wr, wg, nb, tt, tm):
    rows = nb * tt
    nt = tt // tm
    cos, sin = _rope_tables(pos)
    row = lambda w: pl.BlockSpec((tm, w), lambda i: (i, 0))
    full = lambda a: pl.BlockSpec(a.shape, lambda i: (0,) * a.ndim)
    tab = pl.BlockSpec((ROT_HALF, tm), lambda i: (0, i % nt))
    tr = lambda r: pl.BlockSpec((1, r, tm), lambda i: (i // nt, 0, i % nt))
    out_shape = (
        jax.ShapeDtypeStruct((nb, ATTN_WIDTH, tt), BF16),
        jax.ShapeDtypeStruct((nb, ATTN_WIDTH, tt), BF16),
        jax.ShapeDtypeStruct((nb, 4 * KV_COLS, tt), F32),
        jax.ShapeDtypeStruct((nb, 2 * KV_COLS, tt), F32),
        jax.ShapeDtypeStruct((nb, 2 * N_KV_HEADS, HEAD_DIM, tt), BF16),
        jax.ShapeDtypeStruct((nb, 2 * N_KV_HEADS, 2 * HEAD_DIM, tt), BF16),
        jax.ShapeDtypeStruct((rows, D_RNN), F32),
        jax.ShapeDtypeStruct((rows, D_RNN), F32),
        jax.ShapeDtypeStruct((nb, N_KV_HEADS * GATE_ROWS, tt), F32),
    )
    tr4 = lambda r: pl.BlockSpec((1, 2 * N_KV_HEADS, r, tm), lambda i: (i // nt, 0, 0, i % nt))
    out_specs = (tr(ATTN_WIDTH), tr(ATTN_WIDTH), tr(4 * KV_COLS), tr(2 * KV_COLS), tr4(HEAD_DIM), tr4(2 * HEAD_DIM),
                 row(D_RNN), row(D_RNN), tr(N_KV_HEADS * GATE_ROWS))
    return pl.pallas_call(
        _proj_kernel, grid=(rows // tm,),
        in_specs=[row(D_MODEL), full(nm), full(wq), full(wkv), full(wr), full(wg), tab, tab],
        out_specs=out_specs, out_shape=out_shape, compiler_params=_params("parallel"), name="proj",
    )(x, nm, wq, wkv, wr, wg, cos, sin)


def _expm1_from_exp(x, u):
    near = jnp.where(u == 1.0, x, (u - 1.0) * x / jnp.log(u))
    return jnp.where(x < EXPM1_FLOOR, -1.0, near)


def _softplus(x):
    return jnp.maximum(x, 0.0) + jnp.log1p(jnp.exp(-jnp.abs(x)))


def _lru_coeffs(xc, wra_ref, wri_ref, bra_ref, bri_ref, lam_ref):
    xb = xc.astype(BF16)
    r = jax.nn.sigmoid(_nn(xb, wra_ref[...]) + bra_ref[...])
    i = jax.nn.sigmoid(_nn(xb, wri_ref[...]) + bri_ref[...])
    log_a = -LRU_C * r * _softplus(-lam_ref[...])
    a = jnp.exp(log_a)
    b = jnp.sqrt(-_expm1_from_exp(2.0 * log_a, a * a)) * (i * xc)
    return a, b


def _rnn_out(rg, h, g):
    return _rms(jax.nn.gelu(rg) * h, g).astype(BF16)


def _rglru_prompt_kernel(rx_ref, rg_ref, cw_ref, cb_ref, wra_ref, wri_ref, bra_ref, bri_ref, lam_ref, g_ref,
                         y_ref, hl_ref, prev_sc, h_sc):
    tt = rx_ref.shape[0]

    @pl.when(pl.program_id(1) == 0)
    def _():
        prev_sc[...] = jnp.zeros_like(prev_sc)
        h_sc[...] = jnp.zeros_like(h_sc)

    rx = rx_ref[...]
    ext = jnp.concatenate([prev_sc[...], rx], axis=0)
    shifted = lambda d: pltpu.roll(ext, d, 0)[SUBLANES:SUBLANES + tt]
    cw = cw_ref[...]
    xc = cb_ref[...] + cw[0:1] * shifted(3)
    xc = xc + cw[1:2] * shifted(2)
    xc = xc + cw[2:3] * shifted(1)
    xc = xc + cw[3:4] * rx
    prev_sc[...] = rx[tt - SUBLANES:tt]

    a, b = _lru_coeffs(xc, wra_ref, wri_ref, bra_ref, bri_ref, lam_ref)
    n_tiles, width = tt // SUBLANES, a.shape[1]
    a = a.reshape(n_tiles, SUBLANES, width)
    b = b.reshape(n_tiles, SUBLANES, width)
    sub = lax.broadcasted_iota(jnp.int32, (1, SUBLANES, 1), 1)
    s = 1
    while s < SUBLANES:
        a_sh = jnp.where(sub >= s, pltpu.roll(a, s, 1), 1.0)
        b_sh = jnp.where(sub >= s, pltpu.roll(b, s, 1), 0.0)
        b = a * b_sh + b
        a = a * a_sh
        s *= 2
    state, tiles = h_sc[...], []
    for t in range(n_tiles):
        h_t = a[t] * state + b[t]
        tiles.append(h_t)
        state = h_t[SUBLANES - 1:SUBLANES]
    h = jnp.concatenate(tiles, axis=0)
    h_sc[...] = state
    hl_ref[0] = state
    y_ref[...] = _rnn_out(rg_ref[...], h, g_ref[...])


def _rglru_sample_kernel(rx_ref, rg_ref, cp_ref, h0_ref, cw_ref, cb_ref, wra_ref, wri_ref, bra_ref, bri_ref,
                         lam_ref, g_ref, y_ref, cs_ref, hl_ref):
    nb = h0_ref.shape[0]
    steps = rx_ref.shape[0] // nb
    xp = [cp_ref[k] for k in range(CONV_WIDTH - 1)] + [rx_ref[t * nb:(t + 1) * nb] for t in range(steps)]
    cw, cb = cw_ref[...], cb_ref[...]
    xcs = []
    for t in range(steps):
        xc = cb + cw[0:1] * xp[t]
        for tap in range(1, CONV_WIDTH):
            xc = xc + cw[tap:tap + 1] * xp[t + tap]
        xcs.append(xc)
    a, b = _lru_coeffs(jnp.concatenate(xcs, axis=0), wra_ref, wri_ref, bra_ref, bri_ref, lam_ref)
    h = h0_ref[...]
    for t in range(steps):
        sl = slice(t * nb, (t + 1) * nb)
        h = a[sl] * h + b[sl]
        y_ref[sl] = _rnn_out(rg_ref[sl], h, g_ref[...])
    for k in range(CONV_WIDTH - 1):
        cs_ref[k] = xp[steps + k]
    hl_ref[...] = h


def _rglru_prompt(rx, rg, weights, nb, tt, chunk):
    nc = tt // chunk
    row = pl.BlockSpec((chunk, D_RNN), lambda n, c: (n * nc + c, 0))
    full = lambda a: pl.BlockSpec(a.shape, lambda n, c: (0,) * a.ndim)
    return pl.pallas_call(
        _rglru_prompt_kernel, grid=(nb, nc),
        in_specs=[row, row] + [full(w) for w in weights],
        out_specs=(row, pl.BlockSpec((1, 1, D_RNN), lambda n, c: (n, 0, 0))),
        out_shape=(jax.ShapeDtypeStruct((nb * tt, D_RNN), BF16), jax.ShapeDtypeStruct((nb, 1, D_RNN), F32)),
        scratch_shapes=[pltpu.VMEM((SUBLANES, D_RNN), F32), pltpu.VMEM((1, D_RNN), F32)],
        compiler_params=_params("parallel", "arbitrary"), name="rglru_prompt",
    )(rx, rg, *weights)


def _rglru_sample(rx, rg, conv_prev, h0, weights):
    nb = h0.shape[0]
    return pl.pallas_call(
        _rglru_sample_kernel,
        out_shape=(jax.ShapeDtypeStruct(rx.shape, BF16), jax.ShapeDtypeStruct(conv_prev.shape, F32),
                   jax.ShapeDtypeStruct((nb, D_RNN), F32)),
        compiler_params=pltpu.CompilerParams(vmem_limit_bytes=VMEM_LIMIT), name="rglru_sample",
    )(rx, rg, conv_prev, h0, *weights)


CMP_COLS = 2 * KV_COLS
HALF = CMP_BLOCK // 2
BLOCKS_PER_PAGE = PAGE_SIZE // CMP_STRIDE


def _compress_kernel(pt_ref, *refs, n_pages):
    del pt_ref
    page_refs = refs[:n_pages]
    pet_ref, perm_ref, w1_ref, w2_ref, out_ref, z_sc = refs[n_pages:]
    seqs = out_ref.shape[0]
    for k in range(n_pages):
        xt = page_refs[k][0].reshape(CMP_COLS, PAGE_SIZE)
        for half in range(2):
            a = (xt + pet_ref[half]).astype(BF16)
            z_sc[half, k] = _nt(perm_ref[...], a)
    pre = []
    for half in range(2):
        acc = jnp.zeros((seqs * N_CMP_PAD, CMP_COLS), F32)
        for j in range(CMP_STRIDE):
            rows = z_sc[half, :, j * BLOCKS_PER_PAGE:(j + 1) * BLOCKS_PER_PAGE, :]
            acc = acc + _nn(rows.reshape(seqs * N_CMP_PAD, CMP_COLS).astype(BF16), w1_ref[half, j])
        pre.append(acc)
    hid = jax.nn.gelu(pre[0] + pltpu.roll(pre[1], seqs * N_CMP_PAD - 1, 0))
    out_ref[...] = _nn(hid.astype(BF16), w2_ref[...]).astype(BF16).reshape(seqs, N_CMP_PAD, CMP_COLS)


def _compress(pages, page_spec, page_table, pet, w1, w2, nb, seqs):
    n_pages = page_table.shape[1]
    assert n_pages * BLOCKS_PER_PAGE == N_CMP_PAD
    pos = np.arange(PAGE_SIZE)
    perm = jnp.asarray((pos % CMP_STRIDE * BLOCKS_PER_PAGE + pos // CMP_STRIDE)[None, :]
                       == np.arange(PAGE_SIZE)[:, None], BF16)
    full = lambda a: pl.BlockSpec(a.shape, lambda n, pt: (0,) * a.ndim)
    once = lambda a: pl.BlockSpec(a.shape, lambda n, pt: (0,) * a.ndim, pipeline_mode=pl.Buffered(1))
    page_specs = [page_spec(i, k) for i in range(seqs) for k in range(n_pages)]
    grid_spec = pltpu.PrefetchScalarGridSpec(
        num_scalar_prefetch=1, grid=(nb // seqs,),
        in_specs=page_specs + [full(pet), full(perm), once(w1), full(w2)],
        out_specs=pl.BlockSpec((seqs, N_CMP_PAD, CMP_COLS), lambda n, pt: (n, 0, 0)),
        scratch_shapes=[pltpu.VMEM((2, len(page_specs), PAGE_SIZE, CMP_COLS), F32)])
    return pl.pallas_call(
        functools.partial(_compress_kernel, n_pages=len(page_specs)), grid_spec=grid_spec,
        out_shape=jax.ShapeDtypeStruct((nb, N_CMP_PAD, CMP_COLS), BF16),
        compiler_params=_params("parallel"), name="compress",
    )(page_table, *([pages] * len(page_specs)), pet, perm, w1, w2)


def _cmp_cols(cmp_ref, i, slot, h):
    c0 = slot * KV_COLS + h * HEAD_DIM
    return cmp_ref[i, :, c0:c0 + HEAD_DIM]


def _overlap_matrix(n_sel):
    n_cmp = N_CMP_PAD - 1
    c0 = np.arange(N_CMP_PAD)[:, None] * CMP_STRIDE
    j0 = np.arange(LANES)[None, :] * SEL_BLOCK
    ov = (c0 < j0 + SEL_BLOCK) & (c0 + CMP_BLOCK > j0)
    ov &= (np.arange(N_CMP_PAD)[:, None] < n_cmp) & (np.arange(LANES)[None, :] < n_sel)
    return ov.astype(np.float32)


def _topk_member(score, idx, n, axis):
    rank = jnp.zeros(score.shape, jnp.int32)
    for i in range(n):
        si = score[i:i + 1, :] if axis == 0 else score[:, i:i + 1]
        beats = (si > score) | ((si == score) & (i < idx))
        rank = rank + beats.astype(jnp.int32)
    return rank < TOP_N


def _sel_score(imp, j, cur):
    valid = j <= cur
    forced = (j == 0) | (j == cur) | (j == cur - 1)
    return jnp.where(valid, imp, -SEL_BONUS) + jnp.where(forced, SEL_BONUS, 0.0)


def _nsa_prompt_kernel(qt_ref, qrt_ref, kb_ref, vb_ref, cmp_ref, gt_ref, ovt_ref, e_ref,
                       o_ref, m_sc, acc_sc, *, tq, slc_chunk, col_block, n_sel):
    heads = range(N_KV_HEADS)
    q0 = pl.program_id(1) * tq
    wide = GQA * tq
    qpos = q0 + lax.broadcasted_iota(jnp.int32, (1, tq), 1)
    tile = lambda a, n: jnp.concatenate([a] * n, axis=1)
    group = lambda ref, h: jnp.concatenate(
        [ref[0, (h * GQA + g) * HEAD_DIM:(h * GQA + g + 1) * HEAD_DIM, :] for g in range(GQA)], axis=1)

    cidx = lax.broadcasted_iota(jnp.int32, (N_CMP_PAD, 1), 0)
    real = cidx < N_CMP_PAD - 1
    qpos_all = q0 + lax.broadcasted_iota(jnp.int32, (1, N_HEADS * tq), 1) % tq
    cmask = (cidx * CMP_STRIDE + CMP_BLOCK - 1 <= qpos_all) & real
    s = jnp.concatenate([_nn(_cmp_cols(cmp_ref, 0, 0, h), group(qt_ref, h)) for h in heads], axis=1)
    s = jnp.where(real, jnp.where(cmask, s, NEG), -jnp.inf)
    e = jnp.exp(s - jnp.max(s, axis=0, keepdims=True))
    p = (e / jnp.sum(e, axis=0, keepdims=True)) * cmask.astype(F32)
    pb = p.astype(BF16)
    o_cmp = [lax.dot_general(_cmp_cols(cmp_ref, 0, 1, h), pb[:, h * wide:(h + 1) * wide], TN_DIMS,
                             preferred_element_type=F32) for h in heads]

    imp = []
    for h in heads:
        imp_h = _nn(ovt_ref[...], pb[:, h * wide:h * wide + tq])
        for g in range(1, GQA):
            imp_h = imp_h + _nn(ovt_ref[...], pb[:, h * wide + g * tq:h * wide + (g + 1) * tq])
        imp.append(imp_h)
    j = lax.broadcasted_iota(jnp.int32, (n_sel, 1), 0)
    qpos_2 = q0 + lax.broadcasted_iota(jnp.int32, (1, N_KV_HEADS * tq), 1) % tq
    picked = _topk_member(_sel_score(jnp.concatenate(imp, axis=1), j, qpos_2 // SEL_BLOCK), j, n_sel, axis=0)
    unpicked = jnp.where(picked, 0.0, -SEL_MASK).astype(BF16)
    pad_rows = jnp.zeros((HEAD_DIM - n_sel, wide), BF16)
    qr = [group(qrt_ref, h) for h in heads]
    qsel = [jnp.concatenate([qr[h], tile(unpicked[:, h * tq:(h + 1) * tq], GQA), pad_rows], axis=0)
            for h in heads]

    def attend(h, qx, k, v, allowed, first):
        bias = jnp.where(allowed, 0.0, NEG)
        pes, ms, alphas = [], [], []
        for c0 in range(0, wide, col_block):
            t0 = c0 % tq
            sc = lax.dot_general(k, qx[:, c0:c0 + col_block], TN_DIMS, preferred_element_type=F32)
            x = sc + bias[:, t0:t0 + col_block]
            m = jnp.max(x, axis=0, keepdims=True)
            if not first:
                m_prev = m_sc[h, :, c0:c0 + col_block]
                m = jnp.maximum(m_prev, m)
                alphas.append(jnp.exp(m_prev - m))
            ms.append(m)
            pes.append(jnp.exp((x - m).astype(BF16)))
        pv = _nn(v, jnp.concatenate(pes, axis=1))
        m_sc[h] = jnp.concatenate(ms, axis=1)
        acc_sc[h] = pv if first else jnp.concatenate(alphas, axis=1) * acc_sc[h] + pv

    def normalised(h):
        acc = acc_sc[h]
        return acc[0:HEAD_DIM] / acc[HEAD_DIM:HEAD_DIM + 1]

    wn = WINDOW + tq
    w0 = pl.multiple_of(jnp.maximum(q0 - WINDOW, 0), LANES)
    kpos = w0 + lax.broadcasted_iota(jnp.int32, (wn, 1), 0)
    in_window = (kpos <= qpos) & (kpos > qpos - WINDOW)
    for h in heads:
        attend(h, qr[h], kb_ref[0, N_KV_HEADS + h, :, pl.ds(w0, wn)], vb_ref[0, N_KV_HEADS + h, :, pl.ds(w0, wn)],
               in_window, True)
    o_win = [normalised(h) for h in heads]

    for c0 in range(0, kb_ref.shape[3], slc_chunk):
        def chunk(c0=c0):
            kpos = c0 + lax.broadcasted_iota(jnp.int32, (slc_chunk, 1), 0)
            member = jnp.concatenate([e_ref[:, c0:c0 + slc_chunk],
                                      jnp.zeros((HEAD_DIM - n_sel, slc_chunk), BF16)], axis=0)
            for h in heads:
                k = jnp.concatenate([kb_ref[0, h, :, c0:c0 + slc_chunk], member], axis=0)
                attend(h, qsel[h], k, vb_ref[0, h, :, c0:c0 + slc_chunk], kpos <= qpos, c0 == 0)

        if c0 == 0:
            chunk()
        else:
            pl.when(q0 + tq > c0)(chunk)
    o_slc = [normalised(h) for h in heads]

    gt = gt_ref[0]
    for h in heads:
        for g in range(GQA):
            gate = lambda b: gt[h * GATE_ROWS + b * GQA + g:h * GATE_ROWS + b * GQA + g + 1, :]
            cols = slice(g * tq, (g + 1) * tq)
            o = gate(0) * o_cmp[h][:, cols] + gate(1) * o_slc[h][:, cols]
            o = o + gate(2) * o_win[h][:, cols]
            o_ref[(h * GQA + g) * HEAD_DIM:(h * GQA + g + 1) * HEAD_DIM, :] = o


def _nsa_prompt(qt, qrt, kb, vb, cmp, gt, nb, tt, tq, slc_chunk):
    n_sel = -(-tt // SEL_BLOCK)
    nq = tt // tq
    ovt = jnp.asarray(_overlap_matrix(n_sel)[:, :n_sel].T, BF16)
    expand = jnp.asarray(np.arange(tt)[None, :] // SEL_BLOCK == np.arange(n_sel)[:, None], BF16)
    cols = lambda a: pl.BlockSpec((1, a.shape[1], tq), lambda n, i: (n, 0, i))
    seq = lambda a: pl.BlockSpec((1,) + a.shape[1:], lambda n, i: (n,) + (0,) * (a.ndim - 1))
    full = lambda a: pl.BlockSpec(a.shape, lambda n, i: (0,) * a.ndim)
    return pl.pallas_call(
        functools.partial(_nsa_prompt_kernel, tq=tq, slc_chunk=slc_chunk, col_block=LANES, n_sel=n_sel),
        grid=(nb, nq),
        in_specs=[cols(qt), cols(qrt), seq(kb), seq(vb), seq(cmp), cols(gt), full(ovt), full(expand)],
        out_specs=pl.BlockSpec((ATTN_WIDTH, tq), lambda n, i: (0, n * nq + i)),
        out_shape=jax.ShapeDtypeStruct((ATTN_WIDTH, nb * tt), F32),
        scratch_shapes=[pltpu.VMEM((N_KV_HEADS, 1, GQA * tq), F32),
                        pltpu.VMEM((N_KV_HEADS, 2 * HEAD_DIM, GQA * tq), F32)],
        compiler_params=_params("parallel", "arbitrary"), name="nsa_prompt",
    )(qt, qrt, kb, vb, cmp, gt, ovt, expand)


def _nsa_sample_kernel(pt_ref, *refs, n_pages, steps, seqs):
    del pt_ref
    q_ref, qr_ref = refs[0:2]
    page_refs = refs[2:2 + seqs * n_pages]
    cmp_ref, cw_ref, newk_ref, newv_ref, gates_ref, ov_ref, o_ref, wout_ref = refs[2 + seqs * n_pages:]
    rows = GQA * steps
    past = n_pages * PAGE_SIZE
    wc = cw_ref.shape[-1]
    trow = lax.broadcasted_iota(jnp.int32, (rows, 1), 0) % steps
    lane = lax.broadcasted_iota(jnp.int32, (1, LANES), 1)
    n_sel = -(-(past + steps) // SEL_BLOCK)
    cur = (past + trow) // SEL_BLOCK
    real = lane < N_CMP_PAD - 1
    wlane = lax.broadcasted_iota(jnp.int32, (1, wc), 1)
    in_window = jnp.broadcast_to(wlane + (WINDOW - wc) > trow, (rows, wc))
    per_tile = PAGE_SIZE // SEL_BLOCK

    chains = [(i, h) for i in range(seqs) for h in range(N_KV_HEADS)]
    assert len(chains) * steps == rows, "the stacked top-k below holds one chain per query-group slot"
    s = jnp.concatenate([_nt(q_ref[i, h], _cmp_cols(cmp_ref, i, 0, h)) for i, h in chains], axis=0)
    s = jnp.where(real, s, -jnp.inf)
    e = jnp.exp(s - jnp.max(s, axis=-1, keepdims=True))
    pb_all = (e / jnp.sum(e, axis=-1, keepdims=True)).astype(BF16)
    group = lax.broadcasted_iota(jnp.int32, (rows, 1), 0) // steps
    o_cmps, imp_all = [], None
    for c, (i, h) in enumerate(chains):
        pb = pb_all[c * rows:(c + 1) * rows]
        o_cmps.append(_nn(pb, _cmp_cols(cmp_ref, i, 1, h)))
        part = _nn(pb, ov_ref[...])
        imp = part
        for g in range(1, GQA):
            imp = imp + pltpu.roll(part, g * steps, 0)
        imp_all = imp if c == 0 else jnp.where(group == c, imp, imp_all)
    score = jnp.where(lane < n_sel, _sel_score(imp_all, lane, cur), -jnp.inf)
    sel_all = (_topk_member(score, lane, n_sel, axis=1) & (lane < n_sel)).astype(F32)
    rolled = [sel_all] + [pltpu.roll(sel_all, k * steps, 0) for k in range(1, GQA)]

    for c, (i, h) in enumerate(chains):
        newk, newv = newk_ref[i], newv_ref[i]
        newk_b, newv_b = newk.astype(BF16), newv.astype(BF16)
        qr = qr_ref[i, h]
        pages = page_refs[i * n_pages:(i + 1) * n_pages]

        def new_ok(branch, h=h):
            base = (branch * N_KV_HEADS + h) * steps
            return (lane >= base) & (lane < base + steps) & (lane - base <= trow)

        sel_f = rolled[(0 - c) % GQA]
        for g in range(1, GQA):
            sel_f = jnp.where(group == g, rolled[(g - c) % GQA], sel_f)

        ks = jnp.concatenate([p[0, 0, h].astype(BF16) for p in pages] + [newk_b], axis=1)
        vs = jnp.concatenate([p[0, 1, h].astype(BF16) for p in pages] + [newv_b], axis=1)
        tiles = []
        for k in range(n_pages):
            m = sel_f[:, per_tile * k:per_tile * k + 1]
            for b in range(1, per_tile):
                m = jnp.where(lane < b * SEL_BLOCK, m, sel_f[:, per_tile * k + b:per_tile * k + b + 1])
            tiles.append(m > 0.5)
        tiles.append((sel_f[:, n_sel - 1:n_sel] > 0.5) & new_ok(0))
        msk = jnp.concatenate(tiles, axis=1)
        sc = jnp.where(msk, _nn(qr, ks), NEG)
        pe = jnp.where(msk, jnp.exp(sc - jnp.max(sc, axis=-1, keepdims=True)), 0.0)
        o_slc = _nt(pe.astype(BF16), vs) / jnp.sum(pe, axis=-1, keepdims=True)

        kw = jnp.concatenate([cw_ref[i, 0, h].astype(BF16), newk_b], axis=1)
        vw = jnp.concatenate([cw_ref[i, 1, h].astype(BF16), newv_b], axis=1)
        wmsk = jnp.concatenate([in_window, jnp.broadcast_to(new_ok(1), (rows, LANES))], axis=1)
        sw = jnp.where(wmsk, _nn(qr, kw), NEG)
        pw = jnp.where(wmsk, jnp.exp(sw - jnp.max(sw, axis=-1, keepdims=True)), 0.0)
        o_win = _nt(pw.astype(BF16), vw) / jnp.sum(pw, axis=-1, keepdims=True)

        gt = gates_ref[i, h]
        o_ref[i, h] = gt[:, 0:1] * o_cmps[c] + gt[:, 1:2] * o_slc + gt[:, 2:3] * o_win

        to_end = (LANES - steps - (N_KV_HEADS + h) * steps) % LANES
        for slot, new in ((0, newk), (1, newv)):
            shifted = pltpu.roll(cw_ref[i, slot, h], wc - steps, 1)
            wout_ref[i, slot, h, :, 0:wc - LANES] = shifted[:, 0:wc - LANES]
            wout_ref[i, slot, h, :, wc - LANES:wc] = jnp.where(lane >= LANES - steps, pltpu.roll(new, to_end, 1),
                                                               shifted[:, wc - LANES:wc])


def _nsa_sample(q, qr, pages, page_table, cmp, cache_win_t, new_k, new_v, gates, steps, seqs):
    nb, n_pages = page_table.shape
    n_sel = -(-(n_pages * PAGE_SIZE + steps) // SEL_BLOCK)
    ov = jnp.asarray(_overlap_matrix(n_sel), BF16)
    rows = GQA * steps
    lead = lambda a: pl.BlockSpec((seqs,) + a.shape[1:], lambda n, pt: (n,) + (0,) * (a.ndim - 1))
    page_spec = lambda i, k: pl.BlockSpec((1, 2, N_KV_HEADS, HEAD_DIM, PAGE_SIZE),
                                          lambda n, pt: (pt[n * seqs + i, k], 1, 0, 0, 0))
    page_specs = [page_spec(i, k) for i in range(seqs) for k in range(n_pages)]
    grid_spec = pltpu.PrefetchScalarGridSpec(
        num_scalar_prefetch=1, grid=(nb // seqs,),
        in_specs=[lead(q), lead(qr)] + page_specs
        + [lead(cmp), lead(cache_win_t), lead(new_k), lead(new_v), lead(gates),
           pl.BlockSpec(ov.shape, lambda n, pt: (0, 0))],
        out_specs=(pl.BlockSpec((seqs, N_KV_HEADS, rows, HEAD_DIM), lambda n, pt: (n, 0, 0, 0)), lead(cache_win_t)))
    return pl.pallas_call(
        functools.partial(_nsa_sample_kernel, n_pages=n_pages, steps=steps, seqs=seqs), grid_spec=grid_spec,
        out_shape=(jax.ShapeDtypeStruct((nb, N_KV_HEADS, rows, HEAD_DIM), F32),
                   jax.ShapeDtypeStruct(cache_win_t.shape, F32)),
        compiler_params=_params("parallel"), name="nsa_sample",
    )(page_table, q, qr, *([pages] * len(page_specs)), cmp, cache_win_t, new_k, new_v, gates, ov)


def _mlp_kernel(x_ref, oat_ref, yr_ref, ga_ref, wo_ref, nmlp_ref, wup_ref, wdn_ref, nfin_ref, y_ref, *, ff_chunk):
    oat = oat_ref[...]
    at = (oat * lax.rsqrt(jnp.mean(oat * oat, axis=0, keepdims=True) + EPS) * ga_ref[...]).astype(BF16)
    attn = lax.dot_general(at, wo_ref[0:ATTN_WIDTH, :], TN_DIMS, preferred_element_type=F32)
    x1 = x_ref[...] + (attn + _nn(yr_ref[...], wo_ref[ATTN_WIDTH:, :]))
    v = _rms(x1, nmlp_ref[...]).astype(BF16)
    acc = jnp.zeros_like(x1)
    for c in range(D_FF // ff_chunk):
        sl = slice(c * ff_chunk, (c + 1) * ff_chunk)
        hid = jnp.square(jnp.maximum(_nn(v, wup_ref[:, sl]), 0.0)).astype(BF16)
        acc = acc + _nn(hid, wdn_ref[sl, :])
    y_ref[...] = _rms(x1 + acc, nfin_ref[...])


def _mlp(x, o_attn_t, y_rnn, g_attn, w_out, norm_mlp, w_up, w_down, norm_final, tm, ff_chunk):
    rows = x.shape[0]
    row = lambda w: pl.BlockSpec((tm, w), lambda i: (i, 0))
    full = lambda a: pl.BlockSpec(a.shape, lambda i: (0,) * a.ndim)
    once = lambda a: pl.BlockSpec(a.shape, lambda i: (0,) * a.ndim, pipeline_mode=pl.Buffered(1))
    return pl.pallas_call(
        functools.partial(_mlp_kernel, ff_chunk=ff_chunk), grid=(rows // tm,),
        in_specs=[row(D_MODEL), pl.BlockSpec((ATTN_WIDTH, tm), lambda i: (0, i)), row(D_RNN), full(g_attn),
                  once(w_out), full(norm_mlp),
                  once(w_up), once(w_down), full(norm_final)],
        out_specs=row(D_MODEL), out_shape=jax.ShapeDtypeStruct((rows, D_MODEL), F32),
        compiler_params=_params("parallel"), name="mlp",
    )(x, o_attn_t, y_rnn, g_attn, w_out, norm_mlp, w_up, w_down, norm_final)


def _block_diag(w):
    nb, bs = w.shape[-3], w.shape[-1]
    tiled = jnp.concatenate([jnp.tile(w[..., b, :, :], (1,) * (w.ndim - 2) + (nb,)) for b in range(nb)], axis=-2)
    on_diagonal = np.kron(np.eye(nb, dtype=bool), np.ones((bs, bs), dtype=bool))
    return jnp.where(on_diagonal, tiled, jnp.zeros((), w.dtype))


def kernel(x_prompt, x_sample, cache_kv, cache_win, state_conv, state_rnn, page_table, w_in, pe_ck, w_ck1, w_ck2,
           pe_cv, w_cv1, w_cv2, g_attn, g_rnn, conv_w, conv_b, w_ra, b_ra, w_ri, b_ri, lam, w_out, norm_mix,
           norm_mlp, w_up, w_down, norm_final):
    assert w_in.shape[0] == 1, "single layer"
    nbp, tp, _ = x_prompt.shape
    nbs, steps, _ = x_sample.shape

    wt = jnp.transpose(w_in[0]).astype(BF16)
    c_kv, c_g = ATTN_WIDTH, ATTN_WIDTH + 6 * KV_COLS
    c_rg = c_g + 3 * N_HEADS
    wq, wkv = wt[0:c_kv], wt[c_kv:c_g]
    wg = wt[c_g:c_rg].reshape(3, N_KV_HEADS, GQA, D_MODEL).transpose(1, 0, 2, 3).reshape(N_KV_HEADS, 3 * GQA, D_MODEL)
    wg = jnp.pad(wg, ((0, 0), (0, GATE_ROWS - 3 * GQA), (0, 0))).reshape(N_KV_HEADS * GATE_ROWS, D_MODEL)
    wr = wt[c_rg:c_rg + 2 * D_RNN]
    row2 = lambda a: a.reshape(1, -1)
    rnn_w = (conv_w[0], row2(conv_b[0]), _block_diag(w_ra[0].astype(BF16)), _block_diag(w_ri[0].astype(BF16)),
             row2(b_ra[0]), row2(b_ri[0]), row2(lam[0]), row2(g_rnn[0]))
    per_cs = lambda k, v: jnp.repeat(jnp.stack([k, v]), N_KV_HEADS, axis=0)
    pe_cs = per_cs(pe_ck[0], pe_cv[0]).reshape(2 * N_KV_HEADS, 2, HALF, HEAD_DIM)
    cmp_pet = jnp.tile(pe_cs.transpose(1, 0, 3, 2), (1, 1, 1, BLOCKS_PER_PAGE)).reshape(2, CMP_COLS, PAGE_SIZE)
    w1_cs = per_cs(w_ck1[0], w_cv1[0]).transpose(1, 0, 2, 3)
    cmp_w1 = _block_diag(w1_cs.astype(BF16)).reshape(2, HALF, CMP_COLS, CMP_COLS)
    cmp_w2 = _block_diag(per_cs(w_ck2[0], w_cv2[0]).astype(BF16))
    mlp_w = (g_attn[0].reshape(-1, 1), w_out[0].astype(BF16), row2(norm_mlp[0]), w_up[0].astype(BF16),
             w_down[0].astype(BF16), row2(norm_final))
    nm = row2(norm_mix[0])

    xp = x_prompt.reshape(nbp * tp, D_MODEL)
    qt, qrt, kvt, wint, kb, vb, rg, rx, gt = _proj(xp, jnp.arange(tp), nm, wq, wkv, wr, wg, nbp, tp, ROW_TILE)
    y_rnn, h_last = _rglru_prompt(rx, rg, rnn_w, nbp, tp, RNN_CHUNK)
    own_pages = jnp.zeros((nbp, tp // PAGE_SIZE), jnp.int32)
    own_page = lambda i, k: pl.BlockSpec((1, CMP_COLS, PAGE_SIZE), lambda n, pt: (n * CMP_SEQS + i, 0, k))
    cmp = _compress(kvt, own_page, own_pages, cmp_pet, cmp_w1, cmp_w2, nbp, CMP_SEQS)
    o_attn_t = _nsa_prompt(qt, qrt, kb, vb, cmp, gt, nbp, tp, Q_TILE, SLC_CHUNK)
    y_prompt = _mlp(xp, o_attn_t, y_rnn, *mlp_w, ROW_TILE, FF_CHUNK).reshape(nbp, tp, D_MODEL)
    wlen = min(WINDOW, tp)
    kv_prompt = kvt.reshape(nbp, 4, N_KV_HEADS, HEAD_DIM, tp).transpose(0, 4, 1, 2, 3)[None]
    win_prompt = wint[:, :, tp - wlen:].reshape(nbp, 2, N_KV_HEADS, HEAD_DIM, wlen).transpose(0, 4, 1, 2, 3)[None]
    conv_prompt = rx.reshape(nbp, tp, D_RNN)[:, tp - (CONV_WIDTH - 1):][None]
    h_prompt = h_last.reshape(1, nbp, D_RNN)

    rows_s = nbs * steps
    xs = x_sample.transpose(1, 0, 2).reshape(rows_s, D_MODEL)
    pos_s = PAST_LEN + jnp.arange(rows_s) // nbs
    qt, qrt, kvt, wint, _, _, rg, rx, gt = _proj(xs, pos_s, nm, wq, wkv, wr, wg, 1, rows_s, rows_s)
    y_rnn, conv_s, h_s = _rglru_sample(rx, rg, state_conv[0].transpose(1, 0, 2), state_rnn[0], rnn_w)
    pages_t = cache_kv[0].transpose(0, 2, 3, 4, 1)
    cmp_page = lambda i, k: pl.BlockSpec((1, 2, N_KV_HEADS, HEAD_DIM, PAGE_SIZE),
                                         lambda n, pt: (pt[n * CMP_SEQS + i, k], 0, 0, 0, 0))
    cmp = _compress(pages_t, cmp_page, page_table, cmp_pet, cmp_w1, cmp_w2, nbs, CMP_SEQS)
    by_seq = lambda a: a.reshape(N_KV_HEADS, GQA, HEAD_DIM, steps, nbs).transpose(4, 0, 1, 3, 2).reshape(
        nbs, N_KV_HEADS, GQA * steps, HEAD_DIM)
    def new_tile(slc_rows, win_rows):
        t = jnp.stack([slc_rows, win_rows]).reshape(2 * N_KV_HEADS, HEAD_DIM, steps, nbs)
        t = t.transpose(3, 1, 0, 2).reshape(nbs, HEAD_DIM, 2 * N_KV_HEADS * steps)
        return jnp.pad(t, ((0, 0), (0, 0), (0, LANES - 2 * N_KV_HEADS * steps)))
    new_k = new_tile(kvt[0, 2 * KV_COLS:3 * KV_COLS], wint[0, 0:KV_COLS])
    new_v = new_tile(kvt[0, 3 * KV_COLS:4 * KV_COLS], wint[0, KV_COLS:2 * KV_COLS])
    gates_s = gt[0].reshape(N_KV_HEADS, GATE_ROWS, steps, nbs)[:, :3 * GQA].reshape(N_KV_HEADS, 3, GQA, steps, nbs)
    gates_s = gates_s.transpose(4, 0, 2, 3, 1).reshape(nbs, N_KV_HEADS, GQA * steps, 3)
    cache_win_t = cache_win[0].transpose(0, 2, 3, 4, 1)
    o_s, win_t = _nsa_sample(by_seq(qt[0]), by_seq(qrt[0]), pages_t, page_table, cmp, cache_win_t, new_k, new_v,
                             gates_s, steps, SAMPLE_SEQS)
    o_attn_t = o_s.reshape(nbs, N_KV_HEADS, GQA, steps, HEAD_DIM).transpose(1, 2, 4, 3, 0).reshape(ATTN_WIDTH, rows_s)
    y_sample = _mlp(xs, o_attn_t, y_rnn, *mlp_w, rows_s, FF_CHUNK).reshape(steps, nbs, D_MODEL).transpose(1, 0, 2)
    kv_sample = kvt[0].reshape(4, N_KV_HEADS, HEAD_DIM, steps, nbs).transpose(4, 3, 0, 1, 2)[None]
    win_sample = win_t.transpose(0, 4, 1, 2, 3)[None]
    conv_sample = conv_s.transpose(1, 0, 2)[None]
    h_sample = h_s[None]

    return (y_prompt, y_sample, kv_prompt, kv_sample, win_prompt, win_sample, conv_prompt, conv_sample,
            h_prompt, h_sample)
```

```python
import functools

import jax
import jax.numpy as jnp
import numpy as np
from jax import lax
from jax.experimental import pallas as pl
from jax.experimental.pallas import tpu as pltpu

D_MODEL = 1024
PAST_LEN = 2048
PAGE_SIZE = 128
HEAD_DIM = 64
N_HEADS = 8
N_KV_HEADS = 2
GQA = N_HEADS // N_KV_HEADS
ATTN_WIDTH = N_HEADS * HEAD_DIM
D_RNN = D_MODEL - ATTN_WIDTH
CONV_WIDTH = 4
LRU_C = 8.0
D_FF = 4 * D_MODEL
ROT_DIM = HEAD_DIM // 4
ROT_HALF = ROT_DIM // 2
ROPE_THETA = 500000.0
CMP_BLOCK = 32
CMP_STRIDE = 16
SEL_BLOCK = 64
TOP_N = 16
WINDOW = 512
KV_COLS = N_KV_HEADS * HEAD_DIM
EPS = 1e-6
NEG = -1e30
SEL_BONUS = 1e4
SEL_MASK = 2.0 ** 100
SCALE = HEAD_DIM ** -0.5

SUBLANES = 8
LANES = 128
VMEM_LIMIT = 48 * 1024 * 1024

ROW_TILE = 512
RNN_CHUNK = 512
Q_TILE = 256
SLC_CHUNK = 512
FF_CHUNK = 1024
CMP_SEQS = 4
PROMPT_CMP_SEQS = 2
SAMPLE_SEQS = 2
EXPM1_FLOOR = -30.0

N_CMP_PAD = 128
GATE_ROWS = 16

F32 = jnp.float32
BF16 = jnp.bfloat16
NT_DIMS = (((1,), (1,)), ((), ()))
TN_DIMS = (((0,), (0,)), ((), ()))


def _nt(a, b):
    return lax.dot_general(a, b, NT_DIMS, preferred_element_type=F32)


def _nn(a, b):
    return jnp.dot(a, b, preferred_element_type=F32)


def _rms(x, g):
    return x * lax.rsqrt(jnp.mean(x * x, axis=-1, keepdims=True) + EPS) * g


def _params(*sem):
    return pltpu.CompilerParams(dimension_semantics=sem, vmem_limit_bytes=VMEM_LIMIT)


def _proj_kernel(x_ref, nm_ref, wq_ref, wkv_ref, wr_ref, wg_ref, cos_ref, sin_ref,
                 qt_ref, qrt_ref, kvt_ref, wint_ref, kb_ref, vb_ref, rg_ref, rx_ref, gt_ref):
    u = _rms(x_ref[...], nm_ref[...]).astype(BF16)
    cos, sin = cos_ref[...], sin_ref[...]

    def rotated(a, r0):
        x1, x2 = a[r0:r0 + ROT_HALF], a[r0 + ROT_HALF:r0 + ROT_DIM]
        return x1 * cos - x2 * sin, x2 * cos + x1 * sin

    qt = _nt(wq_ref[...], u)
    qt_ref[0] = (qt * SCALE).astype(BF16)
    parts = []
    for head in range(N_HEADS):
        r0 = head * HEAD_DIM
        parts += [*rotated(qt, r0), qt[r0 + ROT_DIM:r0 + HEAD_DIM]]
    qrt_ref[0] = (jnp.concatenate(parts, axis=0) * SCALE).astype(BF16)

    kvt = _nt(wkv_ref[...], u)
    kvt_ref[0] = kvt[0:4 * KV_COLS]
    wint_ref[0] = kvt[4 * KV_COLS:6 * KV_COLS]
    for out_ref, src0, dst0 in ((kvt_ref, 2 * KV_COLS, 2 * KV_COLS), (wint_ref, 4 * KV_COLS, 0)):
        for h in range(N_KV_HEADS):
            d = dst0 + h * HEAD_DIM
            out_ref[0, d:d + ROT_HALF], out_ref[0, d + ROT_HALF:d + ROT_DIM] = rotated(kvt, src0 + h * HEAD_DIM)
    tm = x_ref.shape[0]
    ones_row = (lax.broadcasted_iota(jnp.int32, (HEAD_DIM, tm), 0) == 0).astype(BF16)
    for branch, src_ref, k0, v0 in ((0, kvt_ref, 2 * KV_COLS, 3 * KV_COLS), (1, wint_ref, 0, KV_COLS)):
        for h in range(N_KV_HEADS):
            i, r = branch * N_KV_HEADS + h, h * HEAD_DIM
            kb_ref[0, i] = src_ref[0, k0 + r:k0 + r + HEAD_DIM].astype(BF16)
            vb_ref[0, i, 0:HEAD_DIM] = src_ref[0, v0 + r:v0 + r + HEAD_DIM].astype(BF16)
            vb_ref[0, i, HEAD_DIM:2 * HEAD_DIM] = ones_row

    r = _nt(u, wr_ref[...])
    rg_ref[...] = r[:, 0:D_RNN]
    rx_ref[...] = r[:, D_RNN:2 * D_RNN]
    gt_ref[0] = jax.nn.sigmoid(_nt(wg_ref[...], u))


def _rope_tables(pos):
    inv = ROPE_THETA ** (-jnp.arange(ROT_HALF, dtype=F32) / ROT_HALF)
    ang = pos.astype(F32)[:, None] * inv
    return jnp.cos(ang).T, jnp.sin(ang).T


def _proj(x, pos, nm, wq, wkv, wr, wg, nb, tt, tm):
    rows = nb * tt
    nt = tt // tm
    cos, sin = _rope_tables(pos)
    row = lambda w: pl.BlockSpec((tm, w), lambda i: (i, 0))
    full = lambda a: pl.BlockSpec(a.shape, lambda i: (0,) * a.ndim)
    tab = pl.BlockSpec((ROT_HALF, tm), lambda i: (0, i % nt))
    tr = lambda r: pl.BlockSpec((1, r, tm), lambda i: (i // nt, 0, i % nt))
    out_shape = (
        jax.ShapeDtypeStruct((nb, ATTN_WIDTH, tt), BF16),
        jax.ShapeDtypeStruct((nb, ATTN_WIDTH, tt), BF16),
        jax.ShapeDtypeStruct((nb, 4 * KV_COLS, tt), F32),
        jax.ShapeDtypeStruct((nb, 2 * KV_COLS, tt), F32),
        jax.ShapeDtypeStruct((nb, 2 * N_KV_HEADS, HEAD_DIM, tt), BF16),
        jax.ShapeDtypeStruct((nb, 2 * N_KV_HEADS, 2 * HEAD_DIM, tt), BF16),
        jax.ShapeDtypeStruct((rows, D_RNN), F32),
        jax.ShapeDtypeStruct((rows, D_RNN), F32),
        jax.ShapeDtypeStruct((nb, N_KV_HEADS * GATE_ROWS, tt), F32),
    )
    tr4 = lambda r: pl.BlockSpec((1, 2 * N_KV_HEADS, r, tm), lambda i: (i // nt, 0, 0, i % nt))
    out_specs = (tr(ATTN_WIDTH), tr(ATTN_WIDTH), tr(4 * KV_COLS), tr(2 * KV_COLS), tr4(HEAD_DIM), tr4(2 * HEAD_DIM),
                 row(D_RNN), row(D_RNN), tr(N_KV_HEADS * GATE_ROWS))
    return pl.pallas_call(
        _proj_kernel, grid=(rows // tm,),
        in_specs=[row(D_MODEL), full(nm), full(wq), full(wkv), full(wr), full(wg), tab, tab],
        out_specs=out_specs, out_shape=out_shape, compiler_params=_params("parallel"), name="proj",
    )(x, nm, wq, wkv, wr, wg, cos, sin)


def _expm1_from_exp(x, u):
    near = jnp.where(u == 1.0, x, (u - 1.0) * x / jnp.log(u))
    return jnp.where(x < EXPM1_FLOOR, -1.0, near)


def _softplus(x):
    return jnp.maximum(x, 0.0) + jnp.log1p(jnp.exp(-jnp.abs(x)))


def _lru_coeffs(xc, wra_ref, wri_ref, bra_ref, bri_ref, lam_ref):
    xb = xc.astype(BF16)
    r = jax.nn.sigmoid(_nn(xb, wra_ref[...]) + bra_ref[...])
    i = jax.nn.sigmoid(_nn(xb, wri_ref[...]) + bri_ref[...])
    log_a = -LRU_C * r * _softplus(-lam_ref[...])
    a = jnp.exp(log_a)
    b = jnp.sqrt(-_expm1_from_exp(2.0 * log_a, a * a)) * (i * xc)
    return a, b


def _rnn_out(rg, h, g):
    return _rms(jax.nn.gelu(rg) * h, g).astype(BF16)


def _rglru_prompt_kernel(rx_ref, rg_ref, cw_ref, cb_ref, wra_ref, wri_ref, bra_ref, bri_ref, lam_ref, g_ref,
                         y_ref, hl_ref, prev_sc, h_sc):
    tt = rx_ref.shape[0]

    @pl.when(pl.program_id(1) == 0)
    def _():
        prev_sc[...] = jnp.zeros_like(prev_sc)
        h_sc[...] = jnp.zeros_like(h_sc)

    rx = rx_ref[...]
    ext = jnp.concatenate([prev_sc[...], rx], axis=0)
    shifted = lambda d: pltpu.roll(ext, d, 0)[SUBLANES:SUBLANES + tt]
    cw = cw_ref[...]
    xc = cb_ref[...] + cw[0:1] * shifted(3)
    xc = xc + cw[1:2] * shifted(2)
    xc = xc + cw[2:3] * shifted(1)
    xc = xc + cw[3:4] * rx
    prev_sc[...] = rx[tt - SUBLANES:tt]

    a, b = _lru_coeffs(xc, wra_ref, wri_ref, bra_ref, bri_ref, lam_ref)
    n_tiles, width = tt // SUBLANES, a.shape[1]
    a = a.reshape(n_tiles, SUBLANES, width)
    b = b.reshape(n_tiles, SUBLANES, width)
    sub = lax.broadcasted_iota(jnp.int32, (1, SUBLANES, 1), 1)
    s = 1
    while s < SUBLANES:
        a_sh = jnp.where(sub >= s, pltpu.roll(a, s, 1), 1.0)
        b_sh = jnp.where(sub >= s, pltpu.roll(b, s, 1), 0.0)
        b = a * b_sh + b
        a = a * a_sh
        s *= 2
    state, tiles = h_sc[...], []
    for t in range(n_tiles):
        h_t = a[t] * state + b[t]
        tiles.append(h_t)
        state = h_t[SUBLANES - 1:SUBLANES]
    h = jnp.concatenate(tiles, axis=0)
    h_sc[...] = state
    hl_ref[0] = state
    y_ref[...] = _rnn_out(rg_ref[...], h, g_ref[...])


def _rglru_sample_kernel(rx_ref, rg_ref, cp_ref, h0_ref, cw_ref, cb_ref, wra_ref, wri_ref, bra_ref, bri_ref,
                         lam_ref, g_ref, y_ref, cs_ref, hl_ref):
    nb = h0_ref.shape[0]
    steps = rx_ref.shape[0] // nb
    xp = [cp_ref[k] for k in range(CONV_WIDTH - 1)] + [rx_ref[t * nb:(t + 1) * nb] for t in range(steps)]
    cw, cb = cw_ref[...], cb_ref[...]
    xcs = []
    for t in range(steps):
        xc = cb + cw[0:1] * xp[t]
        for tap in range(1, CONV_WIDTH):
            xc = xc + cw[tap:tap + 1] * xp[t + tap]
        xcs.append(xc)
    a, b = _lru_coeffs(jnp.concatenate(xcs, axis=0), wra_ref, wri_ref, bra_ref, bri_ref, lam_ref)
    h = h0_ref[...]
    for t in range(steps):
        sl = slice(t * nb, (t + 1) * nb)
        h = a[sl] * h + b[sl]
        y_ref[sl] = _rnn_out(rg_ref[sl], h, g_ref[...])
    for k in range(CONV_WIDTH - 1):
        cs_ref[k] = xp[steps + k]
    hl_ref[...] = h


def _rglru_prompt(rx, rg, weights, nb, tt, chunk):
    nc = tt // chunk
    row = pl.BlockSpec((chunk, D_RNN), lambda n, c: (n * nc + c, 0))
    full = lambda a: pl.BlockSpec(a.shape, lambda n, c: (0,) * a.ndim)
    return pl.pallas_call(
        _rglru_prompt_kernel, grid=(nb, nc),
        in_specs=[row, row] + [full(w) for w in weights],
        out_specs=(row, pl.BlockSpec((1, 1, D_RNN), lambda n, c: (n, 0, 0))),
        out_shape=(jax.ShapeDtypeStruct((nb * tt, D_RNN), BF16), jax.ShapeDtypeStruct((nb, 1, D_RNN), F32)),
        scratch_shapes=[pltpu.VMEM((SUBLANES, D_RNN), F32), pltpu.VMEM((1, D_RNN), F32)],
        compiler_params=_params("parallel", "arbitrary"), name="rglru_prompt",
    )(rx, rg, *weights)


def _rglru_sample(rx, rg, conv_prev, h0, weights):
    nb = h0.shape[0]
    return pl.pallas_call(
        _rglru_sample_kernel,
        out_shape=(jax.ShapeDtypeStruct(rx.shape, BF16), jax.ShapeDtypeStruct(conv_prev.shape, F32),
                   jax.ShapeDtypeStruct((nb, D_RNN), F32)),
        compiler_params=pltpu.CompilerParams(vmem_limit_bytes=VMEM_LIMIT), name="rglru_sample",
    )(rx, rg, conv_prev, h0, *weights)


CMP_COLS = 2 * KV_COLS
HALF = CMP_BLOCK // 2
BLOCKS_PER_PAGE = PAGE_SIZE // CMP_STRIDE


def _compress_kernel(pt_ref, *refs, n_pages):
    del pt_ref
    page_refs = refs[:n_pages]
    pet_ref, perm_ref, w1_ref, w2_ref, out_ref, z_sc = refs[n_pages:]
    seqs = out_ref.shape[0]
    for k in range(n_pages):
        xt = page_refs[k][0].reshape(CMP_COLS, PAGE_SIZE)
        for half in range(2):
            a = (xt + pet_ref[half]).astype(BF16)
            z_sc[half, k] = _nt(perm_ref[...], a)
    pre = []
    for half in range(2):
        acc = jnp.zeros((seqs * N_CMP_PAD, CMP_COLS), F32)
        for j in range(CMP_STRIDE):
            rows = z_sc[half, :, j * BLOCKS_PER_PAGE:(j + 1) * BLOCKS_PER_PAGE, :]
            acc = acc + _nn(rows.reshape(seqs * N_CMP_PAD, CMP_COLS).astype(BF16), w1_ref[half, j])
        pre.append(acc)
    hid = jax.nn.gelu(pre[0] + pltpu.roll(pre[1], seqs * N_CMP_PAD - 1, 0))
    out_ref[...] = _nn(hid.astype(BF16), w2_ref[...]).astype(BF16).reshape(seqs, N_CMP_PAD, CMP_COLS)


def _compress(pages, page_spec, page_table, pet, w1, w2, nb, seqs):
    n_pages = page_table.shape[1]
    assert n_pages * BLOCKS_PER_PAGE == N_CMP_PAD
    pos = np.arange(PAGE_SIZE)
    perm = jnp.asarray((pos % CMP_STRIDE * BLOCKS_PER_PAGE + pos // CMP_STRIDE)[None, :]
                       == np.arange(PAGE_SIZE)[:, None], BF16)
    full = lambda a: pl.BlockSpec(a.shape, lambda n, pt: (0,) * a.ndim)
    once = lambda a: pl.BlockSpec(a.shape, lambda n, pt: (0,) * a.ndim, pipeline_mode=pl.Buffered(1))
    page_specs = [page_spec(i, k) for i in range(seqs) for k in range(n_pages)]
    grid_spec = pltpu.PrefetchScalarGridSpec(
        num_scalar_prefetch=1, grid=(nb // seqs,),
        in_specs=page_specs + [full(pet), full(perm), once(w1), full(w2)],
        out_specs=pl.BlockSpec((seqs, N_CMP_PAD, CMP_COLS), lambda n, pt: (n, 0, 0)),
        scratch_shapes=[pltpu.VMEM((2, len(page_specs), PAGE_SIZE, CMP_COLS), F32)])
    return pl.pallas_call(
        functools.partial(_compress_kernel, n_pages=len(page_specs)), grid_spec=grid_spec,
        out_shape=jax.ShapeDtypeStruct((nb, N_CMP_PAD, CMP_COLS), BF16),
        compiler_params=_params("parallel"), name="compress",
    )(page_table, *([pages] * len(page_specs)), pet, perm, w1, w2)


def _cmp_cols(cmp_ref, i, slot, h):
    c0 = slot * KV_COLS + h * HEAD_DIM
    return cmp_ref[i, :, c0:c0 + HEAD_DIM]


def _overlap_matrix(n_sel):
    n_cmp = N_CMP_PAD - 1
    c0 = np.arange(N_CMP_PAD)[:, None] * CMP_STRIDE
    j0 = np.arange(LANES)[None, :] * SEL_BLOCK
    ov = (c0 < j0 + SEL_BLOCK) & (c0 + CMP_BLOCK > j0)
    ov &= (np.arange(N_CMP_PAD)[:, None] < n_cmp) & (np.arange(LANES)[None, :] < n_sel)
    return ov.astype(np.float32)


def _topk_member(score, idx, n, axis):
    rank = jnp.zeros(score.shape, jnp.int32)
    for i in range(n):
        si = score[i:i + 1, :] if axis == 0 else score[:, i:i + 1]
        beats = (si > score) | ((si == score) & (i < idx))
        rank = rank + beats.astype(jnp.int32)
    return rank < TOP_N


def _sel_score(imp, j, cur):
    valid = j <= cur
    forced = (j == 0) | (j == cur) | (j == cur - 1)
    return jnp.where(valid, imp, -SEL_BONUS) + jnp.where(forced, SEL_BONUS, 0.0)


def _nsa_prompt_kernel(qt_ref, qrt_ref, kb_ref, vb_ref, cmp_ref, gt_ref, ovt_ref, e_ref,
                       o_ref, m_sc, acc_sc, *, tq, slc_chunk, col_block, n_sel):
    heads = range(N_KV_HEADS)
    q0 = pl.program_id(1) * tq
    wide = GQA * tq
    qpos = q0 + lax.broadcasted_iota(jnp.int32, (1, tq), 1)
    tile = lambda a, n: jnp.concatenate([a] * n, axis=1)
    group = lambda ref, h: jnp.concatenate(
        [ref[0, (h * GQA + g) * HEAD_DIM:(h * GQA + g + 1) * HEAD_DIM, :] for g in range(GQA)], axis=1)

    cidx = lax.broadcasted_iota(jnp.int32, (N_CMP_PAD, 1), 0)
    real = cidx < N_CMP_PAD - 1
    qpos_all = q0 + lax.broadcasted_iota(jnp.int32, (1, N_HEADS * tq), 1) % tq
    cmask = (cidx * CMP_STRIDE + CMP_BLOCK - 1 <= qpos_all) & real
    s = jnp.concatenate([_nn(_cmp_cols(cmp_ref, 0, 0, h), group(qt_ref, h)) for h in heads], axis=1)
    s = jnp.where(real, jnp.where(cmask, s, NEG), -jnp.inf)
    e = jnp.exp(s - jnp.max(s, axis=0, keepdims=True))
    p = (e / jnp.sum(e, axis=0, keepdims=True)) * cmask.astype(F32)
    pb = p.astype(BF16)
    o_cmp = [lax.dot_general(_cmp_cols(cmp_ref, 0, 1, h), pb[:, h * wide:(h + 1) * wide], TN_DIMS,
                             preferred_element_type=F32) for h in heads]

    imp = []
    for h in heads:
        imp_h = _nn(ovt_ref[...], pb[:, h * wide:h * wide + tq])
        for g in range(1, GQA):
            imp_h = imp_h + _nn(ovt_ref[...], pb[:, h * wide + g * tq:h * wide + (g + 1) * tq])
        imp.append(imp_h)
    j = lax.broadcasted_iota(jnp.int32, (n_sel, 1), 0)
    qpos_2 = q0 + lax.broadcasted_iota(jnp.int32, (1, N_KV_HEADS * tq), 1) % tq
    picked = _topk_member(_sel_score(jnp.concatenate(imp, axis=1), j, qpos_2 // SEL_BLOCK), j, n_sel, axis=0)
    unpicked = jnp.where(picked, 0.0, -SEL_MASK).astype(BF16)
    pad_rows = jnp.zeros((HEAD_DIM - n_sel, wide), BF16)
    qr = [group(qrt_ref, h) for h in heads]
    qsel = [jnp.concatenate([qr[h], tile(unpicked[:, h * tq:(h + 1) * tq], GQA), pad_rows], axis=0)
            for h in heads]

    def attend(h, qx, k, v, allowed, first):
        bias = jnp.where(allowed, 0.0, NEG)
        pes, ms, alphas = [], [], []
        for c0 in range(0, wide, col_block):
            t0 = c0 % tq
            sc = lax.dot_general(k, qx[:, c0:c0 + col_block], TN_DIMS, preferred_element_type=F32)
            x = sc + bias[:, t0:t0 + col_block]
            m = jnp.max(x, axis=0, keepdims=True)
            if not first:
                m_prev = m_sc[h, :, c0:c0 + col_block]
                m = jnp.maximum(m_prev, m)
                alphas.append(jnp.exp(m_prev - m))
            ms.append(m)
            pes.append(jnp.exp((x - m).astype(BF16)))
        pv = _nn(v, jnp.concatenate(pes, axis=1))
        m_sc[h] = jnp.concatenate(ms, axis=1)
        acc_sc[h] = pv if first else jnp.concatenate(alphas, axis=1) * acc_sc[h] + pv

    def normalised(h):
        acc = acc_sc[h]
        return acc[0:HEAD_DIM] / acc[HEAD_DIM:HEAD_DIM + 1]

    wn = WINDOW + tq
    w0 = pl.multiple_of(jnp.maximum(q0 - WINDOW, 0), LANES)
    kpos = w0 + lax.broadcasted_iota(jnp.int32, (wn, 1), 0)
    in_window = (kpos <= qpos) & (kpos > qpos - WINDOW)
    for h in heads:
        attend(h, qr[h], kb_ref[0, N_KV_HEADS + h, :, pl.ds(w0, wn)], vb_ref[0, N_KV_HEADS + h, :, pl.ds(w0, wn)],
               in_window, True)
    o_win = [normalised(h) for h in heads]

    for c0 in range(0, kb_ref.shape[3], slc_chunk):
        def chunk(c0=c0):
            kpos = c0 + lax.broadcasted_iota(jnp.int32, (slc_chunk, 1), 0)
            member = jnp.concatenate([e_ref[:, c0:c0 + slc_chunk],
                                      jnp.zeros((HEAD_DIM - n_sel, slc_chunk), BF16)], axis=0)
            for h in heads:
                k = jnp.concatenate([kb_ref[0, h, :, c0:c0 + slc_chunk], member], axis=0)
                attend(h, qsel[h], k, vb_ref[0, h, :, c0:c0 + slc_chunk], kpos <= qpos, c0 == 0)

        if c0 == 0:
            chunk()
        else:
            pl.when(q0 + tq > c0)(chunk)
    o_slc = [normalised(h) for h in heads]

    gt = gt_ref[0]
    for h in heads:
        for g in range(GQA):
            gate = lambda b: gt[h * GATE_ROWS + b * GQA + g:h * GATE_ROWS + b * GQA + g + 1, :]
            cols = slice(g * tq, (g + 1) * tq)
            o = gate(0) * o_cmp[h][:, cols] + gate(1) * o_slc[h][:, cols]
            o = o + gate(2) * o_win[h][:, cols]
            o_ref[(h * GQA + g) * HEAD_DIM:(h * GQA + g + 1) * HEAD_DIM, :] = o


def _nsa_prompt(qt, qrt, kb, vb, cmp, gt, nb, tt, tq, slc_chunk):
    n_sel = -(-tt // SEL_BLOCK)
    nq = tt // tq
    ovt = jnp.asarray(_overlap_matrix(n_sel)[:, :n_sel].T, BF16)
    expand = jnp.asarray(np.arange(tt)[None, :] // SEL_BLOCK == np.arange(n_sel)[:, None], BF16)
    cols = lambda a: pl.BlockSpec((1, a.shape[1], tq), lambda n, i: (n, 0, i))
    seq = lambda a: pl.BlockSpec((1,) + a.shape[1:], lambda n, i: (n,) + (0,) * (a.ndim - 1))
    full = lambda a: pl.BlockSpec(a.shape, lambda n, i: (0,) * a.ndim)
    return pl.pallas_call(
        functools.partial(_nsa_prompt_kernel, tq=tq, slc_chunk=slc_chunk, col_block=LANES, n_sel=n_sel),
        grid=(nb, nq),
        in_specs=[cols(qt), cols(qrt), seq(kb), seq(vb), seq(cmp), cols(gt), full(ovt), full(expand)],
        out_specs=pl.BlockSpec((ATTN_WIDTH, tq), lambda n, i: (0, n * nq + i)),
        out_shape=jax.ShapeDtypeStruct((ATTN_WIDTH, nb * tt), F32),
        scratch_shapes=[pltpu.VMEM((N_KV_HEADS, 1, GQA * tq), F32),
                        pltpu.VMEM((N_KV_HEADS, 2 * HEAD_DIM, GQA * tq), F32)],
        compiler_params=_params("parallel", "arbitrary"), name="nsa_prompt",
    )(qt, qrt, kb, vb, cmp, gt, ovt, expand)


def _nsa_sample_kernel(pt_ref, *refs, n_pages, steps, seqs):
    del pt_ref
    q_ref, qr_ref = refs[0:2]
    page_refs = refs[2:2 + seqs * n_pages]
    cmp_ref, cw_ref, newk_ref, newv_ref, gates_ref, ov_ref, o_ref, wout_ref = refs[2 + seqs * n_pages:]
    rows = GQA * steps
    past = n_pages * PAGE_SIZE
    wc = cw_ref.shape[-1]
    trow = lax.broadcasted_iota(jnp.int32, (rows, 1), 0) % steps
    lane = lax.broadcasted_iota(jnp.int32, (1, LANES), 1)
    n_sel = -(-(past + steps) // SEL_BLOCK)
    cur = (past + trow) // SEL_BLOCK
    real = lane < N_CMP_PAD - 1
    wlane = lax.broadcasted_iota(jnp.int32, (1, wc), 1)
    in_window = jnp.broadcast_to(wlane + (WINDOW - wc) > trow, (rows, wc))
    per_tile = PAGE_SIZE // SEL_BLOCK

    chains = [(i, h) for i in range(seqs) for h in range(N_KV_HEADS)]
    assert len(chains) * steps == rows, "the stacked top-k below holds one chain per query-group slot"
    s = jnp.concatenate([_nt(q_ref[i, h], _cmp_cols(cmp_ref, i, 0, h)) for i, h in chains], axis=0)
    s = jnp.where(real, s, -jnp.inf)
    e = jnp.exp(s - jnp.max(s, axis=-1, keepdims=True))
    pb_all = (e / jnp.sum(e, axis=-1, keepdims=True)).astype(BF16)
    group = lax.broadcasted_iota(jnp.int32, (rows, 1), 0) // steps
    o_cmps, imp_all = [], None
    for c, (i, h) in enumerate(chains):
        pb = pb_all[c * rows:(c + 1) * rows]
        o_cmps.append(_nn(pb, _cmp_cols(cmp_ref, i, 1, h)))
        part = _nn(pb, ov_ref[...])
        imp = part
        for g in range(1, GQA):
            imp = imp + pltpu.roll(part, g * steps, 0)
        imp_all = imp if c == 0 else jnp.where(group == c, imp, imp_all)
    score = jnp.where(lane < n_sel, _sel_score(imp_all, lane, cur), -jnp.inf)
    sel_all = (_topk_member(score, lane, n_sel, axis=1) & (lane < n_sel)).astype(F32)
    rolled = [sel_all] + [pltpu.roll(sel_all, k * steps, 0) for k in range(1, GQA)]

    for c, (i, h) in enumerate(chains):
        newk, newv = newk_ref[i], newv_ref[i]
        newk_b, newv_b = newk.astype(BF16), newv.astype(BF16)
        qr = qr_ref[i, h]
        pages = page_refs[i * n_pages:(i + 1) * n_pages]

        def new_ok(branch, h=h):
            base = (branch * N_KV_HEADS + h) * steps
            return (lane >= base) & (lane < base + steps) & (lane - base <= trow)

        sel_f = rolled[(0 - c) % GQA]
        for g in range(1, GQA):
            sel_f = jnp.where(group == g, rolled[(g - c) % GQA], sel_f)

        ks = jnp.concatenate([p[0, 0, h].astype(BF16) for p in pages] + [newk_b], axis=1)
        vs = jnp.concatenate([p[0, 1, h].astype(BF16) for p in pages] + [newv_b], axis=1)
        tiles = []
        for k in range(n_pages):
            m = sel_f[:, per_tile * k:per_tile * k + 1]
            for b in range(1, per_tile):
                m = jnp.where(lane < b * SEL_BLOCK, m, sel_f[:, per_tile * k + b:per_tile * k + b + 1])
            tiles.append(m > 0.5)
        tiles.append((sel_f[:, n_sel - 1:n_sel] > 0.5) & new_ok(0))
        msk = jnp.concatenate(tiles, axis=1)
        sc = jnp.where(msk, _nn(qr, ks), NEG)
        pe = jnp.where(msk, jnp.exp(sc - jnp.max(sc, axis=-1, keepdims=True)), 0.0)
        o_slc = _nt(pe.astype(BF16), vs) / jnp.sum(pe, axis=-1, keepdims=True)

        kw = jnp.concatenate([cw_ref[i, 0, h].astype(BF16), newk_b], axis=1)
        vw = jnp.concatenate([cw_ref[i, 1, h].astype(BF16), newv_b], axis=1)
        wmsk = jnp.concatenate([in_window, jnp.broadcast_to(new_ok(1), (rows, LANES))], axis=1)
        sw = jnp.where(wmsk, _nn(qr, kw), NEG)
        pw = jnp.where(wmsk, jnp.exp(sw - jnp.max(sw, axis=-1, keepdims=True)), 0.0)
        o_win = _nt(pw.astype(BF16), vw) / jnp.sum(pw, axis=-1, keepdims=True)

        gt = gates_ref[i, h]
        o_ref[i, h] = gt[:, 0:1] * o_cmps[c] + gt[:, 1:2] * o_slc + gt[:, 2:3] * o_win

        to_end = (LANES - steps - (N_KV_HEADS + h) * steps) % LANES
        for slot, new in ((0, newk), (1, newv)):
            shifted = pltpu.roll(cw_ref[i, slot, h], wc - steps, 1)
            wout_ref[i, slot, h, :, 0:wc - LANES] = shifted[:, 0:wc - LANES]
            wout_ref[i, slot, h, :, wc - LANES:wc] = jnp.where(lane >= LANES - steps, pltpu.roll(new, to_end, 1),
                                                               shifted[:, wc - LANES:wc])


def _nsa_sample(q, qr, pages, page_table, cmp, cache_win_t, new_k, new_v, gates, steps, seqs):
    nb, n_pages = page_table.shape
    n_sel = -(-(n_pages * PAGE_SIZE + steps) // SEL_BLOCK)
    ov = jnp.asarray(_overlap_matrix(n_sel), BF16)
    rows = GQA * steps
    lead = lambda a: pl.BlockSpec((seqs,) + a.shape[1:], lambda n, pt: (n,) + (0,) * (a.ndim - 1))
    page_spec = lambda i, k: pl.BlockSpec((1, 2, N_KV_HEADS, HEAD_DIM, PAGE_SIZE),
                                          lambda n, pt: (pt[n * seqs + i, k], 1, 0, 0, 0))
    page_specs = [page_spec(i, k) for i in range(seqs) for k in range(n_pages)]
    grid_spec = pltpu.PrefetchScalarGridSpec(
        num_scalar_prefetch=1, grid=(nb // seqs,),
        in_specs=[lead(q), lead(qr)] + page_specs
        + [lead(cmp), lead(cache_win_t), lead(new_k), lead(new_v), lead(gates),
           pl.BlockSpec(ov.shape, lambda n, pt: (0, 0))],
        out_specs=(pl.BlockSpec((seqs, N_KV_HEADS, rows, HEAD_DIM), lambda n, pt: (n, 0, 0, 0)), lead(cache_win_t)))
    return pl.pallas_call(
        functools.partial(_nsa_sample_kernel, n_pages=n_pages, steps=steps, seqs=seqs), grid_spec=grid_spec,
        out_shape=(jax.ShapeDtypeStruct((nb, N_KV_HEADS, rows, HEAD_DIM), F32),
                   jax.ShapeDtypeStruct(cache_win_t.shape, F32)),
        compiler_params=_params("parallel"), name="nsa_sample",
    )(page_table, q, qr, *([pages] * len(page_specs)), cmp, cache_win_t, new_k, new_v, gates, ov)


def _mlp_kernel(x_ref, oat_ref, yr_ref, ga_ref, wo_ref, nmlp_ref, wup_ref, wdn_ref, nfin_ref, y_ref, *, ff_chunk):
    oat = oat_ref[...]
    at = (oat * lax.rsqrt(jnp.mean(oat * oat, axis=0, keepdims=True) + EPS) * ga_ref[...]).astype(BF16)
    attn = lax.dot_general(at, wo_ref[0:ATTN_WIDTH, :], TN_DIMS, preferred_element_type=F32)
    x1 = x_ref[...] + (attn + _nn(yr_ref[...], wo_ref[ATTN_WIDTH:, :]))
    v = _rms(x1, nmlp_ref[...]).astype(BF16)
    acc = jnp.zeros_like(x1)
    for c in range(D_FF // ff_chunk):
        sl = slice(c * ff_chunk, (c + 1) * ff_chunk)
        hid = jnp.square(jnp.maximum(_nn(v, wup_ref[:, sl]), 0.0)).astype(BF16)
        acc = acc + _nn(hid, wdn_ref[sl, :])
    y_ref[...] = _rms(x1 + acc, nfin_ref[...])


def _mlp(x, o_attn_t, y_rnn, g_attn, w_out, norm_mlp, w_up, w_down, norm_final, tm, ff_chunk):
    rows = x.shape[0]
    row = lambda w: pl.BlockSpec((tm, w), lambda i: (i, 0))
    full = lambda a: pl.BlockSpec(a.shape, lambda i: (0,) * a.ndim)
    once = lambda a: pl.BlockSpec(a.shape, lambda i: (0,) * a.ndim, pipeline_mode=pl.Buffered(1))
    return pl.pallas_call(
        functools.partial(_mlp_kernel, ff_chunk=ff_chunk), grid=(rows // tm,),
        in_specs=[row(D_MODEL), pl.BlockSpec((ATTN_WIDTH, tm), lambda i: (0, i)), row(D_RNN), full(g_attn),
                  once(w_out), full(norm_mlp),
                  once(w_up), once(w_down), full(norm_final)],
        out_specs=row(D_MODEL), out_shape=jax.ShapeDtypeStruct((rows, D_MODEL), F32),
        compiler_params=_params("parallel"), name="mlp",
    )(x, o_attn_t, y_rnn, g_attn, w_out, norm_mlp, w_up, w_down, norm_final)


def _block_diag(w):
    nb, bs = w.shape[-3], w.shape[-1]
    tiled = jnp.concatenate([jnp.tile(w[..., b, :, :], (1,) * (w.ndim - 2) + (nb,)) for b in range(nb)], axis=-2)
    on_diagonal = np.kron(np.eye(nb, dtype=bool), np.ones((bs, bs), dtype=bool))
    return jnp.where(on_diagonal, tiled, jnp.zeros((), w.dtype))


def kernel(x_prompt, x_sample, cache_kv, cache_win, state_conv, state_rnn, page_table, w_in, pe_ck, w_ck1, w_ck2,
           pe_cv, w_cv1, w_cv2, g_attn, g_rnn, conv_w, conv_b, w_ra, b_ra, w_ri, b_ri, lam, w_out, norm_mix,
           norm_mlp, w_up, w_down, norm_final):
    assert w_in.shape[0] == 1, "single layer"
    nbp, tp, _ = x_prompt.shape
    nbs, steps, _ = x_sample.shape

    wt = jnp.transpose(w_in[0]).astype(BF16)
    c_kv, c_g = ATTN_WIDTH, ATTN_WIDTH + 6 * KV_COLS
    c_rg = c_g + 3 * N_HEADS
    wq, wkv = wt[0:c_kv], wt[c_kv:c_g]
    wg = wt[c_g:c_rg].reshape(3, N_KV_HEADS, GQA, D_MODEL).transpose(1, 0, 2, 3).reshape(N_KV_HEADS, 3 * GQA, D_MODEL)
    wg = jnp.pad(wg, ((0, 0), (0, GATE_ROWS - 3 * GQA), (0, 0))).reshape(N_KV_HEADS * GATE_ROWS, D_MODEL)
    wr = wt[c_rg:c_rg + 2 * D_RNN]
    row2 = lambda a: a.reshape(1, -1)
    rnn_w = (conv_w[0], row2(conv_b[0]), _block_diag(w_ra[0].astype(BF16)), _block_diag(w_ri[0].astype(BF16)),
             row2(b_ra[0]), row2(b_ri[0]), row2(lam[0]), row2(g_rnn[0]))
    per_cs = lambda k, v: jnp.repeat(jnp.stack([k, v]), N_KV_HEADS, axis=0)
    pe_cs = per_cs(pe_ck[0], pe_cv[0]).reshape(2 * N_KV_HEADS, 2, HALF, HEAD_DIM)
    cmp_pet = jnp.tile(pe_cs.transpose(1, 0, 3, 2), (1, 1, 1, BLOCKS_PER_PAGE)).reshape(2, CMP_COLS, PAGE_SIZE)
    w1_cs = per_cs(w_ck1[0], w_cv1[0]).transpose(1, 0, 2, 3)
    cmp_w1 = _block_diag(w1_cs.astype(BF16)).reshape(2, HALF, CMP_COLS, CMP_COLS)
    cmp_w2 = _block_diag(per_cs(w_ck2[0], w_cv2[0]).astype(BF16))
    mlp_w = (g_attn[0].reshape(-1, 1), w_out[0].astype(BF16), row2(norm_mlp[0]), w_up[0].astype(BF16),
             w_down[0].astype(BF16), row2(norm_final))
    nm = row2(norm_mix[0])

    xp = x_prompt.reshape(nbp * tp, D_MODEL)
    qt, qrt, kvt, wint, kb, vb, rg, rx, gt = _proj(xp, jnp.arange(tp), nm, wq, wkv, wr, wg, nbp, tp, ROW_TILE)
    y_rnn, h_last = _rglru_prompt(rx, rg, rnn_w, nbp, tp, RNN_CHUNK)
    own_pages = jnp.zeros((nbp, tp // PAGE_SIZE), jnp.int32)
    own_page = lambda i, k: pl.BlockSpec((1, CMP_COLS, PAGE_SIZE), lambda n, pt: (n * PROMPT_CMP_SEQS + i, 0, k))
    cmp = _compress(kvt, own_page, own_pages, cmp_pet, cmp_w1, cmp_w2, nbp, PROMPT_CMP_SEQS)
    o_attn_t = _nsa_prompt(qt, qrt, kb, vb, cmp, gt, nbp, tp, Q_TILE, SLC_CHUNK)
    y_prompt = _mlp(xp, o_attn_t, y_rnn, *mlp_w, ROW_TILE, FF_CHUNK).reshape(nbp, tp, D_MODEL)
    wlen = min(WINDOW, tp)
    kv_prompt = kvt.reshape(nbp, 4, N_KV_HEADS, HEAD_DIM, tp).transpose(0, 4, 1, 2, 3)[None]
    win_prompt = wint[:, :, tp - wlen:].reshape(nbp, 2, N_KV_HEADS, HEAD_DIM, wlen).transpose(0, 4, 1, 2, 3)[None]
    conv_prompt = rx.reshape(nbp, tp, D_RNN)[:, tp - (CONV_WIDTH - 1):][None]
    h_prompt = h_last.reshape(1, nbp, D_RNN)

    rows_s = nbs * steps
    xs = x_sample.transpose(1, 0, 2).reshape(rows_s, D_MODEL)
    pos_s = PAST_LEN + jnp.arange(rows_s) // nbs
    qt, qrt, kvt, wint, _, _, rg, rx, gt = _proj(xs, pos_s, nm, wq, wkv, wr, wg, 1, rows_s, rows_s)
    y_rnn, conv_s, h_s = _rglru_sample(rx, rg, state_conv[0].transpose(1, 0, 2), state_rnn[0], rnn_w)
    pages_t = cache_kv[0].transpose(0, 2, 3, 4, 1)
    cmp_page = lambda i, k: pl.BlockSpec((1, 2, N_KV_HEADS, HEAD_DIM, PAGE_SIZE),
                                         lambda n, pt: (pt[n * CMP_SEQS + i, k], 0, 0, 0, 0))
    cmp = _compress(pages_t, cmp_page, page_table, cmp_pet, cmp_w1, cmp_w2, nbs, CMP_SEQS)
    by_seq = lambda a: a.reshape(N_KV_HEADS, GQA, HEAD_DIM, steps, nbs).transpose(4, 0, 1, 3, 2).reshape(
        nbs, N_KV_HEADS, GQA * steps, HEAD_DIM)
    def new_tile(slc_rows, win_rows):
        t = jnp.stack([slc_rows, win_rows]).reshape(2 * N_KV_HEADS, HEAD_DIM, steps, nbs)
        t = t.transpose(3, 1, 0, 2).reshape(nbs, HEAD_DIM, 2 * N_KV_HEADS * steps)
        return jnp.pad(t, ((0, 0), (0, 0), (0, LANES - 2 * N_KV_HEADS * steps)))
    new_k = new_tile(kvt[0, 2 * KV_COLS:3 * KV_COLS], wint[0, 0:KV_COLS])
    new_v = new_tile(kvt[0, 3 * KV_COLS:4 * KV_COLS], wint[0, KV_COLS:2 * KV_COLS])
    gates_s = gt[0].reshape(N_KV_HEADS, GATE_ROWS, steps, nbs)[:, :3 * GQA].reshape(N_KV_HEADS, 3, GQA, steps, nbs)
    gates_s = gates_s.transpose(4, 0, 2, 3, 1).reshape(nbs, N_KV_HEADS, GQA * steps, 3)
    cache_win_t = cache_win[0].transpose(0, 2, 3, 4, 1)
    o_s, win_t = _nsa_sample(by_seq(qt[0]), by_seq(qrt[0]), pages_t, page_table, cmp, cache_win_t, new_k, new_v,
                             gates_s, steps, SAMPLE_SEQS)
    o_attn_t = o_s.reshape(nbs, N_KV_HEADS, GQA, steps, HEAD_DIM).transpose(1, 2, 4, 3, 0).reshape(ATTN_WIDTH, rows_s)
    y_sample = _mlp(xs, o_attn_t, y_rnn, *mlp_w, rows_s, FF_CHUNK).reshape(steps, nbs, D_MODEL).transpose(1, 0, 2)
    kv_sample = kvt[0].reshape(4, N_KV_HEADS, HEAD_DIM, steps, nbs).transpose(4, 3, 0, 1, 2)[None]
    win_sample = win_t.transpose(0, 4, 1, 2, 3)[None]
    conv_sample = conv_s.transpose(1, 0, 2)[None]
    h_sample = h_s[None]

    return (y_prompt, y_sample, kv_prompt, kv_sample, win_prompt, win_sample, conv_prompt, conv_sample,
            h_prompt, h_sample)
```
